```python
import jax
import jax.numpy as jnp
from jax import lax
import numpy as np

D_MODEL = 1024
BATCH = 2
SEQ = 8192
DEPTH = 1

HEAD_DIM = 64
MIX_WIDTH = D_MODEL
FOX_WIDTH = MIX_WIDTH // 2
NSA_WIDTH = MIX_WIDTH - FOX_WIDTH
FOX_HEADS = FOX_WIDTH // HEAD_DIM
NSA_HEADS = NSA_WIDTH // HEAD_DIM
NSA_KV_HEADS = 2
NSA_GROUP = NSA_HEADS // NSA_KV_HEADS
NSA_KV_WIDTH = NSA_KV_HEADS * HEAD_DIM
Q_BLOCK = 128
CMP_BLOCK = 32
CMP_STRIDE = 16
CMP_HIDDEN = 2 * HEAD_DIM
SLC_BLOCK = 64
N_SELECT = 16
WINDOW = 512
ROPE_THETA = 10000.0
RMS_EPS = 1e-6
SLC_RATIO = SLC_BLOCK // CMP_STRIDE
CMP_PAD_LEFT = CMP_BLOCK // CMP_STRIDE - 1
N_OVERLAP = (SLC_BLOCK + CMP_BLOCK) // CMP_STRIDE - 1
IN_SPLITS = (FOX_WIDTH, FOX_WIDTH, FOX_WIDTH, FOX_HEADS, FOX_WIDTH,
             NSA_WIDTH, NSA_KV_WIDTH, NSA_KV_WIDTH, NSA_KV_WIDTH, NSA_KV_WIDTH,
             NSA_KV_WIDTH, NSA_KV_WIDTH, 3 * NSA_HEADS, NSA_WIDTH)
IN_WIDTH = 4 * FOX_WIDTH + FOX_HEADS + 2 * NSA_WIDTH + 6 * NSA_KV_WIDTH + 3 * NSA_HEADS

kernel_name = 'hybrid_fox_nsa_block'


def rms_norm(x, g):
    xf = x.astype(jnp.float32)
    y = xf * lax.rsqrt(jnp.mean(xf * xf, axis=-1, keepdims=True) + RMS_EPS)
    return y.astype(x.dtype) * g


def rope_tables(seq_len):
    inv = 1.0 / (ROPE_THETA ** (jnp.arange(0, HEAD_DIM, 2, dtype=jnp.float32) / HEAD_DIM))
    ang = jnp.arange(seq_len, dtype=jnp.float32)[:, None] * inv[None, :]
    return jnp.cos(ang), jnp.sin(ang)


def apply_rope(t, cos, sin):
    t1, t2 = jnp.split(t.astype(jnp.float32), 2, axis=-1)
    out = jnp.concatenate([t1 * cos - t2 * sin, t2 * cos + t1 * sin], axis=-1)
    return out.astype(t.dtype)


def masked_softmax(logits, mask):
    logits = jnp.where(mask, logits.astype(jnp.float32), -jnp.inf)
    m = jnp.max(logits, axis=-1, keepdims=True)
    m = jnp.where(jnp.isfinite(m), m, 0.0)
    p = jnp.exp(logits - m)
    return p / jnp.maximum(jnp.sum(p, axis=-1, keepdims=True), 1e-30)


def fox_attention(q, k, v, log_f):
    b, h, s, d = q.shape
    scale = d ** -0.5
    cum = jnp.cumsum(log_f.astype(jnp.float32), axis=-1)
    kpos = jnp.arange(s)

    def block(i):
        q0 = i * Q_BLOCK
        qb = lax.dynamic_slice_in_dim(q, q0, Q_BLOCK, axis=2)
        cb = lax.dynamic_slice_in_dim(cum, q0, Q_BLOCK, axis=2)
        tq = q0 + jnp.arange(Q_BLOCK)
        logits = (jnp.einsum('bhqd,bhkd->bhqk', qb, k, preferred_element_type=jnp.float32) * scale
                  + cb[..., :, None] - cum[..., None, :])
        p = masked_softmax(logits, kpos[None, :] <= tq[:, None])
        return jnp.einsum('bhqk,bhkd->bhqd', p.astype(v.dtype), v)

    o = lax.map(block, jnp.arange(s // Q_BLOCK))
    return o.transpose(1, 2, 0, 3, 4).reshape(b, h, s, d)


def nsa_compress(t, w1, w2, pe):
    b, g, s, d = t.shape
    n_cmp = (s - CMP_BLOCK) // CMP_STRIDE + 1
    idx = jnp.arange(n_cmp)[:, None] * CMP_STRIDE + jnp.arange(CMP_BLOCK)[None, :]
    blocks = (t[:, :, idx, :] + pe).reshape(b, g, n_cmp, CMP_BLOCK * d)
    return jax.nn.silu(blocks @ w1) @ w2


def nsa_attention(q, kc, vc, ks, vs, kw, vw, gates):
    b, g, hg, s, d = q.shape
    scale = d ** -0.5
    n_cmp = kc.shape[2]
    n_blk = s // SLC_BLOCK
    n_sel = min(N_SELECT, n_blk)
    cmp_end = jnp.arange(n_cmp) * CMP_STRIDE + CMP_BLOCK - 1
    sel_idx = jnp.arange(n_blk)[:, None] * SLC_RATIO + jnp.arange(N_OVERLAP)[None, :]
    pad_right = max(0, (n_blk - 1) * SLC_RATIO + N_OVERLAP - n_cmp - CMP_PAD_LEFT)
    ks_blk = ks.reshape(b, g, n_blk, SLC_BLOCK, d)
    vs_blk = vs.reshape(b, g, n_blk, SLC_BLOCK, d)
    kw_pad = jnp.pad(kw, ((0, 0), (0, 0), (WINDOW, 0), (0, 0)))
    vw_pad = jnp.pad(vw, ((0, 0), (0, 0), (WINDOW, 0), (0, 0)))
    bi = jnp.arange(b)[:, None, None, None]
    gi = jnp.arange(g)[None, :, None, None]
    blk = jnp.arange(n_blk)

    def block(i):
        q0 = i * Q_BLOCK
        tq = q0 + jnp.arange(Q_BLOCK)
        qb = lax.dynamic_slice_in_dim(q, q0, Q_BLOCK, axis=3)
        gb = lax.dynamic_slice_in_dim(gates, q0, Q_BLOCK, axis=3)
        lc = jnp.einsum('bghqd,bgkd->bghqk', qb, kc, preferred_element_type=jnp.float32) * scale
        pc = masked_softmax(lc, cmp_end[None, :] <= tq[:, None])
        oc = jnp.einsum('bghqk,bgkd->bghqd', pc.astype(vc.dtype), vc)
        imp = jnp.pad(pc.sum(axis=2), ((0, 0), (0, 0), (0, 0), (CMP_PAD_LEFT, pad_right)))
        imp = imp[..., sel_idx].sum(axis=-1)
        cur = (tq // SLC_BLOCK)[:, None]
        forced = (blk[None, :] == 0) | (blk[None, :] == cur) | (blk[None, :] == cur - 1)
        imp = jnp.where(forced, jnp.inf, jnp.where(blk[None, :] > cur, -jnp.inf, imp))
        _, top = lax.top_k(imp, n_sel)
        k_sel = ks_blk[bi, gi, top].reshape(b, g, Q_BLOCK, n_sel * SLC_BLOCK, d)
        v_sel = vs_blk[bi, gi, top].reshape(b, g, Q_BLOCK, n_sel * SLC_BLOCK, d)
        sel_pos = (top[..., None] * SLC_BLOCK + jnp.arange(SLC_BLOCK)).reshape(b, g, Q_BLOCK, n_sel * SLC_BLOCK)
        ls = jnp.einsum('bghqd,bgqkd->bghqk', qb, k_sel, preferred_element_type=jnp.float32) * scale
        ps = masked_softmax(ls, (sel_pos <= tq[:, None])[:, :, None])
        o_sel = jnp.einsum('bghqk,bgqkd->bghqd', ps.astype(v_sel.dtype), v_sel)
        kwb = lax.dynamic_slice_in_dim(kw_pad, q0, WINDOW + Q_BLOCK, axis=2)
        vwb = lax.dynamic_slice_in_dim(vw_pad, q0, WINDOW + Q_BLOCK, axis=2)
        wpos = q0 - WINDOW + jnp.arange(WINDOW + Q_BLOCK)
        dist = tq[:, None] - wpos[None, :]
        wmask = (dist >= 0) & (dist < WINDOW) & (wpos[None, :] >= 0)
        lw = jnp.einsum('bghqd,bgkd->bghqk', qb, kwb, preferred_element_type=jnp.float32) * scale
        pw = masked_softmax(lw, wmask)
        ow = jnp.einsum('bghqk,bgkd->bghqd', pw.astype(vwb.dtype), vwb)
        return gb[..., 0:1] * oc + gb[..., 1:2] * o_sel + gb[..., 2:3] * ow

    o = lax.map(block, jnp.arange(s // Q_BLOCK))
    return o.transpose(1, 2, 3, 0, 4, 5).reshape(b, g * hg, s, d)


def hybrid_layer(x, c, g_pre, g_post, w_ada, b_ada, w_in, b_forget, w_cmp_k1, w_cmp_k2,
                 w_cmp_v1, w_cmp_v2, pe_cmp_k, pe_cmp_v, w_out, cos, sin):
    b, s, _ = x.shape
    shift, scale, gate = jnp.split(jax.nn.silu(c) @ w_ada + b_ada, 3, axis=-1)
    h = rms_norm(x, g_pre) * (1.0 + scale[:, None, :]) + shift[:, None, :]
    proj = h @ w_in
    offsets = [int(o) for o in np.cumsum(IN_SPLITS)[:-1]]
    (fq, fk, fv, ff, fz, nq, kc, vc, ks, vs, kw, vw, ng, nz) = jnp.split(proj, offsets, axis=-1)

    def heads(t, n):
        return t.reshape(b, s, n, HEAD_DIM).transpose(0, 2, 1, 3)

    log_f = jax.nn.log_sigmoid((ff + b_forget).astype(jnp.float32)).transpose(0, 2, 1)
    o_fox = fox_attention(heads(fq, FOX_HEADS), heads(fk, FOX_HEADS), heads(fv, FOX_HEADS), log_f)
    y_fox = o_fox.transpose(0, 2, 1, 3).reshape(b, s, FOX_WIDTH) * jax.nn.silu(fz)

    q_n = apply_rope(heads(nq, NSA_HEADS), cos, sin).reshape(b, NSA_KV_HEADS, NSA_GROUP, s, HEAD_DIM)
    kc_c = nsa_compress(apply_rope(heads(kc, NSA_KV_HEADS), cos, sin), w_cmp_k1, w_cmp_k2, pe_cmp_k)
    vc_c = nsa_compress(heads(vc, NSA_KV_HEADS), w_cmp_v1, w_cmp_v2, pe_cmp_v)
    gates = jax.nn.sigmoid(ng).reshape(b, s, NSA_HEADS, 3).transpose(0, 2, 1, 3)
    gates = gates.reshape(b, NSA_KV_HEADS, NSA_GROUP, s, 3)
    o_nsa = nsa_attention(q_n, kc_c, vc_c,
                          apply_rope(heads(ks, NSA_KV_HEADS), cos, sin), heads(vs, NSA_KV_HEADS),
                          apply_rope(heads(kw, NSA_KV_HEADS), cos, sin), heads(vw, NSA_KV_HEADS),
                          gates)
    y_nsa = o_nsa.transpose(0, 2, 1, 3).reshape(b, s, NSA_WIDTH) * jax.nn.silu(nz)

    y = jnp.concatenate([y_fox, y_nsa], axis=-1) @ w_out
    return x + gate[:, None, :] * rms_norm(y, g_post)


def setup_inputs(seed: int = 0) -> dict:
    key = jax.random.key(seed)
    ks = jax.random.split(key, 16)
    f32 = jnp.float32

    def nrm(k, shape, fan_in):
        return jax.random.normal(k, shape, f32) * (fan_in ** -0.5)

    flat = CMP_BLOCK * HEAD_DIM
    return {
        'x': jax.random.normal(ks[0], (BATCH, SEQ, D_MODEL), f32),
        'c': jax.random.normal(ks[1], (BATCH, D_MODEL), f32),
        'g_pre': 1.0 + 0.05 * jax.random.normal(ks[2], (DEPTH, D_MODEL), f32),
        'g_post': 1.0 + 0.05 * jax.random.normal(ks[3], (DEPTH, D_MODEL), f32),
        'w_ada': nrm(ks[4], (DEPTH, D_MODEL, 3 * D_MODEL), D_MODEL),
        'b_ada': 0.01 * jax.random.normal(ks[5], (DEPTH, 3 * D_MODEL), f32),
        'w_in': nrm(ks[6], (DEPTH, D_MODEL, IN_WIDTH), D_MODEL),
        'b_forget': jax.random.uniform(ks[7], (DEPTH, FOX_HEADS), f32, 3.0, 6.0),
        'w_cmp_k1': nrm(ks[8], (DEPTH, flat, CMP_HIDDEN), flat),
        'w_cmp_k2': nrm(ks[9], (DEPTH, CMP_HIDDEN, HEAD_DIM), CMP_HIDDEN),
        'w_cmp_v1': nrm(ks[10], (DEPTH, flat, CMP_HIDDEN), flat),
        'w_cmp_v2': nrm(ks[11], (DEPTH, CMP_HIDDEN, HEAD_DIM), CMP_HIDDEN),
        'pe_cmp_k': 0.1 * jax.random.normal(ks[12], (DEPTH, CMP_BLOCK, HEAD_DIM), f32),
        'pe_cmp_v': 0.1 * jax.random.normal(ks[13], (DEPTH, CMP_BLOCK, HEAD_DIM), f32),
        'w_out': nrm(ks[14], (DEPTH, MIX_WIDTH, D_MODEL), MIX_WIDTH),
    }


def reference(x, c, g_pre, g_post, w_ada, b_ada, w_in, b_forget, w_cmp_k1, w_cmp_k2,
              w_cmp_v1, w_cmp_v2, pe_cmp_k, pe_cmp_v, w_out):
    cos, sin = rope_tables(x.shape[1])
    for layer in range(DEPTH):
        x = hybrid_layer(x, c, g_pre[layer], g_post[layer], w_ada[layer], b_ada[layer],
                         w_in[layer], b_forget[layer], w_cmp_k1[layer], w_cmp_k2[layer],
                         w_cmp_v1[layer], w_cmp_v2[layer], pe_cmp_k[layer], pe_cmp_v[layer],
                         w_out[layer], cos, sin)
    return x
```

```python
import functools

import jax
import jax.numpy as jnp
from jax import lax
from jax.experimental import pallas as pl
from jax.experimental.pallas import tpu as pltpu

F32 = jnp.float32
BF16 = jnp.bfloat16

D_MODEL = 1024
HEAD_DIM = 64
FOX_WIDTH = 512
NSA_WIDTH = 512
FOX_HEADS = 8
NSA_HEADS = 8
NSA_KV_HEADS = 2
NSA_GROUP = 4
NSA_KV_WIDTH = 128
CMP_BLOCK = 32
CMP_STRIDE = 16
CMP_HIDDEN = 128
SLC_BLOCK = 64
N_SELECT = 16
WINDOW = 512
ROPE_THETA = 10000.0
RMS_EPS = 1e-6
QK_SCALE = HEAD_DIM ** -0.5

LANES = 128
V7X_VMEM_BYTES = 64 * 1024 * 1024
MASK_VALUE = -1e30

PROJ_ROWS = 512
FOX_TILE = 512
NSA_Q = 128
SEL_KV = 512
WIN_KEYS = WINDOW + NSA_Q

C_FQ, C_FK, C_FV, C_FZ, C_NQ = 0, 512, 1024, 1536, 2048
C_KC, C_VC, C_KS, C_VS, C_KW, C_VW = 2560, 2688, 2816, 2944, 3072, 3200
C_NZ, C_MISC, PROJ_COLS = 3328, 3840, 3968
MISC_GATE0 = FOX_HEADS


def _vmem_limit(pipelined_bytes, resident_bytes):
    need = 2 * pipelined_bytes + resident_bytes
    return int(min(max(need, 16 * 1024 * 1024), V7X_VMEM_BYTES - 8 * 1024 * 1024))


def _sigmoid(v):
    return 1.0 / (1.0 + jnp.exp(-v))


def _silu(v):
    return v * _sigmoid(v)


def _dot_nt(a, b):
    return lax.dot_general(a, b, (((1,), (1,)), ((), ())), preferred_element_type=F32)


def _split3(v):
    hi = v.astype(BF16)
    r1 = v - hi.astype(F32)
    mid = r1.astype(BF16)
    lo = (r1 - mid.astype(F32)).astype(BF16)
    return hi, mid, lo


def _ada_kernel(c_ref, w_ref, b_ref, o_ref):
    a = _silu(c_ref[...])
    o_ref[...] = jnp.dot(a, w_ref[...], precision=lax.Precision.HIGHEST,
                         preferred_element_type=F32) + b_ref[...]


def _ada(c8, w_ada, b_ada):
    n = w_ada.shape[1]
    blk = D_MODEL
    return pl.pallas_call(
        _ada_kernel,
        grid=(n // blk,),
        in_specs=[pl.BlockSpec((8, D_MODEL), lambda j: (0, 0)),
                  pl.BlockSpec((D_MODEL, blk), lambda j: (0, j)),
                  pl.BlockSpec((1, blk), lambda j: (0, j))],
        out_specs=pl.BlockSpec((8, blk), lambda j: (0, j)),
        out_shape=jax.ShapeDtypeStruct((8, n), F32),
        compiler_params=pltpu.CompilerParams(
            dimension_semantics=("arbitrary",),
            vmem_limit_bytes=_vmem_limit(D_MODEL * blk * 4, 4 * 1024 * 1024)),
        name="ada",
    )(c8, w_ada, b_ada.reshape(1, n))


def _rope128(t, cos, sin_signed):
    lane = lax.broadcasted_iota(jnp.int32, t.shape, 1)
    first_half = (lane & (HEAD_DIM - 1)) < HEAD_DIM // 2
    partner = jnp.where(first_half,
                        pltpu.roll(t, LANES - HEAD_DIM // 2, 1),
                        pltpu.roll(t, HEAD_DIM // 2, 1))
    return t * cos + partner * sin_signed


def _proj_kernel(x_ref, mod_ref, g_ref, w_ref, cos_ref, sin_ref, bf_ref,
                 fq_ref, fk_ref, fv_ref, fz_ref, nq_ref, kc_ref, vc_ref,
                 ks_ref, vs_ref, kw_ref, vw_ref, nz_ref, misc_ref, lf_ref):
    x = x_ref[0]
    y = x * lax.rsqrt(jnp.mean(x * x, axis=-1, keepdims=True) + RMS_EPS)
    y = y * g_ref[...]
    h = (y * (1.0 + mod_ref[0, 1:2, :]) + mod_ref[0, 0:1, :]).astype(BF16)
    cos = cos_ref[...]
    sin = sin_ref[...]

    def mm(lo, n):
        return jnp.dot(h, w_ref[:, lo:lo + n], preferred_element_type=F32)

    fq_ref[0] = (mm(C_FQ, FOX_WIDTH) * QK_SCALE).astype(BF16)
    fk_ref[0] = mm(C_FK, FOX_WIDTH).astype(BF16)
    fv_ref[0] = mm(C_FV, FOX_WIDTH).astype(BF16)
    fz_ref[0] = _silu(mm(C_FZ, FOX_WIDTH)).astype(BF16)
    for j in range(NSA_WIDTH // LANES):
        t = mm(C_NQ + j * LANES, LANES)
        nq_ref[0, :, j * LANES:(j + 1) * LANES] = (_rope128(t, cos, sin) * QK_SCALE).astype(BF16)
    kc_ref[0] = _rope128(mm(C_KC, LANES), cos, sin)
    vc_ref[0] = mm(C_VC, LANES)
    ks_ref[0] = _rope128(mm(C_KS, LANES), cos, sin).astype(BF16)
    vs_ref[0] = mm(C_VS, LANES).astype(BF16)
    kw_ref[0] = _rope128(mm(C_KW, LANES), cos, sin).astype(BF16)
    vw_ref[0] = mm(C_VW, LANES).astype(BF16)
    nz_ref[0] = _silu(mm(C_NZ, NSA_WIDTH)).astype(BF16)
    misc = mm(C_MISC, LANES)
    misc_ref[0] = misc
    z = misc + bf_ref[...]
    log_f = jnp.minimum(z, 0.0) - jnp.log1p(jnp.exp(-jnp.abs(z)))
    lf_ref[0] = log_f.T[0:FOX_HEADS, :]


def _proj(x, mod, g_pre, w_cat, cos128, sin128, bf128):
    b, s, d = x.shape
    tm = PROJ_ROWS
    row = lambda bi, i: (bi, i, 0)
    wide = lambda dt: jax.ShapeDtypeStruct((b, s, FOX_WIDTH), dt)
    slab = lambda dt: jax.ShapeDtypeStruct((b, s, LANES), dt)
    out_shape = (wide(BF16), wide(BF16), wide(BF16), wide(BF16), wide(BF16),
                 slab(F32), slab(F32), slab(BF16), slab(BF16), slab(BF16), slab(BF16),
                 wide(BF16), slab(F32), jax.ShapeDtypeStruct((b, FOX_HEADS, s), F32))
    wide_spec = pl.BlockSpec((1, tm, FOX_WIDTH), row)
    slab_spec = pl.BlockSpec((1, tm, LANES), row)
    out_specs = (wide_spec,) * 5 + (slab_spec,) * 6 + (wide_spec, slab_spec,
                 pl.BlockSpec((1, FOX_HEADS, tm), lambda bi, i: (bi, 0, i)))
    pipelined = tm * d * 4 + tm * (6 * FOX_WIDTH * 2 + 4 * LANES * 2 + 3 * LANES * 4) + d * PROJ_COLS * 2
    return pl.pallas_call(
        _proj_kernel,
        grid=(b, s // tm),
        in_specs=[pl.BlockSpec((1, tm, d), row),
                  pl.BlockSpec((1, 3, d), lambda bi, i: (bi, 0, 0)),
                  pl.BlockSpec((1, d), lambda bi, i: (0, 0)),
                  pl.BlockSpec((d, PROJ_COLS), lambda bi, i: (0, 0)),
                  pl.BlockSpec((tm, LANES), lambda bi, i: (i, 0)),
                  pl.BlockSpec((tm, LANES), lambda bi, i: (i, 0)),
                  pl.BlockSpec((1, LANES), lambda bi, i: (0, 0))],
        out_specs=out_specs,
        out_shape=out_shape,
        compiler_params=pltpu.CompilerParams(
            dimension_semantics=("arbitrary", "arbitrary"),
            vmem_limit_bytes=_vmem_limit(pipelined, 8 * 1024 * 1024)),
        name="proj",
    )(x, mod, g_pre, w_cat, cos128, sin128, bf128)


def _cumsum_kernel(x_ref, o_ref, *, chunks):
    x = x_ref[...]
    n = x.shape[0]
    parts = _split3(x)
    r = lax.broadcasted_iota(jnp.int32, (LANES, LANES), 0)
    c = lax.broadcasted_iota(jnp.int32, (LANES, LANES), 1)
    tri = (r <= c).astype(BF16)
    rr = lax.broadcasted_iota(jnp.int32, (n, n), 0)
    cc = lax.broadcasted_iota(jnp.int32, (n, n), 1)
    earlier = jnp.logical_and(cc < rr, (cc // chunks) == (rr // chunks)).astype(BF16)
    within = sum(jnp.dot(p, tri, preferred_element_type=F32) for p in parts)
    before = sum(jnp.dot(earlier, p, preferred_element_type=F32) for p in parts)
    o_ref[...] = within + jnp.sum(before, axis=-1, keepdims=True)


def _cumsum_lanes(v):
    rows, s = v.shape
    chunks = s // LANES
    n = rows * chunks
    out = pl.pallas_call(
        functools.partial(_cumsum_kernel, chunks=chunks),
        out_shape=jax.ShapeDtypeStruct((n, LANES), F32),
        compiler_params=pltpu.CompilerParams(
            vmem_limit_bytes=_vmem_limit(2 * n * LANES * 4, 6 * n * n)),
        name="cumsum",
    )(v.reshape(n, LANES))
    return out.reshape(rows, s)


def _online_softmax_update(s, v, m_ref, l_ref, acc_ref, rows):
    tk = s.shape[1]
    dv = acc_ref.shape[-1]
    m_prev = m_ref[rows, :]
    m_next = jnp.maximum(m_prev, jnp.max(s, axis=1, keepdims=True))
    alpha = jnp.exp(m_prev - m_next)
    p = jnp.exp(s - pltpu.repeat(m_next, tk // LANES, 1))
    l_ref[rows, :] = alpha * l_ref[rows, :] + jnp.sum(p, axis=1, keepdims=True)
    m_ref[rows, :] = m_next
    pv = jnp.dot(p.astype(BF16), v, preferred_element_type=F32)
    acc_ref[rows, :] = acc_ref[rows, :] * alpha[:, :dv] + pv


def _fox_kernel(q_ref, k_ref, v_ref, cum_ref, z_ref, o_ref, qm_ref, m_ref, l_ref, acc_ref):
    t = FOX_TILE
    qi = pl.program_id(2)
    q0 = pl.multiple_of(qi * t, t)
    q = q_ref[0]
    lane = lax.broadcasted_iota(jnp.int32, q.shape, 1)
    qm_ref[0] = jnp.where(lane < HEAD_DIM, q, jnp.zeros_like(q))
    qm_ref[1] = jnp.where(lane >= HEAD_DIM, q, jnp.zeros_like(q))
    m_ref[...] = jnp.full(m_ref.shape, MASK_VALUE, F32)
    l_ref[...] = jnp.zeros(l_ref.shape, F32)
    acc_ref[...] = jnp.zeros(acc_ref.shape, F32)
    f_ref = jnp.max(cum_ref[0, 0, :, pl.ds(q0, t)], axis=1, keepdims=True)

    def step(j, masked):
        k0 = pl.multiple_of(j * t, t)
        k_t = k_ref[0, pl.ds(k0, t), :]
        v_t = v_ref[0, pl.ds(k0, t), :]
        decay = f_ref - cum_ref[0, 0, :, pl.ds(k0, t)]
        for hh in range(2):
            s = _dot_nt(qm_ref[hh], k_t) + decay[hh:hh + 1, :]
            if masked:
                row = lax.broadcasted_iota(jnp.int32, s.shape, 0)
                col = lax.broadcasted_iota(jnp.int32, s.shape, 1)
                s = jnp.where(col <= row, s, MASK_VALUE)
            _online_softmax_update(s, v_t, m_ref.at[hh], l_ref.at[hh], acc_ref.at[hh],
                                   slice(None))

    def body(j, carry):
        step(j, False)
        return carry

    lax.fori_loop(0, qi, body, 0)
    step(qi, True)
    o0 = acc_ref[0] * (1.0 / l_ref[0])
    o1 = acc_ref[1] * (1.0 / l_ref[1])
    o = jnp.where(lane < HEAD_DIM, o0, o1)
    o_ref[0] = (o * z_ref[0].astype(F32)).astype(BF16)


def _fox(fq, fk, fv, cum, fz):
    b, s, w = fq.shape
    t = FOX_TILE
    pairs = w // LANES
    cum4 = cum.reshape(b, pairs, 2, s)
    tile = pl.BlockSpec((1, t, LANES), lambda bi, hp, i: (bi, i, hp))
    full = pl.BlockSpec((1, s, LANES), lambda bi, hp, i: (bi, 0, hp))
    resident = 2 * s * LANES * 2 + 8 * s * 4
    scratch = 2 * t * LANES * 2 + 6 * t * LANES * 4
    return pl.pallas_call(
        _fox_kernel,
        grid=(b, pairs, s // t),
        in_specs=[tile, full, full,
                  pl.BlockSpec((1, 1, 2, s), lambda bi, hp, i: (bi, hp, 0, 0)),
                  tile],
        out_specs=tile,
        out_shape=jax.ShapeDtypeStruct((b, s, w), BF16),
        scratch_shapes=[pltpu.VMEM((2, t, LANES), BF16),
                        pltpu.VMEM((2, t, LANES), F32),
                        pltpu.VMEM((2, t, LANES), F32),
                        pltpu.VMEM((2, t, LANES), F32)],
        compiler_params=pltpu.CompilerParams(
            dimension_semantics=("arbitrary", "arbitrary", "arbitrary"),
            vmem_limit_bytes=_vmem_limit(resident + 3 * t * LANES * 2, scratch + 8 * t * t * 4)),
        name="fox",
    )(fq, fk, fv, cum4, fz)


def _compress_kernel(t_ref, w1_ref, w2_ref, pe_ref, o_ref):
    hp = lax.Precision.HIGHEST
    half = CMP_STRIDE * HEAD_DIM
    tt = t_ref[0, 0, 0]
    n = tt.shape[0]
    first = jnp.dot(tt, w1_ref[0, 0:half, :], precision=hp, preferred_element_type=F32)
    second = jnp.dot(tt, w1_ref[0, half:2 * half, :], precision=hp, preferred_element_type=F32)
    pe_term = jnp.dot(pe_ref[0], w1_ref[0], precision=hp, preferred_element_type=F32)[0:1, :]
    hidden = first + pltpu.roll(second, n - 1, 0) + pe_term
    o_ref[0, 0, 0] = jnp.dot(_silu(hidden), w2_ref[0], precision=hp, preferred_element_type=F32)


def _compress(t, w1, w2, pe):
    kinds, b, g, n, flat = t.shape
    return pl.pallas_call(
        _compress_kernel,
        grid=(kinds, b, g),
        in_specs=[pl.BlockSpec((1, 1, 1, n, flat), lambda a, bi, gi: (a, bi, gi, 0, 0)),
                  pl.BlockSpec((1, 2 * flat, CMP_HIDDEN), lambda a, bi, gi: (a, 0, 0)),
                  pl.BlockSpec((1, CMP_HIDDEN, HEAD_DIM), lambda a, bi, gi: (a, 0, 0)),
                  pl.BlockSpec((1, 8, 2 * flat), lambda a, bi, gi: (a, 0, 0))],
        out_specs=pl.BlockSpec((1, 1, 1, n, HEAD_DIM), lambda a, bi, gi: (a, bi, gi, 0, 0)),
        out_shape=jax.ShapeDtypeStruct((kinds, b, g, n, HEAD_DIM), F32),
        compiler_params=pltpu.CompilerParams(
            dimension_semantics=("arbitrary", "arbitrary", "arbitrary"),
            vmem_limit_bytes=_vmem_limit(n * flat * 4 + 2 * flat * CMP_HIDDEN * 4, 8 * 1024 * 1024)),
        name="compress",
    )(t, w1, w2, pe)


def _masked_softmax(s, valid):
    s = jnp.where(valid, s, -jnp.inf)
    m = jnp.max(s, axis=1, keepdims=True)
    m = jnp.where(m == -jnp.inf, 0.0, m)
    p = jnp.exp(s - m)
    inv = 1.0 / jnp.maximum(jnp.sum(p, axis=1, keepdims=True), 1e-30)
    return p * inv


def _stack_heads(q):
    return jnp.concatenate([q[:, h * HEAD_DIM:(h + 1) * HEAD_DIM] for h in range(NSA_GROUP)], axis=0)


def _group_gate(gates, gi, h, j):
    c0 = MISC_GATE0 + 3 * h + j
    c1 = MISC_GATE0 + 3 * (NSA_GROUP + h) + j
    return jnp.where(gi == 0, gates[:, c0:c0 + 1], gates[:, c1:c1 + 1])


def _cmpwin_kernel(q_ref, kc_ref, vc_ref, kw_ref, vw_ref, misc_ref, part_ref, selb_ref):
    tq = NSA_Q
    gi = pl.program_id(1)
    qi = pl.program_id(2)
    q0 = pl.multiple_of(qi * tq, tq)
    q4 = _stack_heads(q_ref[0])
    rows = NSA_GROUP * tq
    n_cmp = kc_ref.shape[2]

    kc = kc_ref[0, 0]
    kc_hi = kc.astype(BF16)
    kc_lo = (kc - kc_hi.astype(F32)).astype(BF16)
    s = _dot_nt(q4, kc_hi) + _dot_nt(q4, kc_lo)
    col = lax.broadcasted_iota(jnp.int32, (rows, n_cmp), 1)
    t_row = q0 + (lax.broadcasted_iota(jnp.int32, (rows, n_cmp), 0) & (tq - 1))
    pc = _masked_softmax(s, col * CMP_STRIDE + (CMP_BLOCK - 1) <= t_row)
    oc = jnp.dot(pc.astype(BF16), vc_ref[0, 0].astype(BF16), preferred_element_type=F32)

    pc_sum = pc[0:tq] + pc[tq:2 * tq] + pc[2 * tq:3 * tq] + pc[3 * tq:4 * tq]
    n_blk = selb_ref.shape[3]
    ratio = SLC_BLOCK // CMP_STRIDE
    jj = lax.broadcasted_iota(jnp.int32, (n_blk, n_cmp), 0)
    ii = lax.broadcasted_iota(jnp.int32, (n_blk, n_cmp), 1)
    lo_i = jj * ratio - (CMP_BLOCK // CMP_STRIDE - 1)
    overlap = jnp.logical_and(ii >= lo_i, ii < lo_i + (SLC_BLOCK + CMP_BLOCK) // CMP_STRIDE - 1)
    overlap = jnp.logical_and(overlap, ii < n_cmp - 1).astype(BF16)
    imp = sum(_dot_nt(overlap, p) for p in _split3(pc_sum))
    blk = lax.broadcasted_iota(jnp.int32, (n_blk, tq), 0)
    cur = (q0 + lax.broadcasted_iota(jnp.int32, (n_blk, tq), 1)) // SLC_BLOCK
    forced = jnp.logical_or(blk == 0, jnp.logical_or(blk == cur, blk == cur - 1))
    imp = jnp.where(forced, jnp.inf, jnp.where(blk > cur, -jnp.inf, imp))

    def pick_one(_, carry):
        rem, sel = carry
        best = jnp.max(rem, axis=0, keepdims=True)
        first = jnp.min(jnp.where(rem == best, blk, n_blk), axis=0, keepdims=True)
        hit = blk == first
        return jnp.where(hit, -jnp.inf, rem), jnp.where(hit, 1.0, sel)

    _, sel = lax.fori_loop(0, min(N_SELECT, n_blk), pick_one, (imp, jnp.zeros_like(imp)))
    selb_ref[0, 0] = jnp.where(sel.T > 0.5, 0.0, MASK_VALUE).astype(BF16)

    start = pl.multiple_of(jnp.maximum(q0 - WINDOW, 0), tq)
    kw = kw_ref[0, 0, pl.ds(start, WIN_KEYS), :]
    vw = vw_ref[0, 0, pl.ds(start, WIN_KEYS), :]
    sw = _dot_nt(q4, kw)
    pos = start + lax.broadcasted_iota(jnp.int32, sw.shape, 1)
    dist = q0 + (lax.broadcasted_iota(jnp.int32, sw.shape, 0) & (tq - 1)) - pos
    pw = _masked_softmax(sw, jnp.logical_and(dist >= 0, dist < WINDOW))
    ow = jnp.dot(pw.astype(BF16), vw, preferred_element_type=F32)

    gates = _sigmoid(misc_ref[0])
    heads = []
    for h in range(NSA_GROUP):
        r = slice(h * tq, (h + 1) * tq)
        heads.append(_group_gate(gates, gi, h, 0) * oc[r] + _group_gate(gates, gi, h, 2) * ow[r])
    part_ref[0] = jnp.concatenate(heads, axis=1)


def _cmpwin(nq, kc_c, vc_c, kw, vw, misc):
    b, s, w = nq.shape
    g = NSA_KV_HEADS
    tq = NSA_Q
    gw = NSA_GROUP * HEAD_DIM
    n_cmp = kc_c.shape[2]
    n_blk = s // SLC_BLOCK
    q_spec = pl.BlockSpec((1, tq, gw), lambda bi, gi, i: (bi, i, gi))
    cmp_spec = pl.BlockSpec((1, 1, n_cmp, HEAD_DIM), lambda bi, gi, i: (bi, gi, 0, 0))
    seq_spec = pl.BlockSpec((1, 1, s, HEAD_DIM), lambda bi, gi, i: (bi, gi, 0, 0))
    resident = 2 * s * LANES * 2 + 2 * n_cmp * LANES * 4
    rows = NSA_GROUP * tq
    return pl.pallas_call(
        _cmpwin_kernel,
        grid=(b, g, s // tq),
        in_specs=[q_spec, cmp_spec, cmp_spec, seq_spec, seq_spec,
                  pl.BlockSpec((1, tq, LANES), lambda bi, gi, i: (bi, i, 0))],
        out_specs=(q_spec, pl.BlockSpec((1, 1, tq, n_blk), lambda bi, gi, i: (bi, gi, i, 0))),
        out_shape=(jax.ShapeDtypeStruct((b, s, w), F32),
                   jax.ShapeDtypeStruct((b, g, s, n_blk), BF16)),
        compiler_params=pltpu.CompilerParams(
            dimension_semantics=("arbitrary", "arbitrary", "arbitrary"),
            vmem_limit_bytes=_vmem_limit(resident + tq * gw * 8, 12 * rows * WIN_KEYS * 4)),
        name="cmpwin",
    )(nq, kc_c, vc_c, kw, vw, misc)


def _sel_kernel(q_ref, k_ref, v_ref, selb_ref, part_ref, misc_ref, z_ref, o_ref,
                q4_ref, m_ref, l_ref, acc_ref):
    tq, tk = NSA_Q, SEL_KV
    gi = pl.program_id(1)
    qi = pl.program_id(2)
    q0 = pl.multiple_of(qi * tq, tq)
    q4_ref[...] = _stack_heads(q_ref[0])
    m_ref[...] = jnp.full(m_ref.shape, MASK_VALUE, F32)
    l_ref[...] = jnp.zeros(l_ref.shape, F32)
    acc_ref[...] = jnp.zeros(acc_ref.shape, F32)
    n_blk = selb_ref.shape[3]

    def step(j, masked):
        k0 = pl.multiple_of(j * tk, tk)
        k_t = k_ref[0, 0, pl.ds(k0, tk), :]
        v_t = v_ref[0, 0, pl.ds(k0, tk), :]
        s = _dot_nt(q4_ref[...], k_t)
        blk = lax.broadcasted_iota(jnp.int32, (n_blk, tk), 0)
        key = k0 + lax.broadcasted_iota(jnp.int32, (n_blk, tk), 1)
        expand = (blk == key // SLC_BLOCK).astype(BF16)
        bias = jnp.dot(selb_ref[0, 0], expand, preferred_element_type=F32)
        if masked:
            t_row = q0 + lax.broadcasted_iota(jnp.int32, (tq, tk), 0)
            pos = k0 + lax.broadcasted_iota(jnp.int32, (tq, tk), 1)
            bias = jnp.where(pos <= t_row, bias, MASK_VALUE)
        for h in range(NSA_GROUP):
            r = slice(h * tq, (h + 1) * tq)
            _online_softmax_update(s[r] + bias, v_t, m_ref, l_ref, acc_ref, r)

    def body(j, carry):
        step(j, False)
        return carry

    diag = q0 // tk
    lax.fori_loop(0, diag, body, 0)
    step(diag, True)

    gates = _sigmoid(misc_ref[0])
    part = part_ref[0]
    heads = []
    for h in range(NSA_GROUP):
        r = slice(h * tq, (h + 1) * tq)
        o_sel = acc_ref[r, :] * (1.0 / l_ref[r, 0:HEAD_DIM])
        heads.append(part[:, h * HEAD_DIM:(h + 1) * HEAD_DIM] + _group_gate(gates, gi, h, 1) * o_sel)
    o_ref[0] = (jnp.concatenate(heads, axis=1) * z_ref[0].astype(F32)).astype(BF16)


def _sel(nq, ks, vs, selb, part, misc, nz):
    b, s, w = nq.shape
    g = NSA_KV_HEADS
    tq = NSA_Q
    gw = NSA_GROUP * HEAD_DIM
    n_blk = s // SLC_BLOCK
    rows = NSA_GROUP * tq
    q_spec = pl.BlockSpec((1, tq, gw), lambda bi, gi, i: (bi, i, gi))
    seq_spec = pl.BlockSpec((1, 1, s, HEAD_DIM), lambda bi, gi, i: (bi, gi, 0, 0))
    resident = 2 * s * LANES * 2
    return pl.pallas_call(
        _sel_kernel,
        grid=(b, g, s // tq),
        in_specs=[q_spec, seq_spec, seq_spec,
                  pl.BlockSpec((1, 1, tq, n_blk), lambda bi, gi, i: (bi, gi, i, 0)),
                  q_spec,
                  pl.BlockSpec((1, tq, LANES), lambda bi, gi, i: (bi, i, 0)),
                  q_spec],
        out_specs=q_spec,
        out_shape=jax.ShapeDtypeStruct((b, s, w), BF16),
        scratch_shapes=[pltpu.VMEM((rows, HEAD_DIM), BF16),
                        pltpu.VMEM((rows, LANES), F32),
                        pltpu.VMEM((rows, LANES), F32),
                        pltpu.VMEM((rows, HEAD_DIM), F32)],
        compiler_params=pltpu.CompilerParams(
            dimension_semantics=("arbitrary", "arbitrary", "arbitrary"),
            vmem_limit_bytes=_vmem_limit(resident + tq * gw * 10, 8 * rows * SEL_KV * 4)),
        name="sel",
    )(nq, ks, vs, selb, part, misc, nz)


def _out_kernel(yf_ref, yn_ref, w_ref, g_ref, mod_ref, x_ref, o_ref):
    y = (jnp.dot(yf_ref[0], w_ref[0:FOX_WIDTH, :], preferred_element_type=F32)
         + jnp.dot(yn_ref[0], w_ref[FOX_WIDTH:, :], preferred_element_type=F32))
    yn = y * lax.rsqrt(jnp.mean(y * y, axis=-1, keepdims=True) + RMS_EPS)
    o_ref[0] = x_ref[0] + mod_ref[0, 2:3, :] * (yn * g_ref[...])


def _out(y_fox, y_nsa, w_out, g_post, mod, x):
    b, s, d = x.shape
    tm = PROJ_ROWS
    row = lambda bi, i: (bi, i, 0)
    half = pl.BlockSpec((1, tm, FOX_WIDTH), row)
    pipelined = 2 * tm * FOX_WIDTH * 2 + 2 * tm * d * 4 + d * d * 2
    return pl.pallas_call(
        _out_kernel,
        grid=(b, s // tm),
        in_specs=[half, half,
                  pl.BlockSpec((d, d), lambda bi, i: (0, 0)),
                  pl.BlockSpec((1, d), lambda bi, i: (0, 0)),
                  pl.BlockSpec((1, 3, d), lambda bi, i: (bi, 0, 0)),
                  pl.BlockSpec((1, tm, d), row)],
        out_specs=pl.BlockSpec((1, tm, d), row),
        out_shape=jax.ShapeDtypeStruct((b, s, d), F32),
        compiler_params=pltpu.CompilerParams(
            dimension_semantics=("arbitrary", "arbitrary"),
            vmem_limit_bytes=_vmem_limit(pipelined, 6 * tm * d * 4)),
        name="out",
    )(y_fox, y_nsa, w_out, g_post, mod, x)


def _rope_slabs(seq_len):
    inv = 1.0 / (ROPE_THETA ** (jnp.arange(0, HEAD_DIM, 2, dtype=F32) / HEAD_DIM))
    ang = jnp.arange(seq_len, dtype=F32)[:, None] * inv[None, :]
    cos, sin = jnp.cos(ang), jnp.sin(ang)
    reps = LANES // (HEAD_DIM // 2)
    sign = jnp.tile(jnp.concatenate([-jnp.ones((HEAD_DIM // 2,), F32), jnp.ones((HEAD_DIM // 2,), F32)]),
                    LANES // HEAD_DIM)
    return jnp.tile(cos, (1, reps)), jnp.tile(sin, (1, reps)) * sign[None, :]


def _reorder_w_in(w_in):
    fw, kv = FOX_WIDTH, NSA_KV_WIDTH
    o = 0
    cols = {}
    for name, n in (("fq", fw), ("fk", fw), ("fv", fw), ("ff", FOX_HEADS), ("fz", fw), ("nq", NSA_WIDTH),
                    ("kc", kv), ("vc", kv), ("ks", kv), ("vs", kv), ("kw", kv), ("vw", kv),
                    ("ng", 3 * NSA_HEADS), ("nz", NSA_WIDTH)):
        cols[name] = w_in[:, o:o + n]
        o += n
    pad = jnp.zeros((w_in.shape[0], LANES - FOX_HEADS - 3 * NSA_HEADS), w_in.dtype)
    order = ("fq", "fk", "fv", "fz", "nq", "kc", "vc", "ks", "vs", "kw", "vw", "nz", "ff", "ng")
    return jnp.concatenate([cols[k] for k in order] + [pad], axis=1).astype(BF16)


def _group_major(t):
    b, s, _ = t.shape
    return t.reshape(b, s, NSA_KV_HEADS, HEAD_DIM).transpose(0, 2, 1, 3)


def _layer(x, c8, g_pre, g_post, w_ada, b_ada, w_in, b_forget, w_cmp_k1, w_cmp_k2,
           w_cmp_v1, w_cmp_v2, pe_cmp_k, pe_cmp_v, w_out, cos128, sin128):
    b, s, d = x.shape
    mod = _ada(c8, w_ada, b_ada)[:b].reshape(b, 3, d)
    bf128 = jnp.pad(b_forget, (0, LANES - FOX_HEADS)).reshape(1, LANES)
    (fq, fk, fv, fz, nq, kc, vc, ks, vs, kw, vw, nz, misc, log_f) = _proj(
        x, mod, g_pre.reshape(1, d), _reorder_w_in(w_in), cos128, sin128, bf128)

    cum = _cumsum_lanes(log_f.reshape(b * FOX_HEADS, s)).reshape(b, FOX_HEADS, s)
    y_fox = _fox(fq, fk, fv, cum, fz)

    flat = CMP_STRIDE * HEAD_DIM
    t = jnp.stack([_group_major(kc), _group_major(vc)]).reshape(2, b, NSA_KV_HEADS, s // CMP_STRIDE, flat)
    pe = jnp.stack([pe_cmp_k.reshape(1, 2 * flat), pe_cmp_v.reshape(1, 2 * flat)])
    cmp = _compress(t, jnp.stack([w_cmp_k1, w_cmp_v1]), jnp.stack([w_cmp_k2, w_cmp_v2]),
                    jnp.broadcast_to(pe, (2, 8, 2 * flat)))
    part, selb = _cmpwin(nq, cmp[0], cmp[1], _group_major(kw), _group_major(vw), misc)
    y_nsa = _sel(nq, _group_major(ks), _group_major(vs), selb, part, misc, nz)

    return _out(y_fox, y_nsa, w_out.astype(BF16), g_post.reshape(1, d), mod, x)


def kernel(x, c, g_pre, g_post, w_ada, b_ada, w_in, b_forget, w_cmp_k1, w_cmp_k2,
           w_cmp_v1, w_cmp_v2, pe_cmp_k, pe_cmp_v, w_out):
    cos128, sin128 = _rope_slabs(x.shape[1])
    c8 = jnp.pad(c, ((0, 8 - c.shape[0]), (0, 0)))
    for layer in range(g_pre.shape[0]):
        x = _layer(x, c8, g_pre[layer], g_post[layer], w_ada[layer], b_ada[layer], w_in[layer],
                   b_forget[layer], w_cmp_k1[layer], w_cmp_k2[layer], w_cmp_v1[layer],
                   w_cmp_v2[layer], pe_cmp_k[layer], pe_cmp_v[layer], w_out[layer], cos128, sin128)
    return x
```

```python
import functools

import jax
import jax.numpy as jnp
from jax import lax
from jax.experimental import pallas as pl
from jax.experimental.pallas import tpu as pltpu

F32 = jnp.float32
BF16 = jnp.bfloat16

D_MODEL = 1024
HEAD_DIM = 64
FOX_WIDTH = 512
NSA_WIDTH = 512
FOX_HEADS = 8
NSA_HEADS = 8
NSA_KV_HEADS = 2
NSA_GROUP = 4
NSA_KV_WIDTH = 128
CMP_BLOCK = 32
CMP_STRIDE = 16
CMP_HIDDEN = 128
SLC_BLOCK = 64
N_SELECT = 16
WINDOW = 512
ROPE_THETA = 10000.0
RMS_EPS = 1e-6
LOG2E = 1.4426950408889634
QK_SCALE = HEAD_DIM ** -0.5 * LOG2E

LANES = 128
SUBLANES = 8
V7X_VMEM_BYTES = 64 * 1024 * 1024
MASK_VALUE = -1e30

PROJ_ROWS = 512
FOX_TILE = 512
NSA_Q = 128
SEL_KV = 512
WIN_KEYS = WINDOW + NSA_Q
FOX_AUG = 6

C_FQ, C_FK, C_FV, C_FZ, C_NQ = 0, 512, 1024, 1536, 2048
C_KC, C_VC, C_KS, C_VS, C_KW, C_VW = 2560, 2688, 2816, 2944, 3072, 3200
C_NZ, C_MISC, PROJ_COLS = 3328, 3840, 3968
MISC_GATE0 = FOX_HEADS


def _vmem_limit(pipelined_bytes, resident_bytes):
    need = 2 * pipelined_bytes + resident_bytes
    return int(min(max(need, 16 * 1024 * 1024), V7X_VMEM_BYTES - 8 * 1024 * 1024))


def _sigmoid(v):
    return 1.0 / (1.0 + jnp.exp(-v))


def _silu(v):
    return v * _sigmoid(v)


def _dot_nt(a, b):
    return lax.dot_general(a, b, (((1,), (1,)), ((), ())), preferred_element_type=F32)


def _split3(v):
    hi = v.astype(BF16)
    r1 = v - hi.astype(F32)
    mid = r1.astype(BF16)
    lo = (r1 - mid.astype(F32)).astype(BF16)
    return hi, mid, lo


def _ada_kernel(c_ref, w_ref, b_ref, o_ref):
    a = _silu(c_ref[...])
    o_ref[...] = jnp.dot(a, w_ref[...], precision=lax.Precision.HIGHEST,
                         preferred_element_type=F32) + b_ref[...]


def _ada(c8, w_ada, b_ada):
    n = w_ada.shape[1]
    blk = D_MODEL
    return pl.pallas_call(
        _ada_kernel,
        grid=(n // blk,),
        in_specs=[pl.BlockSpec((SUBLANES, D_MODEL), lambda j: (0, 0)),
                  pl.BlockSpec((D_MODEL, blk), lambda j: (0, j)),
                  pl.BlockSpec((1, blk), lambda j: (0, j))],
        out_specs=pl.BlockSpec((SUBLANES, blk), lambda j: (0, j)),
        out_shape=jax.ShapeDtypeStruct((SUBLANES, n), F32),
        compiler_params=pltpu.CompilerParams(
            dimension_semantics=("arbitrary",),
            vmem_limit_bytes=_vmem_limit(D_MODEL * blk * 4, 4 * 1024 * 1024)),
        name="ada",
    )(c8, w_ada, b_ada.reshape(1, n))


def _rope128(t, cos, sin_signed):
    lane = lax.broadcasted_iota(jnp.int32, t.shape, 1)
    first_half = (lane & (HEAD_DIM - 1)) < HEAD_DIM // 2
    partner = jnp.where(first_half,
                        pltpu.roll(t, LANES - HEAD_DIM // 2, 1),
                        pltpu.roll(t, HEAD_DIM // 2, 1))
    return t * cos + partner * sin_signed


def _proj_kernel(x_ref, mod_ref, g_ref, w_ref, cos_ref, sin_ref, bf_ref,
                 fq_ref, fk_ref, fv_ref, fz_ref, nq_ref, kc_ref, vc_ref,
                 ks_ref, vs_ref, kw_ref, vw_ref, nz_ref, misc_ref, lf_ref):
    x = x_ref[0]
    y = x * lax.rsqrt(jnp.mean(x * x, axis=-1, keepdims=True) + RMS_EPS)
    y = y * g_ref[...]
    h = (y * (1.0 + mod_ref[0, 1:2, :]) + mod_ref[0, 0:1, :]).astype(BF16)
    cos = cos_ref[...]
    sin = sin_ref[...]

    def mm(lo, n):
        return jnp.dot(h, w_ref[:, lo:lo + n], preferred_element_type=F32)

    fq_ref[0] = (mm(C_FQ, FOX_WIDTH) * QK_SCALE).astype(BF16)
    fk_ref[0] = mm(C_FK, FOX_WIDTH).astype(BF16)
    fv_ref[0] = mm(C_FV, FOX_WIDTH).astype(BF16)
    fz_ref[0] = _silu(mm(C_FZ, FOX_WIDTH)).astype(BF16)
    for j in range(NSA_WIDTH // LANES):
        t = mm(C_NQ + j * LANES, LANES)
        nq_ref[0, :, j * LANES:(j + 1) * LANES] = (_rope128(t, cos, sin) * QK_SCALE).astype(BF16)
    kc_ref[0] = _rope128(mm(C_KC, LANES), cos, sin)
    vc_ref[0] = mm(C_VC, LANES)
    ks_ref[0] = _rope128(mm(C_KS, LANES), cos, sin).astype(BF16)
    vs_ref[0] = mm(C_VS, LANES).astype(BF16)
    kw_ref[0] = _rope128(mm(C_KW, LANES), cos, sin).astype(BF16)
    vw_ref[0] = mm(C_VW, LANES).astype(BF16)
    nz_ref[0] = _silu(mm(C_NZ, NSA_WIDTH)).astype(BF16)
    misc = mm(C_MISC, LANES)
    misc_ref[0] = misc
    z = misc + bf_ref[...]
    log_f = jnp.minimum(z, 0.0) - jnp.log1p(jnp.exp(-jnp.abs(z)))
    lf_ref[0] = log_f.T[0:FOX_HEADS, :]


def _proj(x, mod, g_pre, w_cat, cos128, sin128, bf128):
    b, s, d = x.shape
    tm = PROJ_ROWS
    row = lambda bi, i: (bi, i, 0)
    wide = lambda dt: jax.ShapeDtypeStruct((b, s, FOX_WIDTH), dt)
    slab = lambda dt: jax.ShapeDtypeStruct((b, s, LANES), dt)
    out_shape = (wide(BF16), wide(BF16), wide(BF16), wide(BF16), wide(BF16),
                 slab(F32), slab(F32), slab(BF16), slab(BF16), slab(BF16), slab(BF16),
                 wide(BF16), slab(F32), jax.ShapeDtypeStruct((b, FOX_HEADS, s), F32))
    wide_spec = pl.BlockSpec((1, tm, FOX_WIDTH), row)
    slab_spec = pl.BlockSpec((1, tm, LANES), row)
    out_specs = (wide_spec,) * 5 + (slab_spec,) * 6 + (wide_spec, slab_spec,
                 pl.BlockSpec((1, FOX_HEADS, tm), lambda bi, i: (bi, 0, i)))
    pipelined = tm * d * 4 + tm * (6 * FOX_WIDTH * 2 + 4 * LANES * 2 + 3 * LANES * 4) + d * PROJ_COLS * 2
    return pl.pallas_call(
        _proj_kernel,
        grid=(b, s // tm),
        in_specs=[pl.BlockSpec((1, tm, d), row),
                  pl.BlockSpec((1, 3, d), lambda bi, i: (bi, 0, 0)),
                  pl.BlockSpec((1, d), lambda bi, i: (0, 0)),
                  pl.BlockSpec((d, PROJ_COLS), lambda bi, i: (0, 0)),
                  pl.BlockSpec((tm, LANES), lambda bi, i: (i, 0)),
                  pl.BlockSpec((tm, LANES), lambda bi, i: (i, 0)),
                  pl.BlockSpec((1, LANES), lambda bi, i: (0, 0))],
        out_specs=out_specs,
        out_shape=out_shape,
        compiler_params=pltpu.CompilerParams(
            dimension_semantics=("arbitrary", "arbitrary"),
            vmem_limit_bytes=_vmem_limit(pipelined, 8 * 1024 * 1024)),
        name="proj",
    )(x, mod, g_pre, w_cat, cos128, sin128, bf128)


def _cumsum_kernel(x_ref, hi_ref, mid_ref, lo_ref, *, chunks):
    x = x_ref[...]
    n = x.shape[0]
    parts = _split3(x)
    r = lax.broadcasted_iota(jnp.int32, (LANES, LANES), 0)
    c = lax.broadcasted_iota(jnp.int32, (LANES, LANES), 1)
    tri = (r <= c).astype(BF16)
    rr = lax.broadcasted_iota(jnp.int32, (n, n), 0)
    cc = lax.broadcasted_iota(jnp.int32, (n, n), 1)
    earlier = jnp.logical_and(cc < rr, (cc // chunks) == (rr // chunks)).astype(BF16)
    within = sum(jnp.dot(p, tri, preferred_element_type=F32) for p in parts)
    before = sum(jnp.dot(earlier, p, preferred_element_type=F32) for p in parts)
    total = (within + jnp.sum(before, axis=-1, keepdims=True)) * LOG2E
    hi_ref[...], mid_ref[...], lo_ref[...] = _split3(total)


def _cumsum_lanes_split(v):
    rows, s = v.shape
    chunks = s // LANES
    n = rows * chunks
    part = jax.ShapeDtypeStruct((n, LANES), BF16)
    parts = pl.pallas_call(
        functools.partial(_cumsum_kernel, chunks=chunks),
        out_shape=(part, part, part),
        compiler_params=pltpu.CompilerParams(
            vmem_limit_bytes=_vmem_limit(2 * n * LANES * 4, 6 * n * n)),
        name="cumsum",
    )(v.reshape(n, LANES))
    return tuple(p.reshape(rows, s) for p in parts)


def _flash_update(s, v, m_ref, acc_ref):
    w = s.shape[1]
    reps = acc_ref.shape[-1] // LANES
    m_prev = m_ref[...]
    m_next = jnp.maximum(m_prev, jnp.max(s, axis=1, keepdims=True))
    alpha = jnp.exp2(m_prev - m_next)
    p = jnp.exp2(s - pltpu.repeat(m_next, w // LANES, 1))
    pv = jnp.dot(p.astype(BF16), v, preferred_element_type=F32)
    acc_ref[...] = acc_ref[...] * (alpha if reps == 1 else pltpu.repeat(alpha, reps, 1)) + pv
    m_ref[...] = m_next


def _causal_kv_schedule(n_full, tile, step):
    def pair(j, carry):
        step(pl.multiple_of(j * 2 * tile, 2 * tile), 2 * tile, False)
        return carry

    lax.fori_loop(0, n_full // 2, pair, 0)

    @pl.when(n_full % 2 == 1)
    def _():
        step(pl.multiple_of((n_full - 1) * tile, tile), tile, False)

    step(pl.multiple_of(n_full * tile, tile), tile, True)


def _fox_aug(cum_split):
    hi, mid, lo = (jnp.transpose(p, (0, 2, 1)) for p in cum_split)
    b, s, h = hi.shape
    one = jnp.ones_like(hi)
    pad = jnp.zeros((b, s, h, HEAD_DIM - FOX_AUG), BF16)
    qa = jnp.concatenate([jnp.stack([hi, mid, lo, one, one, one], axis=-1), pad], axis=-1)
    ka = jnp.concatenate([jnp.stack([one, one, one, -hi, -mid, -lo], axis=-1), pad], axis=-1)
    return qa.reshape(b, s, h * HEAD_DIM), ka.reshape(b, s, h * HEAD_DIM)


def _fox_kernel(q_ref, qa_ref, k_ref, ka_ref, v_ref, z_ref, o_ref, qm_ref, ones_ref, m_ref, acc_ref):
    t = FOX_TILE
    qi = pl.program_id(2)
    q = q_ref[0]
    qa = qa_ref[0]
    lane = lax.broadcasted_iota(jnp.int32, q.shape, 1)
    for hh in range(2):
        own = (lane < HEAD_DIM) if hh == 0 else (lane >= HEAD_DIM)
        qm_ref[hh, :, 0:LANES] = jnp.where(own, q, jnp.zeros_like(q))
        qm_ref[hh, :, LANES:2 * LANES] = jnp.where(own, qa, jnp.zeros_like(qa))
    ones_lane = lax.broadcasted_iota(jnp.int32, ones_ref.shape, 1)
    ones_ref[...] = jnp.where(ones_lane == 0, 1.0, 0.0).astype(BF16)
    m_ref[...] = jnp.full(m_ref.shape, MASK_VALUE, F32)
    acc_ref[...] = jnp.zeros(acc_ref.shape, F32)

    def step(k0, w, masked):
        k_t = jnp.concatenate([k_ref[0, pl.ds(k0, w), :], ka_ref[0, pl.ds(k0, w), :]], axis=1)
        v_t = jnp.concatenate([v_ref[0, pl.ds(k0, w), :], ones_ref[0:w, :]], axis=1)
        for hh in range(2):
            s = _dot_nt(qm_ref[hh], k_t)
            if masked:
                row = lax.broadcasted_iota(jnp.int32, s.shape, 0)
                col = lax.broadcasted_iota(jnp.int32, s.shape, 1)
                s = jnp.where(col <= row, s, MASK_VALUE)
            _flash_update(s, v_t, m_ref.at[hh], acc_ref.at[hh])

    _causal_kv_schedule(qi, t, step)
    o0 = acc_ref[0, :, 0:LANES] * (1.0 / acc_ref[0, :, LANES:LANES + 1])
    o1 = acc_ref[1, :, 0:LANES] * (1.0 / acc_ref[1, :, LANES:LANES + 1])
    o = jnp.where(lane < HEAD_DIM, o0, o1)
    o_ref[0] = (o * z_ref[0].astype(F32)).astype(BF16)


def _fox(fq, fk, fv, cum_split, fz):
    b, s, w = fq.shape
    t = FOX_TILE
    pairs = w // LANES
    qa, ka = _fox_aug(cum_split)
    tile = pl.BlockSpec((1, t, LANES), lambda bi, hp, i: (bi, i, hp))
    full = pl.BlockSpec((1, s, LANES), lambda bi, hp, i: (bi, 0, hp))
    resident = 3 * s * LANES * 2
    scratch = 2 * t * 2 * LANES * 2 + 2 * t * LANES * 2 + 2 * t * 3 * LANES * 4
    return pl.pallas_call(
        _fox_kernel,
        grid=(b, pairs, s // t),
        in_specs=[tile, tile, full, full, full, tile],
        out_specs=tile,
        out_shape=jax.ShapeDtypeStruct((b, s, w), BF16),
        scratch_shapes=[pltpu.VMEM((2, t, 2 * LANES), BF16),
                        pltpu.VMEM((2 * t, LANES), BF16),
                        pltpu.VMEM((2, t, LANES), F32),
                        pltpu.VMEM((2, t, 2 * LANES), F32)],
        compiler_params=pltpu.CompilerParams(
            dimension_semantics=("arbitrary", "arbitrary", "arbitrary"),
            vmem_limit_bytes=_vmem_limit(resident + 4 * t * LANES * 2, scratch + 8 * t * 2 * t * 4)),
        name="fox",
    )(fq, qa, fk, ka, fv, fz)


def _compress_kernel(t_ref, w1_ref, w2_ref, pe_ref, o_ref):
    hp = lax.Precision.HIGHEST
    half = CMP_STRIDE * HEAD_DIM
    tt = t_ref[0, 0, 0]
    n = tt.shape[0]
    first = jnp.dot(tt, w1_ref[0, 0:half, :], precision=hp, preferred_element_type=F32)
    second = jnp.dot(tt, w1_ref[0, half:2 * half, :], precision=hp, preferred_element_type=F32)
    pe_term = jnp.dot(pe_ref[0], w1_ref[0], precision=hp, preferred_element_type=F32)[0:1, :]
    hidden = first + pltpu.roll(second, n - 1, 0) + pe_term
    o_ref[0, 0, 0] = jnp.dot(_silu(hidden), w2_ref[0], precision=hp, preferred_element_type=F32)


def _compress(t, w1, w2, pe):
    kinds, b, g, n, flat = t.shape
    return pl.pallas_call(
        _compress_kernel,
        grid=(kinds, b, g),
        in_specs=[pl.BlockSpec((1, 1, 1, n, flat), lambda a, bi, gi: (a, bi, gi, 0, 0)),
                  pl.BlockSpec((1, 2 * flat, CMP_HIDDEN), lambda a, bi, gi: (a, 0, 0)),
                  pl.BlockSpec((1, CMP_HIDDEN, HEAD_DIM), lambda a, bi, gi: (a, 0, 0)),
                  pl.BlockSpec((1, SUBLANES, 2 * flat), lambda a, bi, gi: (a, 0, 0))],
        out_specs=pl.BlockSpec((1, 1, 1, n, HEAD_DIM), lambda a, bi, gi: (a, bi, gi, 0, 0)),
        out_shape=jax.ShapeDtypeStruct((kinds, b, g, n, HEAD_DIM), F32),
        compiler_params=pltpu.CompilerParams(
            dimension_semantics=("arbitrary", "arbitrary", "arbitrary"),
            vmem_limit_bytes=_vmem_limit(n * flat * 4 + 2 * flat * CMP_HIDDEN * 4, 8 * 1024 * 1024)),
        name="compress",
    )(t, w1, w2, pe)


def _softmax_numerator(s, bias):
    s = s + bias
    m = jnp.max(s, axis=1, keepdims=True)
    m = jnp.where(m == -jnp.inf, 0.0, m)
    return jnp.exp2(s - m)


def _stack_heads(q):
    return jnp.concatenate([q[:, h * HEAD_DIM:(h + 1) * HEAD_DIM] for h in range(NSA_GROUP)], axis=0)


def _group_gate(gates, gi, h, j):
    c0 = MISC_GATE0 + 3 * h + j
    c1 = MISC_GATE0 + 3 * (NSA_GROUP + h) + j
    return jnp.where(gi == 0, gates[:, c0:c0 + 1], gates[:, c1:c1 + 1])


def _cmpwin_kernel(q_ref, kc_ref, vc_ref, kw_ref, vw_ref, misc_ref, part_ref, selb_ref):
    tq = NSA_Q
    gi = pl.program_id(1)
    qi = pl.program_id(2)
    q0 = pl.multiple_of(qi * tq, tq)
    q4 = _stack_heads(q_ref[0])
    n_cmp = kc_ref.shape[2]
    head_rows = [slice(h * tq, (h + 1) * tq) for h in range(NSA_GROUP)]

    kc = kc_ref[0, 0]
    kc_hi = kc.astype(BF16)
    kc_lo = (kc - kc_hi.astype(F32)).astype(BF16)
    s = _dot_nt(q4, kc_hi) + _dot_nt(q4, kc_lo)
    col = lax.broadcasted_iota(jnp.int32, (tq, n_cmp), 1)
    t_row = q0 + lax.broadcasted_iota(jnp.int32, (tq, n_cmp), 0)
    cmp_bias = jnp.where(col * CMP_STRIDE + (CMP_BLOCK - 1) <= t_row, 0.0, -jnp.inf)
    pcs = []
    for r in head_rows:
        p = _softmax_numerator(s[r], cmp_bias)
        pcs.append(p * (1.0 / jnp.maximum(jnp.sum(p, axis=1, keepdims=True), 1e-30)))
    oc = jnp.dot(jnp.concatenate(pcs, axis=0).astype(BF16), vc_ref[0, 0].astype(BF16),
                 preferred_element_type=F32)

    pc_sum = pcs[0] + pcs[1] + pcs[2] + pcs[3]
    n_blk = selb_ref.shape[3]
    ratio = SLC_BLOCK // CMP_STRIDE
    jj = lax.broadcasted_iota(jnp.int32, (n_blk, n_cmp), 0)
    ii = lax.broadcasted_iota(jnp.int32, (n_blk, n_cmp), 1)
    lo_i = jj * ratio - (CMP_BLOCK // CMP_STRIDE - 1)
    overlap = jnp.logical_and(ii >= lo_i, ii < lo_i + (SLC_BLOCK + CMP_BLOCK) // CMP_STRIDE - 1)
    overlap = jnp.logical_and(overlap, ii < n_cmp - 1).astype(BF16)
    imp = sum(_dot_nt(overlap, p) for p in _split3(pc_sum))
    blk = lax.broadcasted_iota(jnp.int32, (n_blk, tq), 0)
    cur = (q0 + lax.broadcasted_iota(jnp.int32, (n_blk, tq), 1)) // SLC_BLOCK
    forced = jnp.logical_or(blk == 0, jnp.logical_or(blk == cur, blk == cur - 1))
    imp = jnp.where(forced, jnp.inf, jnp.where(blk > cur, -jnp.inf, imp))

    def pick_one(_, carry):
        rem, sel = carry
        best = jnp.max(rem, axis=0, keepdims=True)
        first = jnp.min(jnp.where(rem == best, blk, n_blk), axis=0, keepdims=True)
        hit = blk == first
        return jnp.where(hit, -jnp.inf, rem), jnp.where(hit, 1.0, sel)

    _, sel = lax.fori_loop(0, min(N_SELECT, n_blk), pick_one, (imp, jnp.zeros_like(imp)),
                           unroll=True)
    selb_ref[0, 0] = jnp.where(sel.T > 0.5, 0.0, MASK_VALUE).astype(BF16)

    start = pl.multiple_of(jnp.maximum(q0 - WINDOW, 0), tq)
    kw = kw_ref[0, 0, pl.ds(start, WIN_KEYS), :]
    vw = vw_ref[0, 0, pl.ds(start, WIN_KEYS), :]
    sw = _dot_nt(q4, kw)
    pos = start + lax.broadcasted_iota(jnp.int32, (tq, WIN_KEYS), 1)
    dist = q0 + lax.broadcasted_iota(jnp.int32, (tq, WIN_KEYS), 0) - pos
    win_bias = jnp.where(jnp.logical_and(dist >= 0, dist < WINDOW), 0.0, -jnp.inf)
    pw = jnp.concatenate([_softmax_numerator(sw[r], win_bias) for r in head_rows], axis=0)
    ow = jnp.dot(pw.astype(BF16), vw, preferred_element_type=F32)
    ow = ow[:, 0:HEAD_DIM] * (1.0 / jnp.maximum(ow[:, HEAD_DIM:HEAD_DIM + 1], 1e-30))

    gates = _sigmoid(misc_ref[0])
    heads = []
    for h, r in enumerate(head_rows):
        heads.append(_group_gate(gates, gi, h, 0) * oc[r] + _group_gate(gates, gi, h, 2) * ow[r])
    part_ref[0] = jnp.concatenate(heads, axis=1)


def _cmpwin(nq, kc_c, vc_c, kw, vw_ones, misc):
    b, s, w = nq.shape
    g = NSA_KV_HEADS
    tq = NSA_Q
    gw = NSA_GROUP * HEAD_DIM
    n_cmp = kc_c.shape[2]
    n_blk = s // SLC_BLOCK
    q_spec = pl.BlockSpec((1, tq, gw), lambda bi, gi, i: (bi, i, gi))
    cmp_spec = pl.BlockSpec((1, 1, n_cmp, HEAD_DIM), lambda bi, gi, i: (bi, gi, 0, 0))
    resident = 2 * s * LANES * 2 + 2 * n_cmp * LANES * 4
    rows = NSA_GROUP * tq
    return pl.pallas_call(
        _cmpwin_kernel,
        grid=(b, g, s // tq),
        in_specs=[q_spec, cmp_spec, cmp_spec,
                  pl.BlockSpec((1, 1, s, HEAD_DIM), lambda bi, gi, i: (bi, gi, 0, 0)),
                  pl.BlockSpec((1, 1, s, LANES), lambda bi, gi, i: (bi, gi, 0, 0)),
                  pl.BlockSpec((1, tq, LANES), lambda bi, gi, i: (bi, i, 0))],
        out_specs=(q_spec, pl.BlockSpec((1, 1, tq, n_blk), lambda bi, gi, i: (bi, gi, i, 0))),
        out_shape=(jax.ShapeDtypeStruct((b, s, w), F32),
                   jax.ShapeDtypeStruct((b, g, s, n_blk), BF16)),
        compiler_params=pltpu.CompilerParams(
            dimension_semantics=("arbitrary", "arbitrary", "arbitrary"),
            vmem_limit_bytes=_vmem_limit(resident + tq * gw * 8, 12 * rows * WIN_KEYS * 4)),
        name="cmpwin",
    )(nq, kc_c, vc_c, kw, vw_ones, misc)


def _sel_kernel(q_ref, k_ref, v_ref, selb_ref, part_ref, misc_ref, z_ref, o_ref,
                qa_ref, m_ref, acc_ref):
    tq = NSA_Q
    gi = pl.program_id(1)
    qi = pl.program_id(2)
    q0 = pl.multiple_of(qi * tq, tq)
    q = q_ref[0]
    selb = selb_ref[0, 0]
    lane = lax.broadcasted_iota(jnp.int32, (tq, LANES), 1)
    for h in range(NSA_GROUP):
        r = slice(h * tq, (h + 1) * tq)
        slab = q[:, (h // 2) * LANES:(h // 2 + 1) * LANES]
        own = (lane < HEAD_DIM) if h % 2 == 0 else (lane >= HEAD_DIM)
        qa_ref[r, 0:LANES] = jnp.where(own, slab, jnp.zeros_like(slab))
        qa_ref[r, LANES:2 * LANES] = selb
    m_ref[...] = jnp.full(m_ref.shape, MASK_VALUE, F32)
    acc_ref[...] = jnp.zeros(acc_ref.shape, F32)

    def step(k0, w, masked):
        s = _dot_nt(qa_ref[...], k_ref[0, 0, pl.ds(k0, w), :])
        if masked:
            t_row = q0 + (lax.broadcasted_iota(jnp.int32, s.shape, 0) & (tq - 1))
            pos = k0 + lax.broadcasted_iota(jnp.int32, s.shape, 1)
            s = jnp.where(pos <= t_row, s, MASK_VALUE)
        _flash_update(s, v_ref[0, 0, pl.ds(k0, w), :], m_ref, acc_ref)

    _causal_kv_schedule(q0 // SEL_KV, SEL_KV, step)

    gates = _sigmoid(misc_ref[0])
    part = part_ref[0]
    heads = []
    for h in range(NSA_GROUP):
        r = slice(h * tq, (h + 1) * tq)
        o_sel = acc_ref[r, 0:HEAD_DIM] * (1.0 / acc_ref[r, HEAD_DIM:HEAD_DIM + 1])
        heads.append(part[:, h * HEAD_DIM:(h + 1) * HEAD_DIM] + _group_gate(gates, gi, h, 1) * o_sel)
    o_ref[0] = (jnp.concatenate(heads, axis=1) * z_ref[0].astype(F32)).astype(BF16)


def _sel(nq, k_aug, v_ones, selb, part, misc, nz):
    b, s, w = nq.shape
    g = NSA_KV_HEADS
    tq = NSA_Q
    gw = NSA_GROUP * HEAD_DIM
    n_blk = s // SLC_BLOCK
    rows = NSA_GROUP * tq
    q_spec = pl.BlockSpec((1, tq, gw), lambda bi, gi, i: (bi, i, gi))
    resident = s * 3 * LANES * 2
    return pl.pallas_call(
        _sel_kernel,
        grid=(b, g, s // tq),
        in_specs=[q_spec,
                  pl.BlockSpec((1, 1, s, 2 * LANES), lambda bi, gi, i: (bi, gi, 0, 0)),
                  pl.BlockSpec((1, 1, s, LANES), lambda bi, gi, i: (bi, gi, 0, 0)),
                  pl.BlockSpec((1, 1, tq, n_blk), lambda bi, gi, i: (bi, gi, i, 0)),
                  q_spec,
                  pl.BlockSpec((1, tq, LANES), lambda bi, gi, i: (bi, i, 0)),
                  q_spec],
        out_specs=q_spec,
        out_shape=jax.ShapeDtypeStruct((b, s, w), BF16),
        scratch_shapes=[pltpu.VMEM((rows, 2 * LANES), BF16),
                        pltpu.VMEM((rows, LANES), F32),
                        pltpu.VMEM((rows, LANES), F32)],
        compiler_params=pltpu.CompilerParams(
            dimension_semantics=("arbitrary", "arbitrary", "arbitrary"),
            vmem_limit_bytes=_vmem_limit(resident + tq * gw * 10, 8 * rows * 2 * SEL_KV * 4)),
        name="sel",
    )(nq, k_aug, v_ones, selb, part, misc, nz)


def _out_kernel(yf_ref, yn_ref, w_ref, g_ref, mod_ref, x_ref, o_ref):
    y = (jnp.dot(yf_ref[0], w_ref[0:FOX_WIDTH, :], preferred_element_type=F32)
         + jnp.dot(yn_ref[0], w_ref[FOX_WIDTH:, :], preferred_element_type=F32))
    yn = y * lax.rsqrt(jnp.mean(y * y, axis=-1, keepdims=True) + RMS_EPS)
    o_ref[0] = x_ref[0] + mod_ref[0, 2:3, :] * (yn * g_ref[...])


def _out(y_fox, y_nsa, w_out, g_post, mod, x):
    b, s, d = x.shape
    tm = PROJ_ROWS
    row = lambda bi, i: (bi, i, 0)
    half = pl.BlockSpec((1, tm, FOX_WIDTH), row)
    pipelined = 2 * tm * FOX_WIDTH * 2 + 2 * tm * d * 4 + d * d * 2
    return pl.pallas_call(
        _out_kernel,
        grid=(b, s // tm),
        in_specs=[half, half,
                  pl.BlockSpec((d, d), lambda bi, i: (0, 0)),
                  pl.BlockSpec((1, d), lambda bi, i: (0, 0)),
                  pl.BlockSpec((1, 3, d), lambda bi, i: (bi, 0, 0)),
                  pl.BlockSpec((1, tm, d), row)],
        out_specs=pl.BlockSpec((1, tm, d), row),
        out_shape=jax.ShapeDtypeStruct((b, s, d), F32),
        compiler_params=pltpu.CompilerParams(
            dimension_semantics=("arbitrary", "arbitrary"),
            vmem_limit_bytes=_vmem_limit(pipelined, 6 * tm * d * 4)),
        name="out",
    )(y_fox, y_nsa, w_out, g_post, mod, x)


def _rope_slabs(seq_len):
    inv = 1.0 / (ROPE_THETA ** (jnp.arange(0, HEAD_DIM, 2, dtype=F32) / HEAD_DIM))
    ang = jnp.arange(seq_len, dtype=F32)[:, None] * inv[None, :]
    cos, sin = jnp.cos(ang), jnp.sin(ang)
    reps = LANES // (HEAD_DIM // 2)
    sign = jnp.tile(jnp.concatenate([-jnp.ones((HEAD_DIM // 2,), F32), jnp.ones((HEAD_DIM // 2,), F32)]),
                    LANES // HEAD_DIM)
    return jnp.tile(cos, (1, reps)), jnp.tile(sin, (1, reps)) * sign[None, :]


def _reorder_w_in(w_in):
    fw, kv = FOX_WIDTH, NSA_KV_WIDTH
    o = 0
    cols = {}
    for name, n in (("fq", fw), ("fk", fw), ("fv", fw), ("ff", FOX_HEADS), ("fz", fw), ("nq", NSA_WIDTH),
                    ("kc", kv), ("vc", kv), ("ks", kv), ("vs", kv), ("kw", kv), ("vw", kv),
                    ("ng", 3 * NSA_HEADS), ("nz", NSA_WIDTH)):
        cols[name] = w_in[:, o:o + n]
        o += n
    pad = jnp.zeros((w_in.shape[0], LANES - FOX_HEADS - 3 * NSA_HEADS), w_in.dtype)
    order = ("fq", "fk", "fv", "fz", "nq", "kc", "vc", "ks", "vs", "kw", "vw", "nz", "ff", "ng")
    return jnp.concatenate([cols[k] for k in order] + [pad], axis=1).astype(BF16)


def _group_major(t):
    b, s, _ = t.shape
    return t.reshape(b, s, NSA_KV_HEADS, HEAD_DIM).transpose(0, 2, 1, 3)


def _with_ones_column(v):
    lead = v.shape[:-1]
    return jnp.concatenate([v, jnp.ones(lead + (1,), v.dtype),
                            jnp.zeros(lead + (LANES - HEAD_DIM - 1,), v.dtype)], axis=-1)


def _with_block_onehot(k):
    s = k.shape[2]
    n_blk = s // SLC_BLOCK
    onehot = (jnp.arange(s)[:, None] // SLC_BLOCK == jnp.arange(n_blk)[None, :]).astype(k.dtype)
    return jnp.concatenate([k, k, jnp.broadcast_to(onehot, k.shape[:2] + onehot.shape)], axis=-1)


def _layer(x, c8, g_pre, g_post, w_ada, b_ada, w_in, b_forget, w_cmp_k1, w_cmp_k2,
           w_cmp_v1, w_cmp_v2, pe_cmp_k, pe_cmp_v, w_out, cos128, sin128):
    b, s, d = x.shape
    mod = _ada(c8, w_ada, b_ada)[:b].reshape(b, 3, d)
    bf128 = jnp.pad(b_forget, (0, LANES - FOX_HEADS)).reshape(1, LANES)
    (fq, fk, fv, fz, nq, kc, vc, ks, vs, kw, vw, nz, misc, log_f) = _proj(
        x, mod, g_pre.reshape(1, d), _reorder_w_in(w_in), cos128, sin128, bf128)

    cum_split = tuple(p.reshape(b, FOX_HEADS, s)
                      for p in _cumsum_lanes_split(log_f.reshape(b * FOX_HEADS, s)))
    y_fox = _fox(fq, fk, fv, cum_split, fz)

    flat = CMP_STRIDE * HEAD_DIM
    t = jnp.stack([_group_major(kc), _group_major(vc)]).reshape(2, b, NSA_KV_HEADS, s // CMP_STRIDE, flat)
    pe = jnp.stack([pe_cmp_k.reshape(1, 2 * flat), pe_cmp_v.reshape(1, 2 * flat)])
    cmp = _compress(t, jnp.stack([w_cmp_k1, w_cmp_v1]), jnp.stack([w_cmp_k2, w_cmp_v2]),
                    jnp.broadcast_to(pe, (2, SUBLANES, 2 * flat)))
    part, selb = _cmpwin(nq, cmp[0], cmp[1], _group_major(kw), _with_ones_column(_group_major(vw)), misc)
    y_nsa = _sel(nq, _with_block_onehot(_group_major(ks)), _with_ones_column(_group_major(vs)),
                 selb, part, misc, nz)

    return _out(y_fox, y_nsa, w_out.astype(BF16), g_post.reshape(1, d), mod, x)


def kernel(x, c, g_pre, g_post, w_ada, b_ada, w_in, b_forget, w_cmp_k1, w_cmp_k2,
           w_cmp_v1, w_cmp_v2, pe_cmp_k, pe_cmp_v, w_out):
    cos128, sin128 = _rope_slabs(x.shape[1])
    c8 = jnp.pad(c, ((0, SUBLANES - c.shape[0]), (0, 0)))
    for layer in range(g_pre.shape[0]):
        x = _layer(x, c8, g_pre[layer], g_post[layer], w_ada[layer], b_ada[layer], w_in[layer],
                   b_forget[layer], w_cmp_k1[layer], w_cmp_k2[layer], w_cmp_v1[layer],
                   w_cmp_v2[layer], pe_cmp_k[layer], pe_cmp_v[layer], w_out[layer], cos128, sin128)
    return x
```

```python
import functools

import jax
import jax.numpy as jnp
import numpy as np
from jax import lax
from jax.experimental import pallas as pl
from jax.experimental.pallas import tpu as pltpu

F32 = jnp.float32
BF16 = jnp.bfloat16

D_MODEL = 1024
HEAD_DIM = 64
FOX_WIDTH = 512
NSA_WIDTH = 512
FOX_HEADS = 8
NSA_HEADS = 8
NSA_KV_HEADS = 2
NSA_GROUP = 4
NSA_KV_WIDTH = 128
CMP_BLOCK = 32
CMP_STRIDE = 16
CMP_HIDDEN = 128
SLC_BLOCK = 64
N_SELECT = 16
WINDOW = 512
ROPE_THETA = 10000.0
RMS_EPS = 1e-6
LOG2E = 1.4426950408889634
QK_SCALE = HEAD_DIM ** -0.5 * LOG2E

LANES = 128
SUBLANES = 8
V7X_VMEM_BYTES = 64 * 1024 * 1024
MASK_VALUE = -1e30

PROJ_ROWS = 512
FOX_TILE = 512
NSA_Q = 128
SEL_KV = 512
WIN_KEYS = WINDOW + NSA_Q
FOX_AUG = 6

C_FQ, C_FK, C_FV, C_FZ, C_NQ = 0, 512, 1024, 1536, 2048
C_KC, C_VC, C_KS, C_VS, C_KW, C_VW = 2560, 2688, 2816, 2944, 3072, 3200
C_NZ, C_MISC, PROJ_COLS = 3328, 3840, 3968
MISC_GATE0 = FOX_HEADS


def _vmem_limit(pipelined_bytes, resident_bytes):
    need = 2 * pipelined_bytes + resident_bytes
    return int(min(max(need, 16 * 1024 * 1024), V7X_VMEM_BYTES - 8 * 1024 * 1024))


def _sigmoid(v):
    return 1.0 / (1.0 + jnp.exp(-v))


def _silu(v):
    return v * _sigmoid(v)


def _dot_nt(a, b):
    return lax.dot_general(a, b, (((1,), (1,)), ((), ())), preferred_element_type=F32)


def _split3(v):
    hi = v.astype(BF16)
    r1 = v - hi.astype(F32)
    mid = r1.astype(BF16)
    lo = (r1 - mid.astype(F32)).astype(BF16)
    return hi, mid, lo


def _ada_kernel(c_ref, w_ref, b_ref, o_ref):
    a = _silu(c_ref[...])
    o_ref[...] = jnp.dot(a, w_ref[...], precision=lax.Precision.HIGHEST,
                         preferred_element_type=F32) + b_ref[...]


def _ada(c8, w_ada, b_ada):
    n = w_ada.shape[1]
    blk = D_MODEL
    return pl.pallas_call(
        _ada_kernel,
        grid=(n // blk,),
        in_specs=[pl.BlockSpec((SUBLANES, D_MODEL), lambda j: (0, 0)),
                  pl.BlockSpec((D_MODEL, blk), lambda j: (0, j)),
                  pl.BlockSpec((1, blk), lambda j: (0, j))],
        out_specs=pl.BlockSpec((SUBLANES, blk), lambda j: (0, j)),
        out_shape=jax.ShapeDtypeStruct((SUBLANES, n), F32),
        compiler_params=pltpu.CompilerParams(
            dimension_semantics=("arbitrary",),
            vmem_limit_bytes=_vmem_limit(D_MODEL * blk * 4, 4 * 1024 * 1024)),
        name="ada",
    )(c8, w_ada, b_ada.reshape(1, n))


def _rope128(t, cos, sin_signed):
    lane = lax.broadcasted_iota(jnp.int32, t.shape, 1)
    first_half = (lane & (HEAD_DIM - 1)) < HEAD_DIM // 2
    partner = jnp.where(first_half,
                        pltpu.roll(t, LANES - HEAD_DIM // 2, 1),
                        pltpu.roll(t, HEAD_DIM // 2, 1))
    return t * cos + partner * sin_signed


def _proj_kernel(x_ref, mod_ref, g_ref, w_ref, cos_ref, sin_ref, bf_ref,
                 fq_ref, fk_ref, fv_ref, fz_ref, nq_ref, kc_ref, vc_ref,
                 ks_ref, vs_ref, kw_ref, vw_ref, nz_ref, misc_ref, lf_ref):
    x = x_ref[0]
    y = x * lax.rsqrt(jnp.mean(x * x, axis=-1, keepdims=True) + RMS_EPS)
    y = y * g_ref[...]
    h = (y * (1.0 + mod_ref[0, 1:2, :]) + mod_ref[0, 0:1, :]).astype(BF16)
    cos = cos_ref[...]
    sin = sin_ref[...]

    def mm(lo, n):
        return jnp.dot(h, w_ref[:, lo:lo + n], preferred_element_type=F32)

    fq_ref[0] = (mm(C_FQ, FOX_WIDTH) * QK_SCALE).astype(BF16)
    fk_ref[0] = mm(C_FK, FOX_WIDTH).astype(BF16)
    fv_ref[0] = mm(C_FV, FOX_WIDTH).astype(BF16)
    fz_ref[0] = _silu(mm(C_FZ, FOX_WIDTH)).astype(BF16)
    for j in range(NSA_WIDTH // LANES):
        t = mm(C_NQ + j * LANES, LANES)
        nq_ref[0, :, j * LANES:(j + 1) * LANES] = (_rope128(t, cos, sin) * QK_SCALE).astype(BF16)
    kc_ref[0] = _rope128(mm(C_KC, LANES), cos, sin)
    vc_ref[0] = mm(C_VC, LANES)
    ks_ref[0] = _rope128(mm(C_KS, LANES), cos, sin).astype(BF16)
    vs_ref[0] = mm(C_VS, LANES).astype(BF16)
    kw_ref[0] = _rope128(mm(C_KW, LANES), cos, sin).astype(BF16)
    vw_ref[0] = mm(C_VW, LANES).astype(BF16)
    nz_ref[0] = _silu(mm(C_NZ, NSA_WIDTH)).astype(BF16)
    misc = mm(C_MISC, LANES)
    misc_ref[0] = misc
    z = misc + bf_ref[...]
    log_f = jnp.minimum(z, 0.0) - jnp.log1p(jnp.exp(-jnp.abs(z)))
    lf_ref[0] = log_f.T[0:FOX_HEADS, :]


def _proj(x, mod, g_pre, w_cat, cos128, sin128, bf128):
    b, s, d = x.shape
    tm = PROJ_ROWS
    row = lambda bi, i: (bi, i, 0)
    wide = lambda dt: jax.ShapeDtypeStruct((b, s, FOX_WIDTH), dt)
    slab = lambda dt: jax.ShapeDtypeStruct((b, s, LANES), dt)
    out_shape = (wide(BF16), wide(BF16), wide(BF16), wide(BF16), wide(BF16),
                 slab(F32), slab(F32), slab(BF16), slab(BF16), slab(BF16), slab(BF16),
                 wide(BF16), slab(F32), jax.ShapeDtypeStruct((b, FOX_HEADS, s), F32))
    wide_spec = pl.BlockSpec((1, tm, FOX_WIDTH), row)
    slab_spec = pl.BlockSpec((1, tm, LANES), row)
    out_specs = (wide_spec,) * 5 + (slab_spec,) * 6 + (wide_spec, slab_spec,
                 pl.BlockSpec((1, FOX_HEADS, tm), lambda bi, i: (bi, 0, i)))
    pipelined = tm * d * 4 + tm * (6 * FOX_WIDTH * 2 + 4 * LANES * 2 + 3 * LANES * 4) + d * PROJ_COLS * 2
    return pl.pallas_call(
        _proj_kernel,
        grid=(b, s // tm),
        in_specs=[pl.BlockSpec((1, tm, d), row),
                  pl.BlockSpec((1, 3, d), lambda bi, i: (bi, 0, 0)),
                  pl.BlockSpec((1, d), lambda bi, i: (0, 0)),
                  pl.BlockSpec((d, PROJ_COLS), lambda bi, i: (0, 0)),
                  pl.BlockSpec((tm, LANES), lambda bi, i: (i, 0)),
                  pl.BlockSpec((tm, LANES), lambda bi, i: (i, 0)),
                  pl.BlockSpec((1, LANES), lambda bi, i: (0, 0))],
        out_specs=out_specs,
        out_shape=out_shape,
        compiler_params=pltpu.CompilerParams(
            dimension_semantics=("arbitrary", "arbitrary"),
            vmem_limit_bytes=_vmem_limit(pipelined, 8 * 1024 * 1024)),
        name="proj",
    )(x, mod, g_pre, w_cat, cos128, sin128, bf128)


def _cumsum_kernel(x_ref, hi_ref, mid_ref, lo_ref, *, chunks):
    x = x_ref[...]
    n = x.shape[0]
    parts = _split3(x)
    r = lax.broadcasted_iota(jnp.int32, (LANES, LANES), 0)
    c = lax.broadcasted_iota(jnp.int32, (LANES, LANES), 1)
    tri = (r <= c).astype(BF16)
    rr = lax.broadcasted_iota(jnp.int32, (n, n), 0)
    cc = lax.broadcasted_iota(jnp.int32, (n, n), 1)
    earlier = jnp.logical_and(cc < rr, (cc // chunks) == (rr // chunks)).astype(BF16)
    within = sum(jnp.dot(p, tri, preferred_element_type=F32) for p in parts)
    before = sum(jnp.dot(earlier, p, preferred_element_type=F32) for p in parts)
    total = (within + jnp.sum(before, axis=-1, keepdims=True)) * LOG2E
    hi_ref[...], mid_ref[...], lo_ref[...] = _split3(total)


def _cumsum_lanes_split(v):
    rows, s = v.shape
    chunks = s // LANES
    n = rows * chunks
    part = jax.ShapeDtypeStruct((n, LANES), BF16)
    parts = pl.pallas_call(
        functools.partial(_cumsum_kernel, chunks=chunks),
        out_shape=(part, part, part),
        compiler_params=pltpu.CompilerParams(
            vmem_limit_bytes=_vmem_limit(2 * n * LANES * 4, 6 * n * n)),
        name="cumsum",
    )(v.reshape(n, LANES))
    return tuple(p.reshape(rows, s) for p in parts)


def _flash_scratch(rows, tile, acc_lanes):
    return ([pltpu.VMEM((rows, LANES), F32), pltpu.VMEM((rows, acc_lanes), F32)]
            + [pltpu.VMEM((rows, tile), F32)] * 2
            + [pltpu.VMEM((rows, tile), BF16)] * 2
            + [pltpu.VMEM((rows, LANES), F32)] * 2)


def _flash_scratch_bytes(rows, tile, acc_lanes):
    return rows * (LANES * 4 + acc_lanes * 4 + 2 * tile * 4 + 2 * tile * 2 + 2 * LANES * 4)


def _causal_flash(n_full, tile, q_ref, load_k, load_v, diag_mask, scratch):
    m_ref, acc_ref, s0, s1, p0, p1, a0, a1 = scratch
    s_bufs, p_bufs, a_bufs = (s0, s1), (p0, p1), (a0, a1)
    reps = acc_ref.shape[-1] // LANES

    def logits(j):
        return _dot_nt(q_ref[...], load_k(pl.multiple_of(j * tile, tile)))

    def softmax(s):
        m_prev = m_ref[...]
        m_next = jnp.maximum(m_prev, jnp.max(s, axis=1, keepdims=True))
        m_ref[...] = m_next
        p = jnp.exp2(s - pltpu.repeat(m_next, tile // LANES, 1))
        return p.astype(BF16), jnp.exp2(m_prev - m_next)

    def accumulate(p, alpha, j):
        pv = jnp.dot(p, load_v(pl.multiple_of(j * tile, tile)), preferred_element_type=F32)
        acc_ref[...] = acc_ref[...] * (alpha if reps == 1 else pltpu.repeat(alpha, reps, 1)) + pv

    def stage(j, cur):
        s_bufs[1 - cur][...] = logits(j + 1)
        p_bufs[cur][...], a_bufs[cur][...] = softmax(s_bufs[cur][...])
        accumulate(p_bufs[1 - cur][...], a_bufs[1 - cur][...], jnp.maximum(j - 1, 0))

    def finish(cur):
        p, alpha = softmax(diag_mask(s_bufs[cur][...]))
        accumulate(p_bufs[1 - cur][...], a_bufs[1 - cur][...], jnp.maximum(n_full - 1, 0))
        accumulate(p, alpha, n_full)

    m_ref[...] = jnp.full(m_ref.shape, MASK_VALUE, F32)
    acc_ref[...] = jnp.zeros(acc_ref.shape, F32)
    s_bufs[0][...] = logits(0)
    p_bufs[1][...] = jnp.zeros(p_bufs[1].shape, BF16)
    a_bufs[1][...] = jnp.ones(a_bufs[1].shape, F32)

    def two_stages(jj, carry):
        stage(2 * jj, 0)
        stage(2 * jj + 1, 1)
        return carry

    lax.fori_loop(0, n_full // 2, two_stages, 0)

    @pl.when(n_full % 2 == 1)
    def _():
        stage(n_full - 1, 0)
        finish(1)

    @pl.when(n_full % 2 == 0)
    def _():
        finish(0)


def _fox_aug(cum_split):
    b, h, s = cum_split[0].shape
    terms = jnp.concatenate(list(cum_split) + [jnp.ones((b, 1, s), BF16)], axis=1)
    terms = jnp.transpose(terms, (0, 2, 1))
    place_q = np.zeros((3 * h + 1, h * HEAD_DIM), np.float32)
    place_k = np.zeros((3 * h + 1, h * HEAD_DIM), np.float32)
    half = FOX_AUG // 2
    for head in range(h):
        for part in range(half):
            place_q[part * h + head, head * HEAD_DIM + part] = 1.0
            place_q[3 * h, head * HEAD_DIM + half + part] = 1.0
            place_k[3 * h, head * HEAD_DIM + part] = 1.0
            place_k[part * h + head, head * HEAD_DIM + half + part] = -1.0
    scatter = lambda place: jnp.einsum("bsr,rl->bsl", terms, jnp.asarray(place, BF16),
                                       preferred_element_type=F32).astype(BF16)
    return scatter(place_q), scatter(place_k)


def _fox_kernel(q_ref, qa_ref, k_ref, ka_ref, v_ref, z_ref, o_ref, qm_ref, ones_ref, *flash):
    t = FOX_TILE
    qi = pl.program_id(2)
    q = q_ref[0]
    qa = qa_ref[0]
    lane = lax.broadcasted_iota(jnp.int32, q.shape, 1)
    for hh in range(2):
        r = slice(hh * t, (hh + 1) * t)
        own = (lane < HEAD_DIM) if hh == 0 else (lane >= HEAD_DIM)
        qm_ref[r, 0:LANES] = jnp.where(own, q, jnp.zeros_like(q))
        qm_ref[r, LANES:2 * LANES] = jnp.where(own, qa, jnp.zeros_like(qa))
    ones_lane = lax.broadcasted_iota(jnp.int32, ones_ref.shape, 1)
    ones_ref[...] = jnp.where(ones_lane == 0, 1.0, 0.0).astype(BF16)

    def load_k(k0):
        return jnp.concatenate([k_ref[0, pl.ds(k0, t), :], ka_ref[0, pl.ds(k0, t), :]], axis=1)

    def load_v(k0):
        return jnp.concatenate([v_ref[0, pl.ds(k0, t), :], ones_ref[...]], axis=1)

    def diag_mask(s):
        row = lax.broadcasted_iota(jnp.int32, s.shape, 0) & (t - 1)
        col = lax.broadcasted_iota(jnp.int32, s.shape, 1)
        return jnp.where(col <= row, s, MASK_VALUE)

    _causal_flash(qi, t, qm_ref, load_k, load_v, diag_mask, flash)
    acc_ref = flash[1]
    o0 = acc_ref[0:t, 0:LANES] * (1.0 / acc_ref[0:t, LANES:LANES + 1])
    o1 = acc_ref[t:2 * t, 0:LANES] * (1.0 / acc_ref[t:2 * t, LANES:LANES + 1])
    o = jnp.where(lane < HEAD_DIM, o0, o1)
    o_ref[0] = (o * z_ref[0].astype(F32)).astype(BF16)


def _fox(fq, fk, fv, cum_split, fz):
    b, s, w = fq.shape
    t = FOX_TILE
    pairs = w // LANES
    qa, ka = _fox_aug(cum_split)
    tile = pl.BlockSpec((1, t, LANES), lambda bi, hp, i: (bi, i, hp))
    full = pl.BlockSpec((1, s, LANES), lambda bi, hp, i: (bi, 0, hp))
    resident = 3 * s * LANES * 2
    scratch = 2 * t * 2 * LANES * 2 + t * LANES * 2 + _flash_scratch_bytes(2 * t, t, 2 * LANES)
    return pl.pallas_call(
        _fox_kernel,
        grid=(b, pairs, s // t),
        in_specs=[tile, tile, full, full, full, tile],
        out_specs=tile,
        out_shape=jax.ShapeDtypeStruct((b, s, w), BF16),
        scratch_shapes=[pltpu.VMEM((2 * t, 2 * LANES), BF16),
                        pltpu.VMEM((t, LANES), BF16)] + _flash_scratch(2 * t, t, 2 * LANES),
        compiler_params=pltpu.CompilerParams(
            dimension_semantics=("arbitrary", "arbitrary", "arbitrary"),
            vmem_limit_bytes=_vmem_limit(resident + 4 * t * LANES * 2, scratch + 4 * 2 * t * t * 4)),
        name="fox",
    )(fq, qa, fk, ka, fv, fz)


def _compress_kernel(t_ref, w1_ref, w2_ref, pe_ref, o_ref):
    hp = lax.Precision.HIGHEST
    half = CMP_STRIDE * HEAD_DIM
    tt = t_ref[0, 0, 0]
    n = tt.shape[0]
    first = jnp.dot(tt, w1_ref[0, 0:half, :], precision=hp, preferred_element_type=F32)
    second = jnp.dot(tt, w1_ref[0, half:2 * half, :], precision=hp, preferred_element_type=F32)
    pe_term = jnp.dot(pe_ref[0], w1_ref[0], precision=hp, preferred_element_type=F32)[0:1, :]
    hidden = first + pltpu.roll(second, n - 1, 0) + pe_term
    o_ref[0, 0, 0] = jnp.dot(_silu(hidden), w2_ref[0], precision=hp, preferred_element_type=F32)


def _compress(t, w1, w2, pe):
    kinds, b, g, n, flat = t.shape
    return pl.pallas_call(
        _compress_kernel,
        grid=(kinds, b, g),
        in_specs=[pl.BlockSpec((1, 1, 1, n, flat), lambda a, bi, gi: (a, bi, gi, 0, 0)),
                  pl.BlockSpec((1, 2 * flat, CMP_HIDDEN), lambda a, bi, gi: (a, 0, 0)),
                  pl.BlockSpec((1, CMP_HIDDEN, HEAD_DIM), lambda a, bi, gi: (a, 0, 0)),
                  pl.BlockSpec((1, SUBLANES, 2 * flat), lambda a, bi, gi: (a, 0, 0))],
        out_specs=pl.BlockSpec((1, 1, 1, n, HEAD_DIM), lambda a, bi, gi: (a, bi, gi, 0, 0)),
        out_shape=jax.ShapeDtypeStruct((kinds, b, g, n, HEAD_DIM), F32),
        compiler_params=pltpu.CompilerParams(
            dimension_semantics=("arbitrary", "arbitrary", "arbitrary"),
            vmem_limit_bytes=_vmem_limit(n * flat * 4 + 2 * flat * CMP_HIDDEN * 4, 8 * 1024 * 1024)),
        name="compress",
    )(t, w1, w2, pe)


def _softmax_numerator(s, bias):
    s = s + bias
    m = jnp.max(s, axis=1, keepdims=True)
    m = jnp.where(m == -jnp.inf, 0.0, m)
    return jnp.exp2(s - m)


def _stack_heads(q):
    return jnp.concatenate([q[:, h * HEAD_DIM:(h + 1) * HEAD_DIM] for h in range(NSA_GROUP)], axis=0)


def _group_gate(gates, gi, h, j):
    c0 = MISC_GATE0 + 3 * h + j
    c1 = MISC_GATE0 + 3 * (NSA_GROUP + h) + j
    return jnp.where(gi == 0, gates[:, c0:c0 + 1], gates[:, c1:c1 + 1])


def _cmpwin_kernel(q_ref, kc_ref, vc_ref, kw_ref, vw_ref, misc_ref, part_ref, selb_ref):
    tq = NSA_Q
    gi = pl.program_id(1)
    qi = pl.program_id(2)
    q0 = pl.multiple_of(qi * tq, tq)
    q4 = _stack_heads(q_ref[0])
    n_cmp = kc_ref.shape[2]
    head_rows = [slice(h * tq, (h + 1) * tq) for h in range(NSA_GROUP)]

    kc = kc_ref[0, 0]
    kc_hi = kc.astype(BF16)
    kc_lo = (kc - kc_hi.astype(F32)).astype(BF16)
    s = _dot_nt(q4, kc_hi) + _dot_nt(q4, kc_lo)
    col = lax.broadcasted_iota(jnp.int32, (tq, n_cmp), 1)
    t_row = q0 + lax.broadcasted_iota(jnp.int32, (tq, n_cmp), 0)
    cmp_bias = jnp.where(col * CMP_STRIDE + (CMP_BLOCK - 1) <= t_row, 0.0, -jnp.inf)
    pcs = []
    for r in head_rows:
        p = _softmax_numerator(s[r], cmp_bias)
        pcs.append(p * (1.0 / jnp.maximum(jnp.sum(p, axis=1, keepdims=True), 1e-30)))
    oc = jnp.dot(jnp.concatenate(pcs, axis=0).astype(BF16), vc_ref[0, 0].astype(BF16),
                 preferred_element_type=F32)

    pc_sum = pcs[0] + pcs[1] + pcs[2] + pcs[3]
    n_blk = selb_ref.shape[3]
    ratio = SLC_BLOCK // CMP_STRIDE
    jj = lax.broadcasted_iota(jnp.int32, (n_blk, n_cmp), 0)
    ii = lax.broadcasted_iota(jnp.int32, (n_blk, n_cmp), 1)
    lo_i = jj * ratio - (CMP_BLOCK // CMP_STRIDE - 1)
    overlap = jnp.logical_and(ii >= lo_i, ii < lo_i + (SLC_BLOCK + CMP_BLOCK) // CMP_STRIDE - 1)
    overlap = jnp.logical_and(overlap, ii < n_cmp - 1).astype(BF16)
    imp = sum(_dot_nt(overlap, p) for p in _split3(pc_sum))
    blk = lax.broadcasted_iota(jnp.int32, (n_blk, tq), 0)
    cur = (q0 + lax.broadcasted_iota(jnp.int32, (n_blk, tq), 1)) // SLC_BLOCK
    forced = jnp.logical_or(blk == 0, jnp.logical_or(blk == cur, blk == cur - 1))
    imp = jnp.where(forced, jnp.inf, jnp.where(blk > cur, -jnp.inf, imp))

    def pick_one(_, carry):
        rem, sel = carry
        best = jnp.max(rem, axis=0, keepdims=True)
        first = jnp.min(jnp.where(rem == best, blk, n_blk), axis=0, keepdims=True)
        hit = blk == first
        return jnp.where(hit, -jnp.inf, rem), jnp.where(hit, 1.0, sel)

    _, sel = lax.fori_loop(0, min(N_SELECT, n_blk), pick_one, (imp, jnp.zeros_like(imp)),
                           unroll=True)
    selb_ref[0, 0] = jnp.where(sel.T > 0.5, 0.0, MASK_VALUE).astype(BF16)

    start = pl.multiple_of(jnp.maximum(q0 - WINDOW, 0), tq)
    kw = kw_ref[0, 0, pl.ds(start, WIN_KEYS), :]
    vw = vw_ref[0, 0, pl.ds(start, WIN_KEYS), :]
    sw = _dot_nt(q4, kw)
    pos = start + lax.broadcasted_iota(jnp.int32, (tq, WIN_KEYS), 1)
    dist = q0 + lax.broadcasted_iota(jnp.int32, (tq, WIN_KEYS), 0) - pos
    win_bias = jnp.where(jnp.logical_and(dist >= 0, dist < WINDOW), 0.0, -jnp.inf)
    pw = jnp.concatenate([_softmax_numerator(sw[r], win_bias) for r in head_rows], axis=0)
    ow = jnp.dot(pw.astype(BF16), vw, preferred_element_type=F32)
    ow = ow[:, 0:HEAD_DIM] * (1.0 / jnp.maximum(ow[:, HEAD_DIM:HEAD_DIM + 1], 1e-30))

    gates = _sigmoid(misc_ref[0])
    heads = []
    for h, r in enumerate(head_rows):
        heads.append(_group_gate(gates, gi, h, 0) * oc[r] + _group_gate(gates, gi, h, 2) * ow[r])
    part_ref[0] = jnp.concatenate(heads, axis=1)


def _cmpwin(nq, kc_c, vc_c, kw, vw_ones, misc):
    b, s, w = nq.shape
    g = NSA_KV_HEADS
    tq = NSA_Q
    gw = NSA_GROUP * HEAD_DIM
    n_cmp = kc_c.shape[2]
    n_blk = s // SLC_BLOCK
    q_spec = pl.BlockSpec((1, tq, gw), lambda bi, gi, i: (bi, i, gi))
    cmp_spec = pl.BlockSpec((1, 1, n_cmp, HEAD_DIM), lambda bi, gi, i: (bi, gi, 0, 0))
    resident = 2 * s * LANES * 2 + 2 * n_cmp * LANES * 4
    rows = NSA_GROUP * tq
    return pl.pallas_call(
        _cmpwin_kernel,
        grid=(b, g, s // tq),
        in_specs=[q_spec, cmp_spec, cmp_spec,
                  pl.BlockSpec((1, 1, s, HEAD_DIM), lambda bi, gi, i: (bi, gi, 0, 0)),
                  pl.BlockSpec((1, 1, s, LANES), lambda bi, gi, i: (bi, gi, 0, 0)),
                  pl.BlockSpec((1, tq, LANES), lambda bi, gi, i: (bi, i, 0))],
        out_specs=(q_spec, pl.BlockSpec((1, 1, tq, n_blk), lambda bi, gi, i: (bi, gi, i, 0))),
        out_shape=(jax.ShapeDtypeStruct((b, s, w), F32),
                   jax.ShapeDtypeStruct((b, g, s, n_blk), BF16)),
        compiler_params=pltpu.CompilerParams(
            dimension_semantics=("arbitrary", "arbitrary", "arbitrary"),
            vmem_limit_bytes=_vmem_limit(resident + tq * gw * 8, 12 * rows * WIN_KEYS * 4)),
        name="cmpwin",
    )(nq, kc_c, vc_c, kw, vw_ones, misc)


def _sel_kernel(q_ref, k_ref, v_ref, selb_ref, part_ref, misc_ref, z_ref, o_ref,
                qa_ref, *flash):
    tq = NSA_Q
    gi = pl.program_id(1)
    qi = pl.program_id(2)
    q0 = pl.multiple_of(qi * tq, tq)
    q = q_ref[0]
    selb = selb_ref[0, 0]
    lane = lax.broadcasted_iota(jnp.int32, (tq, LANES), 1)
    for h in range(NSA_GROUP):
        r = slice(h * tq, (h + 1) * tq)
        slab = q[:, (h // 2) * LANES:(h // 2 + 1) * LANES]
        own = (lane < HEAD_DIM) if h % 2 == 0 else (lane >= HEAD_DIM)
        qa_ref[r, 0:LANES] = jnp.where(own, slab, jnp.zeros_like(slab))
        qa_ref[r, LANES:2 * LANES] = selb

    n_full = q0 // SEL_KV

    def diag_mask(s):
        t_row = q0 + (lax.broadcasted_iota(jnp.int32, s.shape, 0) & (tq - 1))
        pos = n_full * SEL_KV + lax.broadcasted_iota(jnp.int32, s.shape, 1)
        return jnp.where(pos <= t_row, s, MASK_VALUE)

    _causal_flash(n_full, SEL_KV, qa_ref,
                  lambda k0: k_ref[0, 0, pl.ds(k0, SEL_KV), :],
                  lambda k0: v_ref[0, 0, pl.ds(k0, SEL_KV), :],
                  diag_mask, flash)
    acc_ref = flash[1]

    gates = _sigmoid(misc_ref[0])
    part = part_ref[0]
    heads = []
    for h in range(NSA_GROUP):
        r = slice(h * tq, (h + 1) * tq)
        o_sel = acc_ref[r, 0:HEAD_DIM] * (1.0 / acc_ref[r, HEAD_DIM:HEAD_DIM + 1])
        heads.append(part[:, h * HEAD_DIM:(h + 1) * HEAD_DIM] + _group_gate(gates, gi, h, 1) * o_sel)
    o_ref[0] = (jnp.concatenate(heads, axis=1) * z_ref[0].astype(F32)).astype(BF16)


def _sel(nq, k_aug, v_ones, selb, part, misc, nz):
    b, s, w = nq.shape
    g = NSA_KV_HEADS
    tq = NSA_Q
    gw = NSA_GROUP * HEAD_DIM
    n_blk = s // SLC_BLOCK
    rows = NSA_GROUP * tq
    q_spec = pl.BlockSpec((1, tq, gw), lambda bi, gi, i: (bi, i, gi))
    resident = s * 3 * LANES * 2
    return pl.pallas_call(
        _sel_kernel,
        grid=(b, g, s // tq),
        in_specs=[q_spec,
                  pl.BlockSpec((1, 1, s, 2 * LANES), lambda bi, gi, i: (bi, gi, 0, 0)),
                  pl.BlockSpec((1, 1, s, LANES), lambda bi, gi, i: (bi, gi, 0, 0)),
                  pl.BlockSpec((1, 1, tq, n_blk), lambda bi, gi, i: (bi, gi, i, 0)),
                  q_spec,
                  pl.BlockSpec((1, tq, LANES), lambda bi, gi, i: (bi, i, 0)),
                  q_spec],
        out_specs=q_spec,
        out_shape=jax.ShapeDtypeStruct((b, s, w), BF16),
        scratch_shapes=[pltpu.VMEM((rows, 2 * LANES), BF16)] + _flash_scratch(rows, SEL_KV, LANES),
        compiler_params=pltpu.CompilerParams(
            dimension_semantics=("arbitrary", "arbitrary", "arbitrary"),
            vmem_limit_bytes=_vmem_limit(resident + tq * gw * 10,
                                         _flash_scratch_bytes(rows, SEL_KV, LANES) + 4 * rows * SEL_KV * 4)),
        name="sel",
    )(nq, k_aug, v_ones, selb, part, misc, nz)


def _out_kernel(yf_ref, yn_ref, w_ref, g_ref, mod_ref, x_ref, o_ref):
    y = (jnp.dot(yf_ref[0], w_ref[0:FOX_WIDTH, :], preferred_element_type=F32)
         + jnp.dot(yn_ref[0], w_ref[FOX_WIDTH:, :], preferred_element_type=F32))
    yn = y * lax.rsqrt(jnp.mean(y * y, axis=-1, keepdims=True) + RMS_EPS)
    o_ref[0] = x_ref[0] + mod_ref[0, 2:3, :] * (yn * g_ref[...])


def _out(y_fox, y_nsa, w_out, g_post, mod, x):
    b, s, d = x.shape
    tm = PROJ_ROWS
    row = lambda bi, i: (bi, i, 0)
    half = pl.BlockSpec((1, tm, FOX_WIDTH), row)
    pipelined = 2 * tm * FOX_WIDTH * 2 + 2 * tm * d * 4 + d * d * 2
    return pl.pallas_call(
        _out_kernel,
        grid=(b, s // tm),
        in_specs=[half, half,
                  pl.BlockSpec((d, d), lambda bi, i: (0, 0)),
                  pl.BlockSpec((1, d), lambda bi, i: (0, 0)),
                  pl.BlockSpec((1, 3, d), lambda bi, i: (bi, 0, 0)),
                  pl.BlockSpec((1, tm, d), row)],
        out_specs=pl.BlockSpec((1, tm, d), row),
        out_shape=jax.ShapeDtypeStruct((b, s, d), F32),
        compiler_params=pltpu.CompilerParams(
            dimension_semantics=("arbitrary", "arbitrary"),
            vmem_limit_bytes=_vmem_limit(pipelined, 6 * tm * d * 4)),
        name="out",
    )(y_fox, y_nsa, w_out, g_post, mod, x)


def _rope_slabs(seq_len):
    inv = 1.0 / (ROPE_THETA ** (jnp.arange(0, HEAD_DIM, 2, dtype=F32) / HEAD_DIM))
    ang = jnp.arange(seq_len, dtype=F32)[:, None] * inv[None, :]
    cos, sin = jnp.cos(ang), jnp.sin(ang)
    reps = LANES // (HEAD_DIM // 2)
    sign = jnp.tile(jnp.concatenate([-jnp.ones((HEAD_DIM // 2,), F32), jnp.ones((HEAD_DIM // 2,), F32)]),
                    LANES // HEAD_DIM)
    return jnp.tile(cos, (1, reps)), jnp.tile(sin, (1, reps)) * sign[None, :]


def _reorder_w_in(w_in):
    fw, kv = FOX_WIDTH, NSA_KV_WIDTH
    o = 0
    cols = {}
    for name, n in (("fq", fw), ("fk", fw), ("fv", fw), ("ff", FOX_HEADS), ("fz", fw), ("nq", NSA_WIDTH),
                    ("kc", kv), ("vc", kv), ("ks", kv), ("vs", kv), ("kw", kv), ("vw", kv),
                    ("ng", 3 * NSA_HEADS), ("nz", NSA_WIDTH)):
        cols[name] = w_in[:, o:o + n]
        o += n
    pad = jnp.zeros((w_in.shape[0], LANES - FOX_HEADS - 3 * NSA_HEADS), w_in.dtype)
    order = ("fq", "fk", "fv", "fz", "nq", "kc", "vc", "ks", "vs", "kw", "vw", "nz", "ff", "ng")
    return jnp.concatenate([cols[k] for k in order] + [pad], axis=1).astype(BF16)


def _group_major(t):
    b, s, _ = t.shape
    return t.reshape(b, s, NSA_KV_HEADS, HEAD_DIM).transpose(0, 2, 1, 3)


def _with_ones_column(v):
    lead = v.shape[:-1]
    return jnp.concatenate([v, jnp.ones(lead + (1,), v.dtype),
                            jnp.zeros(lead + (LANES - HEAD_DIM - 1,), v.dtype)], axis=-1)


def _with_block_onehot(k):
    s = k.shape[2]
    n_blk = s // SLC_BLOCK
    onehot = (jnp.arange(s)[:, None] // SLC_BLOCK == jnp.arange(n_blk)[None, :]).astype(k.dtype)
    return jnp.concatenate([k, k, jnp.broadcast_to(onehot, k.shape[:2] + onehot.shape)], axis=-1)


def _layer(x, c8, g_pre, g_post, w_ada, b_ada, w_in, b_forget, w_cmp_k1, w_cmp_k2,
           w_cmp_v1, w_cmp_v2, pe_cmp_k, pe_cmp_v, w_out, cos128, sin128):
    b, s, d = x.shape
    mod = _ada(c8, w_ada, b_ada)[:b].reshape(b, 3, d)
    bf128 = jnp.pad(b_forget, (0, LANES - FOX_HEADS)).reshape(1, LANES)
    (fq, fk, fv, fz, nq, kc, vc, ks, vs, kw, vw, nz, misc, log_f) = _proj(
        x, mod, g_pre.reshape(1, d), _reorder_w_in(w_in), cos128, sin128, bf128)

    cum_split = tuple(p.reshape(b, FOX_HEADS, s)
                      for p in _cumsum_lanes_split(log_f.reshape(b * FOX_HEADS, s)))
    y_fox = _fox(fq, fk, fv, cum_split, fz)

    flat = CMP_STRIDE * HEAD_DIM
    t = jnp.stack([_group_major(kc), _group_major(vc)]).reshape(2, b, NSA_KV_HEADS, s // CMP_STRIDE, flat)
    pe = jnp.stack([pe_cmp_k.reshape(1, 2 * flat), pe_cmp_v.reshape(1, 2 * flat)])
    cmp = _compress(t, jnp.stack([w_cmp_k1, w_cmp_v1]), jnp.stack([w_cmp_k2, w_cmp_v2]),
                    jnp.broadcast_to(pe, (2, SUBLANES, 2 * flat)))
    part, selb = _cmpwin(nq, cmp[0], cmp[1], _group_major(kw), _with_ones_column(_group_major(vw)), misc)
    y_nsa = _sel(nq, _with_block_onehot(_group_major(ks)), _with_ones_column(_group_major(vs)),
                 selb, part, misc, nz)

    return _out(y_fox, y_nsa, w_out.astype(BF16), g_post.reshape(1, d), mod, x)


def kernel(x, c, g_pre, g_post, w_ada, b_ada, w_in, b_forget, w_cmp_k1, w_cmp_k2,
           w_cmp_v1, w_cmp_v2, pe_cmp_k, pe_cmp_v, w_out):
    cos128, sin128 = _rope_slabs(x.shape[1])
    c8 = jnp.pad(c, ((0, SUBLANES - c.shape[0]), (0, 0)))
    for layer in range(g_pre.shape[0]):
        x = _layer(x, c8, g_pre[layer], g_post[layer], w_ada[layer], b_ada[layer], w_in[layer],
                   b_forget[layer], w_cmp_k1[layer], w_cmp_k2[layer], w_cmp_v1[layer],
                   w_cmp_v2[layer], pe_cmp_k[layer], pe_cmp_v[layer], w_out[layer], cos128, sin128)
    return x
```

```python
import functools

import jax
import jax.numpy as jnp
import numpy as np
from jax import lax
from jax.experimental import pallas as pl
from jax.experimental.pallas import tpu as pltpu

F32 = jnp.float32
BF16 = jnp.bfloat16

D_MODEL = 1024
HEAD_DIM = 64
FOX_WIDTH = 512
NSA_WIDTH = 512
FOX_HEADS = 8
NSA_HEADS = 8
NSA_KV_HEADS = 2
NSA_GROUP = 4
NSA_KV_WIDTH = 128
CMP_BLOCK = 32
CMP_STRIDE = 16
CMP_HIDDEN = 128
SLC_BLOCK = 64
N_SELECT = 16
WINDOW = 512
ROPE_THETA = 10000.0
RMS_EPS = 1e-6
LOG2E = 1.4426950408889634
QK_SCALE = HEAD_DIM ** -0.5 * LOG2E

LANES = 128
SUBLANES = 8
V7X_VMEM_BYTES = 64 * 1024 * 1024
MASK_VALUE = -1e30

PROJ_ROWS = 512
FOX_TILE = 512
NSA_Q = 128
SEL_KV = 512
WIN_KEYS = WINDOW + NSA_Q
FLASH_UNROLL = 4
FOX_AUG = 6

C_FQ, C_FK, C_FV, C_FZ, C_NQ = 0, 512, 1024, 1536, 2048
C_KC, C_VC, C_KS, C_VS, C_KW, C_VW = 2560, 2688, 2816, 2944, 3072, 3200
C_NZ, C_MISC, PROJ_COLS = 3328, 3840, 3968
MISC_GATE0 = FOX_HEADS


def _vmem_limit(pipelined_bytes, resident_bytes):
    need = 2 * pipelined_bytes + resident_bytes
    return int(min(max(need, 16 * 1024 * 1024), V7X_VMEM_BYTES - 8 * 1024 * 1024))


def _sigmoid(v):
    return 1.0 / (1.0 + jnp.exp(-v))


def _silu(v):
    return v * _sigmoid(v)


def _dot_nt(a, b):
    return lax.dot_general(a, b, (((1,), (1,)), ((), ())), preferred_element_type=F32)


def _split3(v):
    hi = v.astype(BF16)
    r1 = v - hi.astype(F32)
    mid = r1.astype(BF16)
    lo = (r1 - mid.astype(F32)).astype(BF16)
    return hi, mid, lo


def _ada_kernel(c_ref, w_ref, b_ref, o_ref):
    a = _silu(c_ref[...])
    o_ref[...] = jnp.dot(a, w_ref[...], precision=lax.Precision.HIGHEST,
                         preferred_element_type=F32) + b_ref[...]


def _ada(c8, w_ada, b_ada):
    n = w_ada.shape[1]
    blk = D_MODEL
    return pl.pallas_call(
        _ada_kernel,
        grid=(n // blk,),
        in_specs=[pl.BlockSpec((SUBLANES, D_MODEL), lambda j: (0, 0)),
                  pl.BlockSpec((D_MODEL, blk), lambda j: (0, j)),
                  pl.BlockSpec((1, blk), lambda j: (0, j))],
        out_specs=pl.BlockSpec((SUBLANES, blk), lambda j: (0, j)),
        out_shape=jax.ShapeDtypeStruct((SUBLANES, n), F32),
        compiler_params=pltpu.CompilerParams(
            dimension_semantics=("arbitrary",),
            vmem_limit_bytes=_vmem_limit(D_MODEL * blk * 4, 4 * 1024 * 1024)),
        name="ada",
    )(c8, w_ada, b_ada.reshape(1, n))


def _rope128(t, cos, sin_signed):
    lane = lax.broadcasted_iota(jnp.int32, t.shape, 1)
    first_half = (lane & (HEAD_DIM - 1)) < HEAD_DIM // 2
    partner = jnp.where(first_half,
                        pltpu.roll(t, LANES - HEAD_DIM // 2, 1),
                        pltpu.roll(t, HEAD_DIM // 2, 1))
    return t * cos + partner * sin_signed


def _proj_kernel(x_ref, mod_ref, g_ref, w_ref, cos_ref, sin_ref, bf_ref,
                 fq_ref, fk_ref, fv_ref, fz_ref, nq_ref, kc_ref, vc_ref,
                 ks_ref, vs_ref, kw_ref, vw_ref, nz_ref, misc_ref, lf_ref):
    x = x_ref[0]
    y = x * lax.rsqrt(jnp.mean(x * x, axis=-1, keepdims=True) + RMS_EPS)
    y = y * g_ref[...]
    h = (y * (1.0 + mod_ref[0, 1:2, :]) + mod_ref[0, 0:1, :]).astype(BF16)
    cos = cos_ref[...]
    sin = sin_ref[...]

    def mm(lo, n):
        return jnp.dot(h, w_ref[:, lo:lo + n], preferred_element_type=F32)

    fq_ref[0] = (mm(C_FQ, FOX_WIDTH) * QK_SCALE).astype(BF16)
    fk_ref[0] = mm(C_FK, FOX_WIDTH).astype(BF16)
    fv_ref[0] = mm(C_FV, FOX_WIDTH).astype(BF16)
    fz_ref[0] = _silu(mm(C_FZ, FOX_WIDTH)).astype(BF16)
    for j in range(NSA_WIDTH // LANES):
        t = mm(C_NQ + j * LANES, LANES)
        nq_ref[0, :, j * LANES:(j + 1) * LANES] = (_rope128(t, cos, sin) * QK_SCALE).astype(BF16)
    kc_ref[0] = _rope128(mm(C_KC, LANES), cos, sin)
    vc_ref[0] = mm(C_VC, LANES)
    ks_ref[0] = _rope128(mm(C_KS, LANES), cos, sin).astype(BF16)
    vs_ref[0] = mm(C_VS, LANES).astype(BF16)
    kw_ref[0] = _rope128(mm(C_KW, LANES), cos, sin).astype(BF16)
    vw_ref[0] = mm(C_VW, LANES).astype(BF16)
    nz_ref[0] = _silu(mm(C_NZ, NSA_WIDTH)).astype(BF16)
    misc = mm(C_MISC, LANES)
    misc_ref[0] = misc
    z = misc + bf_ref[...]
    log_f = jnp.minimum(z, 0.0) - jnp.log1p(jnp.exp(-jnp.abs(z)))
    lf_ref[0] = log_f.T[0:FOX_HEADS, :]


def _proj(x, mod, g_pre, w_cat, cos128, sin128, bf128):
    b, s, d = x.shape
    tm = PROJ_ROWS
    row = lambda bi, i: (bi, i, 0)
    wide = lambda dt: jax.ShapeDtypeStruct((b, s, FOX_WIDTH), dt)
    slab = lambda dt: jax.ShapeDtypeStruct((b, s, LANES), dt)
    out_shape = (wide(BF16), wide(BF16), wide(BF16), wide(BF16), wide(BF16),
                 slab(F32), slab(F32), slab(BF16), slab(BF16), slab(BF16), slab(BF16),
                 wide(BF16), slab(F32), jax.ShapeDtypeStruct((b, FOX_HEADS, s), F32))
    wide_spec = pl.BlockSpec((1, tm, FOX_WIDTH), row)
    slab_spec = pl.BlockSpec((1, tm, LANES), row)
    out_specs = (wide_spec,) * 5 + (slab_spec,) * 6 + (wide_spec, slab_spec,
                 pl.BlockSpec((1, FOX_HEADS, tm), lambda bi, i: (bi, 0, i)))
    pipelined = tm * d * 4 + tm * (6 * FOX_WIDTH * 2 + 4 * LANES * 2 + 3 * LANES * 4) + d * PROJ_COLS * 2
    return pl.pallas_call(
        _proj_kernel,
        grid=(b, s // tm),
        in_specs=[pl.BlockSpec((1, tm, d), row),
                  pl.BlockSpec((1, 3, d), lambda bi, i: (bi, 0, 0)),
                  pl.BlockSpec((1, d), lambda bi, i: (0, 0)),
                  pl.BlockSpec((d, PROJ_COLS), lambda bi, i: (0, 0)),
                  pl.BlockSpec((tm, LANES), lambda bi, i: (i, 0)),
                  pl.BlockSpec((tm, LANES), lambda bi, i: (i, 0)),
                  pl.BlockSpec((1, LANES), lambda bi, i: (0, 0))],
        out_specs=out_specs,
        out_shape=out_shape,
        compiler_params=pltpu.CompilerParams(
            dimension_semantics=("arbitrary", "arbitrary"),
            vmem_limit_bytes=_vmem_limit(pipelined, 8 * 1024 * 1024)),
        name="proj",
    )(x, mod, g_pre, w_cat, cos128, sin128, bf128)


def _cumsum_kernel(x_ref, hi_ref, mid_ref, lo_ref, *, chunks):
    x = x_ref[...]
    n = x.shape[0]
    parts = _split3(x)
    r = lax.broadcasted_iota(jnp.int32, (LANES, LANES), 0)
    c = lax.broadcasted_iota(jnp.int32, (LANES, LANES), 1)
    tri = (r <= c).astype(BF16)
    rr = lax.broadcasted_iota(jnp.int32, (n, n), 0)
    cc = lax.broadcasted_iota(jnp.int32, (n, n), 1)
    earlier = jnp.logical_and(cc < rr, (cc // chunks) == (rr // chunks)).astype(BF16)
    within = sum(jnp.dot(p, tri, preferred_element_type=F32) for p in parts)
    before = sum(jnp.dot(earlier, p, preferred_element_type=F32) for p in parts)
    total = (within + jnp.sum(before, axis=-1, keepdims=True)) * LOG2E
    hi_ref[...], mid_ref[...], lo_ref[...] = _split3(total)


def _cumsum_lanes_split(v):
    rows, s = v.shape
    chunks = s // LANES
    n = rows * chunks
    part = jax.ShapeDtypeStruct((n, LANES), BF16)
    parts = pl.pallas_call(
        functools.partial(_cumsum_kernel, chunks=chunks),
        out_shape=(part, part, part),
        compiler_params=pltpu.CompilerParams(
            vmem_limit_bytes=_vmem_limit(2 * n * LANES * 4, 6 * n * n)),
        name="cumsum",
    )(v.reshape(n, LANES))
    return tuple(p.reshape(rows, s) for p in parts)


def _flash_scratch(rows, tile, acc_lanes):
    return ([pltpu.VMEM((rows, LANES), F32), pltpu.VMEM((rows, acc_lanes), F32)]
            + [pltpu.VMEM((rows, tile), F32)] * 2
            + [pltpu.VMEM((rows, tile), BF16)] * 2
            + [pltpu.VMEM((rows, LANES), F32)] * 2)


def _flash_scratch_bytes(rows, tile, acc_lanes):
    return rows * (LANES * 4 + acc_lanes * 4 + 2 * tile * 4 + 2 * tile * 2 + 2 * LANES * 4)


def _causal_flash(n_full, tile, q_ref, load_k, load_v, diag_mask, scratch):
    m_ref, acc_ref, s0, s1, p0, p1, a0, a1 = scratch
    s_bufs, p_bufs, a_bufs = (s0, s1), (p0, p1), (a0, a1)
    reps = acc_ref.shape[-1] // LANES

    def logits(j):
        return _dot_nt(q_ref[...], load_k(pl.multiple_of(j * tile, tile)))

    def softmax(s):
        m_prev = m_ref[...]
        m_next = jnp.maximum(m_prev, jnp.max(s, axis=1, keepdims=True))
        m_ref[...] = m_next
        p = jnp.exp2(s - pltpu.repeat(m_next, tile // LANES, 1))
        return p.astype(BF16), jnp.exp2(m_prev - m_next)

    def accumulate(p, alpha, j):
        pv = jnp.dot(p, load_v(pl.multiple_of(j * tile, tile)), preferred_element_type=F32)
        acc_ref[...] = acc_ref[...] * (alpha if reps == 1 else pltpu.repeat(alpha, reps, 1)) + pv

    def stage(j, cur):
        s_bufs[1 - cur][...] = logits(j + 1)
        p_bufs[cur][...], a_bufs[cur][...] = softmax(s_bufs[cur][...])
        accumulate(p_bufs[1 - cur][...], a_bufs[1 - cur][...], jnp.maximum(j - 1, 0))

    def finish(cur):
        p, alpha = softmax(diag_mask(s_bufs[cur][...]))
        accumulate(p_bufs[1 - cur][...], a_bufs[1 - cur][...], jnp.maximum(n_full - 1, 0))
        accumulate(p, alpha, n_full)

    m_ref[...] = jnp.full(m_ref.shape, MASK_VALUE, F32)
    acc_ref[...] = jnp.zeros(acc_ref.shape, F32)
    s_bufs[0][...] = logits(0)
    p_bufs[1][...] = jnp.zeros(p_bufs[1].shape, BF16)
    a_bufs[1][...] = jnp.ones(a_bufs[1].shape, F32)

    def run_stages(first, count):
        for i in range(count):
            stage(first + i, i % 2)

    def unrolled(jj, carry):
        run_stages(FLASH_UNROLL * jj, FLASH_UNROLL)
        return carry

    lax.fori_loop(0, n_full // FLASH_UNROLL, unrolled, 0)
    rest = n_full % FLASH_UNROLL

    @pl.when(rest >= 2)
    def _():
        run_stages(n_full - rest, 2)

    @pl.when(rest % 2 == 1)
    def _():
        stage(n_full - 1, 0)
        finish(1)

    @pl.when(rest % 2 == 0)
    def _():
        finish(0)


def _fox_aug(cum_split):
    b, h, s = cum_split[0].shape
    terms = jnp.concatenate(list(cum_split) + [jnp.ones((b, 1, s), BF16)], axis=1)
    terms = jnp.transpose(terms, (0, 2, 1))
    place_q = np.zeros((3 * h + 1, h * HEAD_DIM), np.float32)
    place_k = np.zeros((3 * h + 1, h * HEAD_DIM), np.float32)
    half = FOX_AUG // 2
    for head in range(h):
        for part in range(half):
            place_q[part * h + head, head * HEAD_DIM + part] = 1.0
            place_q[3 * h, head * HEAD_DIM + half + part] = 1.0
            place_k[3 * h, head * HEAD_DIM + part] = 1.0
            place_k[part * h + head, head * HEAD_DIM + half + part] = -1.0
    scatter = lambda place: jnp.einsum("bsr,rl->bsl", terms, jnp.asarray(place, BF16),
                                       preferred_element_type=F32).astype(BF16)
    return scatter(place_q), scatter(place_k)


def _fox_kernel(q_ref, qa_ref, k_ref, ka_ref, v_ref, z_ref, o_ref, qm_ref, ones_ref, *flash):
    t = FOX_TILE
    qi = pl.program_id(2)
    q = q_ref[0]
    qa = qa_ref[0]
    lane = lax.broadcasted_iota(jnp.int32, q.shape, 1)
    for hh in range(2):
        r = slice(hh * t, (hh + 1) * t)
        own = (lane < HEAD_DIM) if hh == 0 else (lane >= HEAD_DIM)
        qm_ref[r, 0:LANES] = jnp.where(own, q, jnp.zeros_like(q))
        qm_ref[r, LANES:2 * LANES] = jnp.where(own, qa, jnp.zeros_like(qa))
    ones_lane = lax.broadcasted_iota(jnp.int32, ones_ref.shape, 1)
    ones_ref[...] = jnp.where(ones_lane == 0, 1.0, 0.0).astype(BF16)

    def load_k(k0):
        return jnp.concatenate([k_ref[0, pl.ds(k0, t), :], ka_ref[0, pl.ds(k0, t), :]], axis=1)

    def load_v(k0):
        return jnp.concatenate([v_ref[0, pl.ds(k0, t), :], ones_ref[...]], axis=1)

    def diag_mask(s):
        row = lax.broadcasted_iota(jnp.int32, s.shape, 0) & (t - 1)
        col = lax.broadcasted_iota(jnp.int32, s.shape, 1)
        return jnp.where(col <= row, s, MASK_VALUE)

    _causal_flash(qi, t, qm_ref, load_k, load_v, diag_mask, flash)
    acc_ref = flash[1]
    o0 = acc_ref[0:t, 0:LANES] * (1.0 / acc_ref[0:t, LANES:LANES + 1])
    o1 = acc_ref[t:2 * t, 0:LANES] * (1.0 / acc_ref[t:2 * t, LANES:LANES + 1])
    o = jnp.where(lane < HEAD_DIM, o0, o1)
    o_ref[0] = (o * z_ref[0].astype(F32)).astype(BF16)


def _fox(fq, fk, fv, cum_split, fz):
    b, s, w = fq.shape
    t = FOX_TILE
    pairs = w // LANES
    qa, ka = _fox_aug(cum_split)
    tile = pl.BlockSpec((1, t, LANES), lambda bi, hp, i: (bi, i, hp))
    full = pl.BlockSpec((1, s, LANES), lambda bi, hp, i: (bi, 0, hp))
    resident = 3 * s * LANES * 2
    scratch = 2 * t * 2 * LANES * 2 + t * LANES * 2 + _flash_scratch_bytes(2 * t, t, 2 * LANES)
    return pl.pallas_call(
        _fox_kernel,
        grid=(b, pairs, s // t),
        in_specs=[tile, tile, full, full, full, tile],
        out_specs=tile,
        out_shape=jax.ShapeDtypeStruct((b, s, w), BF16),
        scratch_shapes=[pltpu.VMEM((2 * t, 2 * LANES), BF16),
                        pltpu.VMEM((t, LANES), BF16)] + _flash_scratch(2 * t, t, 2 * LANES),
        compiler_params=pltpu.CompilerParams(
            dimension_semantics=("arbitrary", "arbitrary", "arbitrary"),
            vmem_limit_bytes=_vmem_limit(resident + 4 * t * LANES * 2, scratch + 4 * 2 * t * t * 4)),
        name="fox",
    )(fq, qa, fk, ka, fv, fz)


def _compress_kernel(t_ref, w1_ref, w2_ref, pe_ref, o_ref):
    hp = lax.Precision.HIGHEST
    half = CMP_STRIDE * HEAD_DIM
    tt = t_ref[0, 0, 0]
    n = tt.shape[0]
    first = jnp.dot(tt, w1_ref[0, 0:half, :], precision=hp, preferred_element_type=F32)
    second = jnp.dot(tt, w1_ref[0, half:2 * half, :], precision=hp, preferred_element_type=F32)
    pe_term = jnp.dot(pe_ref[0], w1_ref[0], precision=hp, preferred_element_type=F32)[0:1, :]
    hidden = first + pltpu.roll(second, n - 1, 0) + pe_term
    o_ref[0, 0, 0] = jnp.dot(_silu(hidden), w2_ref[0], precision=hp, preferred_element_type=F32)


def _compress(t, w1, w2, pe):
    kinds, b, g, n, flat = t.shape
    return pl.pallas_call(
        _compress_kernel,
        grid=(kinds, b, g),
        in_specs=[pl.BlockSpec((1, 1, 1, n, flat), lambda a, bi, gi: (a, bi, gi, 0, 0)),
                  pl.BlockSpec((1, 2 * flat, CMP_HIDDEN), lambda a, bi, gi: (a, 0, 0)),
                  pl.BlockSpec((1, CMP_HIDDEN, HEAD_DIM), lambda a, bi, gi: (a, 0, 0)),
                  pl.BlockSpec((1, SUBLANES, 2 * flat), lambda a, bi, gi: (a, 0, 0))],
        out_specs=pl.BlockSpec((1, 1, 1, n, HEAD_DIM), lambda a, bi, gi: (a, bi, gi, 0, 0)),
        out_shape=jax.ShapeDtypeStruct((kinds, b, g, n, HEAD_DIM), F32),
        compiler_params=pltpu.CompilerParams(
            dimension_semantics=("arbitrary", "arbitrary", "arbitrary"),
            vmem_limit_bytes=_vmem_limit(n * flat * 4 + 2 * flat * CMP_HIDDEN * 4, 8 * 1024 * 1024)),
        name="compress",
    )(t, w1, w2, pe)


def _softmax_numerator(s, bias):
    s = s + bias
    m = jnp.max(s, axis=1, keepdims=True)
    m = jnp.where(m == -jnp.inf, 0.0, m)
    return jnp.exp2(s - m)


def _stack_heads(q):
    return jnp.concatenate([q[:, h * HEAD_DIM:(h + 1) * HEAD_DIM] for h in range(NSA_GROUP)], axis=0)


def _group_gate(gates, gi, h, j):
    c0 = MISC_GATE0 + 3 * h + j
    c1 = MISC_GATE0 + 3 * (NSA_GROUP + h) + j
    return jnp.where(gi == 0, gates[:, c0:c0 + 1], gates[:, c1:c1 + 1])


def _cmpwin_kernel(q_ref, kc_ref, vc_ref, kw_ref, vw_ref, misc_ref, part_ref, selb_ref):
    tq = NSA_Q
    gi = pl.program_id(1)
    qi = pl.program_id(2)
    q0 = pl.multiple_of(qi * tq, tq)
    q4 = _stack_heads(q_ref[0])
    n_cmp = kc_ref.shape[2]
    head_rows = [slice(h * tq, (h + 1) * tq) for h in range(NSA_GROUP)]

    kc = kc_ref[0, 0]
    kc_hi = kc.astype(BF16)
    kc_lo = (kc - kc_hi.astype(F32)).astype(BF16)
    s = _dot_nt(q4, kc_hi) + _dot_nt(q4, kc_lo)
    col = lax.broadcasted_iota(jnp.int32, (tq, n_cmp), 1)
    t_row = q0 + lax.broadcasted_iota(jnp.int32, (tq, n_cmp), 0)
    cmp_bias = jnp.where(col * CMP_STRIDE + (CMP_BLOCK - 1) <= t_row, 0.0, -jnp.inf)
    pcs = []
    for r in head_rows:
        p = _softmax_numerator(s[r], cmp_bias)
        pcs.append(p * (1.0 / jnp.maximum(jnp.sum(p, axis=1, keepdims=True), 1e-30)))
    oc = jnp.dot(jnp.concatenate(pcs, axis=0).astype(BF16), vc_ref[0, 0].astype(BF16),
                 preferred_element_type=F32)

    pc_sum = pcs[0] + pcs[1] + pcs[2] + pcs[3]
    n_blk = selb_ref.shape[3]
    ratio = SLC_BLOCK // CMP_STRIDE
    jj = lax.broadcasted_iota(jnp.int32, (n_blk, n_cmp), 0)
    ii = lax.broadcasted_iota(jnp.int32, (n_blk, n_cmp), 1)
    lo_i = jj * ratio - (CMP_BLOCK // CMP_STRIDE - 1)
    overlap = jnp.logical_and(ii >= lo_i, ii < lo_i + (SLC_BLOCK + CMP_BLOCK) // CMP_STRIDE - 1)
    overlap = jnp.logical_and(overlap, ii < n_cmp - 1).astype(BF16)
    imp = sum(_dot_nt(overlap, p) for p in _split3(pc_sum))
    blk = lax.broadcasted_iota(jnp.int32, (n_blk, tq), 0)
    cur = (q0 + lax.broadcasted_iota(jnp.int32, (n_blk, tq), 1)) // SLC_BLOCK
    forced = jnp.logical_or(blk == 0, jnp.logical_or(blk == cur, blk == cur - 1))
    imp = jnp.where(forced, jnp.inf, jnp.where(blk > cur, -jnp.inf, imp))

    def pick_one(_, carry):
        rem, sel = carry
        best = jnp.max(rem, axis=0, keepdims=True)
        first = jnp.min(jnp.where(rem == best, blk, n_blk), axis=0, keepdims=True)
        hit = blk == first
        return jnp.where(hit, -jnp.inf, rem), jnp.where(hit, 1.0, sel)

    _, sel = lax.fori_loop(0, min(N_SELECT, n_blk), pick_one, (imp, jnp.zeros_like(imp)),
                           unroll=True)
    selb_ref[0, 0] = jnp.where(sel.T > 0.5, 0.0, MASK_VALUE).astype(BF16)

    start = pl.multiple_of(jnp.maximum(q0 - WINDOW, 0), tq)
    kw = kw_ref[0, 0, pl.ds(start, WIN_KEYS), :]
    vw = vw_ref[0, 0, pl.ds(start, WIN_KEYS), :]
    sw = _dot_nt(q4, kw)
    pos = start + lax.broadcasted_iota(jnp.int32, (tq, WIN_KEYS), 1)
    dist = q0 + lax.broadcasted_iota(jnp.int32, (tq, WIN_KEYS), 0) - pos
    win_bias = jnp.where(jnp.logical_and(dist >= 0, dist < WINDOW), 0.0, -jnp.inf)
    pw = jnp.concatenate([_softmax_numerator(sw[r], win_bias) for r in head_rows], axis=0)
    ow = jnp.dot(pw.astype(BF16), vw, preferred_element_type=F32)
    ow = ow[:, 0:HEAD_DIM] * (1.0 / jnp.maximum(ow[:, HEAD_DIM:HEAD_DIM + 1], 1e-30))

    gates = _sigmoid(misc_ref[0])
    heads = []
    for h, r in enumerate(head_rows):
        heads.append(_group_gate(gates, gi, h, 0) * oc[r] + _group_gate(gates, gi, h, 2) * ow[r])
    part_ref[0] = jnp.concatenate(heads, axis=1)


def _cmpwin(nq, kc_c, vc_c, kw, vw_ones, misc):
    b, s, w = nq.shape
    g = NSA_KV_HEADS
    tq = NSA_Q
    gw = NSA_GROUP * HEAD_DIM
    n_cmp = kc_c.shape[2]
    n_blk = s // SLC_BLOCK
    q_spec = pl.BlockSpec((1, tq, gw), lambda bi, gi, i: (bi, i, gi))
    cmp_spec = pl.BlockSpec((1, 1, n_cmp, HEAD_DIM), lambda bi, gi, i: (bi, gi, 0, 0))
    resident = 2 * s * LANES * 2 + 2 * n_cmp * LANES * 4
    rows = NSA_GROUP * tq
    return pl.pallas_call(
        _cmpwin_kernel,
        grid=(b, g, s // tq),
        in_specs=[q_spec, cmp_spec, cmp_spec,
                  pl.BlockSpec((1, 1, s, HEAD_DIM), lambda bi, gi, i: (bi, gi, 0, 0)),
                  pl.BlockSpec((1, 1, s, LANES), lambda bi, gi, i: (bi, gi, 0, 0)),
                  pl.BlockSpec((1, tq, LANES), lambda bi, gi, i: (bi, i, 0))],
        out_specs=(q_spec, pl.BlockSpec((1, 1, tq, n_blk), lambda bi, gi, i: (bi, gi, i, 0))),
        out_shape=(jax.ShapeDtypeStruct((b, s, w), F32),
                   jax.ShapeDtypeStruct((b, g, s, n_blk), BF16)),
        compiler_params=pltpu.CompilerParams(
            dimension_semantics=("arbitrary", "arbitrary", "arbitrary"),
            vmem_limit_bytes=_vmem_limit(resident + tq * gw * 8, 12 * rows * WIN_KEYS * 4)),
        name="cmpwin",
    )(nq, kc_c, vc_c, kw, vw_ones, misc)


def _sel_kernel(q_ref, k_ref, v_ref, selb_ref, part_ref, misc_ref, z_ref, o_ref,
                qa_ref, *flash):
    tq = NSA_Q
    gi = pl.program_id(1)
    qi = pl.program_id(2)
    q0 = pl.multiple_of(qi * tq, tq)
    q = q_ref[0]
    selb = selb_ref[0, 0]
    lane = lax.broadcasted_iota(jnp.int32, (tq, LANES), 1)
    for h in range(NSA_GROUP):
        r = slice(h * tq, (h + 1) * tq)
        slab = q[:, (h // 2) * LANES:(h // 2 + 1) * LANES]
        own = (lane < HEAD_DIM) if h % 2 == 0 else (lane >= HEAD_DIM)
        qa_ref[r, 0:LANES] = jnp.where(own, slab, jnp.zeros_like(slab))
        qa_ref[r, LANES:2 * LANES] = selb

    n_full = q0 // SEL_KV

    def diag_mask(s):
        t_row = q0 + (lax.broadcasted_iota(jnp.int32, s.shape, 0) & (tq - 1))
        pos = n_full * SEL_KV + lax.broadcasted_iota(jnp.int32, s.shape, 1)
        return jnp.where(pos <= t_row, s, MASK_VALUE)

    _causal_flash(n_full, SEL_KV, qa_ref,
                  lambda k0: k_ref[0, 0, pl.ds(k0, SEL_KV), :],
                  lambda k0: v_ref[0, 0, pl.ds(k0, SEL_KV), :],
                  diag_mask, flash)
    acc_ref = flash[1]

    gates = _sigmoid(misc_ref[0])
    part = part_ref[0]
    heads = []
    for h in range(NSA_GROUP):
        r = slice(h * tq, (h + 1) * tq)
        o_sel = acc_ref[r, 0:HEAD_DIM] * (1.0 / acc_ref[r, HEAD_DIM:HEAD_DIM + 1])
        heads.append(part[:, h * HEAD_DIM:(h + 1) * HEAD_DIM] + _group_gate(gates, gi, h, 1) * o_sel)
    o_ref[0] = (jnp.concatenate(heads, axis=1) * z_ref[0].astype(F32)).astype(BF16)


def _sel(nq, k_aug, v_ones, selb, part, misc, nz):
    b, s, w = nq.shape
    g = NSA_KV_HEADS
    tq = NSA_Q
    gw = NSA_GROUP * HEAD_DIM
    n_blk = s // SLC_BLOCK
    rows = NSA_GROUP * tq
    q_spec = pl.BlockSpec((1, tq, gw), lambda bi, gi, i: (bi, i, gi))
    resident = s * 3 * LANES * 2
    return pl.pallas_call(
        _sel_kernel,
        grid=(b, g, s // tq),
        in_specs=[q_spec,
                  pl.BlockSpec((1, 1, s, 2 * LANES), lambda bi, gi, i: (bi, gi, 0, 0)),
                  pl.BlockSpec((1, 1, s, LANES), lambda bi, gi, i: (bi, gi, 0, 0)),
                  pl.BlockSpec((1, 1, tq, n_blk), lambda bi, gi, i: (bi, gi, i, 0)),
                  q_spec,
                  pl.BlockSpec((1, tq, LANES), lambda bi, gi, i: (bi, i, 0)),
                  q_spec],
        out_specs=q_spec,
        out_shape=jax.ShapeDtypeStruct((b, s, w), BF16),
        scratch_shapes=[pltpu.VMEM((rows, 2 * LANES), BF16)] + _flash_scratch(rows, SEL_KV, LANES),
        compiler_params=pltpu.CompilerParams(
            dimension_semantics=("arbitrary", "arbitrary", "arbitrary"),
            vmem_limit_bytes=_vmem_limit(resident + tq * gw * 10,
                                         _flash_scratch_bytes(rows, SEL_KV, LANES) + 4 * rows * SEL_KV * 4)),
        name="sel",
    )(nq, k_aug, v_ones, selb, part, misc, nz)


def _out_kernel(yf_ref, yn_ref, w_ref, g_ref, mod_ref, x_ref, o_ref):
    y = (jnp.dot(yf_ref[0], w_ref[0:FOX_WIDTH, :], preferred_element_type=F32)
         + jnp.dot(yn_ref[0], w_ref[FOX_WIDTH:, :], preferred_element_type=F32))
    yn = y * lax.rsqrt(jnp.mean(y * y, axis=-1, keepdims=True) + RMS_EPS)
    o_ref[0] = x_ref[0] + mod_ref[0, 2:3, :] * (yn * g_ref[...])


def _out(y_fox, y_nsa, w_out, g_post, mod, x):
    b, s, d = x.shape
    tm = PROJ_ROWS
    row = lambda bi, i: (bi, i, 0)
    half = pl.BlockSpec((1, tm, FOX_WIDTH), row)
    pipelined = 2 * tm * FOX_WIDTH * 2 + 2 * tm * d * 4 + d * d * 2
    return pl.pallas_call(
        _out_kernel,
        grid=(b, s // tm),
        in_specs=[half, half,
                  pl.BlockSpec((d, d), lambda bi, i: (0, 0)),
                  pl.BlockSpec((1, d), lambda bi, i: (0, 0)),
                  pl.BlockSpec((1, 3, d), lambda bi, i: (bi, 0, 0)),
                  pl.BlockSpec((1, tm, d), row)],
        out_specs=pl.BlockSpec((1, tm, d), row),
        out_shape=jax.ShapeDtypeStruct((b, s, d), F32),
        compiler_params=pltpu.CompilerParams(
            dimension_semantics=("arbitrary", "arbitrary"),
            vmem_limit_bytes=_vmem_limit(pipelined, 6 * tm * d * 4)),
        name="out",
    )(y_fox, y_nsa, w_out, g_post, mod, x)


def _rope_slabs(seq_len):
    inv = 1.0 / (ROPE_THETA ** (jnp.arange(0, HEAD_DIM, 2, dtype=F32) / HEAD_DIM))
    ang = jnp.arange(seq_len, dtype=F32)[:, None] * inv[None, :]
    cos, sin = jnp.cos(ang), jnp.sin(ang)
    reps = LANES // (HEAD_DIM // 2)
    sign = jnp.tile(jnp.concatenate([-jnp.ones((HEAD_DIM // 2,), F32), jnp.ones((HEAD_DIM // 2,), F32)]),
                    LANES // HEAD_DIM)
    return jnp.tile(cos, (1, reps)), jnp.tile(sin, (1, reps)) * sign[None, :]


def _reorder_w_in(w_in):
    fw, kv = FOX_WIDTH, NSA_KV_WIDTH
    o = 0
    cols = {}
    for name, n in (("fq", fw), ("fk", fw), ("fv", fw), ("ff", FOX_HEADS), ("fz", fw), ("nq", NSA_WIDTH),
                    ("kc", kv), ("vc", kv), ("ks", kv), ("vs", kv), ("kw", kv), ("vw", kv),
                    ("ng", 3 * NSA_HEADS), ("nz", NSA_WIDTH)):
        cols[name] = w_in[:, o:o + n]
        o += n
    pad = jnp.zeros((w_in.shape[0], LANES - FOX_HEADS - 3 * NSA_HEADS), w_in.dtype)
    order = ("fq", "fk", "fv", "fz", "nq", "kc", "vc", "ks", "vs", "kw", "vw", "nz", "ff", "ng")
    return jnp.concatenate([cols[k] for k in order] + [pad], axis=1).astype(BF16)


def _group_major(t):
    b, s, _ = t.shape
    return t.reshape(b, s, NSA_KV_HEADS, HEAD_DIM).transpose(0, 2, 1, 3)


def _with_ones_column(v):
    lead = v.shape[:-1]
    return jnp.concatenate([v, jnp.ones(lead + (1,), v.dtype),
                            jnp.zeros(lead + (LANES - HEAD_DIM - 1,), v.dtype)], axis=-1)


def _with_block_onehot(k):
    s = k.shape[2]
    n_blk = s // SLC_BLOCK
    onehot = (jnp.arange(s)[:, None] // SLC_BLOCK == jnp.arange(n_blk)[None, :]).astype(k.dtype)
    return jnp.concatenate([k, k, jnp.broadcast_to(onehot, k.shape[:2] + onehot.shape)], axis=-1)


def _layer(x, c8, g_pre, g_post, w_ada, b_ada, w_in, b_forget, w_cmp_k1, w_cmp_k2,
           w_cmp_v1, w_cmp_v2, pe_cmp_k, pe_cmp_v, w_out, cos128, sin128):
    b, s, d = x.shape
    mod = _ada(c8, w_ada, b_ada)[:b].reshape(b, 3, d)
    bf128 = jnp.pad(b_forget, (0, LANES - FOX_HEADS)).reshape(1, LANES)
    (fq, fk, fv, fz, nq, kc, vc, ks, vs, kw, vw, nz, misc, log_f) = _proj(
        x, mod, g_pre.reshape(1, d), _reorder_w_in(w_in), cos128, sin128, bf128)

    cum_split = tuple(p.reshape(b, FOX_HEADS, s)
                      for p in _cumsum_lanes_split(log_f.reshape(b * FOX_HEADS, s)))
    y_fox = _fox(fq, fk, fv, cum_split, fz)

    flat = CMP_STRIDE * HEAD_DIM
    t = jnp.stack([_group_major(kc), _group_major(vc)]).reshape(2, b, NSA_KV_HEADS, s // CMP_STRIDE, flat)
    pe = jnp.stack([pe_cmp_k.reshape(1, 2 * flat), pe_cmp_v.reshape(1, 2 * flat)])
    cmp = _compress(t, jnp.stack([w_cmp_k1, w_cmp_v1]), jnp.stack([w_cmp_k2, w_cmp_v2]),
                    jnp.broadcast_to(pe, (2, SUBLANES, 2 * flat)))
    part, selb = _cmpwin(nq, cmp[0], cmp[1], _group_major(kw), _with_ones_column(_group_major(vw)), misc)
    y_nsa = _sel(nq, _with_block_onehot(_group_major(ks)), _with_ones_column(_group_major(vs)),
                 selb, part, misc, nz)

    return _out(y_fox, y_nsa, w_out.astype(BF16), g_post.reshape(1, d), mod, x)


def kernel(x, c, g_pre, g_post, w_ada, b_ada, w_in, b_forget, w_cmp_k1, w_cmp_k2,
           w_cmp_v1, w_cmp_v2, pe_cmp_k, pe_cmp_v, w_out):
    cos128, sin128 = _rope_slabs(x.shape[1])
    c8 = jnp.pad(c, ((0, SUBLANES - c.shape[0]), (0, 0)))
    for layer in range(g_pre.shape[0]):
        x = _layer(x, c8, g_pre[layer], g_post[layer], w_ada[layer], b_ada[layer], w_in[layer],
                   b_forget[layer], w_cmp_k1[layer], w_cmp_k2[layer], w_cmp_v1[layer],
                   w_cmp_v2[layer], pe_cmp_k[layer], pe_cmp_v[layer], w_out[layer], cos128, sin128)
    return x
```

```python
import functools

import jax
import jax.numpy as jnp
import numpy as np
from jax import lax
from jax.experimental import pallas as pl
from jax.experimental.pallas import tpu as pltpu

F32 = jnp.float32
BF16 = jnp.bfloat16

D_MODEL = 1024
HEAD_DIM = 64
FOX_WIDTH = 512
NSA_WIDTH = 512
FOX_HEADS = 8
NSA_HEADS = 8
NSA_KV_HEADS = 2
NSA_GROUP = 4
NSA_KV_WIDTH = 128
CMP_BLOCK = 32
CMP_STRIDE = 16
CMP_HIDDEN = 128
SLC_BLOCK = 64
N_SELECT = 16
WINDOW = 512
ROPE_THETA = 10000.0
RMS_EPS = 1e-6
LOG2E = 1.4426950408889634
QK_SCALE = HEAD_DIM ** -0.5 * LOG2E

LANES = 128
SUBLANES = 8
V7X_VMEM_BYTES = 64 * 1024 * 1024
MASK_VALUE = -1e30

PROJ_ROWS = 512
FOX_TILE = 512
NSA_Q = 128
SEL_KV = 512
WIN_KEYS = WINDOW + NSA_Q
FLASH_UNROLL = 4
FOX_AUG = 6

C_FQ, C_FK, C_FV, C_FZ, C_NQ = 0, 512, 1024, 1536, 2048
C_KC, C_VC, C_KS, C_VS, C_KW, C_VW = 2560, 2688, 2816, 2944, 3072, 3200
C_NZ, C_MISC, PROJ_COLS = 3328, 3840, 3968
MISC_GATE0 = FOX_HEADS


def _vmem_limit(pipelined_bytes, resident_bytes):
    need = 2 * pipelined_bytes + resident_bytes
    return int(min(max(need, 16 * 1024 * 1024), V7X_VMEM_BYTES - 8 * 1024 * 1024))


def _sigmoid(v):
    return 1.0 / (1.0 + jnp.exp(-v))


def _silu(v):
    return v * _sigmoid(v)


def _dot_nt(a, b):
    return lax.dot_general(a, b, (((1,), (1,)), ((), ())), preferred_element_type=F32)


def _split3(v):
    hi = v.astype(BF16)
    r1 = v - hi.astype(F32)
    mid = r1.astype(BF16)
    lo = (r1 - mid.astype(F32)).astype(BF16)
    return hi, mid, lo


def _ada_kernel(c_ref, w_ref, b_ref, o_ref):
    a = _silu(c_ref[...])
    o_ref[...] = jnp.dot(a, w_ref[...], precision=lax.Precision.HIGHEST,
                         preferred_element_type=F32) + b_ref[...]


def _ada(c8, w_ada, b_ada):
    n = w_ada.shape[1]
    blk = D_MODEL
    return pl.pallas_call(
        _ada_kernel,
        grid=(n // blk,),
        in_specs=[pl.BlockSpec((SUBLANES, D_MODEL), lambda j: (0, 0)),
                  pl.BlockSpec((D_MODEL, blk), lambda j: (0, j)),
                  pl.BlockSpec((1, blk), lambda j: (0, j))],
        out_specs=pl.BlockSpec((SUBLANES, blk), lambda j: (0, j)),
        out_shape=jax.ShapeDtypeStruct((SUBLANES, n), F32),
        compiler_params=pltpu.CompilerParams(
            dimension_semantics=("arbitrary",),
            vmem_limit_bytes=_vmem_limit(D_MODEL * blk * 4, 4 * 1024 * 1024)),
        name="ada",
    )(c8, w_ada, b_ada.reshape(1, n))


def _rope128(t, cos, sin_signed):
    lane = lax.broadcasted_iota(jnp.int32, t.shape, 1)
    first_half = (lane & (HEAD_DIM - 1)) < HEAD_DIM // 2
    partner = jnp.where(first_half,
                        pltpu.roll(t, LANES - HEAD_DIM // 2, 1),
                        pltpu.roll(t, HEAD_DIM // 2, 1))
    return t * cos + partner * sin_signed


def _proj_kernel(x_ref, mod_ref, g_ref, w_ref, cos_ref, sin_ref, bf_ref,
                 fq_ref, fk_ref, fv_ref, fz_ref, nq_ref, cmp_ref,
                 ks_ref, vs_ref, kw_ref, vw_ref, nz_ref, misc_ref, lf_ref):
    tm = x_ref.shape[1]
    x = x_ref[0]
    y = x * lax.rsqrt(jnp.mean(x * x, axis=-1, keepdims=True) + RMS_EPS)
    y = y * g_ref[...]
    h = (y * (1.0 + mod_ref[0, 1:2, :]) + mod_ref[0, 0:1, :]).astype(BF16)
    cos = cos_ref[...]
    sin = sin_ref[...]

    def mm(lo, n):
        return jnp.dot(h, w_ref[:, lo:lo + n], preferred_element_type=F32)

    fq_ref[0] = (mm(C_FQ, FOX_WIDTH) * QK_SCALE).astype(BF16)
    fk_ref[0] = mm(C_FK, FOX_WIDTH).astype(BF16)
    fv_ref[0] = mm(C_FV, FOX_WIDTH).astype(BF16)
    fz_ref[0] = _silu(mm(C_FZ, FOX_WIDTH)).astype(BF16)
    for j in range(NSA_WIDTH // LANES):
        t = mm(C_NQ + j * LANES, LANES)
        nq_ref[0, :, j * LANES:(j + 1) * LANES] = (_rope128(t, cos, sin) * QK_SCALE).astype(BF16)
    lane = lax.broadcasted_iota(jnp.int32, (tm, LANES), 1)
    low = lane < HEAD_DIM
    ones_col = jnp.where(lane == HEAD_DIM, 1.0, 0.0).astype(BF16)

    def doubled(slab):
        swapped = pltpu.roll(slab, HEAD_DIM, 1)
        return jnp.where(low, slab, swapped), jnp.where(low, swapped, slab)

    def with_ones(slab):
        swapped = pltpu.roll(slab, HEAD_DIM, 1)
        return jnp.where(low, slab, ones_col), jnp.where(low, swapped, ones_col)

    for kind, slab in enumerate((_rope128(mm(C_KC, LANES), cos, sin), mm(C_VC, LANES))):
        for g in range(NSA_KV_HEADS):
            cmp_ref[kind, 0, g] = slab[:, g * HEAD_DIM:(g + 1) * HEAD_DIM]
    pos = pl.program_id(1) * tm + lax.broadcasted_iota(jnp.int32, (tm, LANES), 0)
    onehot = jnp.where(pos // SLC_BLOCK == lane, 1.0, 0.0).astype(BF16)
    for g, k2 in enumerate(doubled(_rope128(mm(C_KS, LANES), cos, sin).astype(BF16))):
        ks_ref[0, g, :, 0:LANES] = k2
        ks_ref[0, g, :, LANES:2 * LANES] = onehot
    vs_ref[0, 0], vs_ref[0, 1] = with_ones(mm(C_VS, LANES).astype(BF16))
    kw_ref[0, 0], kw_ref[0, 1] = doubled(_rope128(mm(C_KW, LANES), cos, sin).astype(BF16))
    vw_ref[0, 0], vw_ref[0, 1] = with_ones(mm(C_VW, LANES).astype(BF16))
    nz_ref[0] = _silu(mm(C_NZ, NSA_WIDTH)).astype(BF16)
    misc = mm(C_MISC, LANES)
    misc_ref[0] = misc
    z = misc + bf_ref[...]
    log_f = jnp.minimum(z, 0.0) - jnp.log1p(jnp.exp(-jnp.abs(z)))
    lf_ref[0] = log_f.T[0:FOX_HEADS, :]


def _proj(x, mod, g_pre, w_cat, cos128, sin128, bf128):
    b, s, d = x.shape
    tm = PROJ_ROWS
    row = lambda bi, i: (bi, i, 0)
    wide = lambda dt: jax.ShapeDtypeStruct((b, s, FOX_WIDTH), dt)
    g = NSA_KV_HEADS
    grouped = lambda lanes: jax.ShapeDtypeStruct((b, g, s, lanes), BF16)
    grouped_spec = lambda lanes: pl.BlockSpec((1, g, tm, lanes), lambda bi, i: (bi, 0, i, 0))
    out_shape = (wide(BF16), wide(BF16), wide(BF16), wide(BF16), wide(BF16),
                 jax.ShapeDtypeStruct((2, b, g, s, HEAD_DIM), F32),
                 grouped(2 * LANES), grouped(LANES), grouped(LANES), grouped(LANES),
                 wide(BF16), jax.ShapeDtypeStruct((b, s, LANES), F32),
                 jax.ShapeDtypeStruct((b, FOX_HEADS, s), F32))
    wide_spec = pl.BlockSpec((1, tm, FOX_WIDTH), row)
    out_specs = (wide_spec,) * 5 + (
        pl.BlockSpec((2, 1, g, tm, HEAD_DIM), lambda bi, i: (0, bi, 0, i, 0)),
        grouped_spec(2 * LANES), grouped_spec(LANES), grouped_spec(LANES), grouped_spec(LANES),
        wide_spec, pl.BlockSpec((1, tm, LANES), row),
        pl.BlockSpec((1, FOX_HEADS, tm), lambda bi, i: (bi, 0, i)))
    pipelined = (tm * d * 4 + tm * (6 * FOX_WIDTH * 2 + 2 * g * LANES * 4 + 5 * g * LANES * 2 + LANES * 4)
                 + d * PROJ_COLS * 2)
    return pl.pallas_call(
        _proj_kernel,
        grid=(b, s // tm),
        in_specs=[pl.BlockSpec((1, tm, d), row),
                  pl.BlockSpec((1, 3, d), lambda bi, i: (bi, 0, 0)),
                  pl.BlockSpec((1, d), lambda bi, i: (0, 0)),
                  pl.BlockSpec((d, PROJ_COLS), lambda bi, i: (0, 0)),
                  pl.BlockSpec((tm, LANES), lambda bi, i: (i, 0)),
                  pl.BlockSpec((tm, LANES), lambda bi, i: (i, 0)),
                  pl.BlockSpec((1, LANES), lambda bi, i: (0, 0))],
        out_specs=out_specs,
        out_shape=out_shape,
        compiler_params=pltpu.CompilerParams(
            dimension_semantics=("arbitrary", "arbitrary"),
            vmem_limit_bytes=_vmem_limit(pipelined, 8 * 1024 * 1024)),
        name="proj",
    )(x, mod, g_pre, w_cat, cos128, sin128, bf128)


def _cumsum_kernel(x_ref, hi_ref, mid_ref, lo_ref, *, chunks):
    x = x_ref[...]
    n = x.shape[0]
    parts = _split3(x)
    r = lax.broadcasted_iota(jnp.int32, (LANES, LANES), 0)
    c = lax.broadcasted_iota(jnp.int32, (LANES, LANES), 1)
    tri = (r <= c).astype(BF16)
    rr = lax.broadcasted_iota(jnp.int32, (n, n), 0)
    cc = lax.broadcasted_iota(jnp.int32, (n, n), 1)
    earlier = jnp.logical_and(cc < rr, (cc // chunks) == (rr // chunks)).astype(BF16)
    within = sum(jnp.dot(p, tri, preferred_element_type=F32) for p in parts)
    before = sum(jnp.dot(earlier, p, preferred_element_type=F32) for p in parts)
    total = (within + jnp.sum(before, axis=-1, keepdims=True)) * LOG2E
    hi_ref[...], mid_ref[...], lo_ref[...] = _split3(total)


def _cumsum_lanes_split(v):
    rows, s = v.shape
    chunks = s // LANES
    n = rows * chunks
    part = jax.ShapeDtypeStruct((n, LANES), BF16)
    parts = pl.pallas_call(
        functools.partial(_cumsum_kernel, chunks=chunks),
        out_shape=(part, part, part),
        compiler_params=pltpu.CompilerParams(
            vmem_limit_bytes=_vmem_limit(2 * n * LANES * 4, 6 * n * n)),
        name="cumsum",
    )(v.reshape(n, LANES))
    return tuple(p.reshape(rows, s) for p in parts)


def _flash_scratch(rows, tile, acc_lanes):
    return ([pltpu.VMEM((rows, LANES), F32), pltpu.VMEM((rows, acc_lanes), F32)]
            + [pltpu.VMEM((rows, tile), F32)] * 2
            + [pltpu.VMEM((rows, tile), BF16)] * 2
            + [pltpu.VMEM((rows, LANES), F32)] * 2)


def _flash_scratch_bytes(rows, tile, acc_lanes):
    return rows * (LANES * 4 + acc_lanes * 4 + 2 * tile * 4 + 2 * tile * 2 + 2 * LANES * 4)


def _causal_flash(n_full, tile, q_ref, load_k, load_v, diag_mask, scratch):
    m_ref, acc_ref, s0, s1, p0, p1, a0, a1 = scratch
    s_bufs, p_bufs, a_bufs = (s0, s1), (p0, p1), (a0, a1)
    reps = acc_ref.shape[-1] // LANES

    def logits(j):
        return _dot_nt(q_ref[...], load_k(pl.multiple_of(j * tile, tile)))

    def softmax(s):
        m_prev = m_ref[...]
        m_next = jnp.maximum(m_prev, jnp.max(s, axis=1, keepdims=True))
        m_ref[...] = m_next
        p = jnp.exp2(s - pltpu.repeat(m_next, tile // LANES, 1))
        return p.astype(BF16), jnp.exp2(m_prev - m_next)

    def accumulate(p, alpha, j):
        pv = jnp.dot(p, load_v(pl.multiple_of(j * tile, tile)), preferred_element_type=F32)
        acc_ref[...] = acc_ref[...] * (alpha if reps == 1 else pltpu.repeat(alpha, reps, 1)) + pv

    def stage(j, cur):
        s_bufs[1 - cur][...] = logits(j + 1)
        p_bufs[cur][...], a_bufs[cur][...] = softmax(s_bufs[cur][...])
        accumulate(p_bufs[1 - cur][...], a_bufs[1 - cur][...], jnp.maximum(j - 1, 0))

    def finish(cur):
        p, alpha = softmax(diag_mask(s_bufs[cur][...]))
        accumulate(p_bufs[1 - cur][...], a_bufs[1 - cur][...], jnp.maximum(n_full - 1, 0))
        accumulate(p, alpha, n_full)

    m_ref[...] = jnp.full(m_ref.shape, MASK_VALUE, F32)
    acc_ref[...] = jnp.zeros(acc_ref.shape, F32)
    s_bufs[0][...] = logits(0)
    p_bufs[1][...] = jnp.zeros(p_bufs[1].shape, BF16)
    a_bufs[1][...] = jnp.ones(a_bufs[1].shape, F32)

    def run_stages(first, count):
        for i in range(count):
            stage(first + i, i % 2)

    def unrolled(jj, carry):
        run_stages(FLASH_UNROLL * jj, FLASH_UNROLL)
        return carry

    lax.fori_loop(0, n_full // FLASH_UNROLL, unrolled, 0)
    rest = n_full % FLASH_UNROLL

    @pl.when(rest >= 2)
    def _():
        run_stages(n_full - rest, 2)

    @pl.when(rest % 2 == 1)
    def _():
        stage(n_full - 1, 0)
        finish(1)

    @pl.when(rest % 2 == 0)
    def _():
        finish(0)


def _fox_aug(cum_split):
    b, h, s = cum_split[0].shape
    terms = jnp.concatenate(list(cum_split) + [jnp.ones((b, 1, s), BF16)], axis=1)
    terms = jnp.transpose(terms, (0, 2, 1))
    place_q = np.zeros((3 * h + 1, h * HEAD_DIM), np.float32)
    place_k = np.zeros((3 * h + 1, h * HEAD_DIM), np.float32)
    half = FOX_AUG // 2
    for head in range(h):
        for part in range(half):
            place_q[part * h + head, head * HEAD_DIM + part] = 1.0
            place_q[3 * h, head * HEAD_DIM + half + part] = 1.0
            place_k[3 * h, head * HEAD_DIM + part] = 1.0
            place_k[part * h + head, head * HEAD_DIM + half + part] = -1.0
    scatter = lambda place: jnp.einsum("bsr,rl->bsl", terms, jnp.asarray(place, BF16),
                                       preferred_element_type=F32).astype(BF16)
    return scatter(place_q), scatter(place_k)


def _fox_kernel(q_ref, qa_ref, k_ref, ka_ref, v_ref, z_ref, o_ref, qm_ref, ones_ref, *flash):
    t = FOX_TILE
    qi = pl.program_id(2)
    q = q_ref[0]
    qa = qa_ref[0]
    lane = lax.broadcasted_iota(jnp.int32, q.shape, 1)
    for hh in range(2):
        r = slice(hh * t, (hh + 1) * t)
        own = (lane < HEAD_DIM) if hh == 0 else (lane >= HEAD_DIM)
        qm_ref[r, 0:LANES] = jnp.where(own, q, jnp.zeros_like(q))
        qm_ref[r, LANES:2 * LANES] = jnp.where(own, qa, jnp.zeros_like(qa))
    ones_lane = lax.broadcasted_iota(jnp.int32, ones_ref.shape, 1)
    ones_ref[...] = jnp.where(ones_lane == 0, 1.0, 0.0).astype(BF16)

    def load_k(k0):
        return jnp.concatenate([k_ref[0, pl.ds(k0, t), :], ka_ref[0, pl.ds(k0, t), :]], axis=1)

    def load_v(k0):
        return jnp.concatenate([v_ref[0, pl.ds(k0, t), :], ones_ref[...]], axis=1)

    def diag_mask(s):
        row = lax.broadcasted_iota(jnp.int32, s.shape, 0) & (t - 1)
        col = lax.broadcasted_iota(jnp.int32, s.shape, 1)
        return jnp.where(col <= row, s, MASK_VALUE)

    _causal_flash(qi, t, qm_ref, load_k, load_v, diag_mask, flash)
    acc_ref = flash[1]
    o0 = acc_ref[0:t, 0:LANES] * (1.0 / acc_ref[0:t, LANES:LANES + 1])
    o1 = acc_ref[t:2 * t, 0:LANES] * (1.0 / acc_ref[t:2 * t, LANES:LANES + 1])
    o = jnp.where(lane < HEAD_DIM, o0, o1)
    o_ref[0] = (o * z_ref[0].astype(F32)).astype(BF16)


def _fox(fq, fk, fv, cum_split, fz):
    b, s, w = fq.shape
    t = FOX_TILE
    pairs = w // LANES
    qa, ka = _fox_aug(cum_split)
    tile = pl.BlockSpec((1, t, LANES), lambda bi, hp, i: (bi, i, hp))
    full = pl.BlockSpec((1, s, LANES), lambda bi, hp, i: (bi, 0, hp))
    resident = 3 * s * LANES * 2
    scratch = 2 * t * 2 * LANES * 2 + t * LANES * 2 + _flash_scratch_bytes(2 * t, t, 2 * LANES)
    return pl.pallas_call(
        _fox_kernel,
        grid=(b, pairs, s // t),
        in_specs=[tile, tile, full, full, full, tile],
        out_specs=tile,
        out_shape=jax.ShapeDtypeStruct((b, s, w), BF16),
        scratch_shapes=[pltpu.VMEM((2 * t, 2 * LANES), BF16),
                        pltpu.VMEM((t, LANES), BF16)] + _flash_scratch(2 * t, t, 2 * LANES),
        compiler_params=pltpu.CompilerParams(
            dimension_semantics=("arbitrary", "arbitrary", "arbitrary"),
            vmem_limit_bytes=_vmem_limit(resident + 4 * t * LANES * 2, scratch + 4 * 2 * t * t * 4)),
        name="fox",
    )(fq, qa, fk, ka, fv, fz)


def _compress_kernel(t_ref, w1_ref, w2_ref, pe_ref, o_ref):
    hp = lax.Precision.HIGHEST
    half = CMP_STRIDE * HEAD_DIM
    tt = t_ref[0, 0, 0]
    n = tt.shape[0]
    first = jnp.dot(tt, w1_ref[0, 0:half, :], precision=hp, preferred_element_type=F32)
    second = jnp.dot(tt, w1_ref[0, half:2 * half, :], precision=hp, preferred_element_type=F32)
    pe_term = jnp.dot(pe_ref[0], w1_ref[0], precision=hp, preferred_element_type=F32)[0:1, :]
    hidden = first + pltpu.roll(second, n - 1, 0) + pe_term
    o_ref[0, 0, 0] = jnp.dot(_silu(hidden), w2_ref[0], precision=hp, preferred_element_type=F32)


def _compress(t, w1, w2, pe):
    kinds, b, g, n, flat = t.shape
    width = w2.shape[-1]
    return pl.pallas_call(
        _compress_kernel,
        grid=(kinds, b, g),
        in_specs=[pl.BlockSpec((1, 1, 1, n, flat), lambda a, bi, gi: (a, bi, gi, 0, 0)),
                  pl.BlockSpec((1, 2 * flat, CMP_HIDDEN), lambda a, bi, gi: (a, 0, 0)),
                  pl.BlockSpec((1, CMP_HIDDEN, width), lambda a, bi, gi: (a, 0, 0)),
                  pl.BlockSpec((1, SUBLANES, 2 * flat), lambda a, bi, gi: (a, 0, 0))],
        out_specs=pl.BlockSpec((1, 1, 1, n, width), lambda a, bi, gi: (a, bi, gi, 0, 0)),
        out_shape=jax.ShapeDtypeStruct((kinds, b, g, n, width), F32),
        compiler_params=pltpu.CompilerParams(
            dimension_semantics=("arbitrary", "arbitrary", "arbitrary"),
            vmem_limit_bytes=_vmem_limit(n * flat * 4 + 2 * flat * CMP_HIDDEN * 4, 8 * 1024 * 1024)),
        name="compress",
    )(t, w1, w2, pe)


def _softmax_numerator(s, bias):
    s = s + bias
    m = jnp.max(s, axis=1, keepdims=True)
    m = jnp.where(m == -jnp.inf, 0.0, m)
    return jnp.exp2(s - m)


def _stack_heads(q):
    tq = q.shape[0]
    lane = lax.broadcasted_iota(jnp.int32, (tq, LANES), 1)
    rows = []
    for h in range(NSA_GROUP):
        slab = q[:, (h // 2) * LANES:(h // 2 + 1) * LANES]
        own = (lane < HEAD_DIM) if h % 2 == 0 else (lane >= HEAD_DIM)
        rows.append(jnp.where(own, slab, jnp.zeros_like(slab)))
    return jnp.concatenate(rows, axis=0)


def _group_gate(gates, gi, h, j):
    c0 = MISC_GATE0 + 3 * h + j
    c1 = MISC_GATE0 + 3 * (NSA_GROUP + h) + j
    return jnp.where(gi == 0, gates[:, c0:c0 + 1], gates[:, c1:c1 + 1])


def _cmpwin_kernel(q_ref, kc_ref, vc_ref, kw_ref, vw_ref, misc_ref, part_ref, selb_ref):
    tq = NSA_Q
    gi = pl.program_id(1)
    qi = pl.program_id(2)
    q0 = pl.multiple_of(qi * tq, tq)
    q4 = _stack_heads(q_ref[0])
    n_cmp = kc_ref.shape[2]
    head_rows = [slice(h * tq, (h + 1) * tq) for h in range(NSA_GROUP)]

    kc = kc_ref[0, 0]
    kc_hi = kc.astype(BF16)
    kc_lo = (kc - kc_hi.astype(F32)).astype(BF16)
    s = _dot_nt(q4, kc_hi) + _dot_nt(q4, kc_lo)
    col = lax.broadcasted_iota(jnp.int32, (tq, n_cmp), 1)
    t_row = q0 + lax.broadcasted_iota(jnp.int32, (tq, n_cmp), 0)
    cmp_bias = jnp.where(col * CMP_STRIDE + (CMP_BLOCK - 1) <= t_row, 0.0, -jnp.inf)
    pcs = []
    for r in head_rows:
        p = _softmax_numerator(s[r], cmp_bias)
        pcs.append(p * (1.0 / jnp.maximum(jnp.sum(p, axis=1, keepdims=True), 1e-30)))
    oc = jnp.dot(jnp.concatenate(pcs, axis=0).astype(BF16), vc_ref[0, 0].astype(BF16),
                 preferred_element_type=F32)

    pc_sum = pcs[0] + pcs[1] + pcs[2] + pcs[3]
    n_blk = selb_ref.shape[3]
    ratio = SLC_BLOCK // CMP_STRIDE
    jj = lax.broadcasted_iota(jnp.int32, (n_blk, n_cmp), 0)
    ii = lax.broadcasted_iota(jnp.int32, (n_blk, n_cmp), 1)
    lo_i = jj * ratio - (CMP_BLOCK // CMP_STRIDE - 1)
    overlap = jnp.logical_and(ii >= lo_i, ii < lo_i + (SLC_BLOCK + CMP_BLOCK) // CMP_STRIDE - 1)
    overlap = jnp.logical_and(overlap, ii < n_cmp - 1).astype(BF16)
    imp = sum(_dot_nt(overlap, p) for p in _split3(pc_sum))
    blk = lax.broadcasted_iota(jnp.int32, (n_blk, tq), 0)
    cur = (q0 + lax.broadcasted_iota(jnp.int32, (n_blk, tq), 1)) // SLC_BLOCK
    forced = jnp.logical_or(blk == 0, jnp.logical_or(blk == cur, blk == cur - 1))
    imp = jnp.where(forced, jnp.inf, jnp.where(blk > cur, -jnp.inf, imp))

    def pick_one(_, carry):
        rem, sel = carry
        best = jnp.max(rem, axis=0, keepdims=True)
        first = jnp.min(jnp.where(rem == best, blk, n_blk), axis=0, keepdims=True)
        hit = blk == first
        return jnp.where(hit, -jnp.inf, rem), jnp.where(hit, 1.0, sel)

    _, sel = lax.fori_loop(0, min(N_SELECT, n_blk), pick_one, (imp, jnp.zeros_like(imp)),
                           unroll=True)
    selb_ref[0, 0] = jnp.where(sel.T > 0.5, 0.0, MASK_VALUE).astype(BF16)

    start = pl.multiple_of(jnp.maximum(q0 - WINDOW, 0), tq)
    kw = kw_ref[0, 0, pl.ds(start, WIN_KEYS), :]
    vw = vw_ref[0, 0, pl.ds(start, WIN_KEYS), :]
    sw = _dot_nt(q4, kw)
    pos = start + lax.broadcasted_iota(jnp.int32, (tq, WIN_KEYS), 1)
    dist = q0 + lax.broadcasted_iota(jnp.int32, (tq, WIN_KEYS), 0) - pos
    win_bias = jnp.where(jnp.logical_and(dist >= 0, dist < WINDOW), 0.0, -jnp.inf)
    pw = jnp.concatenate([_softmax_numerator(sw[r], win_bias) for r in head_rows], axis=0)
    ow = jnp.dot(pw.astype(BF16), vw, preferred_element_type=F32)
    ow = ow[:, 0:HEAD_DIM] * (1.0 / jnp.maximum(ow[:, HEAD_DIM:HEAD_DIM + 1], 1e-30))

    gates = _sigmoid(misc_ref[0])
    heads = []
    for h, r in enumerate(head_rows):
        heads.append(_group_gate(gates, gi, h, 0) * oc[r, 0:HEAD_DIM]
                     + _group_gate(gates, gi, h, 2) * ow[r])
    part_ref[0] = jnp.concatenate(heads, axis=1)


def _cmpwin(nq, kc_c, vc_c, kw, vw_ones, misc):
    b, s, w = nq.shape
    g = NSA_KV_HEADS
    tq = NSA_Q
    gw = NSA_GROUP * HEAD_DIM
    n_cmp = kc_c.shape[2]
    n_blk = s // SLC_BLOCK
    q_spec = pl.BlockSpec((1, tq, gw), lambda bi, gi, i: (bi, i, gi))
    cmp_spec = pl.BlockSpec((1, 1, n_cmp, LANES), lambda bi, gi, i: (bi, gi, 0, 0))
    seq_spec = pl.BlockSpec((1, 1, s, LANES), lambda bi, gi, i: (bi, gi, 0, 0))
    resident = 2 * s * LANES * 2 + 2 * n_cmp * LANES * 4
    rows = NSA_GROUP * tq
    return pl.pallas_call(
        _cmpwin_kernel,
        grid=(b, g, s // tq),
        in_specs=[q_spec, cmp_spec, cmp_spec, seq_spec, seq_spec,
                  pl.BlockSpec((1, tq, LANES), lambda bi, gi, i: (bi, i, 0))],
        out_specs=(q_spec, pl.BlockSpec((1, 1, tq, n_blk), lambda bi, gi, i: (bi, gi, i, 0))),
        out_shape=(jax.ShapeDtypeStruct((b, s, w), F32),
                   jax.ShapeDtypeStruct((b, g, s, n_blk), BF16)),
        compiler_params=pltpu.CompilerParams(
            dimension_semantics=("arbitrary", "arbitrary", "arbitrary"),
            vmem_limit_bytes=_vmem_limit(resident + tq * gw * 8, 12 * rows * WIN_KEYS * 4)),
        name="cmpwin",
    )(nq, kc_c, vc_c, kw, vw_ones, misc)


def _sel_kernel(q_ref, k_ref, v_ref, selb_ref, part_ref, misc_ref, z_ref, o_ref,
                qa_ref, *flash):
    tq = NSA_Q
    gi = pl.program_id(1)
    qi = pl.program_id(2)
    q0 = pl.multiple_of(qi * tq, tq)
    q = q_ref[0]
    selb = selb_ref[0, 0]
    lane = lax.broadcasted_iota(jnp.int32, (tq, LANES), 1)
    for h in range(NSA_GROUP):
        r = slice(h * tq, (h + 1) * tq)
        slab = q[:, (h // 2) * LANES:(h // 2 + 1) * LANES]
        own = (lane < HEAD_DIM) if h % 2 == 0 else (lane >= HEAD_DIM)
        qa_ref[r, 0:LANES] = jnp.where(own, slab, jnp.zeros_like(slab))
        qa_ref[r, LANES:2 * LANES] = selb

    n_full = q0 // SEL_KV

    def diag_mask(s):
        t_row = q0 + (lax.broadcasted_iota(jnp.int32, s.shape, 0) & (tq - 1))
        pos = n_full * SEL_KV + lax.broadcasted_iota(jnp.int32, s.shape, 1)
        return jnp.where(pos <= t_row, s, MASK_VALUE)

    _causal_flash(n_full, SEL_KV, qa_ref,
                  lambda k0: k_ref[0, 0, pl.ds(k0, SEL_KV), :],
                  lambda k0: v_ref[0, 0, pl.ds(k0, SEL_KV), :],
                  diag_mask, flash)
    acc_ref = flash[1]

    gates = _sigmoid(misc_ref[0])
    part = part_ref[0]
    heads = []
    for h in range(NSA_GROUP):
        r = slice(h * tq, (h + 1) * tq)
        o_sel = acc_ref[r, 0:HEAD_DIM] * (1.0 / acc_ref[r, HEAD_DIM:HEAD_DIM + 1])
        heads.append(part[:, h * HEAD_DIM:(h + 1) * HEAD_DIM] + _group_gate(gates, gi, h, 1) * o_sel)
    o_ref[0] = (jnp.concatenate(heads, axis=1) * z_ref[0].astype(F32)).astype(BF16)


def _sel(nq, k_aug, v_ones, selb, part, misc, nz):
    b, s, w = nq.shape
    g = NSA_KV_HEADS
    tq = NSA_Q
    gw = NSA_GROUP * HEAD_DIM
    n_blk = s // SLC_BLOCK
    rows = NSA_GROUP * tq
    q_spec = pl.BlockSpec((1, tq, gw), lambda bi, gi, i: (bi, i, gi))
    resident = s * 3 * LANES * 2
    return pl.pallas_call(
        _sel_kernel,
        grid=(b, g, s // tq),
        in_specs=[q_spec,
                  pl.BlockSpec((1, 1, s, 2 * LANES), lambda bi, gi, i: (bi, gi, 0, 0)),
                  pl.BlockSpec((1, 1, s, LANES), lambda bi, gi, i: (bi, gi, 0, 0)),
                  pl.BlockSpec((1, 1, tq, n_blk), lambda bi, gi, i: (bi, gi, i, 0)),
                  q_spec,
                  pl.BlockSpec((1, tq, LANES), lambda bi, gi, i: (bi, i, 0)),
                  q_spec],
        out_specs=q_spec,
        out_shape=jax.ShapeDtypeStruct((b, s, w), BF16),
        scratch_shapes=[pltpu.VMEM((rows, 2 * LANES), BF16)] + _flash_scratch(rows, SEL_KV, LANES),
        compiler_params=pltpu.CompilerParams(
            dimension_semantics=("arbitrary", "arbitrary", "arbitrary"),
            vmem_limit_bytes=_vmem_limit(resident + tq * gw * 10,
                                         _flash_scratch_bytes(rows, SEL_KV, LANES) + 4 * rows * SEL_KV * 4)),
        name="sel",
    )(nq, k_aug, v_ones, selb, part, misc, nz)


def _out_kernel(yf_ref, yn_ref, w_ref, g_ref, mod_ref, x_ref, o_ref):
    y = (jnp.dot(yf_ref[0], w_ref[0:FOX_WIDTH, :], preferred_element_type=F32)
         + jnp.dot(yn_ref[0], w_ref[FOX_WIDTH:, :], preferred_element_type=F32))
    yn = y * lax.rsqrt(jnp.mean(y * y, axis=-1, keepdims=True) + RMS_EPS)
    o_ref[0] = x_ref[0] + mod_ref[0, 2:3, :] * (yn * g_ref[...])


def _out(y_fox, y_nsa, w_out, g_post, mod, x):
    b, s, d = x.shape
    tm = PROJ_ROWS
    row = lambda bi, i: (bi, i, 0)
    half = pl.BlockSpec((1, tm, FOX_WIDTH), row)
    pipelined = 2 * tm * FOX_WIDTH * 2 + 2 * tm * d * 4 + d * d * 2
    return pl.pallas_call(
        _out_kernel,
        grid=(b, s // tm),
        in_specs=[half, half,
                  pl.BlockSpec((d, d), lambda bi, i: (0, 0)),
                  pl.BlockSpec((1, d), lambda bi, i: (0, 0)),
                  pl.BlockSpec((1, 3, d), lambda bi, i: (bi, 0, 0)),
                  pl.BlockSpec((1, tm, d), row)],
        out_specs=pl.BlockSpec((1, tm, d), row),
        out_shape=jax.ShapeDtypeStruct((b, s, d), F32),
        compiler_params=pltpu.CompilerParams(
            dimension_semantics=("arbitrary", "arbitrary"),
            vmem_limit_bytes=_vmem_limit(pipelined, 6 * tm * d * 4)),
        name="out",
    )(y_fox, y_nsa, w_out, g_post, mod, x)


def _rope_slabs(seq_len):
    inv = 1.0 / (ROPE_THETA ** (jnp.arange(0, HEAD_DIM, 2, dtype=F32) / HEAD_DIM))
    ang = jnp.arange(seq_len, dtype=F32)[:, None] * inv[None, :]
    cos, sin = jnp.cos(ang), jnp.sin(ang)
    reps = LANES // (HEAD_DIM // 2)
    sign = jnp.tile(jnp.concatenate([-jnp.ones((HEAD_DIM // 2,), F32), jnp.ones((HEAD_DIM // 2,), F32)]),
                    LANES // HEAD_DIM)
    return jnp.tile(cos, (1, reps)), jnp.tile(sin, (1, reps)) * sign[None, :]


def _reorder_w_in(w_in):
    fw, kv = FOX_WIDTH, NSA_KV_WIDTH
    o = 0
    cols = {}
    for name, n in (("fq", fw), ("fk", fw), ("fv", fw), ("ff", FOX_HEADS), ("fz", fw), ("nq", NSA_WIDTH),
                    ("kc", kv), ("vc", kv), ("ks", kv), ("vs", kv), ("kw", kv), ("vw", kv),
                    ("ng", 3 * NSA_HEADS), ("nz", NSA_WIDTH)):
        cols[name] = w_in[:, o:o + n]
        o += n
    pad = jnp.zeros((w_in.shape[0], LANES - FOX_HEADS - 3 * NSA_HEADS), w_in.dtype)
    order = ("fq", "fk", "fv", "fz", "nq", "kc", "vc", "ks", "vs", "kw", "vw", "nz", "ff", "ng")
    return jnp.concatenate([cols[k] for k in order] + [pad], axis=1).astype(BF16)


def _layer(x, c8, g_pre, g_post, w_ada, b_ada, w_in, b_forget, w_cmp_k1, w_cmp_k2,
           w_cmp_v1, w_cmp_v2, pe_cmp_k, pe_cmp_v, w_out, cos128, sin128):
    b, s, d = x.shape
    mod = _ada(c8, w_ada, b_ada)[:b].reshape(b, 3, d)
    bf128 = jnp.pad(b_forget, (0, LANES - FOX_HEADS)).reshape(1, LANES)
    (fq, fk, fv, fz, nq, cmp_in, ks_aug, vs_ones, kw2, vw_ones, nz, misc, log_f) = _proj(
        x, mod, g_pre.reshape(1, d), _reorder_w_in(w_in), cos128, sin128, bf128)

    cum_split = tuple(p.reshape(b, FOX_HEADS, s)
                      for p in _cumsum_lanes_split(log_f.reshape(b * FOX_HEADS, s)))
    y_fox = _fox(fq, fk, fv, cum_split, fz)

    flat = CMP_STRIDE * HEAD_DIM
    t = cmp_in.reshape(2, b, NSA_KV_HEADS, s // CMP_STRIDE, flat)
    pe = jnp.stack([pe_cmp_k.reshape(1, 2 * flat), pe_cmp_v.reshape(1, 2 * flat)])
    w2 = jnp.stack([w_cmp_k2, w_cmp_v2])
    cmp = _compress(t, jnp.stack([w_cmp_k1, w_cmp_v1]), jnp.concatenate([w2, w2], axis=-1),
                    jnp.broadcast_to(pe, (2, SUBLANES, 2 * flat)))
    part, selb = _cmpwin(nq, cmp[0], cmp[1], kw2, vw_ones, misc)
    y_nsa = _sel(nq, ks_aug, vs_ones, selb, part, misc, nz)

    return _out(y_fox, y_nsa, w_out.astype(BF16), g_post.reshape(1, d), mod, x)


def kernel(x, c, g_pre, g_post, w_ada, b_ada, w_in, b_forget, w_cmp_k1, w_cmp_k2,
           w_cmp_v1, w_cmp_v2, pe_cmp_k, pe_cmp_v, w_out):
    cos128, sin128 = _rope_slabs(x.shape[1])
    c8 = jnp.pad(c, ((0, SUBLANES - c.shape[0]), (0, 0)))
    for layer in range(g_pre.shape[0]):
        x = _layer(x, c8, g_pre[layer], g_post[layer], w_ada[layer], b_ada[layer], w_in[layer],
                   b_forget[layer], w_cmp_k1[layer], w_cmp_k2[layer], w_cmp_v1[layer],
                   w_cmp_v2[layer], pe_cmp_k[layer], pe_cmp_v[layer], w_out[layer], cos128, sin128)
    return x
```

```python
import functools

import jax
import jax.numpy as jnp
import numpy as np
from jax import lax
from jax.experimental import pallas as pl
from jax.experimental.pallas import tpu as pltpu

F32 = jnp.float32
BF16 = jnp.bfloat16

D_MODEL = 1024
HEAD_DIM = 64
FOX_WIDTH = 512
NSA_WIDTH = 512
FOX_HEADS = 8
NSA_HEADS = 8
NSA_KV_HEADS = 2
NSA_GROUP = 4
NSA_KV_WIDTH = 128
CMP_BLOCK = 32
CMP_STRIDE = 16
CMP_HIDDEN = 128
SLC_BLOCK = 64
N_SELECT = 16
WINDOW = 512
ROPE_THETA = 10000.0
RMS_EPS = 1e-6
LOG2E = 1.4426950408889634
QK_SCALE = HEAD_DIM ** -0.5 * LOG2E

LANES = 128
SUBLANES = 8
V7X_VMEM_BYTES = 64 * 1024 * 1024
MASK_VALUE = -1e30

PROJ_ROWS = 512
FOX_TILE = 512
NSA_Q = 128
SEL_Q = 256
SEL_KV = 512
WIN_KEYS = WINDOW + NSA_Q
FLASH_UNROLL = 4
FOX_AUG = 6

C_FQ, C_FK, C_FV, C_FZ, C_NQ = 0, 512, 1024, 1536, 2048
C_KC, C_VC, C_KS, C_VS, C_KW, C_VW = 2560, 2688, 2816, 2944, 3072, 3200
C_NZ, C_MISC, PROJ_COLS = 3328, 3840, 3968
MISC_GATE0 = FOX_HEADS


def _vmem_limit(pipelined_bytes, resident_bytes):
    need = 2 * pipelined_bytes + resident_bytes
    return int(min(max(need, 16 * 1024 * 1024), V7X_VMEM_BYTES - 8 * 1024 * 1024))


def _sigmoid(v):
    return 1.0 / (1.0 + jnp.exp(-v))


def _silu(v):
    return v * _sigmoid(v)


def _dot_nt(a, b):
    return lax.dot_general(a, b, (((1,), (1,)), ((), ())), preferred_element_type=F32)


def _split3(v):
    hi = v.astype(BF16)
    r1 = v - hi.astype(F32)
    mid = r1.astype(BF16)
    lo = (r1 - mid.astype(F32)).astype(BF16)
    return hi, mid, lo


def _ada_kernel(c_ref, w_ref, b_ref, o_ref):
    a = _silu(c_ref[...])
    o_ref[...] = jnp.dot(a, w_ref[...], precision=lax.Precision.HIGHEST,
                         preferred_element_type=F32) + b_ref[...]


def _ada(c8, w_ada, b_ada):
    n = w_ada.shape[1]
    blk = D_MODEL
    return pl.pallas_call(
        _ada_kernel,
        grid=(n // blk,),
        in_specs=[pl.BlockSpec((SUBLANES, D_MODEL), lambda j: (0, 0)),
                  pl.BlockSpec((D_MODEL, blk), lambda j: (0, j)),
                  pl.BlockSpec((1, blk), lambda j: (0, j))],
        out_specs=pl.BlockSpec((SUBLANES, blk), lambda j: (0, j)),
        out_shape=jax.ShapeDtypeStruct((SUBLANES, n), F32),
        compiler_params=pltpu.CompilerParams(
            dimension_semantics=("arbitrary",),
            vmem_limit_bytes=_vmem_limit(D_MODEL * blk * 4, 4 * 1024 * 1024)),
        name="ada",
    )(c8, w_ada, b_ada.reshape(1, n))


def _rope128(t, cos, sin_signed):
    lane = lax.broadcasted_iota(jnp.int32, t.shape, 1)
    first_half = (lane & (HEAD_DIM - 1)) < HEAD_DIM // 2
    partner = jnp.where(first_half,
                        pltpu.roll(t, LANES - HEAD_DIM // 2, 1),
                        pltpu.roll(t, HEAD_DIM // 2, 1))
    return t * cos + partner * sin_signed


def _proj_kernel(x_ref, mod_ref, g_ref, w_ref, cos_ref, sin_ref, bf_ref,
                 fq_ref, fk_ref, fv_ref, fz_ref, nq_ref, cmp_ref,
                 ks_ref, vs_ref, kw_ref, vw_ref, nz_ref, misc_ref, lf_ref):
    tm = x_ref.shape[1]
    x = x_ref[0]
    y = x * lax.rsqrt(jnp.mean(x * x, axis=-1, keepdims=True) + RMS_EPS)
    y = y * g_ref[...]
    h = (y * (1.0 + mod_ref[0, 1:2, :]) + mod_ref[0, 0:1, :]).astype(BF16)
    cos = cos_ref[...]
    sin = sin_ref[...]

    def mm(lo, n):
        return jnp.dot(h, w_ref[:, lo:lo + n], preferred_element_type=F32)

    fq_ref[0] = (mm(C_FQ, FOX_WIDTH) * QK_SCALE).astype(BF16)
    fk_ref[0] = mm(C_FK, FOX_WIDTH).astype(BF16)
    fv_ref[0] = mm(C_FV, FOX_WIDTH).astype(BF16)
    fz_ref[0] = _silu(mm(C_FZ, FOX_WIDTH)).astype(BF16)
    for j in range(NSA_WIDTH // LANES):
        t = mm(C_NQ + j * LANES, LANES)
        nq_ref[0, :, j * LANES:(j + 1) * LANES] = (_rope128(t, cos, sin) * QK_SCALE).astype(BF16)
    lane = lax.broadcasted_iota(jnp.int32, (tm, LANES), 1)
    low = lane < HEAD_DIM
    ones_col = jnp.where(lane == HEAD_DIM, 1.0, 0.0).astype(BF16)

    def doubled(slab):
        swapped = pltpu.roll(slab, HEAD_DIM, 1)
        return jnp.where(low, slab, swapped), jnp.where(low, swapped, slab)

    def with_ones(slab):
        swapped = pltpu.roll(slab, HEAD_DIM, 1)
        return jnp.where(low, slab, ones_col), jnp.where(low, swapped, ones_col)

    for kind, slab in enumerate((_rope128(mm(C_KC, LANES), cos, sin), mm(C_VC, LANES))):
        for g in range(NSA_KV_HEADS):
            cmp_ref[kind, 0, g] = slab[:, g * HEAD_DIM:(g + 1) * HEAD_DIM]
    pos = pl.program_id(1) * tm + lax.broadcasted_iota(jnp.int32, (tm, LANES), 0)
    onehot = jnp.where(pos // SLC_BLOCK == lane, 1.0, 0.0).astype(BF16)
    for g, k2 in enumerate(doubled(_rope128(mm(C_KS, LANES), cos, sin).astype(BF16))):
        ks_ref[0, g, :, 0:LANES] = k2
        ks_ref[0, g, :, LANES:2 * LANES] = onehot
    vs_ref[0, 0], vs_ref[0, 1] = with_ones(mm(C_VS, LANES).astype(BF16))
    kw_ref[0, 0], kw_ref[0, 1] = doubled(_rope128(mm(C_KW, LANES), cos, sin).astype(BF16))
    vw_ref[0, 0], vw_ref[0, 1] = with_ones(mm(C_VW, LANES).astype(BF16))
    nz_ref[0] = _silu(mm(C_NZ, NSA_WIDTH)).astype(BF16)
    misc = mm(C_MISC, LANES)
    misc_ref[0] = misc
    z = misc + bf_ref[...]
    log_f = jnp.minimum(z, 0.0) - jnp.log1p(jnp.exp(-jnp.abs(z)))
    lf_ref[0] = log_f.T[0:FOX_HEADS, :]


def _proj(x, mod, g_pre, w_cat, cos128, sin128, bf128):
    b, s, d = x.shape
    tm = PROJ_ROWS
    row = lambda bi, i: (bi, i, 0)
    wide = lambda dt: jax.ShapeDtypeStruct((b, s, FOX_WIDTH), dt)
    g = NSA_KV_HEADS
    grouped = lambda lanes: jax.ShapeDtypeStruct((b, g, s, lanes), BF16)
    grouped_spec = lambda lanes: pl.BlockSpec((1, g, tm, lanes), lambda bi, i: (bi, 0, i, 0))
    out_shape = (wide(BF16), wide(BF16), wide(BF16), wide(BF16), wide(BF16),
                 jax.ShapeDtypeStruct((2, b, g, s, HEAD_DIM), F32),
                 grouped(2 * LANES), grouped(LANES), grouped(LANES), grouped(LANES),
                 wide(BF16), jax.ShapeDtypeStruct((b, s, LANES), F32),
                 jax.ShapeDtypeStruct((b, FOX_HEADS, s), F32))
    wide_spec = pl.BlockSpec((1, tm, FOX_WIDTH), row)
    out_specs = (wide_spec,) * 5 + (
        pl.BlockSpec((2, 1, g, tm, HEAD_DIM), lambda bi, i: (0, bi, 0, i, 0)),
        grouped_spec(2 * LANES), grouped_spec(LANES), grouped_spec(LANES), grouped_spec(LANES),
        wide_spec, pl.BlockSpec((1, tm, LANES), row),
        pl.BlockSpec((1, FOX_HEADS, tm), lambda bi, i: (bi, 0, i)))
    pipelined = (tm * d * 4 + tm * (6 * FOX_WIDTH * 2 + 2 * g * LANES * 4 + 5 * g * LANES * 2 + LANES * 4)
                 + d * PROJ_COLS * 2)
    return pl.pallas_call(
        _proj_kernel,
        grid=(b, s // tm),
        in_specs=[pl.BlockSpec((1, tm, d), row),
                  pl.BlockSpec((1, 3, d), lambda bi, i: (bi, 0, 0)),
                  pl.BlockSpec((1, d), lambda bi, i: (0, 0)),
                  pl.BlockSpec((d, PROJ_COLS), lambda bi, i: (0, 0)),
                  pl.BlockSpec((tm, LANES), lambda bi, i: (i, 0)),
                  pl.BlockSpec((tm, LANES), lambda bi, i: (i, 0)),
                  pl.BlockSpec((1, LANES), lambda bi, i: (0, 0))],
        out_specs=out_specs,
        out_shape=out_shape,
        compiler_params=pltpu.CompilerParams(
            dimension_semantics=("arbitrary", "arbitrary"),
            vmem_limit_bytes=_vmem_limit(pipelined, 8 * 1024 * 1024)),
        name="proj",
    )(x, mod, g_pre, w_cat, cos128, sin128, bf128)


def _cumsum_kernel(x_ref, hi_ref, mid_ref, lo_ref, *, chunks):
    x = x_ref[...]
    n = x.shape[0]
    parts = _split3(x)
    r = lax.broadcasted_iota(jnp.int32, (LANES, LANES), 0)
    c = lax.broadcasted_iota(jnp.int32, (LANES, LANES), 1)
    tri = (r <= c).astype(BF16)
    rr = lax.broadcasted_iota(jnp.int32, (n, n), 0)
    cc = lax.broadcasted_iota(jnp.int32, (n, n), 1)
    earlier = jnp.logical_and(cc < rr, (cc // chunks) == (rr // chunks)).astype(BF16)
    within = sum(jnp.dot(p, tri, preferred_element_type=F32) for p in parts)
    before = sum(jnp.dot(earlier, p, preferred_element_type=F32) for p in parts)
    total = (within + jnp.sum(before, axis=-1, keepdims=True)) * LOG2E
    hi_ref[...], mid_ref[...], lo_ref[...] = _split3(total)


def _cumsum_lanes_split(v):
    rows, s = v.shape
    chunks = s // LANES
    n = rows * chunks
    part = jax.ShapeDtypeStruct((n, LANES), BF16)
    parts = pl.pallas_call(
        functools.partial(_cumsum_kernel, chunks=chunks),
        out_shape=(part, part, part),
        compiler_params=pltpu.CompilerParams(
            vmem_limit_bytes=_vmem_limit(2 * n * LANES * 4, 6 * n * n)),
        name="cumsum",
    )(v.reshape(n, LANES))
    return tuple(p.reshape(rows, s) for p in parts)


def _flash_scratch(rows, tile, acc_lanes):
    return ([pltpu.VMEM((rows, LANES), F32), pltpu.VMEM((rows, acc_lanes), F32)]
            + [pltpu.VMEM((rows, tile), F32)] * 2
            + [pltpu.VMEM((rows, tile), BF16)] * 2
            + [pltpu.VMEM((rows, LANES), F32)] * 2)


def _flash_scratch_bytes(rows, tile, acc_lanes):
    return rows * (LANES * 4 + acc_lanes * 4 + 2 * tile * 4 + 2 * tile * 2 + 2 * LANES * 4)


def _causal_flash(n_full, tile, q_ref, load_k, load_v, diag_mask, scratch):
    m_ref, acc_ref, s0, s1, p0, p1, a0, a1 = scratch
    s_bufs, p_bufs, a_bufs = (s0, s1), (p0, p1), (a0, a1)
    reps = acc_ref.shape[-1] // LANES

    def logits(j):
        return _dot_nt(q_ref[...], load_k(pl.multiple_of(j * tile, tile)))

    def softmax(s):
        m_prev = m_ref[...]
        m_next = jnp.maximum(m_prev, jnp.max(s, axis=1, keepdims=True))
        m_ref[...] = m_next
        p = jnp.exp2(s - pltpu.repeat(m_next, tile // LANES, 1))
        return p.astype(BF16), jnp.exp2(m_prev - m_next)

    def accumulate(p, alpha, j):
        pv = jnp.dot(p, load_v(pl.multiple_of(j * tile, tile)), preferred_element_type=F32)
        acc_ref[...] = acc_ref[...] * (alpha if reps == 1 else pltpu.repeat(alpha, reps, 1)) + pv

    def stage(j, cur):
        s_bufs[1 - cur][...] = logits(j + 1)
        p_bufs[cur][...], a_bufs[cur][...] = softmax(s_bufs[cur][...])
        accumulate(p_bufs[1 - cur][...], a_bufs[1 - cur][...], jnp.maximum(j - 1, 0))

    def finish(cur):
        p, alpha = softmax(diag_mask(s_bufs[cur][...]))
        accumulate(p_bufs[1 - cur][...], a_bufs[1 - cur][...], jnp.maximum(n_full - 1, 0))
        accumulate(p, alpha, n_full)

    m_ref[...] = jnp.full(m_ref.shape, MASK_VALUE, F32)
    acc_ref[...] = jnp.zeros(acc_ref.shape, F32)
    s_bufs[0][...] = logits(0)
    p_bufs[1][...] = jnp.zeros(p_bufs[1].shape, BF16)
    a_bufs[1][...] = jnp.ones(a_bufs[1].shape, F32)

    def run_stages(first, count):
        for i in range(count):
            stage(first + i, i % 2)

    def unrolled(jj, carry):
        run_stages(FLASH_UNROLL * jj, FLASH_UNROLL)
        return carry

    lax.fori_loop(0, n_full // FLASH_UNROLL, unrolled, 0)
    rest = n_full % FLASH_UNROLL

    @pl.when(rest >= 2)
    def _():
        run_stages(n_full - rest, 2)

    @pl.when(rest % 2 == 1)
    def _():
        stage(n_full - 1, 0)
        finish(1)

    @pl.when(rest % 2 == 0)
    def _():
        finish(0)


def _fox_aug(cum_split):
    b, h, s = cum_split[0].shape
    terms = jnp.concatenate(list(cum_split) + [jnp.ones((b, 1, s), BF16)], axis=1)
    terms = jnp.transpose(terms, (0, 2, 1))
    place_q = np.zeros((3 * h + 1, h * HEAD_DIM), np.float32)
    place_k = np.zeros((3 * h + 1, h * HEAD_DIM), np.float32)
    half = FOX_AUG // 2
    for head in range(h):
        for part in range(half):
            place_q[part * h + head, head * HEAD_DIM + part] = 1.0
            place_q[3 * h, head * HEAD_DIM + half + part] = 1.0
            place_k[3 * h, head * HEAD_DIM + part] = 1.0
            place_k[part * h + head, head * HEAD_DIM + half + part] = -1.0
    scatter = lambda place: jnp.einsum("bsr,rl->bsl", terms, jnp.asarray(place, BF16),
                                       preferred_element_type=F32).astype(BF16)
    return scatter(place_q), scatter(place_k)


def _fox_kernel(q_ref, qa_ref, k_ref, ka_ref, v_ref, z_ref, o_ref, qm_ref, ones_ref, *flash):
    t = FOX_TILE
    qi = pl.program_id(2)
    q = q_ref[0]
    qa = qa_ref[0]
    lane = lax.broadcasted_iota(jnp.int32, q.shape, 1)
    for hh in range(2):
        r = slice(hh * t, (hh + 1) * t)
        own = (lane < HEAD_DIM) if hh == 0 else (lane >= HEAD_DIM)
        qm_ref[r, 0:LANES] = jnp.where(own, q, jnp.zeros_like(q))
        qm_ref[r, LANES:2 * LANES] = jnp.where(own, qa, jnp.zeros_like(qa))
    ones_lane = lax.broadcasted_iota(jnp.int32, ones_ref.shape, 1)
    ones_ref[...] = jnp.where(ones_lane == 0, 1.0, 0.0).astype(BF16)

    def load_k(k0):
        return jnp.concatenate([k_ref[0, pl.ds(k0, t), :], ka_ref[0, pl.ds(k0, t), :]], axis=1)

    def load_v(k0):
        return jnp.concatenate([v_ref[0, pl.ds(k0, t), :], ones_ref[...]], axis=1)

    def diag_mask(s):
        row = lax.broadcasted_iota(jnp.int32, s.shape, 0) & (t - 1)
        col = lax.broadcasted_iota(jnp.int32, s.shape, 1)
        return jnp.where(col <= row, s, MASK_VALUE)

    _causal_flash(qi, t, qm_ref, load_k, load_v, diag_mask, flash)
    acc_ref = flash[1]
    o0 = acc_ref[0:t, 0:LANES] * (1.0 / acc_ref[0:t, LANES:LANES + 1])
    o1 = acc_ref[t:2 * t, 0:LANES] * (1.0 / acc_ref[t:2 * t, LANES:LANES + 1])
    o = jnp.where(lane < HEAD_DIM, o0, o1)
    o_ref[0] = (o * z_ref[0].astype(F32)).astype(BF16)


def _fox(fq, fk, fv, cum_split, fz):
    b, s, w = fq.shape
    t = FOX_TILE
    pairs = w // LANES
    qa, ka = _fox_aug(cum_split)
    tile = pl.BlockSpec((1, t, LANES), lambda bi, hp, i: (bi, i, hp))
    full = pl.BlockSpec((1, s, LANES), lambda bi, hp, i: (bi, 0, hp))
    resident = 3 * s * LANES * 2
    scratch = 2 * t * 2 * LANES * 2 + t * LANES * 2 + _flash_scratch_bytes(2 * t, t, 2 * LANES)
    return pl.pallas_call(
        _fox_kernel,
        grid=(b, pairs, s // t),
        in_specs=[tile, tile, full, full, full, tile],
        out_specs=tile,
        out_shape=jax.ShapeDtypeStruct((b, s, w), BF16),
        scratch_shapes=[pltpu.VMEM((2 * t, 2 * LANES), BF16),
                        pltpu.VMEM((t, LANES), BF16)] + _flash_scratch(2 * t, t, 2 * LANES),
        compiler_params=pltpu.CompilerParams(
            dimension_semantics=("arbitrary", "arbitrary", "arbitrary"),
            vmem_limit_bytes=_vmem_limit(resident + 4 * t * LANES * 2, scratch + 4 * 2 * t * t * 4)),
        name="fox",
    )(fq, qa, fk, ka, fv, fz)


def _compress_kernel(t_ref, w1_ref, w2_ref, pe_ref, hi_ref, lo_ref):
    hp = lax.Precision.HIGHEST
    half = CMP_STRIDE * HEAD_DIM
    tt = t_ref[0, 0, 0]
    n = tt.shape[0]
    first = jnp.dot(tt, w1_ref[0, 0:half, :], precision=hp, preferred_element_type=F32)
    second = jnp.dot(tt, w1_ref[0, half:2 * half, :], precision=hp, preferred_element_type=F32)
    pe_term = jnp.dot(pe_ref[0], w1_ref[0], precision=hp, preferred_element_type=F32)[0:1, :]
    hidden = first + pltpu.roll(second, n - 1, 0) + pe_term
    out = jnp.dot(_silu(hidden), w2_ref[0], precision=hp, preferred_element_type=F32)
    hi = out.astype(BF16)
    hi_ref[0, 0, 0] = hi
    lo_ref[0, 0, 0] = (out - hi.astype(F32)).astype(BF16)


def _compress(t, w1, w2, pe):
    kinds, b, g, n, flat = t.shape
    width = w2.shape[-1]
    out_spec = pl.BlockSpec((1, 1, 1, n, width), lambda a, bi, gi: (a, bi, gi, 0, 0))
    out_part = jax.ShapeDtypeStruct((kinds, b, g, n, width), BF16)
    return pl.pallas_call(
        _compress_kernel,
        grid=(kinds, b, g),
        in_specs=[pl.BlockSpec((1, 1, 1, n, flat), lambda a, bi, gi: (a, bi, gi, 0, 0)),
                  pl.BlockSpec((1, 2 * flat, CMP_HIDDEN), lambda a, bi, gi: (a, 0, 0)),
                  pl.BlockSpec((1, CMP_HIDDEN, width), lambda a, bi, gi: (a, 0, 0)),
                  pl.BlockSpec((1, SUBLANES, 2 * flat), lambda a, bi, gi: (a, 0, 0))],
        out_specs=(out_spec, out_spec),
        out_shape=(out_part, out_part),
        compiler_params=pltpu.CompilerParams(
            dimension_semantics=("arbitrary", "arbitrary", "arbitrary"),
            vmem_limit_bytes=_vmem_limit(n * flat * 4 + 2 * flat * CMP_HIDDEN * 4, 8 * 1024 * 1024)),
        name="compress",
    )(t, w1, w2, pe)


def _softmax_numerator(s, bias):
    s = s + bias
    m = jnp.max(s, axis=1, keepdims=True)
    m = jnp.where(m == -jnp.inf, 0.0, m)
    return jnp.exp2(s - m)


def _stack_heads(q):
    tq = q.shape[0]
    lane = lax.broadcasted_iota(jnp.int32, (tq, LANES), 1)
    rows = []
    for h in range(NSA_GROUP):
        slab = q[:, (h // 2) * LANES:(h // 2 + 1) * LANES]
        own = (lane < HEAD_DIM) if h % 2 == 0 else (lane >= HEAD_DIM)
        rows.append(jnp.where(own, slab, jnp.zeros_like(slab)))
    return jnp.concatenate(rows, axis=0)


def _group_gate(gates, gi, h, j):
    c0 = MISC_GATE0 + 3 * h + j
    c1 = MISC_GATE0 + 3 * (NSA_GROUP + h) + j
    return jnp.where(gi == 0, gates[:, c0:c0 + 1], gates[:, c1:c1 + 1])


def _cmpwin_kernel(q_ref, kc_hi_ref, kc_lo_ref, vc_ref, overlap_ref, kw_ref, vw_ref, misc_ref,
                   part_ref, selb_ref):
    tq = NSA_Q
    gi = pl.program_id(1)
    qi = pl.program_id(2)
    q0 = pl.multiple_of(qi * tq, tq)
    q4 = _stack_heads(q_ref[0])
    n_cmp = kc_hi_ref.shape[3]
    head_rows = [slice(h * tq, (h + 1) * tq) for h in range(NSA_GROUP)]

    s = _dot_nt(q4, kc_hi_ref[0, 0, 0]) + _dot_nt(q4, kc_lo_ref[0, 0, 0])
    col = lax.broadcasted_iota(jnp.int32, (tq, n_cmp), 1)
    t_row = q0 + lax.broadcasted_iota(jnp.int32, (tq, n_cmp), 0)
    cmp_bias = jnp.where(col * CMP_STRIDE + (CMP_BLOCK - 1) <= t_row, 0.0, -jnp.inf)
    pcs = []
    for r in head_rows:
        p = _softmax_numerator(s[r], cmp_bias)
        pcs.append(p * (1.0 / jnp.maximum(jnp.sum(p, axis=1, keepdims=True), 1e-30)))
    oc = jnp.dot(jnp.concatenate(pcs, axis=0).astype(BF16), vc_ref[0, 0, 0],
                 preferred_element_type=F32)

    pc_sum = pcs[0] + pcs[1] + pcs[2] + pcs[3]
    n_blk = selb_ref.shape[3]
    overlap = overlap_ref[...]
    imp = sum(_dot_nt(overlap, p) for p in _split3(pc_sum))
    blk = lax.broadcasted_iota(jnp.int32, (n_blk, tq), 0)
    cur = (q0 + lax.broadcasted_iota(jnp.int32, (n_blk, tq), 1)) // SLC_BLOCK
    forced = jnp.logical_or(blk == 0, jnp.logical_or(blk == cur, blk == cur - 1))
    imp = jnp.where(forced, jnp.inf, jnp.where(blk > cur, -jnp.inf, imp))

    def pick_one(_, carry):
        rem, sel = carry
        best = jnp.max(rem, axis=0, keepdims=True)
        first = jnp.min(jnp.where(rem == best, blk, n_blk), axis=0, keepdims=True)
        hit = blk == first
        return jnp.where(hit, -jnp.inf, rem), jnp.where(hit, 1.0, sel)

    _, sel = lax.fori_loop(0, min(N_SELECT, n_blk), pick_one, (imp, jnp.zeros_like(imp)),
                           unroll=True)
    selb_ref[0, 0] = jnp.where(sel.T > 0.5, 0.0, MASK_VALUE).astype(BF16)

    start = pl.multiple_of(jnp.maximum(q0 - WINDOW, 0), tq)
    kw = kw_ref[0, 0, pl.ds(start, WIN_KEYS), :]
    vw = vw_ref[0, 0, pl.ds(start, WIN_KEYS), :]
    sw = _dot_nt(q4, kw)
    pos = start + lax.broadcasted_iota(jnp.int32, (tq, WIN_KEYS), 1)
    dist = q0 + lax.broadcasted_iota(jnp.int32, (tq, WIN_KEYS), 0) - pos
    win_bias = jnp.where(jnp.logical_and(dist >= 0, dist < WINDOW), 0.0, -jnp.inf)
    pw = jnp.concatenate([_softmax_numerator(sw[r], win_bias) for r in head_rows], axis=0)
    ow = jnp.dot(pw.astype(BF16), vw, preferred_element_type=F32)
    ow = ow[:, 0:HEAD_DIM] * (1.0 / jnp.maximum(ow[:, HEAD_DIM:HEAD_DIM + 1], 1e-30))

    gates = _sigmoid(misc_ref[0])
    heads = []
    for h, r in enumerate(head_rows):
        heads.append(_group_gate(gates, gi, h, 0) * oc[r, 0:HEAD_DIM]
                     + _group_gate(gates, gi, h, 2) * ow[r])
    part_ref[0] = jnp.concatenate(heads, axis=1)


def _block_overlap(n_blk, n_cmp):
    ratio = SLC_BLOCK // CMP_STRIDE
    lo = np.arange(n_blk)[:, None] * ratio - (CMP_BLOCK // CMP_STRIDE - 1)
    i = np.arange(n_cmp)[None, :]
    n_overlap = (SLC_BLOCK + CMP_BLOCK) // CMP_STRIDE - 1
    return ((i >= lo) & (i < lo + n_overlap) & (i < n_cmp - 1)).astype(np.float32)


def _cmpwin(nq, cmp_hi, cmp_lo, kw, vw_ones, misc):
    b, s, w = nq.shape
    g = NSA_KV_HEADS
    tq = NSA_Q
    gw = NSA_GROUP * HEAD_DIM
    n_cmp = cmp_hi.shape[3]
    n_blk = s // SLC_BLOCK
    q_spec = pl.BlockSpec((1, tq, gw), lambda bi, gi, i: (bi, i, gi))
    key_spec = pl.BlockSpec((1, 1, 1, n_cmp, LANES), lambda bi, gi, i: (0, bi, gi, 0, 0))
    val_spec = pl.BlockSpec((1, 1, 1, n_cmp, LANES), lambda bi, gi, i: (1, bi, gi, 0, 0))
    seq_spec = pl.BlockSpec((1, 1, s, LANES), lambda bi, gi, i: (bi, gi, 0, 0))
    resident = 2 * s * LANES * 2 + 3 * n_cmp * LANES * 2 + n_blk * n_cmp * 2
    rows = NSA_GROUP * tq
    return pl.pallas_call(
        _cmpwin_kernel,
        grid=(b, g, s // tq),
        in_specs=[q_spec, key_spec, key_spec, val_spec,
                  pl.BlockSpec((n_blk, n_cmp), lambda bi, gi, i: (0, 0)),
                  seq_spec, seq_spec,
                  pl.BlockSpec((1, tq, LANES), lambda bi, gi, i: (bi, i, 0))],
        out_specs=(q_spec, pl.BlockSpec((1, 1, tq, n_blk), lambda bi, gi, i: (bi, gi, i, 0))),
        out_shape=(jax.ShapeDtypeStruct((b, s, w), F32),
                   jax.ShapeDtypeStruct((b, g, s, n_blk), BF16)),
        compiler_params=pltpu.CompilerParams(
            dimension_semantics=("arbitrary", "arbitrary", "arbitrary"),
            vmem_limit_bytes=_vmem_limit(resident + tq * gw * 8, 12 * rows * WIN_KEYS * 4)),
        name="cmpwin",
    )(nq, cmp_hi, cmp_lo, cmp_hi, jnp.asarray(_block_overlap(n_blk, n_cmp), BF16), kw, vw_ones, misc)


def _sel_kernel(q_ref, k_ref, v_ref, selb_ref, part_ref, misc_ref, z_ref, o_ref,
                qa_ref, *flash):
    tq = SEL_Q
    gi = pl.program_id(1)
    qi = pl.program_id(2)
    q0 = pl.multiple_of(qi * tq, tq)
    q = q_ref[0]
    selb = selb_ref[0, 0]
    lane = lax.broadcasted_iota(jnp.int32, (tq, LANES), 1)
    for h in range(NSA_GROUP):
        r = slice(h * tq, (h + 1) * tq)
        slab = q[:, (h // 2) * LANES:(h // 2 + 1) * LANES]
        own = (lane < HEAD_DIM) if h % 2 == 0 else (lane >= HEAD_DIM)
        qa_ref[r, 0:LANES] = jnp.where(own, slab, jnp.zeros_like(slab))
        qa_ref[r, LANES:2 * LANES] = selb

    n_full = q0 // SEL_KV

    def diag_mask(s):
        t_row = q0 + (lax.broadcasted_iota(jnp.int32, s.shape, 0) & (tq - 1))
        pos = n_full * SEL_KV + lax.broadcasted_iota(jnp.int32, s.shape, 1)
        return jnp.where(pos <= t_row, s, MASK_VALUE)

    _causal_flash(n_full, SEL_KV, qa_ref,
                  lambda k0: k_ref[0, 0, pl.ds(k0, SEL_KV), :],
                  lambda k0: v_ref[0, 0, pl.ds(k0, SEL_KV), :],
                  diag_mask, flash)
    acc_ref = flash[1]

    gates = _sigmoid(misc_ref[0])
    part = part_ref[0]
    heads = []
    for h in range(NSA_GROUP):
        r = slice(h * tq, (h + 1) * tq)
        o_sel = acc_ref[r, 0:HEAD_DIM] * (1.0 / acc_ref[r, HEAD_DIM:HEAD_DIM + 1])
        heads.append(part[:, h * HEAD_DIM:(h + 1) * HEAD_DIM] + _group_gate(gates, gi, h, 1) * o_sel)
    o_ref[0] = (jnp.concatenate(heads, axis=1) * z_ref[0].astype(F32)).astype(BF16)


def _sel(nq, k_aug, v_ones, selb, part, misc, nz):
    b, s, w = nq.shape
    g = NSA_KV_HEADS
    tq = SEL_Q
    gw = NSA_GROUP * HEAD_DIM
    n_blk = s // SLC_BLOCK
    rows = NSA_GROUP * tq
    q_spec = pl.BlockSpec((1, tq, gw), lambda bi, gi, i: (bi, i, gi))
    resident = s * 3 * LANES * 2
    return pl.pallas_call(
        _sel_kernel,
        grid=(b, g, s // tq),
        in_specs=[q_spec,
                  pl.BlockSpec((1, 1, s, 2 * LANES), lambda bi, gi, i: (bi, gi, 0, 0)),
                  pl.BlockSpec((1, 1, s, LANES), lambda bi, gi, i: (bi, gi, 0, 0)),
                  pl.BlockSpec((1, 1, tq, n_blk), lambda bi, gi, i: (bi, gi, i, 0)),
                  q_spec,
                  pl.BlockSpec((1, tq, LANES), lambda bi, gi, i: (bi, i, 0)),
                  q_spec],
        out_specs=q_spec,
        out_shape=jax.ShapeDtypeStruct((b, s, w), BF16),
        scratch_shapes=[pltpu.VMEM((rows, 2 * LANES), BF16)] + _flash_scratch(rows, SEL_KV, LANES),
        compiler_params=pltpu.CompilerParams(
            dimension_semantics=("arbitrary", "arbitrary", "arbitrary"),
            vmem_limit_bytes=_vmem_limit(resident + tq * gw * 10,
                                         _flash_scratch_bytes(rows, SEL_KV, LANES) + 4 * rows * SEL_KV * 4)),
        name="sel",
    )(nq, k_aug, v_ones, selb, part, misc, nz)


def _out_kernel(yf_ref, yn_ref, w_ref, g_ref, mod_ref, x_ref, o_ref):
    y = (jnp.dot(yf_ref[0], w_ref[0:FOX_WIDTH, :], preferred_element_type=F32)
         + jnp.dot(yn_ref[0], w_ref[FOX_WIDTH:, :], preferred_element_type=F32))
    yn = y * lax.rsqrt(jnp.mean(y * y, axis=-1, keepdims=True) + RMS_EPS)
    o_ref[0] = x_ref[0] + mod_ref[0, 2:3, :] * (yn * g_ref[...])


def _out(y_fox, y_nsa, w_out, g_post, mod, x):
    b, s, d = x.shape
    tm = PROJ_ROWS
    row = lambda bi, i: (bi, i, 0)
    half = pl.BlockSpec((1, tm, FOX_WIDTH), row)
    pipelined = 2 * tm * FOX_WIDTH * 2 + 2 * tm * d * 4 + d * d * 2
    return pl.pallas_call(
        _out_kernel,
        grid=(b, s // tm),
        in_specs=[half, half,
                  pl.BlockSpec((d, d), lambda bi, i: (0, 0)),
                  pl.BlockSpec((1, d), lambda bi, i: (0, 0)),
                  pl.BlockSpec((1, 3, d), lambda bi, i: (bi, 0, 0)),
                  pl.BlockSpec((1, tm, d), row)],
        out_specs=pl.BlockSpec((1, tm, d), row),
        out_shape=jax.ShapeDtypeStruct((b, s, d), F32),
        compiler_params=pltpu.CompilerParams(
            dimension_semantics=("arbitrary", "arbitrary"),
            vmem_limit_bytes=_vmem_limit(pipelined, 6 * tm * d * 4)),
        name="out",
    )(y_fox, y_nsa, w_out, g_post, mod, x)


def _rope_slabs(seq_len):
    inv = 1.0 / (ROPE_THETA ** (jnp.arange(0, HEAD_DIM, 2, dtype=F32) / HEAD_DIM))
    ang = jnp.arange(seq_len, dtype=F32)[:, None] * inv[None, :]
    cos, sin = jnp.cos(ang), jnp.sin(ang)
    reps = LANES // (HEAD_DIM // 2)
    sign = jnp.tile(jnp.concatenate([-jnp.ones((HEAD_DIM // 2,), F32), jnp.ones((HEAD_DIM // 2,), F32)]),
                    LANES // HEAD_DIM)
    return jnp.tile(cos, (1, reps)), jnp.tile(sin, (1, reps)) * sign[None, :]


def _reorder_w_in(w_in):
    fw, kv = FOX_WIDTH, NSA_KV_WIDTH
    o = 0
    cols = {}
    for name, n in (("fq", fw), ("fk", fw), ("fv", fw), ("ff", FOX_HEADS), ("fz", fw), ("nq", NSA_WIDTH),
                    ("kc", kv), ("vc", kv), ("ks", kv), ("vs", kv), ("kw", kv), ("vw", kv),
                    ("ng", 3 * NSA_HEADS), ("nz", NSA_WIDTH)):
        cols[name] = w_in[:, o:o + n]
        o += n
    pad = jnp.zeros((w_in.shape[0], LANES - FOX_HEADS - 3 * NSA_HEADS), w_in.dtype)
    order = ("fq", "fk", "fv", "fz", "nq", "kc", "vc", "ks", "vs", "kw", "vw", "nz", "ff", "ng")
    return jnp.concatenate([cols[k] for k in order] + [pad], axis=1).astype(BF16)


def _layer(x, c8, g_pre, g_post, w_ada, b_ada, w_in, b_forget, w_cmp_k1, w_cmp_k2,
           w_cmp_v1, w_cmp_v2, pe_cmp_k, pe_cmp_v, w_out, cos128, sin128):
    b, s, d = x.shape
    mod = _ada(c8, w_ada, b_ada)[:b].reshape(b, 3, d)
    bf128 = jnp.pad(b_forget, (0, LANES - FOX_HEADS)).reshape(1, LANES)
    (fq, fk, fv, fz, nq, cmp_in, ks_aug, vs_ones, kw2, vw_ones, nz, misc, log_f) = _proj(
        x, mod, g_pre.reshape(1, d), _reorder_w_in(w_in), cos128, sin128, bf128)

    cum_split = tuple(p.reshape(b, FOX_HEADS, s)
                      for p in _cumsum_lanes_split(log_f.reshape(b * FOX_HEADS, s)))
    y_fox = _fox(fq, fk, fv, cum_split, fz)

    flat = CMP_STRIDE * HEAD_DIM
    t = cmp_in.reshape(2, b, NSA_KV_HEADS, s // CMP_STRIDE, flat)
    pe = jnp.stack([pe_cmp_k.reshape(1, 2 * flat), pe_cmp_v.reshape(1, 2 * flat)])
    w2 = jnp.stack([w_cmp_k2, w_cmp_v2])
    cmp_hi, cmp_lo = _compress(t, jnp.stack([w_cmp_k1, w_cmp_v1]), jnp.concatenate([w2, w2], axis=-1),
                               jnp.broadcast_to(pe, (2, SUBLANES, 2 * flat)))
    part, selb = _cmpwin(nq, cmp_hi, cmp_lo, kw2, vw_ones, misc)
    y_nsa = _sel(nq, ks_aug, vs_ones, selb, part, misc, nz)

    return _out(y_fox, y_nsa, w_out.astype(BF16), g_post.reshape(1, d), mod, x)


def kernel(x, c, g_pre, g_post, w_ada, b_ada, w_in, b_forget, w_cmp_k1, w_cmp_k2,
           w_cmp_v1, w_cmp_v2, pe_cmp_k, pe_cmp_v, w_out):
    cos128, sin128 = _rope_slabs(x.shape[1])
    c8 = jnp.pad(c, ((0, SUBLANES - c.shape[0]), (0, 0)))
    for layer in range(g_pre.shape[0]):
        x = _layer(x, c8, g_pre[layer], g_post[layer], w_ada[layer], b_ada[layer], w_in[layer],
                   b_forget[layer], w_cmp_k1[layer], w_cmp_k2[layer], w_cmp_v1[layer],
                   w_cmp_v2[layer], pe_cmp_k[layer], pe_cmp_v[layer], w_out[layer], cos128, sin128)
    return x
```

```python
import functools

import jax
import jax.numpy as jnp
import numpy as np
from jax import lax
from jax.experimental import pallas as pl
from jax.experimental.pallas import tpu as pltpu

F32 = jnp.float32
BF16 = jnp.bfloat16

D_MODEL = 1024
HEAD_DIM = 64
FOX_WIDTH = 512
NSA_WIDTH = 512
FOX_HEADS = 8
NSA_HEADS = 8
NSA_KV_HEADS = 2
NSA_GROUP = 4
NSA_KV_WIDTH = 128
CMP_BLOCK = 32
CMP_STRIDE = 16
CMP_HIDDEN = 128
SLC_BLOCK = 64
N_SELECT = 16
WINDOW = 512
ROPE_THETA = 10000.0
RMS_EPS = 1e-6
LOG2E = 1.4426950408889634
QK_SCALE = HEAD_DIM ** -0.5 * LOG2E

LANES = 128
SUBLANES = 8
V7X_VMEM_BYTES = 64 * 1024 * 1024
MASK_VALUE = -1e30

PROJ_ROWS = 512
FOX_TILE = 512
NSA_Q = 128
SEL_Q = 128
SEL_KV = 512
WIN_KEYS = WINDOW + NSA_Q
FLASH_UNROLL = 4
CMP_QUARTERS = 4
FOX_AUG = 6

C_FQ, C_FK, C_FV, C_FZ, C_NQ = 0, 512, 1024, 1536, 2048
C_KC, C_VC, C_KS, C_VS, C_KW, C_VW = 2560, 2688, 2816, 2944, 3072, 3200
C_NZ, C_MISC, PROJ_COLS = 3328, 3840, 3968
MISC_GATE0 = FOX_HEADS


def _vmem_limit(pipelined_bytes, resident_bytes):
    need = 2 * pipelined_bytes + resident_bytes
    return int(min(max(need, 16 * 1024 * 1024), V7X_VMEM_BYTES - 8 * 1024 * 1024))


def _sigmoid(v):
    return 1.0 / (1.0 + jnp.exp(-v))


def _silu(v):
    return v * _sigmoid(v)


def _dot_nt(a, b):
    return lax.dot_general(a, b, (((1,), (1,)), ((), ())), preferred_element_type=F32)


def _split3(v):
    hi = v.astype(BF16)
    r1 = v - hi.astype(F32)
    mid = r1.astype(BF16)
    lo = (r1 - mid.astype(F32)).astype(BF16)
    return hi, mid, lo


def _ada_kernel(c_ref, w_ref, b_ref, o_ref):
    a = _silu(c_ref[...])
    o_ref[...] = jnp.dot(a, w_ref[...], precision=lax.Precision.HIGHEST,
                         preferred_element_type=F32) + b_ref[...]


def _ada(c8, w_ada, b_ada):
    n = w_ada.shape[1]
    blk = D_MODEL
    return pl.pallas_call(
        _ada_kernel,
        grid=(n // blk,),
        in_specs=[pl.BlockSpec((SUBLANES, D_MODEL), lambda j: (0, 0)),
                  pl.BlockSpec((D_MODEL, blk), lambda j: (0, j)),
                  pl.BlockSpec((1, blk), lambda j: (0, j))],
        out_specs=pl.BlockSpec((SUBLANES, blk), lambda j: (0, j)),
        out_shape=jax.ShapeDtypeStruct((SUBLANES, n), F32),
        compiler_params=pltpu.CompilerParams(
            dimension_semantics=("arbitrary",),
            vmem_limit_bytes=_vmem_limit(D_MODEL * blk * 4, 4 * 1024 * 1024)),
        name="ada",
    )(c8, w_ada, b_ada.reshape(1, n))


def _rope128(t, cos, sin_signed):
    lane = lax.broadcasted_iota(jnp.int32, t.shape, 1)
    first_half = (lane & (HEAD_DIM - 1)) < HEAD_DIM // 2
    partner = jnp.where(first_half,
                        pltpu.roll(t, LANES - HEAD_DIM // 2, 1),
                        pltpu.roll(t, HEAD_DIM // 2, 1))
    return t * cos + partner * sin_signed


def _proj_kernel(x_ref, mod_ref, g_ref, w_ref, cos_ref, sin_ref, bf_ref,
                 fq_ref, fk_ref, fv_ref, fz_ref, nq_ref, cmp_ref,
                 ks_ref, vs_ref, kw_ref, vw_ref, nz_ref, misc_ref, lf_ref):
    tm = x_ref.shape[1]
    x = x_ref[0]
    y = x * lax.rsqrt(jnp.mean(x * x, axis=-1, keepdims=True) + RMS_EPS)
    y = y * g_ref[...]
    h = (y * (1.0 + mod_ref[0, 1:2, :]) + mod_ref[0, 0:1, :]).astype(BF16)
    cos = cos_ref[...]
    sin = sin_ref[...]

    def mm(lo, n):
        return jnp.dot(h, w_ref[:, lo:lo + n], preferred_element_type=F32)

    fq_ref[0] = (mm(C_FQ, FOX_WIDTH) * QK_SCALE).astype(BF16)
    fk_ref[0] = mm(C_FK, FOX_WIDTH).astype(BF16)
    fv_ref[0] = mm(C_FV, FOX_WIDTH).astype(BF16)
    fz_ref[0] = _silu(mm(C_FZ, FOX_WIDTH)).astype(BF16)
    for j in range(NSA_WIDTH // LANES):
        t = mm(C_NQ + j * LANES, LANES)
        nq_ref[0, :, j * LANES:(j + 1) * LANES] = (_rope128(t, cos, sin) * QK_SCALE).astype(BF16)
    lane = lax.broadcasted_iota(jnp.int32, (tm, LANES), 1)
    low = lane < HEAD_DIM
    ones_col = jnp.where(lane == HEAD_DIM, 1.0, 0.0).astype(BF16)

    def doubled(slab):
        swapped = pltpu.roll(slab, HEAD_DIM, 1)
        return jnp.where(low, slab, swapped), jnp.where(low, swapped, slab)

    def with_ones(slab):
        swapped = pltpu.roll(slab, HEAD_DIM, 1)
        return jnp.where(low, slab, ones_col), jnp.where(low, swapped, ones_col)

    for kind, slab in enumerate((_rope128(mm(C_KC, LANES), cos, sin), mm(C_VC, LANES))):
        for g in range(NSA_KV_HEADS):
            cmp_ref[kind, 0, g] = slab[:, g * HEAD_DIM:(g + 1) * HEAD_DIM]
    pos = pl.program_id(1) * tm + lax.broadcasted_iota(jnp.int32, (tm, LANES), 0)
    onehot = jnp.where(pos // SLC_BLOCK == lane, 1.0, 0.0).astype(BF16)
    for g, k2 in enumerate(doubled(_rope128(mm(C_KS, LANES), cos, sin).astype(BF16))):
        ks_ref[0, g, :, 0:LANES] = k2
        ks_ref[0, g, :, LANES:2 * LANES] = onehot
    vs_ref[0, 0], vs_ref[0, 1] = with_ones(mm(C_VS, LANES).astype(BF16))
    kw_ref[0, 0], kw_ref[0, 1] = doubled(_rope128(mm(C_KW, LANES), cos, sin).astype(BF16))
    vw_ref[0, 0], vw_ref[0, 1] = with_ones(mm(C_VW, LANES).astype(BF16))
    nz_ref[0] = _silu(mm(C_NZ, NSA_WIDTH)).astype(BF16)
    misc = mm(C_MISC, LANES)
    misc_ref[0] = misc
    z = misc + bf_ref[...]
    log_f = jnp.minimum(z, 0.0) - jnp.log1p(jnp.exp(-jnp.abs(z)))
    lf_ref[0] = log_f.T[0:FOX_HEADS, :]


def _proj(x, mod, g_pre, w_cat, cos128, sin128, bf128):
    b, s, d = x.shape
    tm = PROJ_ROWS
    row = lambda bi, i: (bi, i, 0)
    wide = lambda dt: jax.ShapeDtypeStruct((b, s, FOX_WIDTH), dt)
    g = NSA_KV_HEADS
    grouped = lambda lanes: jax.ShapeDtypeStruct((b, g, s, lanes), BF16)
    grouped_spec = lambda lanes: pl.BlockSpec((1, g, tm, lanes), lambda bi, i: (bi, 0, i, 0))
    out_shape = (wide(BF16), wide(BF16), wide(BF16), wide(BF16), wide(BF16),
                 jax.ShapeDtypeStruct((2, b, g, s, HEAD_DIM), F32),
                 grouped(2 * LANES), grouped(LANES), grouped(LANES), grouped(LANES),
                 wide(BF16), jax.ShapeDtypeStruct((b, s, LANES), F32),
                 jax.ShapeDtypeStruct((b, FOX_HEADS, s), F32))
    wide_spec = pl.BlockSpec((1, tm, FOX_WIDTH), row)
    out_specs = (wide_spec,) * 5 + (
        pl.BlockSpec((2, 1, g, tm, HEAD_DIM), lambda bi, i: (0, bi, 0, i, 0)),
        grouped_spec(2 * LANES), grouped_spec(LANES), grouped_spec(LANES), grouped_spec(LANES),
        wide_spec, pl.BlockSpec((1, tm, LANES), row),
        pl.BlockSpec((1, FOX_HEADS, tm), lambda bi, i: (bi, 0, i)))
    pipelined = (tm * d * 4 + tm * (6 * FOX_WIDTH * 2 + 2 * g * LANES * 4 + 5 * g * LANES * 2 + LANES * 4)
                 + d * PROJ_COLS * 2)
    return pl.pallas_call(
        _proj_kernel,
        grid=(b, s // tm),
        in_specs=[pl.BlockSpec((1, tm, d), row),
                  pl.BlockSpec((1, 3, d), lambda bi, i: (bi, 0, 0)),
                  pl.BlockSpec((1, d), lambda bi, i: (0, 0)),
                  pl.BlockSpec((d, PROJ_COLS), lambda bi, i: (0, 0)),
                  pl.BlockSpec((tm, LANES), lambda bi, i: (i, 0)),
                  pl.BlockSpec((tm, LANES), lambda bi, i: (i, 0)),
                  pl.BlockSpec((1, LANES), lambda bi, i: (0, 0))],
        out_specs=out_specs,
        out_shape=out_shape,
        compiler_params=pltpu.CompilerParams(
            dimension_semantics=("arbitrary", "arbitrary"),
            vmem_limit_bytes=_vmem_limit(pipelined, 8 * 1024 * 1024)),
        name="proj",
    )(x, mod, g_pre, w_cat, cos128, sin128, bf128)


def _cumsum_kernel(x_ref, hi_ref, mid_ref, lo_ref, *, chunks):
    x = x_ref[...]
    n = x.shape[0]
    parts = _split3(x)
    r = lax.broadcasted_iota(jnp.int32, (LANES, LANES), 0)
    c = lax.broadcasted_iota(jnp.int32, (LANES, LANES), 1)
    tri = (r <= c).astype(BF16)
    rr = lax.broadcasted_iota(jnp.int32, (n, n), 0)
    cc = lax.broadcasted_iota(jnp.int32, (n, n), 1)
    earlier = jnp.logical_and(cc < rr, (cc // chunks) == (rr // chunks)).astype(BF16)
    within = sum(jnp.dot(p, tri, preferred_element_type=F32) for p in parts)
    before = sum(jnp.dot(earlier, p, preferred_element_type=F32) for p in parts)
    total = (within + jnp.sum(before, axis=-1, keepdims=True)) * LOG2E
    hi_ref[...], mid_ref[...], lo_ref[...] = _split3(total)


def _cumsum_lanes_split(v):
    rows, s = v.shape
    chunks = s // LANES
    n = rows * chunks
    part = jax.ShapeDtypeStruct((n, LANES), BF16)
    parts = pl.pallas_call(
        functools.partial(_cumsum_kernel, chunks=chunks),
        out_shape=(part, part, part),
        compiler_params=pltpu.CompilerParams(
            vmem_limit_bytes=_vmem_limit(2 * n * LANES * 4, 6 * n * n)),
        name="cumsum",
    )(v.reshape(n, LANES))
    return tuple(p.reshape(rows, s) for p in parts)


def _flash_scratch(rows, tile, acc_lanes):
    return ([pltpu.VMEM((rows, LANES), F32), pltpu.VMEM((rows, acc_lanes), F32)]
            + [pltpu.VMEM((rows, tile), F32)] * 2
            + [pltpu.VMEM((rows, tile), BF16)] * 2
            + [pltpu.VMEM((rows, LANES), F32)] * 2)


def _flash_scratch_bytes(rows, tile, acc_lanes):
    return rows * (LANES * 4 + acc_lanes * 4 + 2 * tile * 4 + 2 * tile * 2 + 2 * LANES * 4)


def _causal_flash(n_full, tile, q_ref, load_k, load_v, diag_mask, scratch):
    m_ref, acc_ref, s0, s1, p0, p1, a0, a1 = scratch
    s_bufs, p_bufs, a_bufs = (s0, s1), (p0, p1), (a0, a1)
    reps = acc_ref.shape[-1] // LANES

    def logits(j):
        return _dot_nt(q_ref[...], load_k(pl.multiple_of(j * tile, tile)))

    def softmax(s):
        m_prev = m_ref[...]
        m_next = jnp.maximum(m_prev, jnp.max(s, axis=1, keepdims=True))
        m_ref[...] = m_next
        p = jnp.exp2(s - pltpu.repeat(m_next, tile // LANES, 1))
        return p.astype(BF16), jnp.exp2(m_prev - m_next)

    def accumulate(p, alpha, j):
        pv = jnp.dot(p, load_v(pl.multiple_of(j * tile, tile)), preferred_element_type=F32)
        acc_ref[...] = acc_ref[...] * (alpha if reps == 1 else pltpu.repeat(alpha, reps, 1)) + pv

    def stage(j, cur):
        s_bufs[1 - cur][...] = logits(j + 1)
        p_bufs[cur][...], a_bufs[cur][...] = softmax(s_bufs[cur][...])
        accumulate(p_bufs[1 - cur][...], a_bufs[1 - cur][...], jnp.maximum(j - 1, 0))

    def finish(cur):
        p, alpha = softmax(diag_mask(s_bufs[cur][...]))
        accumulate(p_bufs[1 - cur][...], a_bufs[1 - cur][...], jnp.maximum(n_full - 1, 0))
        accumulate(p, alpha, n_full)

    m_ref[...] = jnp.full(m_ref.shape, MASK_VALUE, F32)
    acc_ref[...] = jnp.zeros(acc_ref.shape, F32)
    s_bufs[0][...] = logits(0)
    p_bufs[1][...] = jnp.zeros(p_bufs[1].shape, BF16)
    a_bufs[1][...] = jnp.ones(a_bufs[1].shape, F32)

    def run_stages(first, count):
        for i in range(count):
            stage(first + i, i % 2)

    def unrolled(jj, carry):
        run_stages(FLASH_UNROLL * jj, FLASH_UNROLL)
        return carry

    lax.fori_loop(0, n_full // FLASH_UNROLL, unrolled, 0)
    rest = n_full % FLASH_UNROLL

    @pl.when(rest >= 2)
    def _():
        run_stages(n_full - rest, 2)

    @pl.when(rest % 2 == 1)
    def _():
        stage(n_full - 1, 0)
        finish(1)

    @pl.when(rest % 2 == 0)
    def _():
        finish(0)


def _fox_aug(cum_split):
    b, h, s = cum_split[0].shape
    terms = jnp.concatenate(list(cum_split) + [jnp.ones((b, 1, s), BF16)], axis=1)
    terms = jnp.transpose(terms, (0, 2, 1))
    place_q = np.zeros((3 * h + 1, h * HEAD_DIM), np.float32)
    place_k = np.zeros((3 * h + 1, h * HEAD_DIM), np.float32)
    half = FOX_AUG // 2
    for head in range(h):
        for part in range(half):
            place_q[part * h + head, head * HEAD_DIM + part] = 1.0
            place_q[3 * h, head * HEAD_DIM + half + part] = 1.0
            place_k[3 * h, head * HEAD_DIM + part] = 1.0
            place_k[part * h + head, head * HEAD_DIM + half + part] = -1.0
    scatter = lambda place: jnp.einsum("bsr,rl->bsl", terms, jnp.asarray(place, BF16),
                                       preferred_element_type=F32).astype(BF16)
    return scatter(place_q), scatter(place_k)


def _fox_kernel(q_ref, qa_ref, k_ref, ka_ref, v_ref, z_ref, o_ref, qm_ref, ones_ref, *flash):
    t = FOX_TILE
    qi = pl.program_id(2)
    q = q_ref[0]
    qa = qa_ref[0]
    lane = lax.broadcasted_iota(jnp.int32, q.shape, 1)
    for hh in range(2):
        r = slice(hh * t, (hh + 1) * t)
        own = (lane < HEAD_DIM) if hh == 0 else (lane >= HEAD_DIM)
        qm_ref[r, 0:LANES] = jnp.where(own, q, jnp.zeros_like(q))
        qm_ref[r, LANES:2 * LANES] = jnp.where(own, qa, jnp.zeros_like(qa))
    ones_lane = lax.broadcasted_iota(jnp.int32, ones_ref.shape, 1)
    ones_ref[...] = jnp.where(ones_lane == 0, 1.0, 0.0).astype(BF16)

    def load_k(k0):
        return jnp.concatenate([k_ref[0, pl.ds(k0, t), :], ka_ref[0, pl.ds(k0, t), :]], axis=1)

    def load_v(k0):
        return jnp.concatenate([v_ref[0, pl.ds(k0, t), :], ones_ref[...]], axis=1)

    def diag_mask(s):
        row = lax.broadcasted_iota(jnp.int32, s.shape, 0) & (t - 1)
        col = lax.broadcasted_iota(jnp.int32, s.shape, 1)
        return jnp.where(col <= row, s, MASK_VALUE)

    _causal_flash(qi, t, qm_ref, load_k, load_v, diag_mask, flash)
    acc_ref = flash[1]
    o0 = acc_ref[0:t, 0:LANES] * (1.0 / acc_ref[0:t, LANES:LANES + 1])
    o1 = acc_ref[t:2 * t, 0:LANES] * (1.0 / acc_ref[t:2 * t, LANES:LANES + 1])
    o = jnp.where(lane < HEAD_DIM, o0, o1)
    o_ref[0] = (o * z_ref[0].astype(F32)).astype(BF16)


def _fox(fq, fk, fv, cum_split, fz):
    b, s, w = fq.shape
    t = FOX_TILE
    pairs = w // LANES
    qa, ka = _fox_aug(cum_split)
    tile = pl.BlockSpec((1, t, LANES), lambda bi, hp, i: (bi, i, hp))
    full = pl.BlockSpec((1, s, LANES), lambda bi, hp, i: (bi, 0, hp))
    resident = 3 * s * LANES * 2
    scratch = 2 * t * 2 * LANES * 2 + t * LANES * 2 + _flash_scratch_bytes(2 * t, t, 2 * LANES)
    return pl.pallas_call(
        _fox_kernel,
        grid=(b, pairs, s // t),
        in_specs=[tile, tile, full, full, full, tile],
        out_specs=tile,
        out_shape=jax.ShapeDtypeStruct((b, s, w), BF16),
        scratch_shapes=[pltpu.VMEM((2 * t, 2 * LANES), BF16),
                        pltpu.VMEM((t, LANES), BF16)] + _flash_scratch(2 * t, t, 2 * LANES),
        compiler_params=pltpu.CompilerParams(
            dimension_semantics=("arbitrary", "arbitrary", "arbitrary"),
            vmem_limit_bytes=_vmem_limit(resident + 4 * t * LANES * 2, scratch + 4 * 2 * t * t * 4)),
        name="fox",
    )(fq, qa, fk, ka, fv, fz)


def _compress_kernel(t_ref, w1_ref, w2_ref, pe_ref, hi_ref, lo_ref):
    hp = lax.Precision.HIGHEST
    half = CMP_STRIDE * HEAD_DIM
    tt = t_ref[0, 0, 0]
    n = tt.shape[0]
    first = jnp.dot(tt, w1_ref[0, 0:half, :], precision=hp, preferred_element_type=F32)
    second = jnp.dot(tt, w1_ref[0, half:2 * half, :], precision=hp, preferred_element_type=F32)
    pe_term = jnp.dot(pe_ref[0], w1_ref[0], precision=hp, preferred_element_type=F32)[0:1, :]
    hidden = first + pltpu.roll(second, n - 1, 0) + pe_term
    out = jnp.dot(_silu(hidden), w2_ref[0], precision=hp, preferred_element_type=F32)
    hi = out.astype(BF16)
    hi_ref[0, 0, 0] = hi
    lo_ref[0, 0, 0] = (out - hi.astype(F32)).astype(BF16)


def _compress(t, w1, w2, pe):
    kinds, b, g, n, flat = t.shape
    width = w2.shape[-1]
    out_spec = pl.BlockSpec((1, 1, 1, n, width), lambda a, bi, gi: (a, bi, gi, 0, 0))
    out_part = jax.ShapeDtypeStruct((kinds, b, g, n, width), BF16)
    return pl.pallas_call(
        _compress_kernel,
        grid=(kinds, b, g),
        in_specs=[pl.BlockSpec((1, 1, 1, n, flat), lambda a, bi, gi: (a, bi, gi, 0, 0)),
                  pl.BlockSpec((1, 2 * flat, CMP_HIDDEN), lambda a, bi, gi: (a, 0, 0)),
                  pl.BlockSpec((1, CMP_HIDDEN, width), lambda a, bi, gi: (a, 0, 0)),
                  pl.BlockSpec((1, SUBLANES, 2 * flat), lambda a, bi, gi: (a, 0, 0))],
        out_specs=(out_spec, out_spec),
        out_shape=(out_part, out_part),
        compiler_params=pltpu.CompilerParams(
            dimension_semantics=("arbitrary", "arbitrary", "arbitrary"),
            vmem_limit_bytes=_vmem_limit(n * flat * 4 + 2 * flat * CMP_HIDDEN * 4, 8 * 1024 * 1024)),
        name="compress",
    )(t, w1, w2, pe)


def _softmax_numerator(s, bias):
    s = s + bias
    m = jnp.max(s, axis=1, keepdims=True)
    m = jnp.where(m == -jnp.inf, 0.0, m)
    return jnp.exp2(s - m)


def _stack_heads(q):
    tq = q.shape[0]
    lane = lax.broadcasted_iota(jnp.int32, (tq, LANES), 1)
    rows = []
    for h in range(NSA_GROUP):
        slab = q[:, (h // 2) * LANES:(h // 2 + 1) * LANES]
        own = (lane < HEAD_DIM) if h % 2 == 0 else (lane >= HEAD_DIM)
        rows.append(jnp.where(own, slab, jnp.zeros_like(slab)))
    return jnp.concatenate(rows, axis=0)


def _group_gate(gates, gi, h, j):
    c0 = MISC_GATE0 + 3 * h + j
    c1 = MISC_GATE0 + 3 * (NSA_GROUP + h) + j
    return jnp.where(gi == 0, gates[:, c0:c0 + 1], gates[:, c1:c1 + 1])


def _cmpwin_kernel(q_ref, kc_hi_ref, kc_lo_ref, vc_ref, overlap_ref, kw_ref, vw_ref, misc_ref,
                   part_ref, selb_ref):
    gi = pl.program_id(1)
    q0 = pl.multiple_of(pl.program_id(2) * NSA_Q, NSA_Q)
    quarter = kc_hi_ref.shape[3] // CMP_QUARTERS
    reach = q0 // (quarter * CMP_STRIDE)
    refs = (q_ref, kc_hi_ref, kc_lo_ref, vc_ref, overlap_ref, kw_ref, vw_ref, misc_ref,
            part_ref, selb_ref)
    for quarters in range(1, CMP_QUARTERS + 1):
        pl.when(reach == quarters - 1)(
            functools.partial(_cmpwin_body, refs, gi, q0, quarters * quarter))


def _cmpwin_body(refs, gi, q0, n_cmp):
    (q_ref, kc_hi_ref, kc_lo_ref, vc_ref, overlap_ref, kw_ref, vw_ref, misc_ref,
     part_ref, selb_ref) = refs
    tq = NSA_Q
    q4 = _stack_heads(q_ref[0])
    n_blk = n_cmp * CMP_STRIDE // SLC_BLOCK
    head_rows = [slice(h * tq, (h + 1) * tq) for h in range(NSA_GROUP)]

    s = (_dot_nt(q4, kc_hi_ref[0, 0, 0, 0:n_cmp, :])
         + _dot_nt(q4, kc_lo_ref[0, 0, 0, 0:n_cmp, :]))
    col = lax.broadcasted_iota(jnp.int32, (tq, n_cmp), 1)
    t_row = q0 + lax.broadcasted_iota(jnp.int32, (tq, n_cmp), 0)
    cmp_bias = jnp.where(col * CMP_STRIDE + (CMP_BLOCK - 1) <= t_row, 0.0, -jnp.inf)
    pcs = []
    for r in head_rows:
        p = _softmax_numerator(s[r], cmp_bias)
        pcs.append(p * (1.0 / jnp.maximum(jnp.sum(p, axis=1, keepdims=True), 1e-30)))
    oc = jnp.dot(jnp.concatenate(pcs, axis=0).astype(BF16), vc_ref[0, 0, 0, 0:n_cmp, :],
                 preferred_element_type=F32)

    pc_sum = pcs[0] + pcs[1] + pcs[2] + pcs[3]
    overlap = overlap_ref[0:n_blk, 0:n_cmp]
    imp = sum(_dot_nt(overlap, p) for p in _split3(pc_sum))
    blk = lax.broadcasted_iota(jnp.int32, (n_blk, tq), 0)
    cur = (q0 + lax.broadcasted_iota(jnp.int32, (n_blk, tq), 1)) // SLC_BLOCK
    forced = jnp.logical_or(blk == 0, jnp.logical_or(blk == cur, blk == cur - 1))
    imp = jnp.where(forced, jnp.inf, jnp.where(blk > cur, -jnp.inf, imp))

    def pick_one(_, carry):
        rem, sel = carry
        best = jnp.max(rem, axis=0, keepdims=True)
        first = jnp.min(jnp.where(rem == best, blk, n_blk), axis=0, keepdims=True)
        hit = blk == first
        return jnp.where(hit, -jnp.inf, rem), jnp.where(hit, 1.0, sel)

    _, sel = lax.fori_loop(0, min(N_SELECT, n_blk), pick_one, (imp, jnp.zeros_like(imp)),
                           unroll=True)
    all_blk = selb_ref.shape[3]
    if n_blk < all_blk:
        sel = jnp.concatenate([sel, jnp.zeros((all_blk - n_blk, tq), F32)], axis=0)
    selb_ref[0, 0] = jnp.where(sel.T > 0.5, 0.0, MASK_VALUE).astype(BF16)

    start = pl.multiple_of(jnp.maximum(q0 - WINDOW, 0), tq)
    kw = kw_ref[0, 0, pl.ds(start, WIN_KEYS), :]
    vw = vw_ref[0, 0, pl.ds(start, WIN_KEYS), :]
    sw = _dot_nt(q4, kw)
    pos = start + lax.broadcasted_iota(jnp.int32, (tq, WIN_KEYS), 1)
    dist = q0 + lax.broadcasted_iota(jnp.int32, (tq, WIN_KEYS), 0) - pos
    win_bias = jnp.where(jnp.logical_and(dist >= 0, dist < WINDOW), 0.0, -jnp.inf)
    pw = jnp.concatenate([_softmax_numerator(sw[r], win_bias) for r in head_rows], axis=0)
    ow = jnp.dot(pw.astype(BF16), vw, preferred_element_type=F32)
    ow = ow[:, 0:HEAD_DIM] * (1.0 / jnp.maximum(ow[:, HEAD_DIM:HEAD_DIM + 1], 1e-30))

    gates = _sigmoid(misc_ref[0])
    heads = []
    for h, r in enumerate(head_rows):
        heads.append(_group_gate(gates, gi, h, 0) * oc[r, 0:HEAD_DIM]
                     + _group_gate(gates, gi, h, 2) * ow[r])
    part_ref[0] = jnp.concatenate(heads, axis=1)


def _block_overlap(n_blk, n_cmp):
    ratio = SLC_BLOCK // CMP_STRIDE
    lo = np.arange(n_blk)[:, None] * ratio - (CMP_BLOCK // CMP_STRIDE - 1)
    i = np.arange(n_cmp)[None, :]
    n_overlap = (SLC_BLOCK + CMP_BLOCK) // CMP_STRIDE - 1
    return ((i >= lo) & (i < lo + n_overlap) & (i < n_cmp - 1)).astype(np.float32)


def _cmpwin(nq, cmp_hi, cmp_lo, kw, vw_ones, misc):
    b, s, w = nq.shape
    g = NSA_KV_HEADS
    tq = NSA_Q
    gw = NSA_GROUP * HEAD_DIM
    n_cmp = cmp_hi.shape[3]
    n_blk = s // SLC_BLOCK
    q_spec = pl.BlockSpec((1, tq, gw), lambda bi, gi, i: (bi, i, gi))
    key_spec = pl.BlockSpec((1, 1, 1, n_cmp, LANES), lambda bi, gi, i: (0, bi, gi, 0, 0))
    val_spec = pl.BlockSpec((1, 1, 1, n_cmp, LANES), lambda bi, gi, i: (1, bi, gi, 0, 0))
    seq_spec = pl.BlockSpec((1, 1, s, LANES), lambda bi, gi, i: (bi, gi, 0, 0))
    resident = 2 * s * LANES * 2 + 3 * n_cmp * LANES * 2 + n_blk * n_cmp * 2
    rows = NSA_GROUP * tq
    return pl.pallas_call(
        _cmpwin_kernel,
        grid=(b, g, s // tq),
        in_specs=[q_spec, key_spec, key_spec, val_spec,
                  pl.BlockSpec((n_blk, n_cmp), lambda bi, gi, i: (0, 0)),
                  seq_spec, seq_spec,
                  pl.BlockSpec((1, tq, LANES), lambda bi, gi, i: (bi, i, 0))],
        out_specs=(q_spec, pl.BlockSpec((1, 1, tq, n_blk), lambda bi, gi, i: (bi, gi, i, 0))),
        out_shape=(jax.ShapeDtypeStruct((b, s, w), F32),
                   jax.ShapeDtypeStruct((b, g, s, n_blk), BF16)),
        compiler_params=pltpu.CompilerParams(
            dimension_semantics=("arbitrary", "arbitrary", "arbitrary"),
            vmem_limit_bytes=_vmem_limit(resident + tq * gw * 8, 12 * rows * WIN_KEYS * 4)),
        name="cmpwin",
    )(nq, cmp_hi, cmp_lo, cmp_hi, jnp.asarray(_block_overlap(n_blk, n_cmp), BF16), kw, vw_ones, misc)


def _sel_kernel(q_ref, k_ref, v_ref, selb_ref, part_ref, misc_ref, z_ref, o_ref,
                qa_ref, *flash):
    tq = SEL_Q
    gi = pl.program_id(1)
    qi = pl.program_id(2)
    q0 = pl.multiple_of(qi * tq, tq)
    q = q_ref[0]
    selb = selb_ref[0, 0]
    lane = lax.broadcasted_iota(jnp.int32, (tq, LANES), 1)
    for h in range(NSA_GROUP):
        r = slice(h * tq, (h + 1) * tq)
        slab = q[:, (h // 2) * LANES:(h // 2 + 1) * LANES]
        own = (lane < HEAD_DIM) if h % 2 == 0 else (lane >= HEAD_DIM)
        qa_ref[r, 0:LANES] = jnp.where(own, slab, jnp.zeros_like(slab))
        qa_ref[r, LANES:2 * LANES] = selb

    n_full = q0 // SEL_KV

    def diag_mask(s):
        t_row = q0 + (lax.broadcasted_iota(jnp.int32, s.shape, 0) & (tq - 1))
        pos = n_full * SEL_KV + lax.broadcasted_iota(jnp.int32, s.shape, 1)
        return jnp.where(pos <= t_row, s, MASK_VALUE)

    _causal_flash(n_full, SEL_KV, qa_ref,
                  lambda k0: k_ref[0, 0, pl.ds(k0, SEL_KV), :],
                  lambda k0: v_ref[0, 0, pl.ds(k0, SEL_KV), :],
                  diag_mask, flash)
    acc_ref = flash[1]

    gates = _sigmoid(misc_ref[0])
    part = part_ref[0]
    heads = []
    for h in range(NSA_GROUP):
        r = slice(h * tq, (h + 1) * tq)
        o_sel = acc_ref[r, 0:HEAD_DIM] * (1.0 / acc_ref[r, HEAD_DIM:HEAD_DIM + 1])
        heads.append(part[:, h * HEAD_DIM:(h + 1) * HEAD_DIM] + _group_gate(gates, gi, h, 1) * o_sel)
    o_ref[0] = (jnp.concatenate(heads, axis=1) * z_ref[0].astype(F32)).astype(BF16)


def _sel(nq, k_aug, v_ones, selb, part, misc, nz):
    b, s, w = nq.shape
    g = NSA_KV_HEADS
    tq = SEL_Q
    gw = NSA_GROUP * HEAD_DIM
    n_blk = s // SLC_BLOCK
    rows = NSA_GROUP * tq
    q_spec = pl.BlockSpec((1, tq, gw), lambda bi, gi, i: (bi, i, gi))
    resident = s * 3 * LANES * 2
    return pl.pallas_call(
        _sel_kernel,
        grid=(b, g, s // tq),
        in_specs=[q_spec,
                  pl.BlockSpec((1, 1, s, 2 * LANES), lambda bi, gi, i: (bi, gi, 0, 0)),
                  pl.BlockSpec((1, 1, s, LANES), lambda bi, gi, i: (bi, gi, 0, 0)),
                  pl.BlockSpec((1, 1, tq, n_blk), lambda bi, gi, i: (bi, gi, i, 0)),
                  q_spec,
                  pl.BlockSpec((1, tq, LANES), lambda bi, gi, i: (bi, i, 0)),
                  q_spec],
        out_specs=q_spec,
        out_shape=jax.ShapeDtypeStruct((b, s, w), BF16),
        scratch_shapes=[pltpu.VMEM((rows, 2 * LANES), BF16)] + _flash_scratch(rows, SEL_KV, LANES),
        compiler_params=pltpu.CompilerParams(
            dimension_semantics=("arbitrary", "arbitrary", "arbitrary"),
            vmem_limit_bytes=_vmem_limit(resident + tq * gw * 10,
                                         _flash_scratch_bytes(rows, SEL_KV, LANES) + 4 * rows * SEL_KV * 4)),
        name="sel",
    )(nq, k_aug, v_ones, selb, part, misc, nz)


def _out_kernel(yf_ref, yn_ref, w_ref, g_ref, mod_ref, x_ref, o_ref):
    y = (jnp.dot(yf_ref[0], w_ref[0:FOX_WIDTH, :], preferred_element_type=F32)
         + jnp.dot(yn_ref[0], w_ref[FOX_WIDTH:, :], preferred_element_type=F32))
    yn = y * lax.rsqrt(jnp.mean(y * y, axis=-1, keepdims=True) + RMS_EPS)
    o_ref[0] = x_ref[0] + mod_ref[0, 2:3, :] * (yn * g_ref[...])


def _out(y_fox, y_nsa, w_out, g_post, mod, x):
    b, s, d = x.shape
    tm = PROJ_ROWS
    row = lambda bi, i: (bi, i, 0)
    half = pl.BlockSpec((1, tm, FOX_WIDTH), row)
    pipelined = 2 * tm * FOX_WIDTH * 2 + 2 * tm * d * 4 + d * d * 2
    return pl.pallas_call(
        _out_kernel,
        grid=(b, s // tm),
        in_specs=[half, half,
                  pl.BlockSpec((d, d), lambda bi, i: (0, 0)),
                  pl.BlockSpec((1, d), lambda bi, i: (0, 0)),
                  pl.BlockSpec((1, 3, d), lambda bi, i: (bi, 0, 0)),
                  pl.BlockSpec((1, tm, d), row)],
        out_specs=pl.BlockSpec((1, tm, d), row),
        out_shape=jax.ShapeDtypeStruct((b, s, d), F32),
        compiler_params=pltpu.CompilerParams(
            dimension_semantics=("arbitrary", "arbitrary"),
            vmem_limit_bytes=_vmem_limit(pipelined, 6 * tm * d * 4)),
        name="out",
    )(y_fox, y_nsa, w_out, g_post, mod, x)


def _rope_slabs(seq_len):
    inv = 1.0 / (ROPE_THETA ** (jnp.arange(0, HEAD_DIM, 2, dtype=F32) / HEAD_DIM))
    ang = jnp.arange(seq_len, dtype=F32)[:, None] * inv[None, :]
    cos, sin = jnp.cos(ang), jnp.sin(ang)
    reps = LANES // (HEAD_DIM // 2)
    sign = jnp.tile(jnp.concatenate([-jnp.ones((HEAD_DIM // 2,), F32), jnp.ones((HEAD_DIM // 2,), F32)]),
                    LANES // HEAD_DIM)
    return jnp.tile(cos, (1, reps)), jnp.tile(sin, (1, reps)) * sign[None, :]


def _reorder_w_in(w_in):
    fw, kv = FOX_WIDTH, NSA_KV_WIDTH
    o = 0
    cols = {}
    for name, n in (("fq", fw), ("fk", fw), ("fv", fw), ("ff", FOX_HEADS), ("fz", fw), ("nq", NSA_WIDTH),
                    ("kc", kv), ("vc", kv), ("ks", kv), ("vs", kv), ("kw", kv), ("vw", kv),
                    ("ng", 3 * NSA_HEADS), ("nz", NSA_WIDTH)):
        cols[name] = w_in[:, o:o + n]
        o += n
    pad = jnp.zeros((w_in.shape[0], LANES - FOX_HEADS - 3 * NSA_HEADS), w_in.dtype)
    order = ("fq", "fk", "fv", "fz", "nq", "kc", "vc", "ks", "vs", "kw", "vw", "nz", "ff", "ng")
    return jnp.concatenate([cols[k] for k in order] + [pad], axis=1).astype(BF16)


def _layer(x, c8, g_pre, g_post, w_ada, b_ada, w_in, b_forget, w_cmp_k1, w_cmp_k2,
           w_cmp_v1, w_cmp_v2, pe_cmp_k, pe_cmp_v, w_out, cos128, sin128):
    b, s, d = x.shape
    mod = _ada(c8, w_ada, b_ada)[:b].reshape(b, 3, d)
    bf128 = jnp.pad(b_forget, (0, LANES - FOX_HEADS)).reshape(1, LANES)
    (fq, fk, fv, fz, nq, cmp_in, ks_aug, vs_ones, kw2, vw_ones, nz, misc, log_f) = _proj(
        x, mod, g_pre.reshape(1, d), _reorder_w_in(w_in), cos128, sin128, bf128)

    cum_split = tuple(p.reshape(b, FOX_HEADS, s)
                      for p in _cumsum_lanes_split(log_f.reshape(b * FOX_HEADS, s)))
    y_fox = _fox(fq, fk, fv, cum_split, fz)

    flat = CMP_STRIDE * HEAD_DIM
    t = cmp_in.reshape(2, b, NSA_KV_HEADS, s // CMP_STRIDE, flat)
    pe = jnp.stack([pe_cmp_k.reshape(1, 2 * flat), pe_cmp_v.reshape(1, 2 * flat)])
    w2 = jnp.stack([w_cmp_k2, w_cmp_v2])
    cmp_hi, cmp_lo = _compress(t, jnp.stack([w_cmp_k1, w_cmp_v1]), jnp.concatenate([w2, w2], axis=-1),
                               jnp.broadcast_to(pe, (2, SUBLANES, 2 * flat)))
    part, selb = _cmpwin(nq, cmp_hi, cmp_lo, kw2, vw_ones, misc)
    y_nsa = _sel(nq, ks_aug, vs_ones, selb, part, misc, nz)

    return _out(y_fox, y_nsa, w_out.astype(BF16), g_post.reshape(1, d), mod, x)


def kernel(x, c, g_pre, g_post, w_ada, b_ada, w_in, b_forget, w_cmp_k1, w_cmp_k2,
           w_cmp_v1, w_cmp_v2, pe_cmp_k, pe_cmp_v, w_out):
    cos128, sin128 = _rope_slabs(x.shape[1])
    c8 = jnp.pad(c, ((0, SUBLANES - c.shape[0]), (0, 0)))
    for layer in range(g_pre.shape[0]):
        x = _layer(x, c8, g_pre[layer], g_post[layer], w_ada[layer], b_ada[layer], w_in[layer],
                   b_forget[layer], w_cmp_k1[layer], w_cmp_k2[layer], w_cmp_v1[layer],
                   w_cmp_v2[layer], pe_cmp_k[layer], pe_cmp_v[layer], w_out[layer], cos128, sin128)
    return x
```

```python
import functools

import jax
import jax.numpy as jnp
import numpy as np
from jax import lax
from jax.experimental import pallas as pl
from jax.experimental.pallas import tpu as pltpu

F32 = jnp.float32
BF16 = jnp.bfloat16

D_MODEL = 1024
HEAD_DIM = 64
FOX_WIDTH = 512
NSA_WIDTH = 512
FOX_HEADS = 8
NSA_HEADS = 8
NSA_KV_HEADS = 2
NSA_GROUP = 4
NSA_KV_WIDTH = 128
CMP_BLOCK = 32
CMP_STRIDE = 16
CMP_HIDDEN = 128
SLC_BLOCK = 64
N_SELECT = 16
WINDOW = 512
ROPE_THETA = 10000.0
RMS_EPS = 1e-6
LOG2E = 1.4426950408889634
QK_SCALE = HEAD_DIM ** -0.5 * LOG2E

LANES = 128
SUBLANES = 8
V7X_VMEM_BYTES = 64 * 1024 * 1024
MASK_VALUE = -1e30

PROJ_ROWS = 512
FOX_TILE = 512
NSA_Q = 128
SEL_Q = 128
SEL_KV = 512
WIN_KEYS = WINDOW + NSA_Q
FLASH_UNROLL = 4
CMP_QUARTERS = 4
FOX_AUG = 6

C_FQ, C_FK, C_FV, C_FZ, C_NQ = 0, 512, 1024, 1536, 2048
C_KC, C_VC, C_KS, C_VS, C_KW, C_VW = 2560, 2688, 2816, 2944, 3072, 3200
C_NZ, C_MISC, PROJ_COLS = 3328, 3840, 3968
MISC_GATE0 = FOX_HEADS


def _vmem_limit(pipelined_bytes, resident_bytes):
    need = 2 * pipelined_bytes + resident_bytes
    return int(min(max(need, 16 * 1024 * 1024), V7X_VMEM_BYTES - 8 * 1024 * 1024))


def _sigmoid(v):
    return 1.0 / (1.0 + jnp.exp(-v))


def _silu(v):
    return v * _sigmoid(v)


def _dot_nt(a, b):
    return lax.dot_general(a, b, (((1,), (1,)), ((), ())), preferred_element_type=F32)


def _split3(v):
    hi = v.astype(BF16)
    r1 = v - hi.astype(F32)
    mid = r1.astype(BF16)
    lo = (r1 - mid.astype(F32)).astype(BF16)
    return hi, mid, lo


def _ada_kernel(c_ref, w_ref, b_ref, o_ref):
    a = _silu(c_ref[...])
    o_ref[...] = jnp.dot(a, w_ref[...], precision=lax.Precision.HIGHEST,
                         preferred_element_type=F32) + b_ref[...]


def _ada(c8, w_ada, b_ada):
    n = w_ada.shape[1]
    blk = D_MODEL
    return pl.pallas_call(
        _ada_kernel,
        grid=(n // blk,),
        in_specs=[pl.BlockSpec((SUBLANES, D_MODEL), lambda j: (0, 0)),
                  pl.BlockSpec((D_MODEL, blk), lambda j: (0, j)),
                  pl.BlockSpec((1, blk), lambda j: (0, j))],
        out_specs=pl.BlockSpec((SUBLANES, blk), lambda j: (0, j)),
        out_shape=jax.ShapeDtypeStruct((SUBLANES, n), F32),
        compiler_params=pltpu.CompilerParams(
            dimension_semantics=("arbitrary",),
            vmem_limit_bytes=_vmem_limit(D_MODEL * blk * 4, 4 * 1024 * 1024)),
        name="ada",
    )(c8, w_ada, b_ada.reshape(1, n))


def _rope128(t, cos, sin_signed):
    lane = lax.broadcasted_iota(jnp.int32, t.shape, 1)
    first_half = (lane & (HEAD_DIM - 1)) < HEAD_DIM // 2
    partner = jnp.where(first_half,
                        pltpu.roll(t, LANES - HEAD_DIM // 2, 1),
                        pltpu.roll(t, HEAD_DIM // 2, 1))
    return t * cos + partner * sin_signed


def _proj_kernel(x_ref, mod_ref, g_ref, w_ref, cos_ref, sin_ref, bf_ref,
                 fq_ref, fk_ref, fv_ref, fz_ref, nq_ref, cmp_ref,
                 ks_ref, vs_ref, kw_ref, vw_ref, nz_ref, misc_ref, lf_ref):
    tm = x_ref.shape[1]
    x = x_ref[0]
    y = x * lax.rsqrt(jnp.mean(x * x, axis=-1, keepdims=True) + RMS_EPS)
    y = y * g_ref[...]
    h = (y * (1.0 + mod_ref[0, 1:2, :]) + mod_ref[0, 0:1, :]).astype(BF16)
    cos = cos_ref[...]
    sin = sin_ref[...]

    def mm(lo, n):
        return jnp.dot(h, w_ref[:, lo:lo + n], preferred_element_type=F32)

    fq_ref[0] = (mm(C_FQ, FOX_WIDTH) * QK_SCALE).astype(BF16)
    fk_ref[0] = mm(C_FK, FOX_WIDTH).astype(BF16)
    fv_ref[0] = mm(C_FV, FOX_WIDTH).astype(BF16)
    fz_ref[0] = _silu(mm(C_FZ, FOX_WIDTH)).astype(BF16)
    def slab_pair(lo):
        both = mm(lo, 2 * LANES)
        return both[:, 0:LANES], both[:, LANES:2 * LANES]

    for j in range(NSA_WIDTH // (2 * LANES)):
        for i, t in enumerate(slab_pair(C_NQ + 2 * j * LANES)):
            c = (2 * j + i) * LANES
            nq_ref[0, :, c:c + LANES] = (_rope128(t, cos, sin) * QK_SCALE).astype(BF16)
    lane = lax.broadcasted_iota(jnp.int32, (tm, LANES), 1)
    low = lane < HEAD_DIM
    ones_col = jnp.where(lane == HEAD_DIM, 1.0, 0.0).astype(BF16)

    def doubled(slab):
        swapped = pltpu.roll(slab, HEAD_DIM, 1)
        return jnp.where(low, slab, swapped), jnp.where(low, swapped, slab)

    def with_ones(slab):
        swapped = pltpu.roll(slab, HEAD_DIM, 1)
        return jnp.where(low, slab, ones_col), jnp.where(low, swapped, ones_col)

    kc, vc = slab_pair(C_KC)
    for kind, slab in enumerate((_rope128(kc, cos, sin), vc)):
        for g in range(NSA_KV_HEADS):
            cmp_ref[kind, 0, g] = slab[:, g * HEAD_DIM:(g + 1) * HEAD_DIM]
    pos = pl.program_id(1) * tm + lax.broadcasted_iota(jnp.int32, (tm, LANES), 0)
    onehot = jnp.where(pos // SLC_BLOCK == lane, 1.0, 0.0).astype(BF16)
    ks, vs = slab_pair(C_KS)
    for g, k2 in enumerate(doubled(_rope128(ks, cos, sin).astype(BF16))):
        ks_ref[0, g, :, 0:LANES] = k2
        ks_ref[0, g, :, LANES:2 * LANES] = onehot
    vs_ref[0, 0], vs_ref[0, 1] = with_ones(vs.astype(BF16))
    kw, vw = slab_pair(C_KW)
    kw_ref[0, 0], kw_ref[0, 1] = doubled(_rope128(kw, cos, sin).astype(BF16))
    vw_ref[0, 0], vw_ref[0, 1] = doubled(vw.astype(BF16))
    nz_ref[0] = _silu(mm(C_NZ, NSA_WIDTH)).astype(BF16)
    misc = mm(C_MISC, LANES)
    misc_ref[0] = misc
    z = misc + bf_ref[...]
    log_f = jnp.minimum(z, 0.0) - jnp.log1p(jnp.exp(-jnp.abs(z)))
    lf_ref[0] = log_f.T[0:FOX_HEADS, :]


def _proj(x, mod, g_pre, w_cat, cos128, sin128, bf128):
    b, s, d = x.shape
    tm = PROJ_ROWS
    row = lambda bi, i: (bi, i, 0)
    wide = lambda dt: jax.ShapeDtypeStruct((b, s, FOX_WIDTH), dt)
    g = NSA_KV_HEADS
    grouped = lambda lanes: jax.ShapeDtypeStruct((b, g, s, lanes), BF16)
    grouped_spec = lambda lanes: pl.BlockSpec((1, g, tm, lanes), lambda bi, i: (bi, 0, i, 0))
    out_shape = (wide(BF16), wide(BF16), wide(BF16), wide(BF16), wide(BF16),
                 jax.ShapeDtypeStruct((2, b, g, s, HEAD_DIM), F32),
                 grouped(2 * LANES), grouped(LANES), grouped(LANES), grouped(LANES),
                 wide(BF16), jax.ShapeDtypeStruct((b, s, LANES), F32),
                 jax.ShapeDtypeStruct((b, FOX_HEADS, s), F32))
    wide_spec = pl.BlockSpec((1, tm, FOX_WIDTH), row)
    out_specs = (wide_spec,) * 5 + (
        pl.BlockSpec((2, 1, g, tm, HEAD_DIM), lambda bi, i: (0, bi, 0, i, 0)),
        grouped_spec(2 * LANES), grouped_spec(LANES), grouped_spec(LANES), grouped_spec(LANES),
        wide_spec, pl.BlockSpec((1, tm, LANES), row),
        pl.BlockSpec((1, FOX_HEADS, tm), lambda bi, i: (bi, 0, i)))
    pipelined = (tm * d * 4 + tm * (6 * FOX_WIDTH * 2 + 2 * g * LANES * 4 + 5 * g * LANES * 2 + LANES * 4)
                 + d * PROJ_COLS * 2)
    return pl.pallas_call(
        _proj_kernel,
        grid=(b, s // tm),
        in_specs=[pl.BlockSpec((1, tm, d), row),
                  pl.BlockSpec((1, 3, d), lambda bi, i: (bi, 0, 0)),
                  pl.BlockSpec((1, d), lambda bi, i: (0, 0)),
                  pl.BlockSpec((d, PROJ_COLS), lambda bi, i: (0, 0)),
                  pl.BlockSpec((tm, LANES), lambda bi, i: (i, 0)),
                  pl.BlockSpec((tm, LANES), lambda bi, i: (i, 0)),
                  pl.BlockSpec((1, LANES), lambda bi, i: (0, 0))],
        out_specs=out_specs,
        out_shape=out_shape,
        compiler_params=pltpu.CompilerParams(
            dimension_semantics=("arbitrary", "arbitrary"),
            vmem_limit_bytes=_vmem_limit(pipelined, 8 * 1024 * 1024)),
        name="proj",
    )(x, mod, g_pre, w_cat, cos128, sin128, bf128)


def _cumsum_kernel(x_ref, hi_ref, mid_ref, lo_ref, *, chunks):
    x = x_ref[...]
    n = x.shape[0]
    parts = _split3(x)
    r = lax.broadcasted_iota(jnp.int32, (LANES, LANES), 0)
    c = lax.broadcasted_iota(jnp.int32, (LANES, LANES), 1)
    tri = (r <= c).astype(BF16)
    rr = lax.broadcasted_iota(jnp.int32, (n, n), 0)
    cc = lax.broadcasted_iota(jnp.int32, (n, n), 1)
    earlier = jnp.logical_and(cc < rr, (cc // chunks) == (rr // chunks)).astype(BF16)
    within = sum(jnp.dot(p, tri, preferred_element_type=F32) for p in parts)
    before = sum(jnp.dot(earlier, p, preferred_element_type=F32) for p in parts)
    total = (within + jnp.sum(before, axis=-1, keepdims=True)) * LOG2E
    hi_ref[...], mid_ref[...], lo_ref[...] = _split3(total)


def _cumsum_lanes_split(v):
    rows, s = v.shape
    chunks = s // LANES
    n = rows * chunks
    part = jax.ShapeDtypeStruct((n, LANES), BF16)
    parts = pl.pallas_call(
        functools.partial(_cumsum_kernel, chunks=chunks),
        out_shape=(part, part, part),
        compiler_params=pltpu.CompilerParams(
            vmem_limit_bytes=_vmem_limit(2 * n * LANES * 4, 6 * n * n)),
        name="cumsum",
    )(v.reshape(n, LANES))
    return tuple(p.reshape(rows, s) for p in parts)


def _flash_scratch(rows, tile, acc_lanes):
    return ([pltpu.VMEM((rows, LANES), F32), pltpu.VMEM((rows, acc_lanes), F32)]
            + [pltpu.VMEM((rows, tile), F32)] * 2
            + [pltpu.VMEM((rows, tile), BF16)] * 2
            + [pltpu.VMEM((rows, LANES), F32)] * 2)


def _flash_scratch_bytes(rows, tile, acc_lanes):
    return rows * (LANES * 4 + acc_lanes * 4 + 2 * tile * 4 + 2 * tile * 2 + 2 * LANES * 4)


def _causal_flash(n_full, tile, q_ref, load_k, load_v, diag_mask, scratch):
    m_ref, acc_ref, s0, s1, p0, p1, a0, a1 = scratch
    s_bufs, p_bufs, a_bufs = (s0, s1), (p0, p1), (a0, a1)
    reps = acc_ref.shape[-1] // LANES

    def logits(j):
        return _dot_nt(q_ref[...], load_k(pl.multiple_of(j * tile, tile)))

    def softmax(s):
        m_prev = m_ref[...]
        m_next = jnp.maximum(m_prev, jnp.max(s, axis=1, keepdims=True))
        m_ref[...] = m_next
        p = jnp.exp2(s - pltpu.repeat(m_next, tile // LANES, 1))
        return p.astype(BF16), jnp.exp2(m_prev - m_next)

    def accumulate(p, alpha, j):
        pv = jnp.dot(p, load_v(pl.multiple_of(j * tile, tile)), preferred_element_type=F32)
        acc_ref[...] = acc_ref[...] * (alpha if reps == 1 else pltpu.repeat(alpha, reps, 1)) + pv

    def stage(j, cur):
        s_bufs[1 - cur][...] = logits(j + 1)
        p_bufs[cur][...], a_bufs[cur][...] = softmax(s_bufs[cur][...])
        accumulate(p_bufs[1 - cur][...], a_bufs[1 - cur][...], jnp.maximum(j - 1, 0))

    def finish(cur):
        p, alpha = softmax(diag_mask(s_bufs[cur][...]))
        accumulate(p_bufs[1 - cur][...], a_bufs[1 - cur][...], jnp.maximum(n_full - 1, 0))
        accumulate(p, alpha, n_full)

    m_ref[...] = jnp.full(m_ref.shape, MASK_VALUE, F32)
    acc_ref[...] = jnp.zeros(acc_ref.shape, F32)
    s_bufs[0][...] = logits(0)
    p_bufs[1][...] = jnp.zeros(p_bufs[1].shape, BF16)
    a_bufs[1][...] = jnp.ones(a_bufs[1].shape, F32)

    def run_stages(first, count):
        for i in range(count):
            stage(first + i, i % 2)

    def unrolled(jj, carry):
        run_stages(FLASH_UNROLL * jj, FLASH_UNROLL)
        return carry

    lax.fori_loop(0, n_full // FLASH_UNROLL, unrolled, 0)
    rest = n_full % FLASH_UNROLL

    @pl.when(rest >= 2)
    def _():
        run_stages(n_full - rest, 2)

    @pl.when(rest % 2 == 1)
    def _():
        stage(n_full - 1, 0)
        finish(1)

    @pl.when(rest % 2 == 0)
    def _():
        finish(0)


def _fox_aug(cum_split):
    b, h, s = cum_split[0].shape
    terms = jnp.concatenate(list(cum_split) + [jnp.ones((b, 1, s), BF16)], axis=1)
    terms = jnp.transpose(terms, (0, 2, 1))
    place_q = np.zeros((3 * h + 1, h * HEAD_DIM), np.float32)
    place_k = np.zeros((3 * h + 1, h * HEAD_DIM), np.float32)
    half = FOX_AUG // 2
    for head in range(h):
        for part in range(half):
            place_q[part * h + head, head * HEAD_DIM + part] = 1.0
            place_q[3 * h, head * HEAD_DIM + half + part] = 1.0
            place_k[3 * h, head * HEAD_DIM + part] = 1.0
            place_k[part * h + head, head * HEAD_DIM + half + part] = -1.0
    scatter = lambda place: jnp.einsum("bsr,rl->bsl", terms, jnp.asarray(place, BF16),
                                       preferred_element_type=F32).astype(BF16)
    return scatter(place_q), scatter(place_k)


def _fox_kernel(q_ref, qa_ref, k_ref, ka_ref, v_ref, z_ref, o_ref, qm_ref, ones_ref, *flash):
    t = FOX_TILE
    qi = pl.program_id(2)
    q = q_ref[0]
    qa = qa_ref[0]
    lane = lax.broadcasted_iota(jnp.int32, q.shape, 1)
    for hh in range(2):
        r = slice(hh * t, (hh + 1) * t)
        own = (lane < HEAD_DIM) if hh == 0 else (lane >= HEAD_DIM)
        qm_ref[r, 0:LANES] = jnp.where(own, q, jnp.zeros_like(q))
        qm_ref[r, LANES:2 * LANES] = jnp.where(own, qa, jnp.zeros_like(qa))
    ones_lane = lax.broadcasted_iota(jnp.int32, ones_ref.shape, 1)
    ones_ref[...] = jnp.where(ones_lane == 0, 1.0, 0.0).astype(BF16)

    def load_k(k0):
        return jnp.concatenate([k_ref[0, pl.ds(k0, t), :], ka_ref[0, pl.ds(k0, t), :]], axis=1)

    def load_v(k0):
        return jnp.concatenate([v_ref[0, pl.ds(k0, t), :], ones_ref[...]], axis=1)

    def diag_mask(s):
        row = lax.broadcasted_iota(jnp.int32, s.shape, 0) & (t - 1)
        col = lax.broadcasted_iota(jnp.int32, s.shape, 1)
        return jnp.where(col <= row, s, MASK_VALUE)

    _causal_flash(qi, t, qm_ref, load_k, load_v, diag_mask, flash)
    acc_ref = flash[1]
    o0 = acc_ref[0:t, 0:LANES] * (1.0 / acc_ref[0:t, LANES:LANES + 1])
    o1 = acc_ref[t:2 * t, 0:LANES] * (1.0 / acc_ref[t:2 * t, LANES:LANES + 1])
    o = jnp.where(lane < HEAD_DIM, o0, o1)
    o_ref[0] = (o * z_ref[0].astype(F32)).astype(BF16)


def _fox(fq, fk, fv, cum_split, fz):
    b, s, w = fq.shape
    t = FOX_TILE
    pairs = w // LANES
    qa, ka = _fox_aug(cum_split)
    tile = pl.BlockSpec((1, t, LANES), lambda bi, hp, i: (bi, i, hp))
    full = pl.BlockSpec((1, s, LANES), lambda bi, hp, i: (bi, 0, hp))
    resident = 3 * s * LANES * 2
    scratch = 2 * t * 2 * LANES * 2 + t * LANES * 2 + _flash_scratch_bytes(2 * t, t, 2 * LANES)
    return pl.pallas_call(
        _fox_kernel,
        grid=(b, pairs, s // t),
        in_specs=[tile, tile, full, full, full, tile],
        out_specs=tile,
        out_shape=jax.ShapeDtypeStruct((b, s, w), BF16),
        scratch_shapes=[pltpu.VMEM((2 * t, 2 * LANES), BF16),
                        pltpu.VMEM((t, LANES), BF16)] + _flash_scratch(2 * t, t, 2 * LANES),
        compiler_params=pltpu.CompilerParams(
            dimension_semantics=("arbitrary", "arbitrary", "arbitrary"),
            vmem_limit_bytes=_vmem_limit(resident + 4 * t * LANES * 2, scratch + 4 * 2 * t * t * 4)),
        name="fox",
    )(fq, qa, fk, ka, fv, fz)


def _compress_kernel(t_ref, w1_ref, w2_ref, pe_ref, hi_ref, lo_ref):
    hp = lax.Precision.HIGHEST
    half = CMP_STRIDE * HEAD_DIM
    tt = t_ref[0, 0, 0]
    n = tt.shape[0]
    first = jnp.dot(tt, w1_ref[0, 0:half, :], precision=hp, preferred_element_type=F32)
    second = jnp.dot(tt, w1_ref[0, half:2 * half, :], precision=hp, preferred_element_type=F32)
    pe_term = jnp.dot(pe_ref[0], w1_ref[0], precision=hp, preferred_element_type=F32)[0:1, :]
    hidden = first + pltpu.roll(second, n - 1, 0) + pe_term
    out = jnp.dot(_silu(hidden), w2_ref[0], precision=hp, preferred_element_type=F32)
    hi = out.astype(BF16)
    hi_ref[0, 0, 0] = hi
    lo_ref[0, 0, 0] = (out - hi.astype(F32)).astype(BF16)


def _compress(t, w1, w2, pe):
    kinds, b, g, n, flat = t.shape
    width = w2.shape[-1]
    out_spec = pl.BlockSpec((1, 1, 1, n, width), lambda a, bi, gi: (a, bi, gi, 0, 0))
    out_part = jax.ShapeDtypeStruct((kinds, b, g, n, width), BF16)
    return pl.pallas_call(
        _compress_kernel,
        grid=(kinds, b, g),
        in_specs=[pl.BlockSpec((1, 1, 1, n, flat), lambda a, bi, gi: (a, bi, gi, 0, 0)),
                  pl.BlockSpec((1, 2 * flat, CMP_HIDDEN), lambda a, bi, gi: (a, 0, 0)),
                  pl.BlockSpec((1, CMP_HIDDEN, width), lambda a, bi, gi: (a, 0, 0)),
                  pl.BlockSpec((1, SUBLANES, 2 * flat), lambda a, bi, gi: (a, 0, 0))],
        out_specs=(out_spec, out_spec),
        out_shape=(out_part, out_part),
        compiler_params=pltpu.CompilerParams(
            dimension_semantics=("arbitrary", "arbitrary", "arbitrary"),
            vmem_limit_bytes=_vmem_limit(n * flat * 4 + 2 * flat * CMP_HIDDEN * 4, 8 * 1024 * 1024)),
        name="compress",
    )(t, w1, w2, pe)


def _softmax_numerator(s, bias):
    s = s + bias
    m = jnp.max(s, axis=1, keepdims=True)
    m = jnp.where(m == -jnp.inf, 0.0, m)
    return jnp.exp2(s - m)


def _stack_heads(q):
    tq = q.shape[0]
    lane = lax.broadcasted_iota(jnp.int32, (tq, LANES), 1)
    rows = []
    for h in range(NSA_GROUP):
        slab = q[:, (h // 2) * LANES:(h // 2 + 1) * LANES]
        own = (lane < HEAD_DIM) if h % 2 == 0 else (lane >= HEAD_DIM)
        rows.append(jnp.where(own, slab, jnp.zeros_like(slab)))
    return jnp.concatenate(rows, axis=0)


def _group_gates(misc, gi):
    gates = _sigmoid(misc)
    return jnp.where(gi == 0, gates, pltpu.roll(gates, LANES - 3 * NSA_GROUP, 1))


def _gate_lanes(gates, h, j):
    c = MISC_GATE0 + 3 * h + j
    return jnp.broadcast_to(gates[:, c:c + 1], gates.shape)


def _pair_slabs(per_head):
    lane = lax.broadcasted_iota(jnp.int32, per_head[0].shape, 1)
    return [jnp.where(lane < HEAD_DIM, per_head[2 * pp], per_head[2 * pp + 1])
            for pp in range(NSA_GROUP // 2)]


def _cmpwin_kernel(q_ref, kc_hi_ref, kc_lo_ref, vc_ref, overlap_ref, kw_ref, vw_ref, misc_ref,
                   part_ref, selb_ref):
    gi = pl.program_id(1)
    q0 = pl.multiple_of(pl.program_id(2) * NSA_Q, NSA_Q)
    quarter = kc_hi_ref.shape[3] // CMP_QUARTERS
    reach = q0 // (quarter * CMP_STRIDE)
    refs = (q_ref, kc_hi_ref, kc_lo_ref, vc_ref, overlap_ref, kw_ref, vw_ref, misc_ref,
            part_ref, selb_ref)
    for quarters in range(1, CMP_QUARTERS + 1):
        pl.when(reach == quarters - 1)(
            functools.partial(_cmpwin_body, refs, gi, q0, quarters * quarter))


def _cmpwin_body(refs, gi, q0, n_cmp):
    (q_ref, kc_hi_ref, kc_lo_ref, vc_ref, overlap_ref, kw_ref, vw_ref, misc_ref,
     part_ref, selb_ref) = refs
    tq = NSA_Q
    q4 = _stack_heads(q_ref[0])
    n_blk = n_cmp * CMP_STRIDE // SLC_BLOCK
    head_rows = [slice(h * tq, (h + 1) * tq) for h in range(NSA_GROUP)]

    s = (_dot_nt(q4, kc_hi_ref[0, 0, 0, 0:n_cmp, :])
         + _dot_nt(q4, kc_lo_ref[0, 0, 0, 0:n_cmp, :]))
    col = lax.broadcasted_iota(jnp.int32, (tq, n_cmp), 1)
    t_row = q0 + lax.broadcasted_iota(jnp.int32, (tq, n_cmp), 0)
    cmp_bias = jnp.where(col * CMP_STRIDE + (CMP_BLOCK - 1) <= t_row, 0.0, -jnp.inf)
    pcs = []
    for r in head_rows:
        p = _softmax_numerator(s[r], cmp_bias)
        pcs.append(p * (1.0 / jnp.maximum(jnp.sum(p, axis=1, keepdims=True), 1e-30)))
    oc = jnp.dot(jnp.concatenate(pcs, axis=0).astype(BF16), vc_ref[0, 0, 0, 0:n_cmp, :],
                 preferred_element_type=F32)

    pc_sum = pcs[0] + pcs[1] + pcs[2] + pcs[3]
    overlap = overlap_ref[0:n_blk, 0:n_cmp]
    imp = sum(_dot_nt(overlap, p) for p in _split3(pc_sum))
    blk = lax.broadcasted_iota(jnp.int32, (n_blk, tq), 0)
    cur = (q0 + lax.broadcasted_iota(jnp.int32, (n_blk, tq), 1)) // SLC_BLOCK
    forced = jnp.logical_or(blk == 0, jnp.logical_or(blk == cur, blk == cur - 1))
    imp = jnp.where(forced, jnp.inf, jnp.where(blk > cur, -jnp.inf, imp))

    def pick_one(_, carry):
        rem, sel = carry
        best = jnp.max(rem, axis=0, keepdims=True)
        first = jnp.min(jnp.where(rem == best, blk, n_blk), axis=0, keepdims=True)
        hit = blk == first
        return jnp.where(hit, -jnp.inf, rem), jnp.where(hit, 1.0, sel)

    _, sel = lax.fori_loop(0, min(N_SELECT, n_blk), pick_one, (imp, jnp.zeros_like(imp)),
                           unroll=True)
    all_blk = selb_ref.shape[3]
    if n_blk < all_blk:
        sel = jnp.concatenate([sel, jnp.zeros((all_blk - n_blk, tq), F32)], axis=0)
    selb_ref[0, 0] = jnp.where(sel.T > 0.5, 0.0, MASK_VALUE).astype(BF16)

    start = pl.multiple_of(jnp.maximum(q0 - WINDOW, 0), tq)
    kw = kw_ref[0, 0, pl.ds(start, WIN_KEYS), :]
    vw = vw_ref[0, 0, pl.ds(start, WIN_KEYS), :]
    sw = _dot_nt(q4, kw)
    pos = start + lax.broadcasted_iota(jnp.int32, (tq, WIN_KEYS), 1)
    dist = q0 + lax.broadcasted_iota(jnp.int32, (tq, WIN_KEYS), 0) - pos
    win_bias = jnp.where(jnp.logical_and(dist >= 0, dist < WINDOW), 0.0, -jnp.inf)
    pw = jnp.concatenate([_softmax_numerator(sw[r], win_bias) for r in head_rows],
                         axis=0).astype(BF16)
    ow = jnp.dot(pw, vw, preferred_element_type=F32)
    lw = jnp.dot(pw, jnp.ones((WIN_KEYS, LANES), BF16), preferred_element_type=F32)
    ow = ow * (1.0 / jnp.maximum(lw, 1e-30))

    gates = _group_gates(misc_ref[0], gi)
    gated = [_gate_lanes(gates, h, 0) * oc[r] + _gate_lanes(gates, h, 2) * ow[r]
             for h, r in enumerate(head_rows)]
    for pp, slab in enumerate(_pair_slabs(gated)):
        part_ref[0, :, pp * LANES:(pp + 1) * LANES] = slab


def _block_overlap(n_blk, n_cmp):
    ratio = SLC_BLOCK // CMP_STRIDE
    lo = np.arange(n_blk)[:, None] * ratio - (CMP_BLOCK // CMP_STRIDE - 1)
    i = np.arange(n_cmp)[None, :]
    n_overlap = (SLC_BLOCK + CMP_BLOCK) // CMP_STRIDE - 1
    return ((i >= lo) & (i < lo + n_overlap) & (i < n_cmp - 1)).astype(np.float32)


def _cmpwin(nq, cmp_hi, cmp_lo, kw, vw_ones, misc):
    b, s, w = nq.shape
    g = NSA_KV_HEADS
    tq = NSA_Q
    gw = NSA_GROUP * HEAD_DIM
    n_cmp = cmp_hi.shape[3]
    n_blk = s // SLC_BLOCK
    q_spec = pl.BlockSpec((1, tq, gw), lambda bi, gi, i: (bi, i, gi))
    key_spec = pl.BlockSpec((1, 1, 1, n_cmp, LANES), lambda bi, gi, i: (0, bi, gi, 0, 0))
    val_spec = pl.BlockSpec((1, 1, 1, n_cmp, LANES), lambda bi, gi, i: (1, bi, gi, 0, 0))
    seq_spec = pl.BlockSpec((1, 1, s, LANES), lambda bi, gi, i: (bi, gi, 0, 0))
    resident = 2 * s * LANES * 2 + 3 * n_cmp * LANES * 2 + n_blk * n_cmp * 2
    rows = NSA_GROUP * tq
    return pl.pallas_call(
        _cmpwin_kernel,
        grid=(b, g, s // tq),
        in_specs=[q_spec, key_spec, key_spec, val_spec,
                  pl.BlockSpec((n_blk, n_cmp), lambda bi, gi, i: (0, 0)),
                  seq_spec, seq_spec,
                  pl.BlockSpec((1, tq, LANES), lambda bi, gi, i: (bi, i, 0))],
        out_specs=(q_spec, pl.BlockSpec((1, 1, tq, n_blk), lambda bi, gi, i: (bi, gi, i, 0))),
        out_shape=(jax.ShapeDtypeStruct((b, s, w), F32),
                   jax.ShapeDtypeStruct((b, g, s, n_blk), BF16)),
        compiler_params=pltpu.CompilerParams(
            dimension_semantics=("arbitrary", "arbitrary", "arbitrary"),
            vmem_limit_bytes=_vmem_limit(resident + tq * gw * 8, 12 * rows * WIN_KEYS * 4)),
        name="cmpwin",
    )(nq, cmp_hi, cmp_lo, cmp_hi, jnp.asarray(_block_overlap(n_blk, n_cmp), BF16), kw, vw_ones, misc)


def _sel_kernel(q_ref, k_ref, v_ref, selb_ref, part_ref, misc_ref, z_ref, o_ref,
                qa_ref, gate_ref, *flash):
    tq = SEL_Q
    gi = pl.program_id(1)
    qi = pl.program_id(2)
    q0 = pl.multiple_of(qi * tq, tq)
    q = q_ref[0]
    selb = selb_ref[0, 0]
    lane = lax.broadcasted_iota(jnp.int32, (tq, LANES), 1)
    for h in range(NSA_GROUP):
        r = slice(h * tq, (h + 1) * tq)
        slab = q[:, (h // 2) * LANES:(h // 2 + 1) * LANES]
        own = (lane < HEAD_DIM) if h % 2 == 0 else (lane >= HEAD_DIM)
        qa_ref[r, 0:LANES] = jnp.where(own, slab, jnp.zeros_like(slab))
        qa_ref[r, LANES:2 * LANES] = selb
    gates = _group_gates(misc_ref[0], gi)
    for h in range(NSA_GROUP):
        gate_ref[h] = _gate_lanes(gates, h, 1)

    n_full = q0 // SEL_KV

    def diag_mask(s):
        t_row = q0 + (lax.broadcasted_iota(jnp.int32, s.shape, 0) & (tq - 1))
        pos = n_full * SEL_KV + lax.broadcasted_iota(jnp.int32, s.shape, 1)
        return jnp.where(pos <= t_row, s, MASK_VALUE)

    _causal_flash(n_full, SEL_KV, qa_ref,
                  lambda k0: k_ref[0, 0, pl.ds(k0, SEL_KV), :],
                  lambda k0: v_ref[0, 0, pl.ds(k0, SEL_KV), :],
                  diag_mask, flash)
    acc_ref = flash[1]

    gated = []
    for h in range(NSA_GROUP):
        acc = acc_ref[h * tq:(h + 1) * tq, :]
        scaled = acc * (gate_ref[h] * (1.0 / acc[:, HEAD_DIM:HEAD_DIM + 1]))
        gated.append(scaled if h % 2 == 0 else pltpu.roll(scaled, HEAD_DIM, 1))
    for pp, slab in enumerate(_pair_slabs(gated)):
        c = slice(pp * LANES, (pp + 1) * LANES)
        o_ref[0, :, c] = ((part_ref[0, :, c] + slab) * z_ref[0, :, c].astype(F32)).astype(BF16)


def _sel(nq, k_aug, v_ones, selb, part, misc, nz):
    b, s, w = nq.shape
    g = NSA_KV_HEADS
    tq = SEL_Q
    gw = NSA_GROUP * HEAD_DIM
    n_blk = s // SLC_BLOCK
    rows = NSA_GROUP * tq
    q_spec = pl.BlockSpec((1, tq, gw), lambda bi, gi, i: (bi, i, gi))
    resident = s * 3 * LANES * 2
    return pl.pallas_call(
        _sel_kernel,
        grid=(b, g, s // tq),
        in_specs=[q_spec,
                  pl.BlockSpec((1, 1, s, 2 * LANES), lambda bi, gi, i: (bi, gi, 0, 0)),
                  pl.BlockSpec((1, 1, s, LANES), lambda bi, gi, i: (bi, gi, 0, 0)),
                  pl.BlockSpec((1, 1, tq, n_blk), lambda bi, gi, i: (bi, gi, i, 0)),
                  q_spec,
                  pl.BlockSpec((1, tq, LANES), lambda bi, gi, i: (bi, i, 0)),
                  q_spec],
        out_specs=q_spec,
        out_shape=jax.ShapeDtypeStruct((b, s, w), BF16),
        scratch_shapes=[pltpu.VMEM((rows, 2 * LANES), BF16),
                        pltpu.VMEM((NSA_GROUP, tq, LANES), F32)] + _flash_scratch(rows, SEL_KV, LANES),
        compiler_params=pltpu.CompilerParams(
            dimension_semantics=("arbitrary", "arbitrary", "arbitrary"),
            vmem_limit_bytes=_vmem_limit(resident + tq * gw * 10,
                                         _flash_scratch_bytes(rows, SEL_KV, LANES) + 4 * rows * SEL_KV * 4)),
        name="sel",
    )(nq, k_aug, v_ones, selb, part, misc, nz)


def _out_kernel(yf_ref, yn_ref, w_ref, g_ref, mod_ref, x_ref, o_ref):
    y = (jnp.dot(yf_ref[0], w_ref[0:FOX_WIDTH, :], preferred_element_type=F32)
         + jnp.dot(yn_ref[0], w_ref[FOX_WIDTH:, :], preferred_element_type=F32))
    yn = y * lax.rsqrt(jnp.mean(y * y, axis=-1, keepdims=True) + RMS_EPS)
    o_ref[0] = x_ref[0] + mod_ref[0, 2:3, :] * (yn * g_ref[...])


def _out(y_fox, y_nsa, w_out, g_post, mod, x):
    b, s, d = x.shape
    tm = PROJ_ROWS
    row = lambda bi, i: (bi, i, 0)
    half = pl.BlockSpec((1, tm, FOX_WIDTH), row)
    pipelined = 2 * tm * FOX_WIDTH * 2 + 2 * tm * d * 4 + d * d * 2
    return pl.pallas_call(
        _out_kernel,
        grid=(b, s // tm),
        in_specs=[half, half,
                  pl.BlockSpec((d, d), lambda bi, i: (0, 0)),
                  pl.BlockSpec((1, d), lambda bi, i: (0, 0)),
                  pl.BlockSpec((1, 3, d), lambda bi, i: (bi, 0, 0)),
                  pl.BlockSpec((1, tm, d), row)],
        out_specs=pl.BlockSpec((1, tm, d), row),
        out_shape=jax.ShapeDtypeStruct((b, s, d), F32),
        compiler_params=pltpu.CompilerParams(
            dimension_semantics=("arbitrary", "arbitrary"),
            vmem_limit_bytes=_vmem_limit(pipelined, 6 * tm * d * 4)),
        name="out",
    )(y_fox, y_nsa, w_out, g_post, mod, x)


def _rope_slabs(seq_len):
    inv = 1.0 / (ROPE_THETA ** (jnp.arange(0, HEAD_DIM, 2, dtype=F32) / HEAD_DIM))
    ang = jnp.arange(seq_len, dtype=F32)[:, None] * inv[None, :]
    cos, sin = jnp.cos(ang), jnp.sin(ang)
    reps = LANES // (HEAD_DIM // 2)
    sign = jnp.tile(jnp.concatenate([-jnp.ones((HEAD_DIM // 2,), F32), jnp.ones((HEAD_DIM // 2,), F32)]),
                    LANES // HEAD_DIM)
    return jnp.tile(cos, (1, reps)), jnp.tile(sin, (1, reps)) * sign[None, :]


def _reorder_w_in(w_in):
    fw, kv = FOX_WIDTH, NSA_KV_WIDTH
    o = 0
    cols = {}
    for name, n in (("fq", fw), ("fk", fw), ("fv", fw), ("ff", FOX_HEADS), ("fz", fw), ("nq", NSA_WIDTH),
                    ("kc", kv), ("vc", kv), ("ks", kv), ("vs", kv), ("kw", kv), ("vw", kv),
                    ("ng", 3 * NSA_HEADS), ("nz", NSA_WIDTH)):
        cols[name] = w_in[:, o:o + n]
        o += n
    pad = jnp.zeros((w_in.shape[0], LANES - FOX_HEADS - 3 * NSA_HEADS), w_in.dtype)
    order = ("fq", "fk", "fv", "fz", "nq", "kc", "vc", "ks", "vs", "kw", "vw", "nz", "ff", "ng")
    return jnp.concatenate([cols[k] for k in order] + [pad], axis=1).astype(BF16)


def _layer(x, c8, g_pre, g_post, w_ada, b_ada, w_in, b_forget, w_cmp_k1, w_cmp_k2,
           w_cmp_v1, w_cmp_v2, pe_cmp_k, pe_cmp_v, w_out, cos128, sin128):
    b, s, d = x.shape
    mod = _ada(c8, w_ada, b_ada)[:b].reshape(b, 3, d)
    bf128 = jnp.pad(b_forget, (0, LANES - FOX_HEADS)).reshape(1, LANES)
    (fq, fk, fv, fz, nq, cmp_in, ks_aug, vs_ones, kw2, vw_ones, nz, misc, log_f) = _proj(
        x, mod, g_pre.reshape(1, d), _reorder_w_in(w_in), cos128, sin128, bf128)

    cum_split = tuple(p.reshape(b, FOX_HEADS, s)
                      for p in _cumsum_lanes_split(log_f.reshape(b * FOX_HEADS, s)))
    y_fox = _fox(fq, fk, fv, cum_split, fz)

    flat = CMP_STRIDE * HEAD_DIM
    t = cmp_in.reshape(2, b, NSA_KV_HEADS, s // CMP_STRIDE, flat)
    pe = jnp.stack([pe_cmp_k.reshape(1, 2 * flat), pe_cmp_v.reshape(1, 2 * flat)])
    w2 = jnp.stack([w_cmp_k2, w_cmp_v2])
    cmp_hi, cmp_lo = _compress(t, jnp.stack([w_cmp_k1, w_cmp_v1]), jnp.concatenate([w2, w2], axis=-1),
                               jnp.broadcast_to(pe, (2, SUBLANES, 2 * flat)))
    part, selb = _cmpwin(nq, cmp_hi, cmp_lo, kw2, vw_ones, misc)
    y_nsa = _sel(nq, ks_aug, vs_ones, selb, part, misc, nz)

    return _out(y_fox, y_nsa, w_out.astype(BF16), g_post.reshape(1, d), mod, x)


def kernel(x, c, g_pre, g_post, w_ada, b_ada, w_in, b_forget, w_cmp_k1, w_cmp_k2,
           w_cmp_v1, w_cmp_v2, pe_cmp_k, pe_cmp_v, w_out):
    cos128, sin128 = _rope_slabs(x.shape[1])
    c8 = jnp.pad(c, ((0, SUBLANES - c.shape[0]), (0, 0)))
    for layer in range(g_pre.shape[0]):
        x = _layer(x, c8, g_pre[layer], g_post[layer], w_ada[layer], b_ada[layer], w_in[layer],
                   b_forget[layer], w_cmp_k1[layer], w_cmp_k2[layer], w_cmp_v1[layer],
                   w_cmp_v2[layer], pe_cmp_k[layer], pe_cmp_v[layer], w_out[layer], cos128, sin128)
    return x
```

```python
import functools

import jax
import jax.numpy as jnp
import numpy as np
from jax import lax
from jax.experimental import pallas as pl
from jax.experimental.pallas import tpu as pltpu

F32 = jnp.float32
BF16 = jnp.bfloat16

D_MODEL = 1024
HEAD_DIM = 64
FOX_WIDTH = 512
NSA_WIDTH = 512
FOX_HEADS = 8
NSA_HEADS = 8
NSA_KV_HEADS = 2
NSA_GROUP = 4
NSA_KV_WIDTH = 128
CMP_BLOCK = 32
CMP_STRIDE = 16
CMP_HIDDEN = 128
SLC_BLOCK = 64
N_SELECT = 16
WINDOW = 512
ROPE_THETA = 10000.0
RMS_EPS = 1e-6
LOG2E = 1.4426950408889634
QK_SCALE = HEAD_DIM ** -0.5 * LOG2E

LANES = 128
SUBLANES = 8
V7X_VMEM_BYTES = 64 * 1024 * 1024
MASK_VALUE = -1e30

PROJ_ROWS = 512
FOX_TILE = 512
NSA_Q = 128
SEL_Q = 128
SEL_KV = 512
WIN_KEYS = WINDOW + NSA_Q
FLASH_UNROLL = 4
CMP_QUARTERS = 4
FOX_AUG = 6

C_FQ, C_FK, C_FV, C_FZ, C_NQ = 0, 512, 1024, 1536, 2048
C_KC, C_VC, C_KS, C_VS, C_KW, C_VW = 2560, 2688, 2816, 2944, 3072, 3200
C_NZ, C_MISC, PROJ_COLS = 3328, 3840, 3968
MISC_GATE0 = FOX_HEADS


def _vmem_limit(pipelined_bytes, resident_bytes):
    need = 2 * pipelined_bytes + resident_bytes
    return int(min(max(need, 16 * 1024 * 1024), V7X_VMEM_BYTES - 8 * 1024 * 1024))


def _sigmoid(v):
    return 1.0 / (1.0 + jnp.exp(-v))


def _silu(v):
    return v * _sigmoid(v)


def _dot_nt(a, b):
    return lax.dot_general(a, b, (((1,), (1,)), ((), ())), preferred_element_type=F32)


def _split3(v):
    hi = v.astype(BF16)
    r1 = v - hi.astype(F32)
    mid = r1.astype(BF16)
    lo = (r1 - mid.astype(F32)).astype(BF16)
    return hi, mid, lo


def _ada_kernel(c_ref, w_ref, b_ref, o_ref):
    a = _silu(c_ref[...])
    o_ref[...] = jnp.dot(a, w_ref[...], precision=lax.Precision.HIGHEST,
                         preferred_element_type=F32) + b_ref[...]


def _ada(c8, w_ada, b_ada):
    n = w_ada.shape[1]
    blk = D_MODEL
    return pl.pallas_call(
        _ada_kernel,
        grid=(n // blk,),
        in_specs=[pl.BlockSpec((SUBLANES, D_MODEL), lambda j: (0, 0)),
                  pl.BlockSpec((D_MODEL, blk), lambda j: (0, j)),
                  pl.BlockSpec((1, blk), lambda j: (0, j))],
        out_specs=pl.BlockSpec((SUBLANES, blk), lambda j: (0, j)),
        out_shape=jax.ShapeDtypeStruct((SUBLANES, n), F32),
        compiler_params=pltpu.CompilerParams(
            dimension_semantics=("arbitrary",),
            vmem_limit_bytes=_vmem_limit(D_MODEL * blk * 4, 4 * 1024 * 1024)),
        name="ada",
    )(c8, w_ada, b_ada.reshape(1, n))


def _rope128(t, cos, sin_signed):
    lane = lax.broadcasted_iota(jnp.int32, t.shape, 1)
    first_half = (lane & (HEAD_DIM - 1)) < HEAD_DIM // 2
    partner = jnp.where(first_half,
                        pltpu.roll(t, LANES - HEAD_DIM // 2, 1),
                        pltpu.roll(t, HEAD_DIM // 2, 1))
    return t * cos + partner * sin_signed


def _proj_kernel(x_ref, mod_ref, g_ref, w_ref, cos_ref, sin_ref, bf_ref,
                 fq_ref, fk_ref, fv_ref, fz_ref, nq_ref, cmp_ref,
                 ks_ref, vs_ref, kw_ref, vw_ref, nz_ref, misc_ref, lf_ref):
    tm = x_ref.shape[1]
    x = x_ref[0]
    y = x * lax.rsqrt(jnp.mean(x * x, axis=-1, keepdims=True) + RMS_EPS)
    y = y * g_ref[...]
    h = (y * (1.0 + mod_ref[0, 1:2, :]) + mod_ref[0, 0:1, :]).astype(BF16)
    cos = cos_ref[...]
    sin = sin_ref[...]

    def mm(lo, n):
        return jnp.dot(h, w_ref[:, lo:lo + n], preferred_element_type=F32)

    def slab_pair(lo):
        both = mm(lo, 2 * LANES)
        return both[:, 0:LANES], both[:, LANES:2 * LANES]

    lane = lax.broadcasted_iota(jnp.int32, (tm, LANES), 1)
    low = lane < HEAD_DIM
    ones_col = jnp.where(lane == HEAD_DIM, 1.0, 0.0).astype(BF16)

    def doubled(slab):
        swapped = pltpu.roll(slab, HEAD_DIM, 1)
        return jnp.where(low, slab, swapped), jnp.where(low, swapped, slab)

    def with_ones(slab):
        swapped = pltpu.roll(slab, HEAD_DIM, 1)
        return jnp.where(low, slab, ones_col), jnp.where(low, swapped, ones_col)

    fq_ref[0] = (mm(C_FQ, FOX_WIDTH) * QK_SCALE).astype(BF16)
    fk_ref[0] = mm(C_FK, FOX_WIDTH).astype(BF16)
    fv = mm(C_FV, FOX_WIDTH).astype(BF16)
    for pair in range(FOX_WIDTH // LANES):
        fv_ref[0, 2 * pair], fv_ref[0, 2 * pair + 1] = with_ones(fv[:, pair * LANES:(pair + 1) * LANES])
    fz_ref[0] = _silu(mm(C_FZ, FOX_WIDTH)).astype(BF16)
    for j in range(NSA_WIDTH // (2 * LANES)):
        for i, t in enumerate(slab_pair(C_NQ + 2 * j * LANES)):
            c = (2 * j + i) * LANES
            nq_ref[0, :, c:c + LANES] = (_rope128(t, cos, sin) * QK_SCALE).astype(BF16)

    kc, vc = slab_pair(C_KC)
    for kind, slab in enumerate((_rope128(kc, cos, sin), vc)):
        for g in range(NSA_KV_HEADS):
            cmp_ref[kind, 0, g] = slab[:, g * HEAD_DIM:(g + 1) * HEAD_DIM]
    pos = pl.program_id(1) * tm + lax.broadcasted_iota(jnp.int32, (tm, LANES), 0)
    onehot = jnp.where(pos // SLC_BLOCK == lane, 1.0, 0.0).astype(BF16)
    ks, vs = slab_pair(C_KS)
    for g, k2 in enumerate(doubled(_rope128(ks, cos, sin).astype(BF16))):
        ks_ref[0, g, :, 0:LANES] = k2
        ks_ref[0, g, :, LANES:2 * LANES] = onehot
    vs_ref[0, 0], vs_ref[0, 1] = with_ones(vs.astype(BF16))
    kw, vw = slab_pair(C_KW)
    kw_ref[0, 0], kw_ref[0, 1] = doubled(_rope128(kw, cos, sin).astype(BF16))
    vw_ref[0, 0], vw_ref[0, 1] = doubled(vw.astype(BF16))
    nz_ref[0] = _silu(mm(C_NZ, NSA_WIDTH)).astype(BF16)
    misc = mm(C_MISC, LANES)
    misc_ref[0] = misc
    z = misc + bf_ref[...]
    log_f = jnp.minimum(z, 0.0) - jnp.log1p(jnp.exp(-jnp.abs(z)))
    lf_ref[0] = log_f.T[0:FOX_HEADS, :]


def _proj(x, mod, g_pre, w_cat, cos128, sin128, bf128):
    b, s, d = x.shape
    tm = PROJ_ROWS
    row = lambda bi, i: (bi, i, 0)
    wide = lambda dt: jax.ShapeDtypeStruct((b, s, FOX_WIDTH), dt)
    g = NSA_KV_HEADS
    grouped = lambda lanes: jax.ShapeDtypeStruct((b, g, s, lanes), BF16)
    grouped_spec = lambda lanes: pl.BlockSpec((1, g, tm, lanes), lambda bi, i: (bi, 0, i, 0))
    out_shape = (wide(BF16), wide(BF16), jax.ShapeDtypeStruct((b, FOX_HEADS, s, LANES), BF16),
                 wide(BF16), wide(BF16),
                 jax.ShapeDtypeStruct((2, b, g, s, HEAD_DIM), F32),
                 grouped(2 * LANES), grouped(LANES), grouped(LANES), grouped(LANES),
                 wide(BF16), jax.ShapeDtypeStruct((b, s, LANES), F32),
                 jax.ShapeDtypeStruct((b, FOX_HEADS, s), F32))
    wide_spec = pl.BlockSpec((1, tm, FOX_WIDTH), row)
    out_specs = (wide_spec, wide_spec,
                 pl.BlockSpec((1, FOX_HEADS, tm, LANES), lambda bi, i: (bi, 0, i, 0)),
                 wide_spec, wide_spec) + (
        pl.BlockSpec((2, 1, g, tm, HEAD_DIM), lambda bi, i: (0, bi, 0, i, 0)),
        grouped_spec(2 * LANES), grouped_spec(LANES), grouped_spec(LANES), grouped_spec(LANES),
        wide_spec, pl.BlockSpec((1, tm, LANES), row),
        pl.BlockSpec((1, FOX_HEADS, tm), lambda bi, i: (bi, 0, i)))
    pipelined = (tm * d * 4 + tm * (7 * FOX_WIDTH * 2 + 2 * g * LANES * 4 + 5 * g * LANES * 2 + LANES * 4)
                 + d * PROJ_COLS * 2)
    return pl.pallas_call(
        _proj_kernel,
        grid=(b, s // tm),
        in_specs=[pl.BlockSpec((1, tm, d), row),
                  pl.BlockSpec((1, 3, d), lambda bi, i: (bi, 0, 0)),
                  pl.BlockSpec((1, d), lambda bi, i: (0, 0)),
                  pl.BlockSpec((d, PROJ_COLS), lambda bi, i: (0, 0)),
                  pl.BlockSpec((tm, LANES), lambda bi, i: (i, 0)),
                  pl.BlockSpec((tm, LANES), lambda bi, i: (i, 0)),
                  pl.BlockSpec((1, LANES), lambda bi, i: (0, 0))],
        out_specs=out_specs,
        out_shape=out_shape,
        compiler_params=pltpu.CompilerParams(
            dimension_semantics=("arbitrary", "arbitrary"),
            vmem_limit_bytes=_vmem_limit(pipelined, 8 * 1024 * 1024)),
        name="proj",
    )(x, mod, g_pre, w_cat, cos128, sin128, bf128)


def _cumsum_kernel(x_ref, hi_ref, mid_ref, lo_ref, *, chunks):
    x = x_ref[...]
    n = x.shape[0]
    parts = _split3(x)
    r = lax.broadcasted_iota(jnp.int32, (LANES, LANES), 0)
    c = lax.broadcasted_iota(jnp.int32, (LANES, LANES), 1)
    tri = (r <= c).astype(BF16)
    rr = lax.broadcasted_iota(jnp.int32, (n, n), 0)
    cc = lax.broadcasted_iota(jnp.int32, (n, n), 1)
    earlier = jnp.logical_and(cc < rr, (cc // chunks) == (rr // chunks)).astype(BF16)
    within = sum(jnp.dot(p, tri, preferred_element_type=F32) for p in parts)
    before = sum(jnp.dot(earlier, p, preferred_element_type=F32) for p in parts)
    total = (within + jnp.sum(before, axis=-1, keepdims=True)) * LOG2E
    hi_ref[...], mid_ref[...], lo_ref[...] = _split3(total)


def _cumsum_lanes_split(v):
    rows, s = v.shape
    chunks = s // LANES
    n = rows * chunks
    part = jax.ShapeDtypeStruct((n, LANES), BF16)
    parts = pl.pallas_call(
        functools.partial(_cumsum_kernel, chunks=chunks),
        out_shape=(part, part, part),
        compiler_params=pltpu.CompilerParams(
            vmem_limit_bytes=_vmem_limit(2 * n * LANES * 4, 6 * n * n)),
        name="cumsum",
    )(v.reshape(n, LANES))
    return tuple(p.reshape(rows, s) for p in parts)


def _flash_scratch(rows, tile, acc_lanes):
    return ([pltpu.VMEM((rows, LANES), F32), pltpu.VMEM((rows, acc_lanes), F32)]
            + [pltpu.VMEM((rows, tile), F32)] * 2
            + [pltpu.VMEM((rows, tile), BF16)] * 2
            + [pltpu.VMEM((rows, LANES), F32)] * 2)


def _flash_scratch_bytes(rows, tile, acc_lanes):
    return rows * (LANES * 4 + acc_lanes * 4 + 2 * tile * 4 + 2 * tile * 2 + 2 * LANES * 4)


def _causal_flash(n_full, tile, q_ref, load_k, pv, diag_mask, scratch):
    m_ref, acc_ref, s0, s1, p0, p1, a0, a1 = scratch
    s_bufs, p_bufs, a_bufs = (s0, s1), (p0, p1), (a0, a1)

    def logits(j):
        return _dot_nt(q_ref[...], load_k(pl.multiple_of(j * tile, tile)))

    def softmax(s):
        m_prev = m_ref[...]
        m_next = jnp.maximum(m_prev, jnp.max(s, axis=1, keepdims=True))
        m_ref[...] = m_next
        p = jnp.exp2(s - pltpu.repeat(m_next, tile // LANES, 1))
        return p.astype(BF16), jnp.exp2(m_prev - m_next)

    def accumulate(p, alpha, j):
        acc_ref[...] = acc_ref[...] * alpha + pv(p, pl.multiple_of(j * tile, tile))

    def first_stage():
        s_bufs[1][...] = logits(1)
        p_bufs[0][...], a_bufs[0][...] = softmax(s_bufs[0][...])

    def stage(j, cur):
        s_bufs[1 - cur][...] = logits(j + 1)
        p_bufs[cur][...], a_bufs[cur][...] = softmax(s_bufs[cur][...])
        accumulate(p_bufs[1 - cur][...], a_bufs[1 - cur][...], j - 1)

    def run_stages(first, count):
        for i in range(count):
            stage(first + i, (1 + i) % 2)

    def finish(cur):
        p, alpha = softmax(diag_mask(s_bufs[cur][...]))
        accumulate(p_bufs[1 - cur][...], a_bufs[1 - cur][...], n_full - 1)
        accumulate(p, alpha, n_full)

    m_ref[...] = jnp.full(m_ref.shape, MASK_VALUE, F32)
    acc_ref[...] = jnp.zeros(acc_ref.shape, F32)
    s_bufs[0][...] = logits(0)

    @pl.when(n_full == 0)
    def _():
        p, alpha = softmax(diag_mask(s_bufs[0][...]))
        accumulate(p, alpha, 0)

    @pl.when(n_full > 0)
    def _():
        first_stage()
        later = n_full - 1

        def unrolled(jj, carry):
            run_stages(1 + FLASH_UNROLL * jj, FLASH_UNROLL)
            return carry

        lax.fori_loop(0, later // FLASH_UNROLL, unrolled, 0)
        rest = later % FLASH_UNROLL

        @pl.when(rest >= 2)
        def _():
            run_stages(n_full - rest, 2)

        @pl.when(rest % 2 == 1)
        def _():
            stage(n_full - 1, 1)
            finish(0)

        @pl.when(rest % 2 == 0)
        def _():
            finish(1)


def _fox_aug(cum_split):
    b, h, s = cum_split[0].shape
    terms = jnp.concatenate(list(cum_split) + [jnp.ones((b, 1, s), BF16)], axis=1)
    terms = jnp.transpose(terms, (0, 2, 1))
    place_q = np.zeros((3 * h + 1, h * HEAD_DIM), np.float32)
    place_k = np.zeros((3 * h + 1, h * HEAD_DIM), np.float32)
    half = FOX_AUG // 2
    for head in range(h):
        for part in range(half):
            place_q[part * h + head, head * HEAD_DIM + part] = 1.0
            place_q[3 * h, head * HEAD_DIM + half + part] = 1.0
            place_k[3 * h, head * HEAD_DIM + part] = 1.0
            place_k[part * h + head, head * HEAD_DIM + half + part] = -1.0
    scatter = lambda place: jnp.einsum("bsr,rl->bsl", terms, jnp.asarray(place, BF16),
                                       preferred_element_type=F32).astype(BF16)
    return scatter(place_q), scatter(place_k)


def _fox_kernel(q_ref, qa_ref, k_ref, ka_ref, v_ref, z_ref, o_ref, qm_ref, *flash):
    t = FOX_TILE
    qi = pl.program_id(2)
    q = q_ref[0]
    qa = qa_ref[0]
    lane = lax.broadcasted_iota(jnp.int32, q.shape, 1)
    for hh in range(2):
        r = slice(hh * t, (hh + 1) * t)
        own = (lane < HEAD_DIM) if hh == 0 else (lane >= HEAD_DIM)
        qm_ref[r, 0:LANES] = jnp.where(own, q, jnp.zeros_like(q))
        qm_ref[r, LANES:2 * LANES] = jnp.where(own, qa, jnp.zeros_like(qa))

    def load_k(k0):
        return jnp.concatenate([k_ref[0, pl.ds(k0, t), :], ka_ref[0, pl.ds(k0, t), :]], axis=1)

    def pv(p, k0):
        return jnp.concatenate(
            [jnp.dot(p[hh * t:(hh + 1) * t], v_ref[0, hh, pl.ds(k0, t), :], preferred_element_type=F32)
             for hh in range(2)], axis=0)

    def diag_mask(s):
        row = lax.broadcasted_iota(jnp.int32, s.shape, 0) & (t - 1)
        col = lax.broadcasted_iota(jnp.int32, s.shape, 1)
        return jnp.where(col <= row, s, MASK_VALUE)

    _causal_flash(qi, t, qm_ref, load_k, pv, diag_mask, flash)
    acc_ref = flash[1]
    o0 = acc_ref[0:t, :] * (1.0 / acc_ref[0:t, HEAD_DIM:HEAD_DIM + 1])
    o1 = acc_ref[t:2 * t, :] * (1.0 / acc_ref[t:2 * t, HEAD_DIM:HEAD_DIM + 1])
    o = jnp.where(lane < HEAD_DIM, o0, pltpu.roll(o1, HEAD_DIM, 1))
    o_ref[0] = (o * z_ref[0].astype(F32)).astype(BF16)


def _fox(fq, fk, fv, cum_split, fz):
    b, s, w = fq.shape
    t = FOX_TILE
    pairs = w // LANES
    qa, ka = _fox_aug(cum_split)
    tile = pl.BlockSpec((1, t, LANES), lambda bi, hp, i: (bi, i, hp))
    full = pl.BlockSpec((1, s, LANES), lambda bi, hp, i: (bi, 0, hp))
    resident = 4 * s * LANES * 2
    scratch = 2 * t * 2 * LANES * 2 + _flash_scratch_bytes(2 * t, t, LANES)
    return pl.pallas_call(
        _fox_kernel,
        grid=(b, pairs, s // t),
        in_specs=[tile, tile, full, full,
                  pl.BlockSpec((1, 2, s, LANES), lambda bi, hp, i: (bi, hp, 0, 0)),
                  tile],
        out_specs=tile,
        out_shape=jax.ShapeDtypeStruct((b, s, w), BF16),
        scratch_shapes=[pltpu.VMEM((2 * t, 2 * LANES), BF16)] + _flash_scratch(2 * t, t, LANES),
        compiler_params=pltpu.CompilerParams(
            dimension_semantics=("arbitrary", "arbitrary", "arbitrary"),
            vmem_limit_bytes=_vmem_limit(resident + 4 * t * LANES * 2, scratch + 4 * 2 * t * t * 4)),
        name="fox",
    )(fq, qa, fk, ka, fv, fz)


def _compress_kernel(t_ref, w1_ref, w2_ref, pe_ref, hi_ref, lo_ref):
    hp = lax.Precision.HIGHEST
    half = CMP_STRIDE * HEAD_DIM
    tt = t_ref[0, 0, 0]
    n = tt.shape[0]
    first = jnp.dot(tt, w1_ref[0, 0:half, :], precision=hp, preferred_element_type=F32)
    second = jnp.dot(tt, w1_ref[0, half:2 * half, :], precision=hp, preferred_element_type=F32)
    pe_term = jnp.dot(pe_ref[0], w1_ref[0], precision=hp, preferred_element_type=F32)[0:1, :]
    hidden = first + pltpu.roll(second, n - 1, 0) + pe_term
    out = jnp.dot(_silu(hidden), w2_ref[0], precision=hp, preferred_element_type=F32)
    hi = out.astype(BF16)
    hi_ref[0, 0, 0] = hi
    lo_ref[0, 0, 0] = (out - hi.astype(F32)).astype(BF16)


def _compress(t, w1, w2, pe):
    kinds, b, g, n, flat = t.shape
    width = w2.shape[-1]
    out_spec = pl.BlockSpec((1, 1, 1, n, width), lambda a, bi, gi: (a, bi, gi, 0, 0))
    out_part = jax.ShapeDtypeStruct((kinds, b, g, n, width), BF16)
    return pl.pallas_call(
        _compress_kernel,
        grid=(kinds, b, g),
        in_specs=[pl.BlockSpec((1, 1, 1, n, flat), lambda a, bi, gi: (a, bi, gi, 0, 0)),
                  pl.BlockSpec((1, 2 * flat, CMP_HIDDEN), lambda a, bi, gi: (a, 0, 0)),
                  pl.BlockSpec((1, CMP_HIDDEN, width), lambda a, bi, gi: (a, 0, 0)),
                  pl.BlockSpec((1, SUBLANES, 2 * flat), lambda a, bi, gi: (a, 0, 0))],
        out_specs=(out_spec, out_spec),
        out_shape=(out_part, out_part),
        compiler_params=pltpu.CompilerParams(
            dimension_semantics=("arbitrary", "arbitrary", "arbitrary"),
            vmem_limit_bytes=_vmem_limit(n * flat * 4 + 2 * flat * CMP_HIDDEN * 4, 8 * 1024 * 1024)),
        name="compress",
    )(t, w1, w2, pe)


def _softmax_numerator(s, bias):
    s = s + bias
    m = jnp.max(s, axis=1, keepdims=True)
    m = jnp.where(m == -jnp.inf, 0.0, m)
    return jnp.exp2(s - m)


def _stack_heads(q):
    tq = q.shape[0]
    lane = lax.broadcasted_iota(jnp.int32, (tq, LANES), 1)
    rows = []
    for h in range(NSA_GROUP):
        slab = q[:, (h // 2) * LANES:(h // 2 + 1) * LANES]
        own = (lane < HEAD_DIM) if h % 2 == 0 else (lane >= HEAD_DIM)
        rows.append(jnp.where(own, slab, jnp.zeros_like(slab)))
    return jnp.concatenate(rows, axis=0)


def _group_gates(misc, gi):
    gates = _sigmoid(misc)
    return jnp.where(gi == 0, gates, pltpu.roll(gates, LANES - 3 * NSA_GROUP, 1))


def _gate_lanes(gates, h, j):
    c = MISC_GATE0 + 3 * h + j
    return jnp.broadcast_to(gates[:, c:c + 1], gates.shape)


def _pair_slabs(per_head):
    lane = lax.broadcasted_iota(jnp.int32, per_head[0].shape, 1)
    return [jnp.where(lane < HEAD_DIM, per_head[2 * pp], per_head[2 * pp + 1])
            for pp in range(NSA_GROUP // 2)]


def _cmpwin_kernel(q_ref, kc_hi_ref, kc_lo_ref, vc_ref, overlap_ref, kw_ref, vw_ref, misc_ref,
                   part_ref, selb_ref):
    gi = pl.program_id(1)
    q0 = pl.multiple_of(pl.program_id(2) * NSA_Q, NSA_Q)
    quarter = kc_hi_ref.shape[3] // CMP_QUARTERS
    reach = q0 // (quarter * CMP_STRIDE)
    refs = (q_ref, kc_hi_ref, kc_lo_ref, vc_ref, overlap_ref, kw_ref, vw_ref, misc_ref,
            part_ref, selb_ref)
    for quarters in range(1, CMP_QUARTERS + 1):
        pl.when(reach == quarters - 1)(
            functools.partial(_cmpwin_body, refs, gi, q0, quarters * quarter))


def _cmpwin_body(refs, gi, q0, n_cmp):
    (q_ref, kc_hi_ref, kc_lo_ref, vc_ref, overlap_ref, kw_ref, vw_ref, misc_ref,
     part_ref, selb_ref) = refs
    tq = NSA_Q
    q4 = _stack_heads(q_ref[0])
    n_blk = n_cmp * CMP_STRIDE // SLC_BLOCK
    head_rows = [slice(h * tq, (h + 1) * tq) for h in range(NSA_GROUP)]

    s = (_dot_nt(q4, kc_hi_ref[0, 0, 0, 0:n_cmp, :])
         + _dot_nt(q4, kc_lo_ref[0, 0, 0, 0:n_cmp, :]))
    col = lax.broadcasted_iota(jnp.int32, (tq, n_cmp), 1)
    t_row = q0 + lax.broadcasted_iota(jnp.int32, (tq, n_cmp), 0)
    cmp_bias = jnp.where(col * CMP_STRIDE + (CMP_BLOCK - 1) <= t_row, 0.0, -jnp.inf)
    pcs = []
    for r in head_rows:
        p = _softmax_numerator(s[r], cmp_bias)
        pcs.append(p * (1.0 / jnp.maximum(jnp.sum(p, axis=1, keepdims=True), 1e-30)))
    oc = jnp.dot(jnp.concatenate(pcs, axis=0).astype(BF16), vc_ref[0, 0, 0, 0:n_cmp, :],
                 preferred_element_type=F32)

    pc_sum = pcs[0] + pcs[1] + pcs[2] + pcs[3]
    overlap = overlap_ref[0:n_blk, 0:n_cmp]
    imp = sum(_dot_nt(overlap, p) for p in _split3(pc_sum))
    blk = lax.broadcasted_iota(jnp.int32, (n_blk, tq), 0)
    cur = (q0 + lax.broadcasted_iota(jnp.int32, (n_blk, tq), 1)) // SLC_BLOCK
    forced = jnp.logical_or(blk == 0, jnp.logical_or(blk == cur, blk == cur - 1))
    imp = jnp.where(forced, jnp.inf, jnp.where(blk > cur, -jnp.inf, imp))

    def pick_one(_, carry):
        rem, sel = carry
        best = jnp.max(rem, axis=0, keepdims=True)
        first = jnp.min(jnp.where(rem == best, blk, n_blk), axis=0, keepdims=True)
        hit = blk == first
        return jnp.where(hit, -jnp.inf, rem), jnp.where(hit, 1.0, sel)

    _, sel = lax.fori_loop(0, min(N_SELECT, n_blk), pick_one, (imp, jnp.zeros_like(imp)),
                           unroll=True)
    all_blk = selb_ref.shape[3]
    if n_blk < all_blk:
        sel = jnp.concatenate([sel, jnp.zeros((all_blk - n_blk, tq), F32)], axis=0)
    selb_ref[0, 0] = jnp.where(sel.T > 0.5, 0.0, MASK_VALUE).astype(BF16)

    start = pl.multiple_of(jnp.maximum(q0 - WINDOW, 0), tq)
    kw = kw_ref[0, 0, pl.ds(start, WIN_KEYS), :]
    vw = vw_ref[0, 0, pl.ds(start, WIN_KEYS), :]
    sw = _dot_nt(q4, kw)
    pos = start + lax.broadcasted_iota(jnp.int32, (tq, WIN_KEYS), 1)
    dist = q0 + lax.broadcasted_iota(jnp.int32, (tq, WIN_KEYS), 0) - pos
    win_bias = jnp.where(jnp.logical_and(dist >= 0, dist < WINDOW), 0.0, -jnp.inf)
    pw = jnp.concatenate([_softmax_numerator(sw[r], win_bias) for r in head_rows],
                         axis=0).astype(BF16)
    ow = jnp.dot(pw, vw, preferred_element_type=F32)
    lw = jnp.dot(pw, jnp.ones((WIN_KEYS, LANES), BF16), preferred_element_type=F32)
    ow = ow * (1.0 / jnp.maximum(lw, 1e-30))

    gates = _group_gates(misc_ref[0], gi)
    gated = [_gate_lanes(gates, h, 0) * oc[r] + _gate_lanes(gates, h, 2) * ow[r]
             for h, r in enumerate(head_rows)]
    for pp, slab in enumerate(_pair_slabs(gated)):
        part_ref[0, :, pp * LANES:(pp + 1) * LANES] = slab


def _block_overlap(n_blk, n_cmp):
    ratio = SLC_BLOCK // CMP_STRIDE
    lo = np.arange(n_blk)[:, None] * ratio - (CMP_BLOCK // CMP_STRIDE - 1)
    i = np.arange(n_cmp)[None, :]
    n_overlap = (SLC_BLOCK + CMP_BLOCK) // CMP_STRIDE - 1
    return ((i >= lo) & (i < lo + n_overlap) & (i < n_cmp - 1)).astype(np.float32)


def _cmpwin(nq, cmp_hi, cmp_lo, kw, vw_ones, misc):
    b, s, w = nq.shape
    g = NSA_KV_HEADS
    tq = NSA_Q
    gw = NSA_GROUP * HEAD_DIM
    n_cmp = cmp_hi.shape[3]
    n_blk = s // SLC_BLOCK
    q_spec = pl.BlockSpec((1, tq, gw), lambda bi, gi, i: (bi, i, gi))
    key_spec = pl.BlockSpec((1, 1, 1, n_cmp, LANES), lambda bi, gi, i: (0, bi, gi, 0, 0))
    val_spec = pl.BlockSpec((1, 1, 1, n_cmp, LANES), lambda bi, gi, i: (1, bi, gi, 0, 0))
    seq_spec = pl.BlockSpec((1, 1, s, LANES), lambda bi, gi, i: (bi, gi, 0, 0))
    resident = 2 * s * LANES * 2 + 3 * n_cmp * LANES * 2 + n_blk * n_cmp * 2
    rows = NSA_GROUP * tq
    return pl.pallas_call(
        _cmpwin_kernel,
        grid=(b, g, s // tq),
        in_specs=[q_spec, key_spec, key_spec, val_spec,
                  pl.BlockSpec((n_blk, n_cmp), lambda bi, gi, i: (0, 0)),
                  seq_spec, seq_spec,
                  pl.BlockSpec((1, tq, LANES), lambda bi, gi, i: (bi, i, 0))],
        out_specs=(q_spec, pl.BlockSpec((1, 1, tq, n_blk), lambda bi, gi, i: (bi, gi, i, 0))),
        out_shape=(jax.ShapeDtypeStruct((b, s, w), F32),
                   jax.ShapeDtypeStruct((b, g, s, n_blk), BF16)),
        compiler_params=pltpu.CompilerParams(
            dimension_semantics=("arbitrary", "arbitrary", "arbitrary"),
            vmem_limit_bytes=_vmem_limit(resident + tq * gw * 8, 12 * rows * WIN_KEYS * 4)),
        name="cmpwin",
    )(nq, cmp_hi, cmp_lo, cmp_hi, jnp.asarray(_block_overlap(n_blk, n_cmp), BF16), kw, vw_ones, misc)


def _sel_kernel(q_ref, k_ref, v_ref, selb_ref, part_ref, misc_ref, z_ref, o_ref,
                qa_ref, gate_ref, *flash):
    tq = SEL_Q
    gi = pl.program_id(1)
    qi = pl.program_id(2)
    q0 = pl.multiple_of(qi * tq, tq)
    q = q_ref[0]
    selb = selb_ref[0, 0]
    lane = lax.broadcasted_iota(jnp.int32, (tq, LANES), 1)
    for h in range(NSA_GROUP):
        r = slice(h * tq, (h + 1) * tq)
        slab = q[:, (h // 2) * LANES:(h // 2 + 1) * LANES]
        own = (lane < HEAD_DIM) if h % 2 == 0 else (lane >= HEAD_DIM)
        qa_ref[r, 0:LANES] = jnp.where(own, slab, jnp.zeros_like(slab))
        qa_ref[r, LANES:2 * LANES] = selb
    gates = _group_gates(misc_ref[0], gi)
    for h in range(NSA_GROUP):
        gate_ref[h] = _gate_lanes(gates, h, 1)

    n_full = q0 // SEL_KV

    def diag_mask(s):
        t_row = q0 + (lax.broadcasted_iota(jnp.int32, s.shape, 0) & (tq - 1))
        pos = n_full * SEL_KV + lax.broadcasted_iota(jnp.int32, s.shape, 1)
        return jnp.where(pos <= t_row, s, MASK_VALUE)

    _causal_flash(n_full, SEL_KV, qa_ref,
                  lambda k0: k_ref[0, 0, pl.ds(k0, SEL_KV), :],
                  lambda p, k0: jnp.dot(p, v_ref[0, 0, pl.ds(k0, SEL_KV), :],
                                        preferred_element_type=F32),
                  diag_mask, flash)
    acc_ref = flash[1]

    gated = []
    for h in range(NSA_GROUP):
        acc = acc_ref[h * tq:(h + 1) * tq, :]
        scaled = acc * (gate_ref[h] * (1.0 / acc[:, HEAD_DIM:HEAD_DIM + 1]))
        gated.append(scaled if h % 2 == 0 else pltpu.roll(scaled, HEAD_DIM, 1))
    for pp, slab in enumerate(_pair_slabs(gated)):
        c = slice(pp * LANES, (pp + 1) * LANES)
        o_ref[0, :, c] = ((part_ref[0, :, c] + slab) * z_ref[0, :, c].astype(F32)).astype(BF16)


def _sel(nq, k_aug, v_ones, selb, part, misc, nz):
    b, s, w = nq.shape
    g = NSA_KV_HEADS
    tq = SEL_Q
    gw = NSA_GROUP * HEAD_DIM
    n_blk = s // SLC_BLOCK
    rows = NSA_GROUP * tq
    q_spec = pl.BlockSpec((1, tq, gw), lambda bi, gi, i: (bi, i, gi))
    resident = s * 3 * LANES * 2
    return pl.pallas_call(
        _sel_kernel,
        grid=(b, g, s // tq),
        in_specs=[q_spec,
                  pl.BlockSpec((1, 1, s, 2 * LANES), lambda bi, gi, i: (bi, gi, 0, 0)),
                  pl.BlockSpec((1, 1, s, LANES), lambda bi, gi, i: (bi, gi, 0, 0)),
                  pl.BlockSpec((1, 1, tq, n_blk), lambda bi, gi, i: (bi, gi, i, 0)),
                  q_spec,
                  pl.BlockSpec((1, tq, LANES), lambda bi, gi, i: (bi, i, 0)),
                  q_spec],
        out_specs=q_spec,
        out_shape=jax.ShapeDtypeStruct((b, s, w), BF16),
        scratch_shapes=[pltpu.VMEM((rows, 2 * LANES), BF16),
                        pltpu.VMEM((NSA_GROUP, tq, LANES), F32)] + _flash_scratch(rows, SEL_KV, LANES),
        compiler_params=pltpu.CompilerParams(
            dimension_semantics=("arbitrary", "arbitrary", "arbitrary"),
            vmem_limit_bytes=_vmem_limit(resident + tq * gw * 10,
                                         _flash_scratch_bytes(rows, SEL_KV, LANES) + 4 * rows * SEL_KV * 4)),
        name="sel",
    )(nq, k_aug, v_ones, selb, part, misc, nz)


def _out_kernel(yf_ref, yn_ref, w_ref, g_ref, mod_ref, x_ref, o_ref):
    y = (jnp.dot(yf_ref[0], w_ref[0:FOX_WIDTH, :], preferred_element_type=F32)
         + jnp.dot(yn_ref[0], w_ref[FOX_WIDTH:, :], preferred_element_type=F32))
    yn = y * lax.rsqrt(jnp.mean(y * y, axis=-1, keepdims=True) + RMS_EPS)
    o_ref[0] = x_ref[0] + mod_ref[0, 2:3, :] * (yn * g_ref[...])


def _out(y_fox, y_nsa, w_out, g_post, mod, x):
    b, s, d = x.shape
    tm = PROJ_ROWS
    row = lambda bi, i: (bi, i, 0)
    half = pl.BlockSpec((1, tm, FOX_WIDTH), row)
    pipelined = 2 * tm * FOX_WIDTH * 2 + 2 * tm * d * 4 + d * d * 2
    return pl.pallas_call(
        _out_kernel,
        grid=(b, s // tm),
        in_specs=[half, half,
                  pl.BlockSpec((d, d), lambda bi, i: (0, 0)),
                  pl.BlockSpec((1, d), lambda bi, i: (0, 0)),
                  pl.BlockSpec((1, 3, d), lambda bi, i: (bi, 0, 0)),
                  pl.BlockSpec((1, tm, d), row)],
        out_specs=pl.BlockSpec((1, tm, d), row),
        out_shape=jax.ShapeDtypeStruct((b, s, d), F32),
        compiler_params=pltpu.CompilerParams(
            dimension_semantics=("arbitrary", "arbitrary"),
            vmem_limit_bytes=_vmem_limit(pipelined, 6 * tm * d * 4)),
        name="out",
    )(y_fox, y_nsa, w_out, g_post, mod, x)


def _rope_slabs(seq_len):
    inv = 1.0 / (ROPE_THETA ** (jnp.arange(0, HEAD_DIM, 2, dtype=F32) / HEAD_DIM))
    ang = jnp.arange(seq_len, dtype=F32)[:, None] * inv[None, :]
    cos, sin = jnp.cos(ang), jnp.sin(ang)
    reps = LANES // (HEAD_DIM // 2)
    sign = jnp.tile(jnp.concatenate([-jnp.ones((HEAD_DIM // 2,), F32), jnp.ones((HEAD_DIM // 2,), F32)]),
                    LANES // HEAD_DIM)
    return jnp.tile(cos, (1, reps)), jnp.tile(sin, (1, reps)) * sign[None, :]


def _reorder_w_in(w_in):
    fw, kv = FOX_WIDTH, NSA_KV_WIDTH
    o = 0
    cols = {}
    for name, n in (("fq", fw), ("fk", fw), ("fv", fw), ("ff", FOX_HEADS), ("fz", fw), ("nq", NSA_WIDTH),
                    ("kc", kv), ("vc", kv), ("ks", kv), ("vs", kv), ("kw", kv), ("vw", kv),
                    ("ng", 3 * NSA_HEADS), ("nz", NSA_WIDTH)):
        cols[name] = w_in[:, o:o + n]
        o += n
    pad = jnp.zeros((w_in.shape[0], LANES - FOX_HEADS - 3 * NSA_HEADS), w_in.dtype)
    order = ("fq", "fk", "fv", "fz", "nq", "kc", "vc", "ks", "vs", "kw", "vw", "nz", "ff", "ng")
    return jnp.concatenate([cols[k] for k in order] + [pad], axis=1).astype(BF16)


def _layer(x, c8, g_pre, g_post, w_ada, b_ada, w_in, b_forget, w_cmp_k1, w_cmp_k2,
           w_cmp_v1, w_cmp_v2, pe_cmp_k, pe_cmp_v, w_out, cos128, sin128):
    b, s, d = x.shape
    mod = _ada(c8, w_ada, b_ada)[:b].reshape(b, 3, d)
    bf128 = jnp.pad(b_forget, (0, LANES - FOX_HEADS)).reshape(1, LANES)
    (fq, fk, fv, fz, nq, cmp_in, ks_aug, vs_ones, kw2, vw_ones, nz, misc, log_f) = _proj(
        x, mod, g_pre.reshape(1, d), _reorder_w_in(w_in), cos128, sin128, bf128)

    cum_split = tuple(p.reshape(b, FOX_HEADS, s)
                      for p in _cumsum_lanes_split(log_f.reshape(b * FOX_HEADS, s)))
    y_fox = _fox(fq, fk, fv, cum_split, fz)

    flat = CMP_STRIDE * HEAD_DIM
    t = cmp_in.reshape(2, b, NSA_KV_HEADS, s // CMP_STRIDE, flat)
    pe = jnp.stack([pe_cmp_k.reshape(1, 2 * flat), pe_cmp_v.reshape(1, 2 * flat)])
    w2 = jnp.stack([w_cmp_k2, w_cmp_v2])
    cmp_hi, cmp_lo = _compress(t, jnp.stack([w_cmp_k1, w_cmp_v1]), jnp.concatenate([w2, w2], axis=-1),
                               jnp.broadcast_to(pe, (2, SUBLANES, 2 * flat)))
    part, selb = _cmpwin(nq, cmp_hi, cmp_lo, kw2, vw_ones, misc)
    y_nsa = _sel(nq, ks_aug, vs_ones, selb, part, misc, nz)

    return _out(y_fox, y_nsa, w_out.astype(BF16), g_post.reshape(1, d), mod, x)


def kernel(x, c, g_pre, g_post, w_ada, b_ada, w_in, b_forget, w_cmp_k1, w_cmp_k2,
           w_cmp_v1, w_cmp_v2, pe_cmp_k, pe_cmp_v, w_out):
    cos128, sin128 = _rope_slabs(x.shape[1])
    c8 = jnp.pad(c, ((0, SUBLANES - c.shape[0]), (0, 0)))
    for layer in range(g_pre.shape[0]):
        x = _layer(x, c8, g_pre[layer], g_post[layer], w_ada[layer], b_ada[layer], w_in[layer],
                   b_forget[layer], w_cmp_k1[layer], w_cmp_k2[layer], w_cmp_v1[layer],
                   w_cmp_v2[layer], pe_cmp_k[layer], pe_cmp_v[layer], w_out[layer], cos128, sin128)
    return x
```

```python
import functools

import jax
import jax.numpy as jnp
import numpy as np
from jax import lax
from jax.experimental import pallas as pl
from jax.experimental.pallas import tpu as pltpu

F32 = jnp.float32
BF16 = jnp.bfloat16

D_MODEL = 1024
HEAD_DIM = 64
FOX_WIDTH = 512
NSA_WIDTH = 512
FOX_HEADS = 8
NSA_HEADS = 8
NSA_KV_HEADS = 2
NSA_GROUP = 4
NSA_KV_WIDTH = 128
CMP_BLOCK = 32
CMP_STRIDE = 16
CMP_HIDDEN = 128
SLC_BLOCK = 64
N_SELECT = 16
WINDOW = 512
ROPE_THETA = 10000.0
RMS_EPS = 1e-6
LOG2E = 1.4426950408889634
QK_SCALE = HEAD_DIM ** -0.5 * LOG2E

LANES = 128
SUBLANES = 8
V7X_VMEM_BYTES = 64 * 1024 * 1024
MASK_VALUE = -1e30

PROJ_ROWS = 512
FOX_TILE = 512
NSA_Q = 128
SEL_Q = 128
SEL_KV = 512
WIN_KEYS = WINDOW + NSA_Q
FLASH_UNROLL = 4
CMP_QUARTERS = 4
FOX_AUG = 6

C_FQ, C_FK, C_FV, C_FZ, C_NQ = 0, 512, 1024, 1536, 2048
C_KC, C_VC, C_KS, C_VS, C_KW, C_VW = 2560, 2688, 2816, 2944, 3072, 3200
C_NZ, C_MISC, PROJ_COLS = 3328, 3840, 3968
MISC_GATE0 = FOX_HEADS


def _vmem_limit(pipelined_bytes, resident_bytes):
    need = 2 * pipelined_bytes + resident_bytes
    return int(min(max(need, 16 * 1024 * 1024), V7X_VMEM_BYTES - 8 * 1024 * 1024))


def _sigmoid(v):
    return 1.0 / (1.0 + jnp.exp(-v))


def _silu(v):
    return v * _sigmoid(v)


def _dot_nt(a, b):
    return lax.dot_general(a, b, (((1,), (1,)), ((), ())), preferred_element_type=F32)


def _split3(v):
    hi = v.astype(BF16)
    r1 = v - hi.astype(F32)
    mid = r1.astype(BF16)
    lo = (r1 - mid.astype(F32)).astype(BF16)
    return hi, mid, lo


def _ada_kernel(c_ref, w_ref, b_ref, o_ref):
    a = _silu(c_ref[...])
    o_ref[...] = jnp.dot(a, w_ref[...], precision=lax.Precision.HIGHEST,
                         preferred_element_type=F32) + b_ref[...]


def _ada(c8, w_ada, b_ada):
    n = w_ada.shape[1]
    blk = D_MODEL
    return pl.pallas_call(
        _ada_kernel,
        grid=(n // blk,),
        in_specs=[pl.BlockSpec((SUBLANES, D_MODEL), lambda j: (0, 0)),
                  pl.BlockSpec((D_MODEL, blk), lambda j: (0, j)),
                  pl.BlockSpec((1, blk), lambda j: (0, j))],
        out_specs=pl.BlockSpec((SUBLANES, blk), lambda j: (0, j)),
        out_shape=jax.ShapeDtypeStruct((SUBLANES, n), F32),
        compiler_params=pltpu.CompilerParams(
            dimension_semantics=("arbitrary",),
            vmem_limit_bytes=_vmem_limit(D_MODEL * blk * 4, 4 * 1024 * 1024)),
        name="ada",
    )(c8, w_ada, b_ada.reshape(1, n))


def _rope128(t, cos, sin_signed):
    lane = lax.broadcasted_iota(jnp.int32, t.shape, 1)
    first_half = (lane & (HEAD_DIM - 1)) < HEAD_DIM // 2
    partner = jnp.where(first_half,
                        pltpu.roll(t, LANES - HEAD_DIM // 2, 1),
                        pltpu.roll(t, HEAD_DIM // 2, 1))
    return t * cos + partner * sin_signed


def _proj_kernel(x_ref, mod_ref, g_ref, w_ref, cos_ref, sin_ref, bf_ref,
                 fq_ref, fk_ref, fv_ref, fz_ref, nq_ref, cmp_ref,
                 ks_ref, vs_ref, kw_ref, vw_ref, nz_ref, misc_ref, lf_ref):
    tm = x_ref.shape[1]
    x = x_ref[0]
    y = x * lax.rsqrt(jnp.mean(x * x, axis=-1, keepdims=True) + RMS_EPS)
    y = y * g_ref[...]
    h = (y * (1.0 + mod_ref[0, 1:2, :]) + mod_ref[0, 0:1, :]).astype(BF16)
    cos = cos_ref[...]
    sin = sin_ref[...]

    def mm(lo, n):
        return jnp.dot(h, w_ref[:, lo:lo + n], preferred_element_type=F32)

    def slab_pair(lo):
        both = mm(lo, 2 * LANES)
        return both[:, 0:LANES], both[:, LANES:2 * LANES]

    lane = lax.broadcasted_iota(jnp.int32, (tm, LANES), 1)
    low = lane < HEAD_DIM
    ones_col = jnp.where(lane == HEAD_DIM, 1.0, 0.0).astype(BF16)

    def doubled(slab):
        swapped = pltpu.roll(slab, HEAD_DIM, 1)
        return jnp.where(low, slab, swapped), jnp.where(low, swapped, slab)

    def with_ones(slab):
        swapped = pltpu.roll(slab, HEAD_DIM, 1)
        return jnp.where(low, slab, ones_col), jnp.where(low, swapped, ones_col)

    fq_ref[0] = (mm(C_FQ, FOX_WIDTH) * QK_SCALE).astype(BF16)
    fk_ref[0] = mm(C_FK, FOX_WIDTH).astype(BF16)
    fv = mm(C_FV, FOX_WIDTH).astype(BF16)
    for pair in range(FOX_WIDTH // LANES):
        fv_ref[0, 2 * pair], fv_ref[0, 2 * pair + 1] = with_ones(fv[:, pair * LANES:(pair + 1) * LANES])
    fz_ref[0] = _silu(mm(C_FZ, FOX_WIDTH)).astype(BF16)
    for j in range(NSA_WIDTH // (2 * LANES)):
        for i, t in enumerate(slab_pair(C_NQ + 2 * j * LANES)):
            c = (2 * j + i) * LANES
            nq_ref[0, :, c:c + LANES] = (_rope128(t, cos, sin) * QK_SCALE).astype(BF16)

    kc, vc = slab_pair(C_KC)
    for kind, slab in enumerate((_rope128(kc, cos, sin), vc)):
        for g in range(NSA_KV_HEADS):
            cmp_ref[kind, 0, g] = slab[:, g * HEAD_DIM:(g + 1) * HEAD_DIM]
    pos = pl.program_id(1) * tm + lax.broadcasted_iota(jnp.int32, (tm, LANES), 0)
    onehot = jnp.where(pos // SLC_BLOCK == lane, 1.0, 0.0).astype(BF16)
    ks, vs = slab_pair(C_KS)
    for g, k2 in enumerate(doubled(_rope128(ks, cos, sin).astype(BF16))):
        ks_ref[0, g, :, 0:LANES] = k2
        ks_ref[0, g, :, LANES:2 * LANES] = onehot
    vs_ref[0, 0], vs_ref[0, 1] = with_ones(vs.astype(BF16))
    kw, vw = slab_pair(C_KW)
    kw_ref[0, 0], kw_ref[0, 1] = doubled(_rope128(kw, cos, sin).astype(BF16))
    vw_ref[0, 0], vw_ref[0, 1] = doubled(vw.astype(BF16))
    nz_ref[0] = _silu(mm(C_NZ, NSA_WIDTH)).astype(BF16)
    misc = mm(C_MISC, LANES)
    misc_ref[0] = misc
    z = misc + bf_ref[...]
    log_f = jnp.minimum(z, 0.0) - jnp.log1p(jnp.exp(-jnp.abs(z)))
    lf_ref[0] = log_f.T[0:FOX_HEADS, :]


def _proj(x, mod, g_pre, w_cat, cos128, sin128, bf128):
    b, s, d = x.shape
    tm = PROJ_ROWS
    row = lambda bi, i: (bi, i, 0)
    wide = lambda dt: jax.ShapeDtypeStruct((b, s, FOX_WIDTH), dt)
    g = NSA_KV_HEADS
    grouped = lambda lanes: jax.ShapeDtypeStruct((b, g, s, lanes), BF16)
    grouped_spec = lambda lanes: pl.BlockSpec((1, g, tm, lanes), lambda bi, i: (bi, 0, i, 0))
    out_shape = (wide(BF16), wide(BF16), jax.ShapeDtypeStruct((b, FOX_HEADS, s, LANES), BF16),
                 wide(BF16), wide(BF16),
                 jax.ShapeDtypeStruct((2, b, g, s, HEAD_DIM), F32),
                 grouped(2 * LANES), grouped(LANES), grouped(LANES), grouped(LANES),
                 wide(BF16), jax.ShapeDtypeStruct((b, s, LANES), F32),
                 jax.ShapeDtypeStruct((b, FOX_HEADS, s), F32))
    wide_spec = pl.BlockSpec((1, tm, FOX_WIDTH), row)
    out_specs = (wide_spec, wide_spec,
                 pl.BlockSpec((1, FOX_HEADS, tm, LANES), lambda bi, i: (bi, 0, i, 0)),
                 wide_spec, wide_spec) + (
        pl.BlockSpec((2, 1, g, tm, HEAD_DIM), lambda bi, i: (0, bi, 0, i, 0)),
        grouped_spec(2 * LANES), grouped_spec(LANES), grouped_spec(LANES), grouped_spec(LANES),
        wide_spec, pl.BlockSpec((1, tm, LANES), row),
        pl.BlockSpec((1, FOX_HEADS, tm), lambda bi, i: (bi, 0, i)))
    pipelined = (tm * d * 4 + tm * (7 * FOX_WIDTH * 2 + 2 * g * LANES * 4 + 5 * g * LANES * 2 + LANES * 4)
                 + d * PROJ_COLS * 2)
    return pl.pallas_call(
        _proj_kernel,
        grid=(b, s // tm),
        in_specs=[pl.BlockSpec((1, tm, d), row),
                  pl.BlockSpec((1, 3, d), lambda bi, i: (bi, 0, 0)),
                  pl.BlockSpec((1, d), lambda bi, i: (0, 0)),
                  pl.BlockSpec((d, PROJ_COLS), lambda bi, i: (0, 0)),
                  pl.BlockSpec((tm, LANES), lambda bi, i: (i, 0)),
                  pl.BlockSpec((tm, LANES), lambda bi, i: (i, 0)),
                  pl.BlockSpec((1, LANES), lambda bi, i: (0, 0))],
        out_specs=out_specs,
        out_shape=out_shape,
        compiler_params=pltpu.CompilerParams(
            dimension_semantics=("arbitrary", "arbitrary"),
            vmem_limit_bytes=_vmem_limit(pipelined, 8 * 1024 * 1024)),
        name="proj",
    )(x, mod, g_pre, w_cat, cos128, sin128, bf128)


def _cumsum_kernel(x_ref, hi_ref, mid_ref, lo_ref, *, chunks):
    x = x_ref[...]
    n = x.shape[0]
    parts = _split3(x)
    r = lax.broadcasted_iota(jnp.int32, (LANES, LANES), 0)
    c = lax.broadcasted_iota(jnp.int32, (LANES, LANES), 1)
    tri = (r <= c).astype(BF16)
    rr = lax.broadcasted_iota(jnp.int32, (n, n), 0)
    cc = lax.broadcasted_iota(jnp.int32, (n, n), 1)
    earlier = jnp.logical_and(cc < rr, (cc // chunks) == (rr // chunks)).astype(BF16)
    within = sum(jnp.dot(p, tri, preferred_element_type=F32) for p in parts)
    before = sum(jnp.dot(earlier, p, preferred_element_type=F32) for p in parts)
    total = (within + jnp.sum(before, axis=-1, keepdims=True)) * LOG2E
    hi_ref[...], mid_ref[...], lo_ref[...] = _split3(total)


def _cumsum_lanes_split(v):
    rows, s = v.shape
    chunks = s // LANES
    n = rows * chunks
    part = jax.ShapeDtypeStruct((n, LANES), BF16)
    parts = pl.pallas_call(
        functools.partial(_cumsum_kernel, chunks=chunks),
        out_shape=(part, part, part),
        compiler_params=pltpu.CompilerParams(
            vmem_limit_bytes=_vmem_limit(2 * n * LANES * 4, 6 * n * n)),
        name="cumsum",
    )(v.reshape(n, LANES))
    return tuple(p.reshape(rows, s) for p in parts)


def _flash_scratch(rows, tile, acc_lanes):
    return ([pltpu.VMEM((rows, LANES), F32), pltpu.VMEM((rows, acc_lanes), F32)]
            + [pltpu.VMEM((rows, tile), F32)] * 2
            + [pltpu.VMEM((rows, tile), BF16)] * 2
            + [pltpu.VMEM((rows, LANES), F32)] * 2)


def _flash_scratch_bytes(rows, tile, acc_lanes):
    return rows * (LANES * 4 + acc_lanes * 4 + 2 * tile * 4 + 2 * tile * 2 + 2 * LANES * 4)


def _causal_flash(n_full, tile, q_ref, load_k, pv, causal_mask, scratch):
    m_ref, acc_ref, s0, s1, p0, p1, a0, a1 = scratch
    s_bufs, p_bufs, a_bufs = (s0, s1), (p0, p1), (a0, a1)

    def logits(j):
        return _dot_nt(q_ref[...], load_k(pl.multiple_of(j * tile, tile)))

    def softmax(s):
        m_prev = m_ref[...]
        m_next = jnp.maximum(m_prev, jnp.max(s, axis=1, keepdims=True))
        m_ref[...] = m_next
        p = jnp.exp2(s - pltpu.repeat(m_next, tile // LANES, 1))
        return p.astype(BF16), jnp.exp2(m_prev - m_next)

    def accumulate(p, alpha, j):
        acc_ref[...] = acc_ref[...] * alpha + pv(p, pl.multiple_of(j * tile, tile))

    def stage(j, cur):
        s_bufs[1 - cur][...] = logits(j + 1)
        p_bufs[cur][...], a_bufs[cur][...] = softmax(s_bufs[cur][...])
        accumulate(p_bufs[1 - cur][...], a_bufs[1 - cur][...], j - 1)

    def run_stages(first, count):
        for i in range(count):
            stage(first + i, (1 + i) % 2)

    def finish(cur):
        p, alpha = softmax(causal_mask(s_bufs[cur][...], n_full * tile))
        accumulate(p_bufs[1 - cur][...], a_bufs[1 - cur][...], n_full - 1)
        accumulate(p, alpha, n_full)

    m_ref[...] = jnp.full(m_ref.shape, MASK_VALUE, F32)
    acc_ref[...] = jnp.zeros(acc_ref.shape, F32)
    s_first = logits(0)
    s_bufs[1][...] = logits(jnp.minimum(n_full, 1))
    p_bufs[0][...], a_bufs[0][...] = softmax(causal_mask(s_first, 0))

    @pl.when(n_full == 0)
    def _():
        accumulate(p_bufs[0][...], a_bufs[0][...], 0)

    @pl.when(n_full > 0)
    def _():
        later = n_full - 1

        def unrolled(jj, carry):
            run_stages(1 + FLASH_UNROLL * jj, FLASH_UNROLL)
            return carry

        lax.fori_loop(0, later // FLASH_UNROLL, unrolled, 0)
        rest = later % FLASH_UNROLL

        @pl.when(rest >= 2)
        def _():
            run_stages(n_full - rest, 2)

        @pl.when(rest % 2 == 1)
        def _():
            stage(n_full - 1, 1)
            finish(0)

        @pl.when(rest % 2 == 0)
        def _():
            finish(1)


def _fox_aug(cum_split):
    b, h, s = cum_split[0].shape
    terms = jnp.concatenate(list(cum_split) + [jnp.ones((b, 1, s), BF16)], axis=1)
    terms = jnp.transpose(terms, (0, 2, 1))
    place_q = np.zeros((3 * h + 1, h * HEAD_DIM), np.float32)
    place_k = np.zeros((3 * h + 1, h * HEAD_DIM), np.float32)
    half = FOX_AUG // 2
    for head in range(h):
        for part in range(half):
            place_q[part * h + head, head * HEAD_DIM + part] = 1.0
            place_q[3 * h, head * HEAD_DIM + half + part] = 1.0
            place_k[3 * h, head * HEAD_DIM + part] = 1.0
            place_k[part * h + head, head * HEAD_DIM + half + part] = -1.0
    scatter = lambda place: jnp.einsum("bsr,rl->bsl", terms, jnp.asarray(place, BF16),
                                       preferred_element_type=F32).astype(BF16)
    return scatter(place_q), scatter(place_k)


def _fox_kernel(q_ref, qa_ref, k_ref, ka_ref, v_ref, z_ref, o_ref, qm_ref, *flash):
    t = FOX_TILE
    qi = pl.program_id(2)
    q = q_ref[0]
    qa = qa_ref[0]
    lane = lax.broadcasted_iota(jnp.int32, q.shape, 1)
    for hh in range(2):
        r = slice(hh * t, (hh + 1) * t)
        own = (lane < HEAD_DIM) if hh == 0 else (lane >= HEAD_DIM)
        qm_ref[r, 0:LANES] = jnp.where(own, q, jnp.zeros_like(q))
        qm_ref[r, LANES:2 * LANES] = jnp.where(own, qa, jnp.zeros_like(qa))

    def load_k(k0):
        return jnp.concatenate([k_ref[0, pl.ds(k0, t), :], ka_ref[0, pl.ds(k0, t), :]], axis=1)

    def pv(p, k0):
        return jnp.concatenate(
            [jnp.dot(p[hh * t:(hh + 1) * t], v_ref[0, hh, pl.ds(k0, t), :], preferred_element_type=F32)
             for hh in range(2)], axis=0)

    def causal_mask(s, k0):
        t_row = qi * t + (lax.broadcasted_iota(jnp.int32, s.shape, 0) & (t - 1))
        pos = k0 + lax.broadcasted_iota(jnp.int32, s.shape, 1)
        return jnp.where(pos <= t_row, s, MASK_VALUE)

    _causal_flash(qi, t, qm_ref, load_k, pv, causal_mask, flash)
    acc_ref = flash[1]
    o0 = acc_ref[0:t, :] * (1.0 / acc_ref[0:t, HEAD_DIM:HEAD_DIM + 1])
    o1 = acc_ref[t:2 * t, :] * (1.0 / acc_ref[t:2 * t, HEAD_DIM:HEAD_DIM + 1])
    o = jnp.where(lane < HEAD_DIM, o0, pltpu.roll(o1, HEAD_DIM, 1))
    o_ref[0] = (o * z_ref[0].astype(F32)).astype(BF16)


def _fox(fq, fk, fv, cum_split, fz):
    b, s, w = fq.shape
    t = FOX_TILE
    pairs = w // LANES
    qa, ka = _fox_aug(cum_split)
    tile = pl.BlockSpec((1, t, LANES), lambda bi, hp, i: (bi, i, hp))
    full = pl.BlockSpec((1, s, LANES), lambda bi, hp, i: (bi, 0, hp))
    resident = 4 * s * LANES * 2
    scratch = 2 * t * 2 * LANES * 2 + _flash_scratch_bytes(2 * t, t, LANES)
    return pl.pallas_call(
        _fox_kernel,
        grid=(b, pairs, s // t),
        in_specs=[tile, tile, full, full,
                  pl.BlockSpec((1, 2, s, LANES), lambda bi, hp, i: (bi, hp, 0, 0)),
                  tile],
        out_specs=tile,
        out_shape=jax.ShapeDtypeStruct((b, s, w), BF16),
        scratch_shapes=[pltpu.VMEM((2 * t, 2 * LANES), BF16)] + _flash_scratch(2 * t, t, LANES),
        compiler_params=pltpu.CompilerParams(
            dimension_semantics=("arbitrary", "arbitrary", "arbitrary"),
            vmem_limit_bytes=_vmem_limit(resident + 4 * t * LANES * 2, scratch + 4 * 2 * t * t * 4)),
        name="fox",
    )(fq, qa, fk, ka, fv, fz)


def _compress_kernel(t_ref, w1_ref, w2_ref, pe_ref, hi_ref, lo_ref):
    hp = lax.Precision.HIGHEST
    half = CMP_STRIDE * HEAD_DIM
    tt = t_ref[0, 0, 0]
    n = tt.shape[0]
    first = jnp.dot(tt, w1_ref[0, 0:half, :], precision=hp, preferred_element_type=F32)
    second = jnp.dot(tt, w1_ref[0, half:2 * half, :], precision=hp, preferred_element_type=F32)
    pe_term = jnp.dot(pe_ref[0], w1_ref[0], precision=hp, preferred_element_type=F32)[0:1, :]
    hidden = first + pltpu.roll(second, n - 1, 0) + pe_term
    out = jnp.dot(_silu(hidden), w2_ref[0], precision=hp, preferred_element_type=F32)
    hi = out.astype(BF16)
    hi_ref[0, 0, 0] = hi
    lo_ref[0, 0, 0] = (out - hi.astype(F32)).astype(BF16)


def _compress(t, w1, w2, pe):
    kinds, b, g, n, flat = t.shape
    width = w2.shape[-1]
    out_spec = pl.BlockSpec((1, 1, 1, n, width), lambda a, bi, gi: (a, bi, gi, 0, 0))
    out_part = jax.ShapeDtypeStruct((kinds, b, g, n, width), BF16)
    return pl.pallas_call(
        _compress_kernel,
        grid=(kinds, b, g),
        in_specs=[pl.BlockSpec((1, 1, 1, n, flat), lambda a, bi, gi: (a, bi, gi, 0, 0)),
                  pl.BlockSpec((1, 2 * flat, CMP_HIDDEN), lambda a, bi, gi: (a, 0, 0)),
                  pl.BlockSpec((1, CMP_HIDDEN, width), lambda a, bi, gi: (a, 0, 0)),
                  pl.BlockSpec((1, SUBLANES, 2 * flat), lambda a, bi, gi: (a, 0, 0))],
        out_specs=(out_spec, out_spec),
        out_shape=(out_part, out_part),
        compiler_params=pltpu.CompilerParams(
            dimension_semantics=("arbitrary", "arbitrary", "arbitrary"),
            vmem_limit_bytes=_vmem_limit(n * flat * 4 + 2 * flat * CMP_HIDDEN * 4, 8 * 1024 * 1024)),
        name="compress",
    )(t, w1, w2, pe)


def _softmax_numerator(s, bias):
    s = s + bias
    m = jnp.max(s, axis=1, keepdims=True)
    m = jnp.where(m == -jnp.inf, 0.0, m)
    return jnp.exp2(s - m)


def _stack_heads(q):
    tq = q.shape[0]
    lane = lax.broadcasted_iota(jnp.int32, (tq, LANES), 1)
    rows = []
    for h in range(NSA_GROUP):
        slab = q[:, (h // 2) * LANES:(h // 2 + 1) * LANES]
        own = (lane < HEAD_DIM) if h % 2 == 0 else (lane >= HEAD_DIM)
        rows.append(jnp.where(own, slab, jnp.zeros_like(slab)))
    return jnp.concatenate(rows, axis=0)


def _group_gates(misc, gi):
    gates = _sigmoid(misc)
    return jnp.where(gi == 0, gates, pltpu.roll(gates, LANES - 3 * NSA_GROUP, 1))


def _gate_lanes(gates, h, j):
    c = MISC_GATE0 + 3 * h + j
    return jnp.broadcast_to(gates[:, c:c + 1], gates.shape)


def _pair_slabs(per_head):
    lane = lax.broadcasted_iota(jnp.int32, per_head[0].shape, 1)
    return [jnp.where(lane < HEAD_DIM, per_head[2 * pp], per_head[2 * pp + 1])
            for pp in range(NSA_GROUP // 2)]


def _cmpwin_kernel(q_ref, kc_hi_ref, kc_lo_ref, vc_ref, overlap_ref, kw_ref, vw_ref, misc_ref,
                   part_ref, selb_ref):
    gi = pl.program_id(1)
    q0 = pl.multiple_of(pl.program_id(2) * NSA_Q, NSA_Q)
    quarter = kc_hi_ref.shape[3] // CMP_QUARTERS
    reach = q0 // (quarter * CMP_STRIDE)
    refs = (q_ref, kc_hi_ref, kc_lo_ref, vc_ref, overlap_ref, kw_ref, vw_ref, misc_ref,
            part_ref, selb_ref)
    for quarters in range(1, CMP_QUARTERS + 1):
        pl.when(reach == quarters - 1)(
            functools.partial(_cmpwin_body, refs, gi, q0, quarters * quarter))


def _cmpwin_body(refs, gi, q0, n_cmp):
    (q_ref, kc_hi_ref, kc_lo_ref, vc_ref, overlap_ref, kw_ref, vw_ref, misc_ref,
     part_ref, selb_ref) = refs
    tq = NSA_Q
    q4 = _stack_heads(q_ref[0])
    n_blk = n_cmp * CMP_STRIDE // SLC_BLOCK
    head_rows = [slice(h * tq, (h + 1) * tq) for h in range(NSA_GROUP)]

    s = (_dot_nt(q4, kc_hi_ref[0, 0, 0, 0:n_cmp, :])
         + _dot_nt(q4, kc_lo_ref[0, 0, 0, 0:n_cmp, :]))
    col = lax.broadcasted_iota(jnp.int32, (tq, n_cmp), 1)
    t_row = q0 + lax.broadcasted_iota(jnp.int32, (tq, n_cmp), 0)
    cmp_bias = jnp.where(col * CMP_STRIDE + (CMP_BLOCK - 1) <= t_row, 0.0, -jnp.inf)
    pcs = []
    for r in head_rows:
        p = _softmax_numerator(s[r], cmp_bias)
        pcs.append(p * (1.0 / jnp.maximum(jnp.sum(p, axis=1, keepdims=True), 1e-30)))
    oc = jnp.dot(jnp.concatenate(pcs, axis=0).astype(BF16), vc_ref[0, 0, 0, 0:n_cmp, :],
                 preferred_element_type=F32)

    pc_sum = pcs[0] + pcs[1] + pcs[2] + pcs[3]
    overlap = overlap_ref[0:n_blk, 0:n_cmp]
    imp = sum(_dot_nt(overlap, p) for p in _split3(pc_sum))
    blk = lax.broadcasted_iota(jnp.int32, (n_blk, tq), 0)
    cur = (q0 + lax.broadcasted_iota(jnp.int32, (n_blk, tq), 1)) // SLC_BLOCK
    forced = jnp.logical_or(blk == 0, jnp.logical_or(blk == cur, blk == cur - 1))
    imp = jnp.where(forced, jnp.inf, jnp.where(blk > cur, -jnp.inf, imp))

    def pick_one(_, carry):
        rem, sel = carry
        best = jnp.max(rem, axis=0, keepdims=True)
        first = jnp.min(jnp.where(rem == best, blk, n_blk), axis=0, keepdims=True)
        hit = blk == first
        return jnp.where(hit, -jnp.inf, rem), jnp.where(hit, 1.0, sel)

    _, sel = lax.fori_loop(0, min(N_SELECT, n_blk), pick_one, (imp, jnp.zeros_like(imp)),
                           unroll=True)
    all_blk = selb_ref.shape[3]
    if n_blk < all_blk:
        sel = jnp.concatenate([sel, jnp.zeros((all_blk - n_blk, tq), F32)], axis=0)
    selb_ref[0, 0] = jnp.where(sel.T > 0.5, 0.0, MASK_VALUE).astype(BF16)

    start = pl.multiple_of(jnp.maximum(q0 - WINDOW, 0), tq)
    kw = kw_ref[0, 0, pl.ds(start, WIN_KEYS), :]
    vw = vw_ref[0, 0, pl.ds(start, WIN_KEYS), :]
    sw = _dot_nt(q4, kw)
    pos = start + lax.broadcasted_iota(jnp.int32, (tq, WIN_KEYS), 1)
    dist = q0 + lax.broadcasted_iota(jnp.int32, (tq, WIN_KEYS), 0) - pos
    win_bias = jnp.where(jnp.logical_and(dist >= 0, dist < WINDOW), 0.0, -jnp.inf)
    pw = jnp.concatenate([_softmax_numerator(sw[r], win_bias) for r in head_rows],
                         axis=0).astype(BF16)
    ow = jnp.dot(pw, vw, preferred_element_type=F32)
    lw = jnp.dot(pw, jnp.ones((WIN_KEYS, LANES), BF16), preferred_element_type=F32)
    ow = ow * (1.0 / jnp.maximum(lw, 1e-30))

    gates = _group_gates(misc_ref[0], gi)
    gated = [_gate_lanes(gates, h, 0) * oc[r] + _gate_lanes(gates, h, 2) * ow[r]
             for h, r in enumerate(head_rows)]
    for pp, slab in enumerate(_pair_slabs(gated)):
        part_ref[0, :, pp * LANES:(pp + 1) * LANES] = slab


def _block_overlap(n_blk, n_cmp):
    ratio = SLC_BLOCK // CMP_STRIDE
    lo = np.arange(n_blk)[:, None] * ratio - (CMP_BLOCK // CMP_STRIDE - 1)
    i = np.arange(n_cmp)[None, :]
    n_overlap = (SLC_BLOCK + CMP_BLOCK) // CMP_STRIDE - 1
    return ((i >= lo) & (i < lo + n_overlap) & (i < n_cmp - 1)).astype(np.float32)


def _cmpwin(nq, cmp_hi, cmp_lo, kw, vw_ones, misc):
    b, s, w = nq.shape
    g = NSA_KV_HEADS
    tq = NSA_Q
    gw = NSA_GROUP * HEAD_DIM
    n_cmp = cmp_hi.shape[3]
    n_blk = s // SLC_BLOCK
    q_spec = pl.BlockSpec((1, tq, gw), lambda bi, gi, i: (bi, i, gi))
    key_spec = pl.BlockSpec((1, 1, 1, n_cmp, LANES), lambda bi, gi, i: (0, bi, gi, 0, 0))
    val_spec = pl.BlockSpec((1, 1, 1, n_cmp, LANES), lambda bi, gi, i: (1, bi, gi, 0, 0))
    seq_spec = pl.BlockSpec((1, 1, s, LANES), lambda bi, gi, i: (bi, gi, 0, 0))
    resident = 2 * s * LANES * 2 + 3 * n_cmp * LANES * 2 + n_blk * n_cmp * 2
    rows = NSA_GROUP * tq
    return pl.pallas_call(
        _cmpwin_kernel,
        grid=(b, g, s // tq),
        in_specs=[q_spec, key_spec, key_spec, val_spec,
                  pl.BlockSpec((n_blk, n_cmp), lambda bi, gi, i: (0, 0)),
                  seq_spec, seq_spec,
                  pl.BlockSpec((1, tq, LANES), lambda bi, gi, i: (bi, i, 0))],
        out_specs=(q_spec, pl.BlockSpec((1, 1, tq, n_blk), lambda bi, gi, i: (bi, gi, i, 0))),
        out_shape=(jax.ShapeDtypeStruct((b, s, w), F32),
                   jax.ShapeDtypeStruct((b, g, s, n_blk), BF16)),
        compiler_params=pltpu.CompilerParams(
            dimension_semantics=("arbitrary", "arbitrary", "arbitrary"),
            vmem_limit_bytes=_vmem_limit(resident + tq * gw * 8, 12 * rows * WIN_KEYS * 4)),
        name="cmpwin",
    )(nq, cmp_hi, cmp_lo, cmp_hi, jnp.asarray(_block_overlap(n_blk, n_cmp), BF16), kw, vw_ones, misc)


def _sel_kernel(q_ref, k_ref, v_ref, selb_ref, part_ref, misc_ref, z_ref, o_ref,
                qa_ref, gate_ref, *flash):
    tq = SEL_Q
    gi = pl.program_id(1)
    qi = pl.program_id(2)
    q0 = pl.multiple_of(qi * tq, tq)
    q = q_ref[0]
    selb = selb_ref[0, 0]
    lane = lax.broadcasted_iota(jnp.int32, (tq, LANES), 1)
    for h in range(NSA_GROUP):
        r = slice(h * tq, (h + 1) * tq)
        slab = q[:, (h // 2) * LANES:(h // 2 + 1) * LANES]
        own = (lane < HEAD_DIM) if h % 2 == 0 else (lane >= HEAD_DIM)
        qa_ref[r, 0:LANES] = jnp.where(own, slab, jnp.zeros_like(slab))
        qa_ref[r, LANES:2 * LANES] = selb
    gates = _group_gates(misc_ref[0], gi)
    for h in range(NSA_GROUP):
        gate_ref[h] = _gate_lanes(gates, h, 1)

    n_full = q0 // SEL_KV

    def causal_mask(s, k0):
        t_row = q0 + (lax.broadcasted_iota(jnp.int32, s.shape, 0) & (tq - 1))
        pos = k0 + lax.broadcasted_iota(jnp.int32, s.shape, 1)
        return jnp.where(pos <= t_row, s, MASK_VALUE)

    _causal_flash(n_full, SEL_KV, qa_ref,
                  lambda k0: k_ref[0, 0, pl.ds(k0, SEL_KV), :],
                  lambda p, k0: jnp.dot(p, v_ref[0, 0, pl.ds(k0, SEL_KV), :],
                                        preferred_element_type=F32),
                  causal_mask, flash)
    acc_ref = flash[1]

    gated = []
    for h in range(NSA_GROUP):
        acc = acc_ref[h * tq:(h + 1) * tq, :]
        scaled = acc * (gate_ref[h] * (1.0 / acc[:, HEAD_DIM:HEAD_DIM + 1]))
        gated.append(scaled if h % 2 == 0 else pltpu.roll(scaled, HEAD_DIM, 1))
    for pp, slab in enumerate(_pair_slabs(gated)):
        c = slice(pp * LANES, (pp + 1) * LANES)
        o_ref[0, :, c] = ((part_ref[0, :, c] + slab) * z_ref[0, :, c].astype(F32)).astype(BF16)


def _sel(nq, k_aug, v_ones, selb, part, misc, nz):
    b, s, w = nq.shape
    g = NSA_KV_HEADS
    tq = SEL_Q
    gw = NSA_GROUP * HEAD_DIM
    n_blk = s // SLC_BLOCK
    rows = NSA_GROUP * tq
    q_spec = pl.BlockSpec((1, tq, gw), lambda bi, gi, i: (bi, i, gi))
    resident = s * 3 * LANES * 2
    return pl.pallas_call(
        _sel_kernel,
        grid=(b, g, s // tq),
        in_specs=[q_spec,
                  pl.BlockSpec((1, 1, s, 2 * LANES), lambda bi, gi, i: (bi, gi, 0, 0)),
                  pl.BlockSpec((1, 1, s, LANES), lambda bi, gi, i: (bi, gi, 0, 0)),
                  pl.BlockSpec((1, 1, tq, n_blk), lambda bi, gi, i: (bi, gi, i, 0)),
                  q_spec,
                  pl.BlockSpec((1, tq, LANES), lambda bi, gi, i: (bi, i, 0)),
                  q_spec],
        out_specs=q_spec,
        out_shape=jax.ShapeDtypeStruct((b, s, w), BF16),
        scratch_shapes=[pltpu.VMEM((rows, 2 * LANES), BF16),
                        pltpu.VMEM((NSA_GROUP, tq, LANES), F32)] + _flash_scratch(rows, SEL_KV, LANES),
        compiler_params=pltpu.CompilerParams(
            dimension_semantics=("arbitrary", "arbitrary", "arbitrary"),
            vmem_limit_bytes=_vmem_limit(resident + tq * gw * 10,
                                         _flash_scratch_bytes(rows, SEL_KV, LANES) + 4 * rows * SEL_KV * 4)),
        name="sel",
    )(nq, k_aug, v_ones, selb, part, misc, nz)


def _out_kernel(yf_ref, yn_ref, w_ref, g_ref, mod_ref, x_ref, o_ref):
    y = (jnp.dot(yf_ref[0], w_ref[0:FOX_WIDTH, :], preferred_element_type=F32)
         + jnp.dot(yn_ref[0], w_ref[FOX_WIDTH:, :], preferred_element_type=F32))
    yn = y * lax.rsqrt(jnp.mean(y * y, axis=-1, keepdims=True) + RMS_EPS)
    o_ref[0] = x_ref[0] + mod_ref[0, 2:3, :] * (yn * g_ref[...])


def _out(y_fox, y_nsa, w_out, g_post, mod, x):
    b, s, d = x.shape
    tm = PROJ_ROWS
    row = lambda bi, i: (bi, i, 0)
    half = pl.BlockSpec((1, tm, FOX_WIDTH), row)
    pipelined = 2 * tm * FOX_WIDTH * 2 + 2 * tm * d * 4 + d * d * 2
    return pl.pallas_call(
        _out_kernel,
        grid=(b, s // tm),
        in_specs=[half, half,
                  pl.BlockSpec((d, d), lambda bi, i: (0, 0)),
                  pl.BlockSpec((1, d), lambda bi, i: (0, 0)),
                  pl.BlockSpec((1, 3, d), lambda bi, i: (bi, 0, 0)),
                  pl.BlockSpec((1, tm, d), row)],
        out_specs=pl.BlockSpec((1, tm, d), row),
        out_shape=jax.ShapeDtypeStruct((b, s, d), F32),
        compiler_params=pltpu.CompilerParams(
            dimension_semantics=("arbitrary", "arbitrary"),
            vmem_limit_bytes=_vmem_limit(pipelined, 6 * tm * d * 4)),
        name="out",
    )(y_fox, y_nsa, w_out, g_post, mod, x)


def _rope_slabs(seq_len):
    inv = 1.0 / (ROPE_THETA ** (jnp.arange(0, HEAD_DIM, 2, dtype=F32) / HEAD_DIM))
    ang = jnp.arange(seq_len, dtype=F32)[:, None] * inv[None, :]
    cos, sin = jnp.cos(ang), jnp.sin(ang)
    reps = LANES // (HEAD_DIM // 2)
    sign = jnp.tile(jnp.concatenate([-jnp.ones((HEAD_DIM // 2,), F32), jnp.ones((HEAD_DIM // 2,), F32)]),
                    LANES // HEAD_DIM)
    return jnp.tile(cos, (1, reps)), jnp.tile(sin, (1, reps)) * sign[None, :]


def _reorder_w_in(w_in):
    fw, kv = FOX_WIDTH, NSA_KV_WIDTH
    o = 0
    cols = {}
    for name, n in (("fq", fw), ("fk", fw), ("fv", fw), ("ff", FOX_HEADS), ("fz", fw), ("nq", NSA_WIDTH),
                    ("kc", kv), ("vc", kv), ("ks", kv), ("vs", kv), ("kw", kv), ("vw", kv),
                    ("ng", 3 * NSA_HEADS), ("nz", NSA_WIDTH)):
        cols[name] = w_in[:, o:o + n]
        o += n
    pad = jnp.zeros((w_in.shape[0], LANES - FOX_HEADS - 3 * NSA_HEADS), w_in.dtype)
    order = ("fq", "fk", "fv", "fz", "nq", "kc", "vc", "ks", "vs", "kw", "vw", "nz", "ff", "ng")
    return jnp.concatenate([cols[k] for k in order] + [pad], axis=1).astype(BF16)


def _layer(x, c8, g_pre, g_post, w_ada, b_ada, w_in, b_forget, w_cmp_k1, w_cmp_k2,
           w_cmp_v1, w_cmp_v2, pe_cmp_k, pe_cmp_v, w_out, cos128, sin128):
    b, s, d = x.shape
    mod = _ada(c8, w_ada, b_ada)[:b].reshape(b, 3, d)
    bf128 = jnp.pad(b_forget, (0, LANES - FOX_HEADS)).reshape(1, LANES)
    (fq, fk, fv, fz, nq, cmp_in, ks_aug, vs_ones, kw2, vw_ones, nz, misc, log_f) = _proj(
        x, mod, g_pre.reshape(1, d), _reorder_w_in(w_in), cos128, sin128, bf128)

    cum_split = tuple(p.reshape(b, FOX_HEADS, s)
                      for p in _cumsum_lanes_split(log_f.reshape(b * FOX_HEADS, s)))
    y_fox = _fox(fq, fk, fv, cum_split, fz)

    flat = CMP_STRIDE * HEAD_DIM
    t = cmp_in.reshape(2, b, NSA_KV_HEADS, s // CMP_STRIDE, flat)
    pe = jnp.stack([pe_cmp_k.reshape(1, 2 * flat), pe_cmp_v.reshape(1, 2 * flat)])
    w2 = jnp.stack([w_cmp_k2, w_cmp_v2])
    cmp_hi, cmp_lo = _compress(t, jnp.stack([w_cmp_k1, w_cmp_v1]), jnp.concatenate([w2, w2], axis=-1),
                               jnp.broadcast_to(pe, (2, SUBLANES, 2 * flat)))
    part, selb = _cmpwin(nq, cmp_hi, cmp_lo, kw2, vw_ones, misc)
    y_nsa = _sel(nq, ks_aug, vs_ones, selb, part, misc, nz)

    return _out(y_fox, y_nsa, w_out.astype(BF16), g_post.reshape(1, d), mod, x)


def kernel(x, c, g_pre, g_post, w_ada, b_ada, w_in, b_forget, w_cmp_k1, w_cmp_k2,
           w_cmp_v1, w_cmp_v2, pe_cmp_k, pe_cmp_v, w_out):
    cos128, sin128 = _rope_slabs(x.shape[1])
    c8 = jnp.pad(c, ((0, SUBLANES - c.shape[0]), (0, 0)))
    for layer in range(g_pre.shape[0]):
        x = _layer(x, c8, g_pre[layer], g_post[layer], w_ada[layer], b_ada[layer], w_in[layer],
                   b_forget[layer], w_cmp_k1[layer], w_cmp_k2[layer], w_cmp_v1[layer],
                   w_cmp_v2[layer], pe_cmp_k[layer], pe_cmp_v[layer], w_out[layer], cos128, sin128)
    return x
```

```python
import functools

import jax
import jax.numpy as jnp
import numpy as np
from jax import lax
from jax.experimental import pallas as pl
from jax.experimental.pallas import tpu as pltpu

F32 = jnp.float32
BF16 = jnp.bfloat16

D_MODEL = 1024
HEAD_DIM = 64
FOX_WIDTH = 512
NSA_WIDTH = 512
FOX_HEADS = 8
NSA_HEADS = 8
NSA_KV_HEADS = 2
NSA_GROUP = 4
NSA_KV_WIDTH = 128
CMP_BLOCK = 32
CMP_STRIDE = 16
CMP_HIDDEN = 128
SLC_BLOCK = 64
N_SELECT = 16
WINDOW = 512
ROPE_THETA = 10000.0
RMS_EPS = 1e-6
LOG2E = 1.4426950408889634
QK_SCALE = HEAD_DIM ** -0.5 * LOG2E

LANES = 128
SUBLANES = 8
V7X_VMEM_BYTES = 64 * 1024 * 1024
MASK_VALUE = -1e30

PROJ_ROWS = 512
FOX_TILE = 512
NSA_Q = 256
SEL_Q = 128
SEL_KV = 512
WIN_KEYS = WINDOW + NSA_Q
FLASH_UNROLL = 4
CMP_QUARTERS = 4
CMPWIN_TILES = 1
FOX_AUG = 6

C_FQ, C_FK, C_FV, C_FZ, C_NQ = 0, 512, 1024, 1536, 2048
C_KC, C_VC, C_KS, C_VS, C_KW, C_VW = 2560, 2688, 2816, 2944, 3072, 3200
C_NZ, C_MISC, PROJ_COLS = 3328, 3840, 3968
MISC_GATE0 = FOX_HEADS


def _vmem_limit(pipelined_bytes, resident_bytes):
    need = 2 * pipelined_bytes + resident_bytes
    return int(min(max(need, 16 * 1024 * 1024), V7X_VMEM_BYTES - 8 * 1024 * 1024))


def _sigmoid(v):
    return 1.0 / (1.0 + jnp.exp(-v))


def _silu(v):
    return v * _sigmoid(v)


def _dot_nt(a, b):
    return lax.dot_general(a, b, (((1,), (1,)), ((), ())), preferred_element_type=F32)


def _split3(v):
    hi = v.astype(BF16)
    r1 = v - hi.astype(F32)
    mid = r1.astype(BF16)
    lo = (r1 - mid.astype(F32)).astype(BF16)
    return hi, mid, lo


def _ada_kernel(c_ref, w_ref, b_ref, o_ref):
    a = _silu(c_ref[...])
    o_ref[...] = jnp.dot(a, w_ref[...], precision=lax.Precision.HIGHEST,
                         preferred_element_type=F32) + b_ref[...]


def _ada(c8, w_ada, b_ada):
    n = w_ada.shape[1]
    blk = D_MODEL
    return pl.pallas_call(
        _ada_kernel,
        grid=(n // blk,),
        in_specs=[pl.BlockSpec((SUBLANES, D_MODEL), lambda j: (0, 0)),
                  pl.BlockSpec((D_MODEL, blk), lambda j: (0, j)),
                  pl.BlockSpec((1, blk), lambda j: (0, j))],
        out_specs=pl.BlockSpec((SUBLANES, blk), lambda j: (0, j)),
        out_shape=jax.ShapeDtypeStruct((SUBLANES, n), F32),
        compiler_params=pltpu.CompilerParams(
            dimension_semantics=("arbitrary",),
            vmem_limit_bytes=_vmem_limit(D_MODEL * blk * 4, 4 * 1024 * 1024)),
        name="ada",
    )(c8, w_ada, b_ada.reshape(1, n))


def _rope128(t, cos, sin_signed):
    lane = lax.broadcasted_iota(jnp.int32, t.shape, 1)
    first_half = (lane & (HEAD_DIM - 1)) < HEAD_DIM // 2
    partner = jnp.where(first_half,
                        pltpu.roll(t, LANES - HEAD_DIM // 2, 1),
                        pltpu.roll(t, HEAD_DIM // 2, 1))
    return t * cos + partner * sin_signed


def _proj_kernel(x_ref, mod_ref, g_ref, w_ref, cos_ref, sin_ref, bf_ref,
                 fq_ref, fk_ref, fv_ref, fz_ref, nq_ref, cmp_ref,
                 ks_ref, vs_ref, kw_ref, vw_ref, nz_ref, misc_ref, lf_ref):
    tm = x_ref.shape[1]
    x = x_ref[0]
    y = x * lax.rsqrt(jnp.mean(x * x, axis=-1, keepdims=True) + RMS_EPS)
    y = y * g_ref[...]
    h = (y * (1.0 + mod_ref[0, 1:2, :]) + mod_ref[0, 0:1, :]).astype(BF16)
    cos = cos_ref[...]
    sin = sin_ref[...]

    def mm(lo, n):
        return jnp.dot(h, w_ref[:, lo:lo + n], preferred_element_type=F32)

    def slab_pair(lo):
        both = mm(lo, 2 * LANES)
        return both[:, 0:LANES], both[:, LANES:2 * LANES]

    lane = lax.broadcasted_iota(jnp.int32, (tm, LANES), 1)
    low = lane < HEAD_DIM
    ones_col = jnp.where(lane == HEAD_DIM, 1.0, 0.0).astype(BF16)

    def doubled(slab):
        swapped = pltpu.roll(slab, HEAD_DIM, 1)
        return jnp.where(low, slab, swapped), jnp.where(low, swapped, slab)

    def with_ones(slab):
        swapped = pltpu.roll(slab, HEAD_DIM, 1)
        return jnp.where(low, slab, ones_col), jnp.where(low, swapped, ones_col)

    fq_ref[0] = (mm(C_FQ, FOX_WIDTH) * QK_SCALE).astype(BF16)
    fk_ref[0] = mm(C_FK, FOX_WIDTH).astype(BF16)
    fv = mm(C_FV, FOX_WIDTH).astype(BF16)
    for pair in range(FOX_WIDTH // LANES):
        fv_ref[0, 2 * pair], fv_ref[0, 2 * pair + 1] = with_ones(fv[:, pair * LANES:(pair + 1) * LANES])
    fz_ref[0] = _silu(mm(C_FZ, FOX_WIDTH)).astype(BF16)
    for j in range(NSA_WIDTH // (2 * LANES)):
        for i, t in enumerate(slab_pair(C_NQ + 2 * j * LANES)):
            c = (2 * j + i) * LANES
            nq_ref[0, :, c:c + LANES] = (_rope128(t, cos, sin) * QK_SCALE).astype(BF16)

    kc, vc = slab_pair(C_KC)
    for kind, slab in enumerate((_rope128(kc, cos, sin), vc)):
        for g in range(NSA_KV_HEADS):
            cmp_ref[kind, 0, g] = slab[:, g * HEAD_DIM:(g + 1) * HEAD_DIM]
    pos = pl.program_id(1) * tm + lax.broadcasted_iota(jnp.int32, (tm, LANES), 0)
    onehot = jnp.where(pos // SLC_BLOCK == lane, 1.0, 0.0).astype(BF16)
    ks, vs = slab_pair(C_KS)
    for g, k2 in enumerate(doubled(_rope128(ks, cos, sin).astype(BF16))):
        ks_ref[0, g, :, 0:LANES] = k2
        ks_ref[0, g, :, LANES:2 * LANES] = onehot
    vs_ref[0, 0], vs_ref[0, 1] = with_ones(vs.astype(BF16))
    kw, vw = slab_pair(C_KW)
    kw_ref[0, 0], kw_ref[0, 1] = doubled(_rope128(kw, cos, sin).astype(BF16))
    vw_ref[0, 0], vw_ref[0, 1] = doubled(vw.astype(BF16))
    nz_ref[0] = _silu(mm(C_NZ, NSA_WIDTH)).astype(BF16)
    misc = mm(C_MISC, LANES)
    misc_ref[0] = misc
    z = misc + bf_ref[...]
    log_f = jnp.minimum(z, 0.0) - jnp.log1p(jnp.exp(-jnp.abs(z)))
    lf_ref[0] = log_f.T[0:FOX_HEADS, :]


def _proj(x, mod, g_pre, w_cat, cos128, sin128, bf128):
    b, s, d = x.shape
    tm = PROJ_ROWS
    row = lambda bi, i: (bi, i, 0)
    wide = lambda dt: jax.ShapeDtypeStruct((b, s, FOX_WIDTH), dt)
    g = NSA_KV_HEADS
    grouped = lambda lanes: jax.ShapeDtypeStruct((b, g, s, lanes), BF16)
    grouped_spec = lambda lanes: pl.BlockSpec((1, g, tm, lanes), lambda bi, i: (bi, 0, i, 0))
    out_shape = (wide(BF16), wide(BF16), jax.ShapeDtypeStruct((b, FOX_HEADS, s, LANES), BF16),
                 wide(BF16), wide(BF16),
                 jax.ShapeDtypeStruct((2, b, g, s, HEAD_DIM), F32),
                 grouped(2 * LANES), grouped(LANES), grouped(LANES), grouped(LANES),
                 wide(BF16), jax.ShapeDtypeStruct((b, s, LANES), F32),
                 jax.ShapeDtypeStruct((b, FOX_HEADS, s), F32))
    wide_spec = pl.BlockSpec((1, tm, FOX_WIDTH), row)
    out_specs = (wide_spec, wide_spec,
                 pl.BlockSpec((1, FOX_HEADS, tm, LANES), lambda bi, i: (bi, 0, i, 0)),
                 wide_spec, wide_spec) + (
        pl.BlockSpec((2, 1, g, tm, HEAD_DIM), lambda bi, i: (0, bi, 0, i, 0)),
        grouped_spec(2 * LANES), grouped_spec(LANES), grouped_spec(LANES), grouped_spec(LANES),
        wide_spec, pl.BlockSpec((1, tm, LANES), row),
        pl.BlockSpec((1, FOX_HEADS, tm), lambda bi, i: (bi, 0, i)))
    pipelined = (tm * d * 4 + tm * (7 * FOX_WIDTH * 2 + 2 * g * LANES * 4 + 5 * g * LANES * 2 + LANES * 4)
                 + d * PROJ_COLS * 2)
    return pl.pallas_call(
        _proj_kernel,
        grid=(b, s // tm),
        in_specs=[pl.BlockSpec((1, tm, d), row),
                  pl.BlockSpec((1, 3, d), lambda bi, i: (bi, 0, 0)),
                  pl.BlockSpec((1, d), lambda bi, i: (0, 0)),
                  pl.BlockSpec((d, PROJ_COLS), lambda bi, i: (0, 0)),
                  pl.BlockSpec((tm, LANES), lambda bi, i: (i, 0)),
                  pl.BlockSpec((tm, LANES), lambda bi, i: (i, 0)),
                  pl.BlockSpec((1, LANES), lambda bi, i: (0, 0))],
        out_specs=out_specs,
        out_shape=out_shape,
        compiler_params=pltpu.CompilerParams(
            dimension_semantics=("arbitrary", "arbitrary"),
            vmem_limit_bytes=_vmem_limit(pipelined, 8 * 1024 * 1024)),
        name="proj",
    )(x, mod, g_pre, w_cat, cos128, sin128, bf128)


def _cumsum_kernel(x_ref, hi_ref, mid_ref, lo_ref, *, chunks):
    x = x_ref[...]
    n = x.shape[0]
    parts = _split3(x)
    r = lax.broadcasted_iota(jnp.int32, (LANES, LANES), 0)
    c = lax.broadcasted_iota(jnp.int32, (LANES, LANES), 1)
    tri = (r <= c).astype(BF16)
    rr = lax.broadcasted_iota(jnp.int32, (n, n), 0)
    cc = lax.broadcasted_iota(jnp.int32, (n, n), 1)
    earlier = jnp.logical_and(cc < rr, (cc // chunks) == (rr // chunks)).astype(BF16)
    within = sum(jnp.dot(p, tri, preferred_element_type=F32) for p in parts)
    before = sum(jnp.dot(earlier, p, preferred_element_type=F32) for p in parts)
    total = (within + jnp.sum(before, axis=-1, keepdims=True)) * LOG2E
    hi_ref[...], mid_ref[...], lo_ref[...] = _split3(total)


def _cumsum_lanes_split(v):
    rows, s = v.shape
    chunks = s // LANES
    n = rows * chunks
    part = jax.ShapeDtypeStruct((n, LANES), BF16)
    parts = pl.pallas_call(
        functools.partial(_cumsum_kernel, chunks=chunks),
        out_shape=(part, part, part),
        compiler_params=pltpu.CompilerParams(
            vmem_limit_bytes=_vmem_limit(2 * n * LANES * 4, 6 * n * n)),
        name="cumsum",
    )(v.reshape(n, LANES))
    return tuple(p.reshape(rows, s) for p in parts)


def _flash_scratch(rows, tile, acc_lanes):
    return ([pltpu.VMEM((rows, LANES), F32), pltpu.VMEM((rows, acc_lanes), F32)]
            + [pltpu.VMEM((rows, tile), F32)] * 2
            + [pltpu.VMEM((rows, tile), BF16)] * 2
            + [pltpu.VMEM((rows, LANES), F32)] * 2)


def _flash_scratch_bytes(rows, tile, acc_lanes):
    return rows * (LANES * 4 + acc_lanes * 4 + 2 * tile * 4 + 2 * tile * 2 + 2 * LANES * 4)


def _causal_flash(n_full, tile, q_ref, load_k, pv, causal_mask, scratch):
    m_ref, acc_ref, s0, s1, p0, p1, a0, a1 = scratch
    s_bufs, p_bufs, a_bufs = (s0, s1), (p0, p1), (a0, a1)

    def logits(j):
        return _dot_nt(q_ref[...], load_k(pl.multiple_of(j * tile, tile)))

    def softmax(s):
        m_prev = m_ref[...]
        m_next = jnp.maximum(m_prev, jnp.max(s, axis=1, keepdims=True))
        m_ref[...] = m_next
        p = jnp.exp2(s - pltpu.repeat(m_next, tile // LANES, 1))
        return p.astype(BF16), jnp.exp2(m_prev - m_next)

    def accumulate(p, alpha, j):
        acc_ref[...] = acc_ref[...] * alpha + pv(p, pl.multiple_of(j * tile, tile))

    def stage(j, cur):
        s_bufs[1 - cur][...] = logits(j + 1)
        p_bufs[cur][...], a_bufs[cur][...] = softmax(s_bufs[cur][...])
        accumulate(p_bufs[1 - cur][...], a_bufs[1 - cur][...], j - 1)

    def run_stages(first, count):
        for i in range(count):
            stage(first + i, (1 + i) % 2)

    def finish(cur):
        p, alpha = softmax(causal_mask(s_bufs[cur][...], n_full * tile))
        accumulate(p_bufs[1 - cur][...], a_bufs[1 - cur][...], n_full - 1)
        accumulate(p, alpha, n_full)

    m_ref[...] = jnp.full(m_ref.shape, MASK_VALUE, F32)
    acc_ref[...] = jnp.zeros(acc_ref.shape, F32)
    s_first = logits(0)
    s_bufs[1][...] = logits(jnp.minimum(n_full, 1))
    p_bufs[0][...], a_bufs[0][...] = softmax(causal_mask(s_first, 0))

    @pl.when(n_full == 0)
    def _():
        accumulate(p_bufs[0][...], a_bufs[0][...], 0)

    @pl.when(n_full > 0)
    def _():
        later = n_full - 1

        def unrolled(jj, carry):
            run_stages(1 + FLASH_UNROLL * jj, FLASH_UNROLL)
            return carry

        lax.fori_loop(0, later // FLASH_UNROLL, unrolled, 0)
        rest = later % FLASH_UNROLL

        @pl.when(rest >= 2)
        def _():
            run_stages(n_full - rest, 2)

        @pl.when(rest % 2 == 1)
        def _():
            stage(n_full - 1, 1)
            finish(0)

        @pl.when(rest % 2 == 0)
        def _():
            finish(1)


def _fox_aug(cum_split):
    b, h, s = cum_split[0].shape
    terms = jnp.concatenate(list(cum_split) + [jnp.ones((b, 1, s), BF16)], axis=1)
    terms = jnp.transpose(terms, (0, 2, 1))
    place_q = np.zeros((3 * h + 1, h * HEAD_DIM), np.float32)
    place_k = np.zeros((3 * h + 1, h * HEAD_DIM), np.float32)
    half = FOX_AUG // 2
    for head in range(h):
        for part in range(half):
            place_q[part * h + head, head * HEAD_DIM + part] = 1.0
            place_q[3 * h, head * HEAD_DIM + half + part] = 1.0
            place_k[3 * h, head * HEAD_DIM + part] = 1.0
            place_k[part * h + head, head * HEAD_DIM + half + part] = -1.0
    scatter = lambda place: jnp.einsum("bsr,rl->bsl", terms, jnp.asarray(place, BF16),
                                       preferred_element_type=F32).astype(BF16)
    return scatter(place_q), scatter(place_k)


def _fox_kernel(q_ref, qa_ref, k_ref, ka_ref, v_ref, z_ref, o_ref, qm_ref, *flash):
    t = FOX_TILE
    qi = pl.program_id(2)
    q = q_ref[0]
    qa = qa_ref[0]
    lane = lax.broadcasted_iota(jnp.int32, q.shape, 1)
    for hh in range(2):
        r = slice(hh * t, (hh + 1) * t)
        own = (lane < HEAD_DIM) if hh == 0 else (lane >= HEAD_DIM)
        qm_ref[r, 0:LANES] = jnp.where(own, q, jnp.zeros_like(q))
        qm_ref[r, LANES:2 * LANES] = jnp.where(own, qa, jnp.zeros_like(qa))

    def load_k(k0):
        return jnp.concatenate([k_ref[0, pl.ds(k0, t), :], ka_ref[0, pl.ds(k0, t), :]], axis=1)

    def pv(p, k0):
        return jnp.concatenate(
            [jnp.dot(p[hh * t:(hh + 1) * t], v_ref[0, hh, pl.ds(k0, t), :], preferred_element_type=F32)
             for hh in range(2)], axis=0)

    def causal_mask(s, k0):
        t_row = qi * t + (lax.broadcasted_iota(jnp.int32, s.shape, 0) & (t - 1))
        pos = k0 + lax.broadcasted_iota(jnp.int32, s.shape, 1)
        return jnp.where(pos <= t_row, s, MASK_VALUE)

    _causal_flash(qi, t, qm_ref, load_k, pv, causal_mask, flash)
    acc_ref = flash[1]
    o0 = acc_ref[0:t, :] * (1.0 / acc_ref[0:t, HEAD_DIM:HEAD_DIM + 1])
    o1 = acc_ref[t:2 * t, :] * (1.0 / acc_ref[t:2 * t, HEAD_DIM:HEAD_DIM + 1])
    o = jnp.where(lane < HEAD_DIM, o0, pltpu.roll(o1, HEAD_DIM, 1))
    o_ref[0] = (o * z_ref[0].astype(F32)).astype(BF16)


def _fox(fq, fk, fv, cum_split, fz):
    b, s, w = fq.shape
    t = FOX_TILE
    pairs = w // LANES
    qa, ka = _fox_aug(cum_split)
    tile = pl.BlockSpec((1, t, LANES), lambda bi, hp, i: (bi, i, hp))
    full = pl.BlockSpec((1, s, LANES), lambda bi, hp, i: (bi, 0, hp))
    resident = 4 * s * LANES * 2
    scratch = 2 * t * 2 * LANES * 2 + _flash_scratch_bytes(2 * t, t, LANES)
    return pl.pallas_call(
        _fox_kernel,
        grid=(b, pairs, s // t),
        in_specs=[tile, tile, full, full,
                  pl.BlockSpec((1, 2, s, LANES), lambda bi, hp, i: (bi, hp, 0, 0)),
                  tile],
        out_specs=tile,
        out_shape=jax.ShapeDtypeStruct((b, s, w), BF16),
        scratch_shapes=[pltpu.VMEM((2 * t, 2 * LANES), BF16)] + _flash_scratch(2 * t, t, LANES),
        compiler_params=pltpu.CompilerParams(
            dimension_semantics=("arbitrary", "arbitrary", "arbitrary"),
            vmem_limit_bytes=_vmem_limit(resident + 4 * t * LANES * 2, scratch + 4 * 2 * t * t * 4)),
        name="fox",
    )(fq, qa, fk, ka, fv, fz)


def _compress_kernel(t_ref, w1_ref, w2_ref, pe_ref, hi_ref, lo_ref):
    hp = lax.Precision.HIGHEST
    half = CMP_STRIDE * HEAD_DIM
    tt = t_ref[0, 0, 0]
    n = tt.shape[0]
    first = jnp.dot(tt, w1_ref[0, 0:half, :], precision=hp, preferred_element_type=F32)
    second = jnp.dot(tt, w1_ref[0, half:2 * half, :], precision=hp, preferred_element_type=F32)
    pe_term = jnp.dot(pe_ref[0], w1_ref[0], precision=hp, preferred_element_type=F32)[0:1, :]
    hidden = first + pltpu.roll(second, n - 1, 0) + pe_term
    out = jnp.dot(_silu(hidden), w2_ref[0], precision=hp, preferred_element_type=F32)
    hi = out.astype(BF16)
    hi_ref[0, 0, 0] = hi
    lo_ref[0, 0, 0] = (out - hi.astype(F32)).astype(BF16)


def _compress(t, w1, w2, pe):
    kinds, b, g, n, flat = t.shape
    width = w2.shape[-1]
    out_spec = pl.BlockSpec((1, 1, 1, n, width), lambda a, bi, gi: (a, bi, gi, 0, 0))
    out_part = jax.ShapeDtypeStruct((kinds, b, g, n, width), BF16)
    return pl.pallas_call(
        _compress_kernel,
        grid=(kinds, b, g),
        in_specs=[pl.BlockSpec((1, 1, 1, n, flat), lambda a, bi, gi: (a, bi, gi, 0, 0)),
                  pl.BlockSpec((1, 2 * flat, CMP_HIDDEN), lambda a, bi, gi: (a, 0, 0)),
                  pl.BlockSpec((1, CMP_HIDDEN, width), lambda a, bi, gi: (a, 0, 0)),
                  pl.BlockSpec((1, SUBLANES, 2 * flat), lambda a, bi, gi: (a, 0, 0))],
        out_specs=(out_spec, out_spec),
        out_shape=(out_part, out_part),
        compiler_params=pltpu.CompilerParams(
            dimension_semantics=("arbitrary", "arbitrary", "arbitrary"),
            vmem_limit_bytes=_vmem_limit(n * flat * 4 + 2 * flat * CMP_HIDDEN * 4, 8 * 1024 * 1024)),
        name="compress",
    )(t, w1, w2, pe)


def _softmax_numerator(s, bias):
    s = s + bias
    m = jnp.max(s, axis=1, keepdims=True)
    m = jnp.where(m == -jnp.inf, 0.0, m)
    return jnp.exp2(s - m)


def _stack_heads(q):
    tq = q.shape[0]
    lane = lax.broadcasted_iota(jnp.int32, (tq, LANES), 1)
    rows = []
    for h in range(NSA_GROUP):
        slab = q[:, (h // 2) * LANES:(h // 2 + 1) * LANES]
        own = (lane < HEAD_DIM) if h % 2 == 0 else (lane >= HEAD_DIM)
        rows.append(jnp.where(own, slab, jnp.zeros_like(slab)))
    return jnp.concatenate(rows, axis=0)


def _group_gates(misc, gi):
    gates = _sigmoid(misc)
    return jnp.where(gi == 0, gates, pltpu.roll(gates, LANES - 3 * NSA_GROUP, 1))


def _gate_lanes(gates, h, j):
    c = MISC_GATE0 + 3 * h + j
    return jnp.broadcast_to(gates[:, c:c + 1], gates.shape)


def _pair_slabs(per_head):
    lane = lax.broadcasted_iota(jnp.int32, per_head[0].shape, 1)
    return [jnp.where(lane < HEAD_DIM, per_head[2 * pp], per_head[2 * pp + 1])
            for pp in range(NSA_GROUP // 2)]


def _cmpwin_kernel(q_ref, kc_hi_ref, kc_lo_ref, vc_ref, overlap_ref, kw_ref, vw_ref, misc_ref,
                   part_ref, selb_ref):
    gi = pl.program_id(1)
    step = NSA_Q * CMPWIN_TILES
    q0 = pl.multiple_of(pl.program_id(2) * step, step)
    quarter = kc_hi_ref.shape[3] // CMP_QUARTERS
    reach = (q0 + step - NSA_Q) // (quarter * CMP_STRIDE)
    refs = (q_ref, kc_hi_ref, kc_lo_ref, vc_ref, overlap_ref, kw_ref, vw_ref, misc_ref,
            part_ref, selb_ref)
    for quarters in range(1, CMP_QUARTERS + 1):
        pl.when(reach == quarters - 1)(
            functools.partial(_cmpwin_tiles, refs, gi, q0, quarters * quarter))


def _cmpwin_tiles(refs, gi, q0, n_cmp):
    for i in range(CMPWIN_TILES):
        _cmpwin_body(refs, gi, q0 + i * NSA_Q, slice(i * NSA_Q, (i + 1) * NSA_Q), n_cmp)


def _cmpwin_body(refs, gi, q0, rows, n_cmp):
    (q_ref, kc_hi_ref, kc_lo_ref, vc_ref, overlap_ref, kw_ref, vw_ref, misc_ref,
     part_ref, selb_ref) = refs
    tq = NSA_Q
    q4 = _stack_heads(q_ref[0, rows, :])
    n_blk = n_cmp * CMP_STRIDE // SLC_BLOCK
    head_rows = [slice(h * tq, (h + 1) * tq) for h in range(NSA_GROUP)]

    s = (_dot_nt(q4, kc_hi_ref[0, 0, 0, 0:n_cmp, :])
         + _dot_nt(q4, kc_lo_ref[0, 0, 0, 0:n_cmp, :]))
    col = lax.broadcasted_iota(jnp.int32, (tq, n_cmp), 1)
    t_row = q0 + lax.broadcasted_iota(jnp.int32, (tq, n_cmp), 0)
    cmp_bias = jnp.where(col * CMP_STRIDE + (CMP_BLOCK - 1) <= t_row, 0.0, -jnp.inf)
    pcs = []
    for r in head_rows:
        p = _softmax_numerator(s[r], cmp_bias)
        pcs.append(p * (1.0 / jnp.maximum(jnp.sum(p, axis=1, keepdims=True), 1e-30)))
    oc = jnp.dot(jnp.concatenate(pcs, axis=0).astype(BF16), vc_ref[0, 0, 0, 0:n_cmp, :],
                 preferred_element_type=F32)

    pc_sum = pcs[0] + pcs[1] + pcs[2] + pcs[3]
    overlap = overlap_ref[0:n_blk, 0:n_cmp]
    imp = sum(_dot_nt(overlap, p) for p in _split3(pc_sum))
    blk = lax.broadcasted_iota(jnp.int32, (n_blk, tq), 0)
    cur = (q0 + lax.broadcasted_iota(jnp.int32, (n_blk, tq), 1)) // SLC_BLOCK
    forced = jnp.logical_or(blk == 0, jnp.logical_or(blk == cur, blk == cur - 1))
    imp = jnp.where(forced, jnp.inf, jnp.where(blk > cur, -jnp.inf, imp))

    def pick_one(_, carry):
        rem, sel = carry
        best = jnp.max(rem, axis=0, keepdims=True)
        first = jnp.min(jnp.where(rem == best, blk, n_blk), axis=0, keepdims=True)
        hit = blk == first
        return jnp.where(hit, -jnp.inf, rem), jnp.where(hit, 1.0, sel)

    _, sel = lax.fori_loop(0, min(N_SELECT, n_blk), pick_one, (imp, jnp.zeros_like(imp)),
                           unroll=True)
    all_blk = selb_ref.shape[3]
    if n_blk < all_blk:
        sel = jnp.concatenate([sel, jnp.zeros((all_blk - n_blk, tq), F32)], axis=0)
    selb_ref[0, 0, rows, :] = jnp.where(sel.T > 0.5, 0.0, MASK_VALUE).astype(BF16)

    start = pl.multiple_of(jnp.maximum(q0 - WINDOW, 0), tq)
    kw = kw_ref[0, 0, pl.ds(start, WIN_KEYS), :]
    vw = vw_ref[0, 0, pl.ds(start, WIN_KEYS), :]
    sw = _dot_nt(q4, kw)
    pos = start + lax.broadcasted_iota(jnp.int32, (tq, WIN_KEYS), 1)
    dist = q0 + lax.broadcasted_iota(jnp.int32, (tq, WIN_KEYS), 0) - pos
    win_bias = jnp.where(jnp.logical_and(dist >= 0, dist < WINDOW), 0.0, -jnp.inf)
    pw = jnp.concatenate([_softmax_numerator(sw[r], win_bias) for r in head_rows],
                         axis=0).astype(BF16)
    ow = jnp.dot(pw, vw, preferred_element_type=F32)
    lw = jnp.dot(pw, jnp.ones((WIN_KEYS, LANES), BF16), preferred_element_type=F32)
    ow = ow * (1.0 / jnp.maximum(lw, 1e-30))

    gates = _group_gates(misc_ref[0, rows, :], gi)
    gated = [_gate_lanes(gates, h, 0) * oc[r] + _gate_lanes(gates, h, 2) * ow[r]
             for h, r in enumerate(head_rows)]
    for pp, slab in enumerate(_pair_slabs(gated)):
        part_ref[0, rows, pp * LANES:(pp + 1) * LANES] = slab


def _block_overlap(n_blk, n_cmp):
    ratio = SLC_BLOCK // CMP_STRIDE
    lo = np.arange(n_blk)[:, None] * ratio - (CMP_BLOCK // CMP_STRIDE - 1)
    i = np.arange(n_cmp)[None, :]
    n_overlap = (SLC_BLOCK + CMP_BLOCK) // CMP_STRIDE - 1
    return ((i >= lo) & (i < lo + n_overlap) & (i < n_cmp - 1)).astype(np.float32)


def _cmpwin(nq, cmp_hi, cmp_lo, kw, vw_ones, misc):
    b, s, w = nq.shape
    g = NSA_KV_HEADS
    tq = NSA_Q * CMPWIN_TILES
    gw = NSA_GROUP * HEAD_DIM
    n_cmp = cmp_hi.shape[3]
    n_blk = s // SLC_BLOCK
    q_spec = pl.BlockSpec((1, tq, gw), lambda bi, gi, i: (bi, i, gi))
    key_spec = pl.BlockSpec((1, 1, 1, n_cmp, LANES), lambda bi, gi, i: (0, bi, gi, 0, 0))
    val_spec = pl.BlockSpec((1, 1, 1, n_cmp, LANES), lambda bi, gi, i: (1, bi, gi, 0, 0))
    seq_spec = pl.BlockSpec((1, 1, s, LANES), lambda bi, gi, i: (bi, gi, 0, 0))
    resident = 2 * s * LANES * 2 + 3 * n_cmp * LANES * 2 + n_blk * n_cmp * 2
    rows = NSA_GROUP * tq
    return pl.pallas_call(
        _cmpwin_kernel,
        grid=(b, g, s // tq),
        in_specs=[q_spec, key_spec, key_spec, val_spec,
                  pl.BlockSpec((n_blk, n_cmp), lambda bi, gi, i: (0, 0)),
                  seq_spec, seq_spec,
                  pl.BlockSpec((1, tq, LANES), lambda bi, gi, i: (bi, i, 0))],
        out_specs=(q_spec, pl.BlockSpec((1, 1, tq, n_blk), lambda bi, gi, i: (bi, gi, i, 0))),
        out_shape=(jax.ShapeDtypeStruct((b, s, w), F32),
                   jax.ShapeDtypeStruct((b, g, s, n_blk), BF16)),
        compiler_params=pltpu.CompilerParams(
            dimension_semantics=("arbitrary", "arbitrary", "arbitrary"),
            vmem_limit_bytes=_vmem_limit(resident + tq * gw * 8, 12 * rows * WIN_KEYS * 4)),
        name="cmpwin",
    )(nq, cmp_hi, cmp_lo, cmp_hi, jnp.asarray(_block_overlap(n_blk, n_cmp), BF16), kw, vw_ones, misc)


def _sel_kernel(q_ref, k_ref, v_ref, selb_ref, part_ref, misc_ref, z_ref, o_ref,
                qa_ref, gate_ref, *flash):
    tq = SEL_Q
    gi = pl.program_id(1)
    qi = pl.program_id(2)
    q0 = pl.multiple_of(qi * tq, tq)
    q = q_ref[0]
    selb = selb_ref[0, 0]
    lane = lax.broadcasted_iota(jnp.int32, (tq, LANES), 1)
    for h in range(NSA_GROUP):
        r = slice(h * tq, (h + 1) * tq)
        slab = q[:, (h // 2) * LANES:(h // 2 + 1) * LANES]
        own = (lane < HEAD_DIM) if h % 2 == 0 else (lane >= HEAD_DIM)
        qa_ref[r, 0:LANES] = jnp.where(own, slab, jnp.zeros_like(slab))
        qa_ref[r, LANES:2 * LANES] = selb
    gates = _group_gates(misc_ref[0], gi)
    for h in range(NSA_GROUP):
        gate_ref[h] = _gate_lanes(gates, h, 1)

    n_full = q0 // SEL_KV

    def causal_mask(s, k0):
        t_row = q0 + (lax.broadcasted_iota(jnp.int32, s.shape, 0) & (tq - 1))
        pos = k0 + lax.broadcasted_iota(jnp.int32, s.shape, 1)
        return jnp.where(pos <= t_row, s, MASK_VALUE)

    _causal_flash(n_full, SEL_KV, qa_ref,
                  lambda k0: k_ref[0, 0, pl.ds(k0, SEL_KV), :],
                  lambda p, k0: jnp.dot(p, v_ref[0, 0, pl.ds(k0, SEL_KV), :],
                                        preferred_element_type=F32),
                  causal_mask, flash)
    acc_ref = flash[1]

    gated = []
    for h in range(NSA_GROUP):
        acc = acc_ref[h * tq:(h + 1) * tq, :]
        scaled = acc * (gate_ref[h] * (1.0 / acc[:, HEAD_DIM:HEAD_DIM + 1]))
        gated.append(scaled if h % 2 == 0 else pltpu.roll(scaled, HEAD_DIM, 1))
    for pp, slab in enumerate(_pair_slabs(gated)):
        c = slice(pp * LANES, (pp + 1) * LANES)
        o_ref[0, :, c] = ((part_ref[0, :, c] + slab) * z_ref[0, :, c].astype(F32)).astype(BF16)


def _sel(nq, k_aug, v_ones, selb, part, misc, nz):
    b, s, w = nq.shape
    g = NSA_KV_HEADS
    tq = SEL_Q
    gw = NSA_GROUP * HEAD_DIM
    n_blk = s // SLC_BLOCK
    rows = NSA_GROUP * tq
    q_spec = pl.BlockSpec((1, tq, gw), lambda bi, gi, i: (bi, i, gi))
    resident = s * 3 * LANES * 2
    return pl.pallas_call(
        _sel_kernel,
        grid=(b, g, s // tq),
        in_specs=[q_spec,
                  pl.BlockSpec((1, 1, s, 2 * LANES), lambda bi, gi, i: (bi, gi, 0, 0)),
                  pl.BlockSpec((1, 1, s, LANES), lambda bi, gi, i: (bi, gi, 0, 0)),
                  pl.BlockSpec((1, 1, tq, n_blk), lambda bi, gi, i: (bi, gi, i, 0)),
                  q_spec,
                  pl.BlockSpec((1, tq, LANES), lambda bi, gi, i: (bi, i, 0)),
                  q_spec],
        out_specs=q_spec,
        out_shape=jax.ShapeDtypeStruct((b, s, w), BF16),
        scratch_shapes=[pltpu.VMEM((rows, 2 * LANES), BF16),
                        pltpu.VMEM((NSA_GROUP, tq, LANES), F32)] + _flash_scratch(rows, SEL_KV, LANES),
        compiler_params=pltpu.CompilerParams(
            dimension_semantics=("arbitrary", "arbitrary", "arbitrary"),
            vmem_limit_bytes=_vmem_limit(resident + tq * gw * 10,
                                         _flash_scratch_bytes(rows, SEL_KV, LANES) + 4 * rows * SEL_KV * 4)),
        name="sel",
    )(nq, k_aug, v_ones, selb, part, misc, nz)


def _out_kernel(yf_ref, yn_ref, w_ref, g_ref, mod_ref, x_ref, o_ref):
    y = (jnp.dot(yf_ref[0], w_ref[0:FOX_WIDTH, :], preferred_element_type=F32)
         + jnp.dot(yn_ref[0], w_ref[FOX_WIDTH:, :], preferred_element_type=F32))
    yn = y * lax.rsqrt(jnp.mean(y * y, axis=-1, keepdims=True) + RMS_EPS)
    o_ref[0] = x_ref[0] + mod_ref[0, 2:3, :] * (yn * g_ref[...])


def _out(y_fox, y_nsa, w_out, g_post, mod, x):
    b, s, d = x.shape
    tm = PROJ_ROWS
    row = lambda bi, i: (bi, i, 0)
    half = pl.BlockSpec((1, tm, FOX_WIDTH), row)
    pipelined = 2 * tm * FOX_WIDTH * 2 + 2 * tm * d * 4 + d * d * 2
    return pl.pallas_call(
        _out_kernel,
        grid=(b, s // tm),
        in_specs=[half, half,
                  pl.BlockSpec((d, d), lambda bi, i: (0, 0)),
                  pl.BlockSpec((1, d), lambda bi, i: (0, 0)),
                  pl.BlockSpec((1, 3, d), lambda bi, i: (bi, 0, 0)),
                  pl.BlockSpec((1, tm, d), row)],
        out_specs=pl.BlockSpec((1, tm, d), row),
        out_shape=jax.ShapeDtypeStruct((b, s, d), F32),
        compiler_params=pltpu.CompilerParams(
            dimension_semantics=("arbitrary", "arbitrary"),
            vmem_limit_bytes=_vmem_limit(pipelined, 6 * tm * d * 4)),
        name="out",
    )(y_fox, y_nsa, w_out, g_post, mod, x)


def _rope_slabs(seq_len):
    inv = 1.0 / (ROPE_THETA ** (jnp.arange(0, HEAD_DIM, 2, dtype=F32) / HEAD_DIM))
    ang = jnp.arange(seq_len, dtype=F32)[:, None] * inv[None, :]
    cos, sin = jnp.cos(ang), jnp.sin(ang)
    reps = LANES // (HEAD_DIM // 2)
    sign = jnp.tile(jnp.concatenate([-jnp.ones((HEAD_DIM // 2,), F32), jnp.ones((HEAD_DIM // 2,), F32)]),
                    LANES // HEAD_DIM)
    return jnp.tile(cos, (1, reps)), jnp.tile(sin, (1, reps)) * sign[None, :]


def _reorder_w_in(w_in):
    fw, kv = FOX_WIDTH, NSA_KV_WIDTH
    o = 0
    cols = {}
    for name, n in (("fq", fw), ("fk", fw), ("fv", fw), ("ff", FOX_HEADS), ("fz", fw), ("nq", NSA_WIDTH),
                    ("kc", kv), ("vc", kv), ("ks", kv), ("vs", kv), ("kw", kv), ("vw", kv),
                    ("ng", 3 * NSA_HEADS), ("nz", NSA_WIDTH)):
        cols[name] = w_in[:, o:o + n]
        o += n
    pad = jnp.zeros((w_in.shape[0], LANES - FOX_HEADS - 3 * NSA_HEADS), w_in.dtype)
    order = ("fq", "fk", "fv", "fz", "nq", "kc", "vc", "ks", "vs", "kw", "vw", "nz", "ff", "ng")
    return jnp.concatenate([cols[k] for k in order] + [pad], axis=1).astype(BF16)


def _layer(x, c8, g_pre, g_post, w_ada, b_ada, w_in, b_forget, w_cmp_k1, w_cmp_k2,
           w_cmp_v1, w_cmp_v2, pe_cmp_k, pe_cmp_v, w_out, cos128, sin128):
    b, s, d = x.shape
    mod = _ada(c8, w_ada, b_ada)[:b].reshape(b, 3, d)
    bf128 = jnp.pad(b_forget, (0, LANES - FOX_HEADS)).reshape(1, LANES)
    (fq, fk, fv, fz, nq, cmp_in, ks_aug, vs_ones, kw2, vw_ones, nz, misc, log_f) = _proj(
        x, mod, g_pre.reshape(1, d), _reorder_w_in(w_in), cos128, sin128, bf128)

    cum_split = tuple(p.reshape(b, FOX_HEADS, s)
                      for p in _cumsum_lanes_split(log_f.reshape(b * FOX_HEADS, s)))
    y_fox = _fox(fq, fk, fv, cum_split, fz)

    flat = CMP_STRIDE * HEAD_DIM
    t = cmp_in.reshape(2, b, NSA_KV_HEADS, s // CMP_STRIDE, flat)
    pe = jnp.stack([pe_cmp_k.reshape(1, 2 * flat), pe_cmp_v.reshape(1, 2 * flat)])
    w2 = jnp.stack([w_cmp_k2, w_cmp_v2])
    cmp_hi, cmp_lo = _compress(t, jnp.stack([w_cmp_k1, w_cmp_v1]), jnp.concatenate([w2, w2], axis=-1),
                               jnp.broadcast_to(pe, (2, SUBLANES, 2 * flat)))
    part, selb = _cmpwin(nq, cmp_hi, cmp_lo, kw2, vw_ones, misc)
    y_nsa = _sel(nq, ks_aug, vs_ones, selb, part, misc, nz)

    return _out(y_fox, y_nsa, w_out.astype(BF16), g_post.reshape(1, d), mod, x)


def kernel(x, c, g_pre, g_post, w_ada, b_ada, w_in, b_forget, w_cmp_k1, w_cmp_k2,
           w_cmp_v1, w_cmp_v2, pe_cmp_k, pe_cmp_v, w_out):
    cos128, sin128 = _rope_slabs(x.shape[1])
    c8 = jnp.pad(c, ((0, SUBLANES - c.shape[0]), (0, 0)))
    for layer in range(g_pre.shape[0]):
        x = _layer(x, c8, g_pre[layer], g_post[layer], w_ada[layer], b_ada[layer], w_in[layer],
                   b_forget[layer], w_cmp_k1[layer], w_cmp_k2[layer], w_cmp_v1[layer],
                   w_cmp_v2[layer], pe_cmp_k[layer], pe_cmp_v[layer], w_out[layer], cos128, sin128)
    return x
```

```python
import functools

import jax
import jax.numpy as jnp
import numpy as np
from jax import lax
from jax.experimental import pallas as pl
from jax.experimental.pallas import tpu as pltpu

F32 = jnp.float32
BF16 = jnp.bfloat16

D_MODEL = 1024
HEAD_DIM = 64
FOX_WIDTH = 512
NSA_WIDTH = 512
FOX_HEADS = 8
NSA_HEADS = 8
NSA_KV_HEADS = 2
NSA_GROUP = 4
NSA_KV_WIDTH = 128
CMP_BLOCK = 32
CMP_STRIDE = 16
CMP_HIDDEN = 128
SLC_BLOCK = 64
N_SELECT = 16
WINDOW = 512
ROPE_THETA = 10000.0
RMS_EPS = 1e-6
LOG2E = 1.4426950408889634
QK_SCALE = HEAD_DIM ** -0.5 * LOG2E

LANES = 128
SUBLANES = 8
V7X_VMEM_BYTES = 64 * 1024 * 1024
MASK_VALUE = -1e30

PROJ_ROWS = 512
FOX_TILE = 512
NSA_Q = 256
SEL_Q = 128
SEL_KV = 512
WIN_KEYS = WINDOW + NSA_Q
FLASH_UNROLL = 4
CMP_QUARTERS = 4
FOX_AUG = 6

C_FQ, C_FK, C_FV, C_FZ, C_NQ = 0, 512, 1024, 1536, 2048
C_KC, C_VC, C_KS, C_VS, C_KW, C_VW = 2560, 2688, 2816, 2944, 3072, 3200
C_NZ, C_MISC, PROJ_COLS = 3328, 3840, 3968
MISC_GATE0 = FOX_HEADS


def _vmem_limit(pipelined_bytes, resident_bytes):
    need = 2 * pipelined_bytes + resident_bytes
    return int(min(max(need, 16 * 1024 * 1024), V7X_VMEM_BYTES - 8 * 1024 * 1024))


def _sigmoid(v):
    return 1.0 / (1.0 + jnp.exp(-v))


def _silu(v):
    return v * _sigmoid(v)


def _dot_nt(a, b):
    return lax.dot_general(a, b, (((1,), (1,)), ((), ())), preferred_element_type=F32)


def _split3(v):
    hi = v.astype(BF16)
    r1 = v - hi.astype(F32)
    mid = r1.astype(BF16)
    lo = (r1 - mid.astype(F32)).astype(BF16)
    return hi, mid, lo


def _ada_kernel(c_ref, w_ref, b_ref, o_ref):
    a = _silu(c_ref[...])
    o_ref[...] = jnp.dot(a, w_ref[...], precision=lax.Precision.HIGHEST,
                         preferred_element_type=F32) + b_ref[...]


def _ada(c8, w_ada, b_ada):
    n = w_ada.shape[1]
    blk = D_MODEL
    return pl.pallas_call(
        _ada_kernel,
        grid=(n // blk,),
        in_specs=[pl.BlockSpec((SUBLANES, D_MODEL), lambda j: (0, 0)),
                  pl.BlockSpec((D_MODEL, blk), lambda j: (0, j)),
                  pl.BlockSpec((1, blk), lambda j: (0, j))],
        out_specs=pl.BlockSpec((SUBLANES, blk), lambda j: (0, j)),
        out_shape=jax.ShapeDtypeStruct((SUBLANES, n), F32),
        compiler_params=pltpu.CompilerParams(
            dimension_semantics=("arbitrary",),
            vmem_limit_bytes=_vmem_limit(D_MODEL * blk * 4, 4 * 1024 * 1024)),
        name="ada",
    )(c8, w_ada, b_ada.reshape(1, n))


def _rope128(t, cos, sin_signed):
    lane = lax.broadcasted_iota(jnp.int32, t.shape, 1)
    first_half = (lane & (HEAD_DIM - 1)) < HEAD_DIM // 2
    partner = jnp.where(first_half,
                        pltpu.roll(t, LANES - HEAD_DIM // 2, 1),
                        pltpu.roll(t, HEAD_DIM // 2, 1))
    return t * cos + partner * sin_signed


def _proj_kernel(x_ref, mod_ref, g_ref, w_ref, cos_ref, sin_ref, bf_ref,
                 fq_ref, fk_ref, fv_ref, fz_ref, nq_ref, cmp_ref,
                 ks_ref, vs_ref, kw_ref, vw_ref, nz_ref, misc_ref, lf_ref):
    tm = x_ref.shape[1]
    x = x_ref[0]
    y = x * lax.rsqrt(jnp.mean(x * x, axis=-1, keepdims=True) + RMS_EPS)
    y = y * g_ref[...]
    h = (y * (1.0 + mod_ref[0, 1:2, :]) + mod_ref[0, 0:1, :]).astype(BF16)
    cos = cos_ref[...]
    sin = sin_ref[...]

    def mm(lo, n):
        return jnp.dot(h, w_ref[:, lo:lo + n], preferred_element_type=F32)

    def slab_pair(lo):
        both = mm(lo, 2 * LANES)
        return both[:, 0:LANES], both[:, LANES:2 * LANES]

    lane = lax.broadcasted_iota(jnp.int32, (tm, LANES), 1)
    low = lane < HEAD_DIM
    ones_col = jnp.where(lane == HEAD_DIM, 1.0, 0.0).astype(BF16)

    def doubled(slab):
        swapped = pltpu.roll(slab, HEAD_DIM, 1)
        return jnp.where(low, slab, swapped), jnp.where(low, swapped, slab)

    def with_ones(slab):
        swapped = pltpu.roll(slab, HEAD_DIM, 1)
        return jnp.where(low, slab, ones_col), jnp.where(low, swapped, ones_col)

    fq_ref[0] = (mm(C_FQ, FOX_WIDTH) * QK_SCALE).astype(BF16)
    fk_ref[0] = mm(C_FK, FOX_WIDTH).astype(BF16)
    fv = mm(C_FV, FOX_WIDTH).astype(BF16)
    for pair in range(FOX_WIDTH // LANES):
        fv_ref[0, 2 * pair], fv_ref[0, 2 * pair + 1] = with_ones(fv[:, pair * LANES:(pair + 1) * LANES])
    fz_ref[0] = _silu(mm(C_FZ, FOX_WIDTH)).astype(BF16)
    for j in range(NSA_WIDTH // (2 * LANES)):
        for i, t in enumerate(slab_pair(C_NQ + 2 * j * LANES)):
            c = (2 * j + i) * LANES
            nq_ref[0, :, c:c + LANES] = (_rope128(t, cos, sin) * QK_SCALE).astype(BF16)

    kc, vc = slab_pair(C_KC)
    for kind, slab in enumerate((_rope128(kc, cos, sin), vc)):
        for g in range(NSA_KV_HEADS):
            cmp_ref[kind, 0, g] = slab[:, g * HEAD_DIM:(g + 1) * HEAD_DIM]
    pos = pl.program_id(1) * tm + lax.broadcasted_iota(jnp.int32, (tm, LANES), 0)
    onehot = jnp.where(pos // SLC_BLOCK == lane, 1.0, 0.0).astype(BF16)
    ks, vs = slab_pair(C_KS)
    for g, k2 in enumerate(doubled(_rope128(ks, cos, sin).astype(BF16))):
        ks_ref[0, g, :, 0:LANES] = k2
        ks_ref[0, g, :, LANES:2 * LANES] = onehot
    vs_ref[0, 0], vs_ref[0, 1] = with_ones(vs.astype(BF16))
    kw, vw = slab_pair(C_KW)
    kw_ref[0, 0], kw_ref[0, 1] = doubled(_rope128(kw, cos, sin).astype(BF16))
    vw_ref[0, 0], vw_ref[0, 1] = doubled(vw.astype(BF16))
    nz_ref[0] = _silu(mm(C_NZ, NSA_WIDTH)).astype(BF16)
    misc = mm(C_MISC, LANES)
    misc_ref[0] = misc
    z = misc + bf_ref[...]
    log_f = jnp.minimum(z, 0.0) - jnp.log1p(jnp.exp(-jnp.abs(z)))
    lf_ref[0] = log_f.T[0:FOX_HEADS, :]


def _proj(x, mod, g_pre, w_cat, cos128, sin128, bf128):
    b, s, d = x.shape
    tm = PROJ_ROWS
    row = lambda bi, i: (bi, i, 0)
    wide = lambda dt: jax.ShapeDtypeStruct((b, s, FOX_WIDTH), dt)
    g = NSA_KV_HEADS
    grouped = lambda lanes: jax.ShapeDtypeStruct((b, g, s, lanes), BF16)
    grouped_spec = lambda lanes: pl.BlockSpec((1, g, tm, lanes), lambda bi, i: (bi, 0, i, 0))
    out_shape = (wide(BF16), wide(BF16), jax.ShapeDtypeStruct((b, FOX_HEADS, s, LANES), BF16),
                 wide(BF16), wide(BF16),
                 jax.ShapeDtypeStruct((2, b, g, s, HEAD_DIM), F32),
                 grouped(2 * LANES), grouped(LANES), grouped(LANES), grouped(LANES),
                 wide(BF16), jax.ShapeDtypeStruct((b, s, LANES), F32),
                 jax.ShapeDtypeStruct((b, FOX_HEADS, s), F32))
    wide_spec = pl.BlockSpec((1, tm, FOX_WIDTH), row)
    out_specs = (wide_spec, wide_spec,
                 pl.BlockSpec((1, FOX_HEADS, tm, LANES), lambda bi, i: (bi, 0, i, 0)),
                 wide_spec, wide_spec) + (
        pl.BlockSpec((2, 1, g, tm, HEAD_DIM), lambda bi, i: (0, bi, 0, i, 0)),
        grouped_spec(2 * LANES), grouped_spec(LANES), grouped_spec(LANES), grouped_spec(LANES),
        wide_spec, pl.BlockSpec((1, tm, LANES), row),
        pl.BlockSpec((1, FOX_HEADS, tm), lambda bi, i: (bi, 0, i)))
    pipelined = (tm * d * 4 + tm * (7 * FOX_WIDTH * 2 + 2 * g * LANES * 4 + 5 * g * LANES * 2 + LANES * 4)
                 + d * PROJ_COLS * 2)
    return pl.pallas_call(
        _proj_kernel,
        grid=(b, s // tm),
        in_specs=[pl.BlockSpec((1, tm, d), row),
                  pl.BlockSpec((1, 3, d), lambda bi, i: (bi, 0, 0)),
                  pl.BlockSpec((1, d), lambda bi, i: (0, 0)),
                  pl.BlockSpec((d, PROJ_COLS), lambda bi, i: (0, 0)),
                  pl.BlockSpec((tm, LANES), lambda bi, i: (i, 0)),
                  pl.BlockSpec((tm, LANES), lambda bi, i: (i, 0)),
                  pl.BlockSpec((1, LANES), lambda bi, i: (0, 0))],
        out_specs=out_specs,
        out_shape=out_shape,
        compiler_params=pltpu.CompilerParams(
            dimension_semantics=("arbitrary", "arbitrary"),
            vmem_limit_bytes=_vmem_limit(pipelined, 8 * 1024 * 1024)),
        name="proj",
    )(x, mod, g_pre, w_cat, cos128, sin128, bf128)


def _cumsum_kernel(x_ref, hi_ref, mid_ref, lo_ref, *, chunks):
    x = x_ref[...]
    n = x.shape[0]
    parts = _split3(x)
    r = lax.broadcasted_iota(jnp.int32, (LANES, LANES), 0)
    c = lax.broadcasted_iota(jnp.int32, (LANES, LANES), 1)
    tri = (r <= c).astype(BF16)
    rr = lax.broadcasted_iota(jnp.int32, (n, n), 0)
    cc = lax.broadcasted_iota(jnp.int32, (n, n), 1)
    earlier = jnp.logical_and(cc < rr, (cc // chunks) == (rr // chunks)).astype(BF16)
    within = sum(jnp.dot(p, tri, preferred_element_type=F32) for p in parts)
    before = sum(jnp.dot(earlier, p, preferred_element_type=F32) for p in parts)
    total = (within + jnp.sum(before, axis=-1, keepdims=True)) * LOG2E
    hi_ref[...], mid_ref[...], lo_ref[...] = _split3(total)


def _cumsum_lanes_split(v):
    rows, s = v.shape
    chunks = s // LANES
    n = rows * chunks
    part = jax.ShapeDtypeStruct((n, LANES), BF16)
    parts = pl.pallas_call(
        functools.partial(_cumsum_kernel, chunks=chunks),
        out_shape=(part, part, part),
        compiler_params=pltpu.CompilerParams(
            vmem_limit_bytes=_vmem_limit(2 * n * LANES * 4, 6 * n * n)),
        name="cumsum",
    )(v.reshape(n, LANES))
    return tuple(p.reshape(rows, s) for p in parts)


def _flash_scratch(rows, tile, acc_lanes):
    return ([pltpu.VMEM((rows, LANES), F32), pltpu.VMEM((rows, acc_lanes), F32)]
            + [pltpu.VMEM((rows, tile), F32)] * 2
            + [pltpu.VMEM((rows, tile), BF16)] * 2
            + [pltpu.VMEM((rows, LANES), F32)] * 2)


def _flash_scratch_bytes(rows, tile, acc_lanes):
    return rows * (LANES * 4 + acc_lanes * 4 + 2 * tile * 4 + 2 * tile * 2 + 2 * LANES * 4)


def _causal_flash(n_full, tile, q_ref, load_k, pv, causal_mask, epilogue, scratch):
    m_ref, acc_ref, s0, s1, p0, p1, a0, a1 = scratch
    s_bufs, p_bufs, a_bufs = (s0, s1), (p0, p1), (a0, a1)

    def logits(j):
        return _dot_nt(q_ref[...], load_k(pl.multiple_of(j * tile, tile)))

    def softmax(s):
        m_prev = m_ref[...]
        m_next = jnp.maximum(m_prev, jnp.max(s, axis=1, keepdims=True))
        m_ref[...] = m_next
        p = jnp.exp2(s - jnp.tile(m_next, (1, tile // LANES)))
        return p.astype(BF16), jnp.exp2(m_prev - m_next)

    def accumulate(p, alpha, j):
        acc_ref[...] = acc_ref[...] * alpha + pv(p, pl.multiple_of(j * tile, tile))

    def stage(j, cur):
        s_bufs[1 - cur][...] = logits(j + 1)
        p_bufs[cur][...], a_bufs[cur][...] = softmax(s_bufs[cur][...])
        accumulate(p_bufs[1 - cur][...], a_bufs[1 - cur][...], j - 1)

    def run_stages(first, count):
        for i in range(count):
            stage(first + i, (1 + i) % 2)

    def finish(cur):
        p, alpha = softmax(causal_mask(s_bufs[cur][...], n_full * tile))
        accumulate(p_bufs[1 - cur][...], a_bufs[1 - cur][...], n_full - 1)
        accumulate(p, alpha, n_full)

    m_ref[...] = jnp.full(m_ref.shape, MASK_VALUE, F32)
    acc_ref[...] = jnp.zeros(acc_ref.shape, F32)
    s_first = logits(0)
    s_bufs[1][...] = logits(jnp.minimum(n_full, 1))
    p_bufs[0][...], a_bufs[0][...] = softmax(causal_mask(s_first, 0))

    @pl.when(n_full == 0)
    def _():
        accumulate(p_bufs[0][...], a_bufs[0][...], 0)
        epilogue()

    @pl.when(n_full > 0)
    def _():
        later = n_full - 1

        def unrolled(jj, carry):
            run_stages(1 + FLASH_UNROLL * jj, FLASH_UNROLL)
            return carry

        lax.fori_loop(0, later // FLASH_UNROLL, unrolled, 0)
        rest = later % FLASH_UNROLL

        def tail(count):
            run_stages(n_full - count, count)
            finish((1 + count) % 2)
            epilogue()

        for count in range(FLASH_UNROLL):
            pl.when(rest == count)(functools.partial(tail, count))


def _fox_aug(cum_split):
    b, h, s = cum_split[0].shape
    terms = jnp.concatenate(list(cum_split) + [jnp.ones((b, 1, s), BF16)], axis=1)
    terms = jnp.transpose(terms, (0, 2, 1))
    place_q = np.zeros((3 * h + 1, h * HEAD_DIM), np.float32)
    place_k = np.zeros((3 * h + 1, h * HEAD_DIM), np.float32)
    half = FOX_AUG // 2
    for head in range(h):
        for part in range(half):
            place_q[part * h + head, head * HEAD_DIM + part] = 1.0
            place_q[3 * h, head * HEAD_DIM + half + part] = 1.0
            place_k[3 * h, head * HEAD_DIM + part] = 1.0
            place_k[part * h + head, head * HEAD_DIM + half + part] = -1.0
    scatter = lambda place: jnp.einsum("bsr,rl->bsl", terms, jnp.asarray(place, BF16),
                                       preferred_element_type=F32).astype(BF16)
    return scatter(place_q), scatter(place_k)


def _fox_kernel(q_ref, qa_ref, k_ref, ka_ref, v_ref, z_ref, o_ref, qm_ref, *flash):
    t = FOX_TILE
    qi = pl.program_id(2)
    q = q_ref[0]
    qa = qa_ref[0]
    lane = lax.broadcasted_iota(jnp.int32, q.shape, 1)
    for hh in range(2):
        r = slice(hh * t, (hh + 1) * t)
        own = (lane < HEAD_DIM) if hh == 0 else (lane >= HEAD_DIM)
        qm_ref[r, 0:LANES] = jnp.where(own, q, jnp.zeros_like(q))
        qm_ref[r, LANES:2 * LANES] = jnp.where(own, qa, jnp.zeros_like(qa))

    def load_k(k0):
        return jnp.concatenate([k_ref[0, pl.ds(k0, t), :], ka_ref[0, pl.ds(k0, t), :]], axis=1)

    def pv(p, k0):
        return jnp.concatenate(
            [jnp.dot(p[hh * t:(hh + 1) * t], v_ref[0, hh, pl.ds(k0, t), :], preferred_element_type=F32)
             for hh in range(2)], axis=0)

    def causal_mask(s, k0):
        t_row = qi * t + (lax.broadcasted_iota(jnp.int32, s.shape, 0) & (t - 1))
        pos = k0 + lax.broadcasted_iota(jnp.int32, s.shape, 1)
        return jnp.where(pos <= t_row, s, MASK_VALUE)

    def epilogue():
        acc_ref = flash[1]
        o0 = acc_ref[0:t, :] * (1.0 / acc_ref[0:t, HEAD_DIM:HEAD_DIM + 1])
        o1 = acc_ref[t:2 * t, :] * (1.0 / acc_ref[t:2 * t, HEAD_DIM:HEAD_DIM + 1])
        o = jnp.where(lane < HEAD_DIM, o0, pltpu.roll(o1, HEAD_DIM, 1))
        o_ref[0] = (o * z_ref[0].astype(F32)).astype(BF16)

    _causal_flash(qi, t, qm_ref, load_k, pv, causal_mask, epilogue, flash)


def _fox(fq, fk, fv, cum_split, fz):
    b, s, w = fq.shape
    t = FOX_TILE
    pairs = w // LANES
    qa, ka = _fox_aug(cum_split)
    tile = pl.BlockSpec((1, t, LANES), lambda bi, hp, i: (bi, i, hp))
    full = pl.BlockSpec((1, s, LANES), lambda bi, hp, i: (bi, 0, hp))
    resident = 4 * s * LANES * 2
    scratch = 2 * t * 2 * LANES * 2 + _flash_scratch_bytes(2 * t, t, LANES)
    return pl.pallas_call(
        _fox_kernel,
        grid=(b, pairs, s // t),
        in_specs=[tile, tile, full, full,
                  pl.BlockSpec((1, 2, s, LANES), lambda bi, hp, i: (bi, hp, 0, 0)),
                  tile],
        out_specs=tile,
        out_shape=jax.ShapeDtypeStruct((b, s, w), BF16),
        scratch_shapes=[pltpu.VMEM((2 * t, 2 * LANES), BF16)] + _flash_scratch(2 * t, t, LANES),
        compiler_params=pltpu.CompilerParams(
            dimension_semantics=("arbitrary", "arbitrary", "arbitrary"),
            vmem_limit_bytes=_vmem_limit(resident + 4 * t * LANES * 2, scratch + 4 * 2 * t * t * 4)),
        name="fox",
    )(fq, qa, fk, ka, fv, fz)


def _compress_kernel(t_ref, w1_ref, w2_ref, pe_ref, hi_ref, lo_ref):
    hp = lax.Precision.HIGHEST
    half = CMP_STRIDE * HEAD_DIM
    tt = t_ref[0, 0, 0]
    n = tt.shape[0]
    first = jnp.dot(tt, w1_ref[0, 0:half, :], precision=hp, preferred_element_type=F32)
    second = jnp.dot(tt, w1_ref[0, half:2 * half, :], precision=hp, preferred_element_type=F32)
    pe_term = jnp.dot(pe_ref[0], w1_ref[0], precision=hp, preferred_element_type=F32)[0:1, :]
    hidden = first + pltpu.roll(second, n - 1, 0) + pe_term
    out = jnp.dot(_silu(hidden), w2_ref[0], precision=hp, preferred_element_type=F32)
    hi = out.astype(BF16)
    hi_ref[0, 0, 0] = hi
    lo_ref[0, 0, 0] = (out - hi.astype(F32)).astype(BF16)


def _compress(t, w1, w2, pe):
    kinds, b, g, n, flat = t.shape
    width = w2.shape[-1]
    out_spec = pl.BlockSpec((1, 1, 1, n, width), lambda a, bi, gi: (a, bi, gi, 0, 0))
    out_part = jax.ShapeDtypeStruct((kinds, b, g, n, width), BF16)
    return pl.pallas_call(
        _compress_kernel,
        grid=(kinds, b, g),
        in_specs=[pl.BlockSpec((1, 1, 1, n, flat), lambda a, bi, gi: (a, bi, gi, 0, 0)),
                  pl.BlockSpec((1, 2 * flat, CMP_HIDDEN), lambda a, bi, gi: (a, 0, 0)),
                  pl.BlockSpec((1, CMP_HIDDEN, width), lambda a, bi, gi: (a, 0, 0)),
                  pl.BlockSpec((1, SUBLANES, 2 * flat), lambda a, bi, gi: (a, 0, 0))],
        out_specs=(out_spec, out_spec),
        out_shape=(out_part, out_part),
        compiler_params=pltpu.CompilerParams(
            dimension_semantics=("arbitrary", "arbitrary", "arbitrary"),
            vmem_limit_bytes=_vmem_limit(n * flat * 4 + 2 * flat * CMP_HIDDEN * 4, 8 * 1024 * 1024)),
        name="compress",
    )(t, w1, w2, pe)


def _softmax_numerator(s, bias):
    s = s + bias
    m = jnp.max(s, axis=1, keepdims=True)
    m = jnp.where(m == -jnp.inf, 0.0, m)
    return jnp.exp2(s - m)


def _stack_heads(q):
    tq = q.shape[0]
    lane = lax.broadcasted_iota(jnp.int32, (tq, LANES), 1)
    rows = []
    for h in range(NSA_GROUP):
        slab = q[:, (h // 2) * LANES:(h // 2 + 1) * LANES]
        own = (lane < HEAD_DIM) if h % 2 == 0 else (lane >= HEAD_DIM)
        rows.append(jnp.where(own, slab, jnp.zeros_like(slab)))
    return jnp.concatenate(rows, axis=0)


def _group_gates(misc, gi):
    gates = _sigmoid(misc)
    return jnp.where(gi == 0, gates, pltpu.roll(gates, LANES - 3 * NSA_GROUP, 1))


def _gate_lanes(gates, h, j):
    c = MISC_GATE0 + 3 * h + j
    return jnp.broadcast_to(gates[:, c:c + 1], gates.shape)


def _pair_slabs(per_head):
    lane = lax.broadcasted_iota(jnp.int32, per_head[0].shape, 1)
    return [jnp.where(lane < HEAD_DIM, per_head[2 * pp], per_head[2 * pp + 1])
            for pp in range(NSA_GROUP // 2)]


def _cmpwin_kernel(q_ref, kc_hi_ref, kc_lo_ref, vc_ref, overlap_ref, kw_ref, vw_ref, misc_ref,
                   part_ref, selb_ref):
    gi = pl.program_id(1)
    q0 = pl.multiple_of(pl.program_id(2) * NSA_Q, NSA_Q)
    quarter = kc_hi_ref.shape[3] // CMP_QUARTERS
    reach = q0 // (quarter * CMP_STRIDE)
    refs = (q_ref, kc_hi_ref, kc_lo_ref, vc_ref, overlap_ref, kw_ref, vw_ref, misc_ref,
            part_ref, selb_ref)
    for quarters in range(1, CMP_QUARTERS + 1):
        pl.when(reach == quarters - 1)(
            functools.partial(_cmpwin_body, refs, gi, q0, quarters * quarter))


def _cmpwin_body(refs, gi, q0, n_cmp):
    (q_ref, kc_hi_ref, kc_lo_ref, vc_ref, overlap_ref, kw_ref, vw_ref, misc_ref,
     part_ref, selb_ref) = refs
    tq = NSA_Q
    q4 = _stack_heads(q_ref[0])
    n_blk = n_cmp * CMP_STRIDE // SLC_BLOCK
    head_rows = [slice(h * tq, (h + 1) * tq) for h in range(NSA_GROUP)]

    s = (_dot_nt(q4, kc_hi_ref[0, 0, 0, 0:n_cmp, :])
         + _dot_nt(q4, kc_lo_ref[0, 0, 0, 0:n_cmp, :]))
    col = lax.broadcasted_iota(jnp.int32, (tq, n_cmp), 1)
    t_row = q0 + lax.broadcasted_iota(jnp.int32, (tq, n_cmp), 0)
    cmp_bias = jnp.where(col * CMP_STRIDE + (CMP_BLOCK - 1) <= t_row, 0.0, -jnp.inf)
    pcs = []
    for r in head_rows:
        p = _softmax_numerator(s[r], cmp_bias)
        pcs.append(p * (1.0 / jnp.maximum(jnp.sum(p, axis=1, keepdims=True), 1e-30)))
    oc = jnp.dot(jnp.concatenate(pcs, axis=0).astype(BF16), vc_ref[0, 0, 0, 0:n_cmp, :],
                 preferred_element_type=F32)

    pc_sum = pcs[0] + pcs[1] + pcs[2] + pcs[3]
    overlap = overlap_ref[0:n_blk, 0:n_cmp]
    imp = sum(_dot_nt(overlap, p) for p in _split3(pc_sum))
    blk = lax.broadcasted_iota(jnp.int32, (n_blk, tq), 0)
    cur = (q0 + lax.broadcasted_iota(jnp.int32, (n_blk, tq), 1)) // SLC_BLOCK
    forced = jnp.logical_or(blk == 0, jnp.logical_or(blk == cur, blk == cur - 1))
    imp = jnp.where(forced, jnp.inf, jnp.where(blk > cur, -jnp.inf, imp))

    def pick_one(_, carry):
        rem, sel = carry
        best = jnp.max(rem, axis=0, keepdims=True)
        first = jnp.min(jnp.where(rem == best, blk, n_blk), axis=0, keepdims=True)
        hit = blk == first
        return jnp.where(hit, -jnp.inf, rem), jnp.where(hit, 1.0, sel)

    _, sel = lax.fori_loop(0, min(N_SELECT, n_blk), pick_one, (imp, jnp.zeros_like(imp)),
                           unroll=True)
    all_blk = selb_ref.shape[3]
    if n_blk < all_blk:
        sel = jnp.concatenate([sel, jnp.zeros((all_blk - n_blk, tq), F32)], axis=0)
    selb_ref[0, 0] =jnp.where(sel.T > 0.5, 0.0, MASK_VALUE).astype(BF16)

    start = pl.multiple_of(jnp.maximum(q0 - WINDOW, 0), tq)
    kw = kw_ref[0, 0, pl.ds(start, WIN_KEYS), :]
    vw = vw_ref[0, 0, pl.ds(start, WIN_KEYS), :]
    sw = _dot_nt(q4, kw)
    pos = start + lax.broadcasted_iota(jnp.int32, (tq, WIN_KEYS), 1)
    dist = q0 + lax.broadcasted_iota(jnp.int32, (tq, WIN_KEYS), 0) - pos
    win_bias = jnp.where(jnp.logical_and(dist >= 0, dist < WINDOW), 0.0, -jnp.inf)
    pw = jnp.concatenate([_softmax_numerator(sw[r], win_bias) for r in head_rows],
                         axis=0).astype(BF16)
    ow = jnp.dot(pw, vw, preferred_element_type=F32)
    lw = jnp.dot(pw, jnp.ones((WIN_KEYS, LANES), BF16), preferred_element_type=F32)
    ow = ow * (1.0 / jnp.maximum(lw, 1e-30))

    gates = _group_gates(misc_ref[0], gi)
    gated = [_gate_lanes(gates, h, 0) * oc[r] + _gate_lanes(gates, h, 2) * ow[r]
             for h, r in enumerate(head_rows)]
    for pp, slab in enumerate(_pair_slabs(gated)):
        part_ref[0, :, pp * LANES:(pp + 1) * LANES] = slab


def _block_overlap(n_blk, n_cmp):
    ratio = SLC_BLOCK // CMP_STRIDE
    lo = np.arange(n_blk)[:, None] * ratio - (CMP_BLOCK // CMP_STRIDE - 1)
    i = np.arange(n_cmp)[None, :]
    n_overlap = (SLC_BLOCK + CMP_BLOCK) // CMP_STRIDE - 1
    return ((i >= lo) & (i < lo + n_overlap) & (i < n_cmp - 1)).astype(np.float32)


def _cmpwin(nq, cmp_hi, cmp_lo, kw, vw_ones, misc):
    b, s, w = nq.shape
    g = NSA_KV_HEADS
    tq = NSA_Q
    gw = NSA_GROUP * HEAD_DIM
    n_cmp = cmp_hi.shape[3]
    n_blk = s // SLC_BLOCK
    q_spec = pl.BlockSpec((1, tq, gw), lambda bi, gi, i: (bi, i, gi))
    key_spec = pl.BlockSpec((1, 1, 1, n_cmp, LANES), lambda bi, gi, i: (0, bi, gi, 0, 0))
    val_spec = pl.BlockSpec((1, 1, 1, n_cmp, LANES), lambda bi, gi, i: (1, bi, gi, 0, 0))
    seq_spec = pl.BlockSpec((1, 1, s, LANES), lambda bi, gi, i: (bi, gi, 0, 0))
    resident = 2 * s * LANES * 2 + 3 * n_cmp * LANES * 2 + n_blk * n_cmp * 2
    rows = NSA_GROUP * tq
    return pl.pallas_call(
        _cmpwin_kernel,
        grid=(b, g, s // tq),
        in_specs=[q_spec, key_spec, key_spec, val_spec,
                  pl.BlockSpec((n_blk, n_cmp), lambda bi, gi, i: (0, 0)),
                  seq_spec, seq_spec,
                  pl.BlockSpec((1, tq, LANES), lambda bi, gi, i: (bi, i, 0))],
        out_specs=(q_spec, pl.BlockSpec((1, 1, tq, n_blk), lambda bi, gi, i: (bi, gi, i, 0))),
        out_shape=(jax.ShapeDtypeStruct((b, s, w), F32),
                   jax.ShapeDtypeStruct((b, g, s, n_blk), BF16)),
        compiler_params=pltpu.CompilerParams(
            dimension_semantics=("arbitrary", "arbitrary", "arbitrary"),
            vmem_limit_bytes=_vmem_limit(resident + tq * gw * 8, 12 * rows * WIN_KEYS * 4)),
        name="cmpwin",
    )(nq, cmp_hi, cmp_lo, cmp_hi, jnp.asarray(_block_overlap(n_blk, n_cmp), BF16), kw, vw_ones, misc)


def _sel_kernel(q_ref, k_ref, v_ref, selb_ref, part_ref, misc_ref, z_ref, o_ref,
                qa_ref, gate_ref, *flash):
    tq = SEL_Q
    gi = pl.program_id(1)
    qi = pl.program_id(2)
    q0 = pl.multiple_of(qi * tq, tq)
    q = q_ref[0]
    selb = selb_ref[0, 0]
    lane = lax.broadcasted_iota(jnp.int32, (tq, LANES), 1)
    for h in range(NSA_GROUP):
        r = slice(h * tq, (h + 1) * tq)
        slab = q[:, (h // 2) * LANES:(h // 2 + 1) * LANES]
        own = (lane < HEAD_DIM) if h % 2 == 0 else (lane >= HEAD_DIM)
        qa_ref[r, 0:LANES] = jnp.where(own, slab, jnp.zeros_like(slab))
        qa_ref[r, LANES:2 * LANES] = selb
    gates = _group_gates(misc_ref[0], gi)
    for h in range(NSA_GROUP):
        gate_ref[h] = _gate_lanes(gates, h, 1)

    n_full = q0 // SEL_KV

    def causal_mask(s, k0):
        t_row = q0 + (lax.broadcasted_iota(jnp.int32, s.shape, 0) & (tq - 1))
        pos = k0 + lax.broadcasted_iota(jnp.int32, s.shape, 1)
        return jnp.where(pos <= t_row, s, MASK_VALUE)

    def epilogue():
        acc_ref = flash[1]
        gated = []
        for h in range(NSA_GROUP):
            acc = acc_ref[h * tq:(h + 1) * tq, :]
            scaled = acc * (gate_ref[h] * (1.0 / acc[:, HEAD_DIM:HEAD_DIM + 1]))
            gated.append(scaled if h % 2 == 0 else pltpu.roll(scaled, HEAD_DIM, 1))
        for pp, slab in enumerate(_pair_slabs(gated)):
            c = slice(pp * LANES, (pp + 1) * LANES)
            o_ref[0, :, c] = ((part_ref[0, :, c] + slab) * z_ref[0, :, c].astype(F32)).astype(BF16)

    _causal_flash(n_full, SEL_KV, qa_ref,
                  lambda k0: k_ref[0, 0, pl.ds(k0, SEL_KV), :],
                  lambda p, k0: jnp.dot(p, v_ref[0, 0, pl.ds(k0, SEL_KV), :],
                                        preferred_element_type=F32),
                  causal_mask, epilogue, flash)


def _sel(nq, k_aug, v_ones, selb, part, misc, nz):
    b, s, w = nq.shape
    g = NSA_KV_HEADS
    tq = SEL_Q
    gw = NSA_GROUP * HEAD_DIM
    n_blk = s // SLC_BLOCK
    rows = NSA_GROUP * tq
    q_spec = pl.BlockSpec((1, tq, gw), lambda bi, gi, i: (bi, i, gi))
    resident = s * 3 * LANES * 2
    return pl.pallas_call(
        _sel_kernel,
        grid=(b, g, s // tq),
        in_specs=[q_spec,
                  pl.BlockSpec((1, 1, s, 2 * LANES), lambda bi, gi, i: (bi, gi, 0, 0)),
                  pl.BlockSpec((1, 1, s, LANES), lambda bi, gi, i: (bi, gi, 0, 0)),
                  pl.BlockSpec((1, 1, tq, n_blk), lambda bi, gi, i: (bi, gi, i, 0)),
                  q_spec,
                  pl.BlockSpec((1, tq, LANES), lambda bi, gi, i: (bi, i, 0)),
                  q_spec],
        out_specs=q_spec,
        out_shape=jax.ShapeDtypeStruct((b, s, w), BF16),
        scratch_shapes=[pltpu.VMEM((rows, 2 * LANES), BF16),
                        pltpu.VMEM((NSA_GROUP, tq, LANES), F32)] + _flash_scratch(rows, SEL_KV, LANES),
        compiler_params=pltpu.CompilerParams(
            dimension_semantics=("arbitrary", "arbitrary", "arbitrary"),
            vmem_limit_bytes=_vmem_limit(resident + tq * gw * 10,
                                         _flash_scratch_bytes(rows, SEL_KV, LANES) + 4 * rows * SEL_KV * 4)),
        name="sel",
    )(nq, k_aug, v_ones, selb, part, misc, nz)


def _out_kernel(yf_ref, yn_ref, w_ref, g_ref, mod_ref, x_ref, o_ref):
    y = (jnp.dot(yf_ref[0], w_ref[0:FOX_WIDTH, :], preferred_element_type=F32)
         + jnp.dot(yn_ref[0], w_ref[FOX_WIDTH:, :], preferred_element_type=F32))
    yn = y * lax.rsqrt(jnp.mean(y * y, axis=-1, keepdims=True) + RMS_EPS)
    o_ref[0] = x_ref[0] + mod_ref[0, 2:3, :] * (yn * g_ref[...])


def _out(y_fox, y_nsa, w_out, g_post, mod, x):
    b, s, d = x.shape
    tm = PROJ_ROWS
    row = lambda bi, i: (bi, i, 0)
    half = pl.BlockSpec((1, tm, FOX_WIDTH), row)
    pipelined = 2 * tm * FOX_WIDTH * 2 + 2 * tm * d * 4 + d * d * 2
    return pl.pallas_call(
        _out_kernel,
        grid=(b, s // tm),
        in_specs=[half, half,
                  pl.BlockSpec((d, d), lambda bi, i: (0, 0)),
                  pl.BlockSpec((1, d), lambda bi, i: (0, 0)),
                  pl.BlockSpec((1, 3, d), lambda bi, i: (bi, 0, 0)),
                  pl.BlockSpec((1, tm, d), row)],
        out_specs=pl.BlockSpec((1, tm, d), row),
        out_shape=jax.ShapeDtypeStruct((b, s, d), F32),
        compiler_params=pltpu.CompilerParams(
            dimension_semantics=("arbitrary", "arbitrary"),
            vmem_limit_bytes=_vmem_limit(pipelined, 6 * tm * d * 4)),
        name="out",
    )(y_fox, y_nsa, w_out, g_post, mod, x)


def _rope_slabs(seq_len):
    inv = 1.0 / (ROPE_THETA ** (jnp.arange(0, HEAD_DIM, 2, dtype=F32) / HEAD_DIM))
    ang = jnp.arange(seq_len, dtype=F32)[:, None] * inv[None, :]
    cos, sin = jnp.cos(ang), jnp.sin(ang)
    reps = LANES // (HEAD_DIM // 2)
    sign = jnp.tile(jnp.concatenate([-jnp.ones((HEAD_DIM // 2,), F32), jnp.ones((HEAD_DIM // 2,), F32)]),
                    LANES // HEAD_DIM)
    return jnp.tile(cos, (1, reps)), jnp.tile(sin, (1, reps)) * sign[None, :]


def _reorder_w_in(w_in):
    fw, kv = FOX_WIDTH, NSA_KV_WIDTH
    o = 0
    cols = {}
    for name, n in (("fq", fw), ("fk", fw), ("fv", fw), ("ff", FOX_HEADS), ("fz", fw), ("nq", NSA_WIDTH),
                    ("kc", kv), ("vc", kv), ("ks", kv), ("vs", kv), ("kw", kv), ("vw", kv),
                    ("ng", 3 * NSA_HEADS), ("nz", NSA_WIDTH)):
        cols[name] = w_in[:, o:o + n]
        o += n
    pad = jnp.zeros((w_in.shape[0], LANES - FOX_HEADS - 3 * NSA_HEADS), w_in.dtype)
    order = ("fq", "fk", "fv", "fz", "nq", "kc", "vc", "ks", "vs", "kw", "vw", "nz", "ff", "ng")
    return jnp.concatenate([cols[k] for k in order] + [pad], axis=1).astype(BF16)


def _layer(x, c8, g_pre, g_post, w_ada, b_ada, w_in, b_forget, w_cmp_k1, w_cmp_k2,
           w_cmp_v1, w_cmp_v2, pe_cmp_k, pe_cmp_v, w_out, cos128, sin128):
    b, s, d = x.shape
    mod = _ada(c8, w_ada, b_ada)[:b].reshape(b, 3, d)
    bf128 = jnp.pad(b_forget, (0, LANES - FOX_HEADS)).reshape(1, LANES)
    (fq, fk, fv, fz, nq, cmp_in, ks_aug, vs_ones, kw2, vw_ones, nz, misc, log_f) = _proj(
        x, mod, g_pre.reshape(1, d), _reorder_w_in(w_in), cos128, sin128, bf128)

    cum_split = tuple(p.reshape(b, FOX_HEADS, s)
                      for p in _cumsum_lanes_split(log_f.reshape(b * FOX_HEADS, s)))
    y_fox = _fox(fq, fk, fv, cum_split, fz)

    flat = CMP_STRIDE * HEAD_DIM
    t = cmp_in.reshape(2, b, NSA_KV_HEADS, s // CMP_STRIDE, flat)
    pe = jnp.stack([pe_cmp_k.reshape(1, 2 * flat), pe_cmp_v.reshape(1, 2 * flat)])
    w2 = jnp.stack([w_cmp_k2, w_cmp_v2])
    cmp_hi, cmp_lo = _compress(t, jnp.stack([w_cmp_k1, w_cmp_v1]), jnp.concatenate([w2, w2], axis=-1),
                               jnp.broadcast_to(pe, (2, SUBLANES, 2 * flat)))
    part, selb = _cmpwin(nq, cmp_hi, cmp_lo, kw2, vw_ones, misc)
    y_nsa = _sel(nq, ks_aug, vs_ones, selb, part, misc, nz)

    return _out(y_fox, y_nsa, w_out.astype(BF16), g_post.reshape(1, d), mod, x)


def kernel(x, c, g_pre, g_post, w_ada, b_ada, w_in, b_forget, w_cmp_k1, w_cmp_k2,
           w_cmp_v1, w_cmp_v2, pe_cmp_k, pe_cmp_v, w_out):
    cos128, sin128 = _rope_slabs(x.shape[1])
    c8 = jnp.pad(c, ((0, SUBLANES - c.shape[0]), (0, 0)))
    for layer in range(g_pre.shape[0]):
        x = _layer(x, c8, g_pre[layer], g_post[layer], w_ada[layer], b_ada[layer], w_in[layer],
                   b_forget[layer], w_cmp_k1[layer], w_cmp_k2[layer], w_cmp_v1[layer],
                   w_cmp_v2[layer], pe_cmp_k[layer], pe_cmp_v[layer], w_out[layer], cos128, sin128)
    return x
```

```python
import functools

import jax
import jax.numpy as jnp
import numpy as np
from jax import lax
from jax.experimental import pallas as pl
from jax.experimental.pallas import tpu as pltpu

F32 = jnp.float32
BF16 = jnp.bfloat16

D_MODEL = 1024
HEAD_DIM = 64
FOX_WIDTH = 512
NSA_WIDTH = 512
FOX_HEADS = 8
NSA_HEADS = 8
NSA_KV_HEADS = 2
NSA_GROUP = 4
NSA_KV_WIDTH = 128
CMP_BLOCK = 32
CMP_STRIDE = 16
CMP_HIDDEN = 128
SLC_BLOCK = 64
N_SELECT = 16
WINDOW = 512
ROPE_THETA = 10000.0
RMS_EPS = 1e-6
LOG2E = 1.4426950408889634
QK_SCALE = HEAD_DIM ** -0.5 * LOG2E

LANES = 128
SUBLANES = 8
V7X_VMEM_BYTES = 64 * 1024 * 1024
MASK_VALUE = -1e30

PROJ_ROWS = 512
FOX_TILE = 512
NSA_Q = 256
SEL_Q = 128
SEL_KV = 512
WIN_KEYS = WINDOW + NSA_Q
FLASH_UNROLL = 6
CMP_QUARTERS = 4
FOX_AUG = 6

C_FQ, C_FK, C_FV, C_FZ, C_NQ = 0, 512, 1024, 1536, 2048
C_KC, C_VC, C_KS, C_VS, C_KW, C_VW = 2560, 2688, 2816, 2944, 3072, 3200
C_NZ, C_MISC, PROJ_COLS = 3328, 3840, 3968
MISC_GATE0 = FOX_HEADS


def _vmem_limit(pipelined_bytes, resident_bytes):
    need = 2 * pipelined_bytes + resident_bytes
    return int(min(max(need, 16 * 1024 * 1024), V7X_VMEM_BYTES - 8 * 1024 * 1024))


def _sigmoid(v):
    return 1.0 / (1.0 + jnp.exp(-v))


def _silu(v):
    return v * _sigmoid(v)


def _dot_nt(a, b):
    return lax.dot_general(a, b, (((1,), (1,)), ((), ())), preferred_element_type=F32)


def _split3(v):
    hi = v.astype(BF16)
    r1 = v - hi.astype(F32)
    mid = r1.astype(BF16)
    lo = (r1 - mid.astype(F32)).astype(BF16)
    return hi, mid, lo


def _ada_kernel(c_ref, w_ref, b_ref, o_ref):
    a = _silu(c_ref[...])
    o_ref[...] = jnp.dot(a, w_ref[...], precision=lax.Precision.HIGHEST,
                         preferred_element_type=F32) + b_ref[...]


def _ada(c8, w_ada, b_ada):
    n = w_ada.shape[1]
    blk = D_MODEL
    return pl.pallas_call(
        _ada_kernel,
        grid=(n // blk,),
        in_specs=[pl.BlockSpec((SUBLANES, D_MODEL), lambda j: (0, 0)),
                  pl.BlockSpec((D_MODEL, blk), lambda j: (0, j)),
                  pl.BlockSpec((1, blk), lambda j: (0, j))],
        out_specs=pl.BlockSpec((SUBLANES, blk), lambda j: (0, j)),
        out_shape=jax.ShapeDtypeStruct((SUBLANES, n), F32),
        compiler_params=pltpu.CompilerParams(
            dimension_semantics=("arbitrary",),
            vmem_limit_bytes=_vmem_limit(D_MODEL * blk * 4, 4 * 1024 * 1024)),
        name="ada",
    )(c8, w_ada, b_ada.reshape(1, n))


def _rope128(t, cos, sin_signed):
    lane = lax.broadcasted_iota(jnp.int32, t.shape, 1)
    first_half = (lane & (HEAD_DIM - 1)) < HEAD_DIM // 2
    partner = jnp.where(first_half,
                        pltpu.roll(t, LANES - HEAD_DIM // 2, 1),
                        pltpu.roll(t, HEAD_DIM // 2, 1))
    return t * cos + partner * sin_signed


def _proj_kernel(x_ref, mod_ref, g_ref, w_ref, cos_ref, sin_ref, bf_ref,
                 fq_ref, fk_ref, fv_ref, fz_ref, nq_ref, cmp_ref,
                 ks_ref, vs_ref, kw_ref, vw_ref, nz_ref, misc_ref, lf_ref):
    tm = x_ref.shape[1]
    x = x_ref[0]
    y = x * lax.rsqrt(jnp.mean(x * x, axis=-1, keepdims=True) + RMS_EPS)
    y = y * g_ref[...]
    h = (y * (1.0 + mod_ref[0, 1:2, :]) + mod_ref[0, 0:1, :]).astype(BF16)
    cos = cos_ref[...]
    sin = sin_ref[...]

    def mm(lo, n):
        return jnp.dot(h, w_ref[:, lo:lo + n], preferred_element_type=F32)

    def slab_pair(lo):
        both = mm(lo, 2 * LANES)
        return both[:, 0:LANES], both[:, LANES:2 * LANES]

    lane = lax.broadcasted_iota(jnp.int32, (tm, LANES), 1)
    low = lane < HEAD_DIM
    ones_col = jnp.where(lane == HEAD_DIM, 1.0, 0.0).astype(BF16)

    def doubled(slab):
        swapped = pltpu.roll(slab, HEAD_DIM, 1)
        return jnp.where(low, slab, swapped), jnp.where(low, swapped, slab)

    def with_ones(slab):
        swapped = pltpu.roll(slab, HEAD_DIM, 1)
        return jnp.where(low, slab, ones_col), jnp.where(low, swapped, ones_col)

    fq_ref[0] = (mm(C_FQ, FOX_WIDTH) * QK_SCALE).astype(BF16)
    fk_ref[0] = mm(C_FK, FOX_WIDTH).astype(BF16)
    fv = mm(C_FV, FOX_WIDTH).astype(BF16)
    for pair in range(FOX_WIDTH // LANES):
        fv_ref[0, 2 * pair], fv_ref[0, 2 * pair + 1] = with_ones(fv[:, pair * LANES:(pair + 1) * LANES])
    fz_ref[0] = _silu(mm(C_FZ, FOX_WIDTH)).astype(BF16)
    for j in range(NSA_WIDTH // (2 * LANES)):
        for i, t in enumerate(slab_pair(C_NQ + 2 * j * LANES)):
            c = (2 * j + i) * LANES
            nq_ref[0, :, c:c + LANES] = (_rope128(t, cos, sin) * QK_SCALE).astype(BF16)

    kc, vc = slab_pair(C_KC)
    for kind, slab in enumerate((_rope128(kc, cos, sin), vc)):
        for g in range(NSA_KV_HEADS):
            cmp_ref[kind, 0, g] = slab[:, g * HEAD_DIM:(g + 1) * HEAD_DIM]
    pos = pl.program_id(1) * tm + lax.broadcasted_iota(jnp.int32, (tm, LANES), 0)
    onehot = jnp.where(pos // SLC_BLOCK == lane, 1.0, 0.0).astype(BF16)
    ks, vs = slab_pair(C_KS)
    for g, k2 in enumerate(doubled(_rope128(ks, cos, sin).astype(BF16))):
        ks_ref[0, g, :, 0:LANES] = k2
        ks_ref[0, g, :, LANES:2 * LANES] = onehot
    vs_ref[0, 0], vs_ref[0, 1] = with_ones(vs.astype(BF16))
    kw, vw = slab_pair(C_KW)
    kw_ref[0, 0], kw_ref[0, 1] = doubled(_rope128(kw, cos, sin).astype(BF16))
    vw_ref[0, 0], vw_ref[0, 1] = doubled(vw.astype(BF16))
    nz_ref[0] = _silu(mm(C_NZ, NSA_WIDTH)).astype(BF16)
    misc = mm(C_MISC, LANES)
    misc_ref[0] = misc
    z = misc + bf_ref[...]
    log_f = jnp.minimum(z, 0.0) - jnp.log1p(jnp.exp(-jnp.abs(z)))
    lf_ref[0] = log_f.T[0:FOX_HEADS, :]


def _proj(x, mod, g_pre, w_cat, cos128, sin128, bf128):
    b, s, d = x.shape
    tm = PROJ_ROWS
    row = lambda bi, i: (bi, i, 0)
    wide = lambda dt: jax.ShapeDtypeStruct((b, s, FOX_WIDTH), dt)
    g = NSA_KV_HEADS
    grouped = lambda lanes: jax.ShapeDtypeStruct((b, g, s, lanes), BF16)
    grouped_spec = lambda lanes: pl.BlockSpec((1, g, tm, lanes), lambda bi, i: (bi, 0, i, 0))
    out_shape = (wide(BF16), wide(BF16), jax.ShapeDtypeStruct((b, FOX_HEADS, s, LANES), BF16),
                 wide(BF16), wide(BF16),
                 jax.ShapeDtypeStruct((2, b, g, s, HEAD_DIM), F32),
                 grouped(2 * LANES), grouped(LANES), grouped(LANES), grouped(LANES),
                 wide(BF16), jax.ShapeDtypeStruct((b, s, LANES), F32),
                 jax.ShapeDtypeStruct((b, FOX_HEADS, s), F32))
    wide_spec = pl.BlockSpec((1, tm, FOX_WIDTH), row)
    out_specs = (wide_spec, wide_spec,
                 pl.BlockSpec((1, FOX_HEADS, tm, LANES), lambda bi, i: (bi, 0, i, 0)),
                 wide_spec, wide_spec) + (
        pl.BlockSpec((2, 1, g, tm, HEAD_DIM), lambda bi, i: (0, bi, 0, i, 0)),
        grouped_spec(2 * LANES), grouped_spec(LANES), grouped_spec(LANES), grouped_spec(LANES),
        wide_spec, pl.BlockSpec((1, tm, LANES), row),
        pl.BlockSpec((1, FOX_HEADS, tm), lambda bi, i: (bi, 0, i)))
    pipelined = (tm * d * 4 + tm * (7 * FOX_WIDTH * 2 + 2 * g * LANES * 4 + 5 * g * LANES * 2 + LANES * 4)
                 + d * PROJ_COLS * 2)
    return pl.pallas_call(
        _proj_kernel,
        grid=(b, s // tm),
        in_specs=[pl.BlockSpec((1, tm, d), row),
                  pl.BlockSpec((1, 3, d), lambda bi, i: (bi, 0, 0)),
                  pl.BlockSpec((1, d), lambda bi, i: (0, 0)),
                  pl.BlockSpec((d, PROJ_COLS), lambda bi, i: (0, 0)),
                  pl.BlockSpec((tm, LANES), lambda bi, i: (i, 0)),
                  pl.BlockSpec((tm, LANES), lambda bi, i: (i, 0)),
                  pl.BlockSpec((1, LANES), lambda bi, i: (0, 0))],
        out_specs=out_specs,
        out_shape=out_shape,
        compiler_params=pltpu.CompilerParams(
            dimension_semantics=("arbitrary", "arbitrary"),
            vmem_limit_bytes=_vmem_limit(pipelined, 8 * 1024 * 1024)),
        name="proj",
    )(x, mod, g_pre, w_cat, cos128, sin128, bf128)


def _cumsum_kernel(x_ref, hi_ref, mid_ref, lo_ref, *, chunks):
    x = x_ref[...]
    n = x.shape[0]
    parts = _split3(x)
    r = lax.broadcasted_iota(jnp.int32, (LANES, LANES), 0)
    c = lax.broadcasted_iota(jnp.int32, (LANES, LANES), 1)
    tri = (r <= c).astype(BF16)
    rr = lax.broadcasted_iota(jnp.int32, (n, n), 0)
    cc = lax.broadcasted_iota(jnp.int32, (n, n), 1)
    earlier = jnp.logical_and(cc < rr, (cc // chunks) == (rr // chunks)).astype(BF16)
    within = sum(jnp.dot(p, tri, preferred_element_type=F32) for p in parts)
    before = sum(jnp.dot(earlier, p, preferred_element_type=F32) for p in parts)
    total = (within + jnp.sum(before, axis=-1, keepdims=True)) * LOG2E
    hi_ref[...], mid_ref[...], lo_ref[...] = _split3(total)


def _cumsum_lanes_split(v):
    rows, s = v.shape
    chunks = s // LANES
    n = rows * chunks
    part = jax.ShapeDtypeStruct((n, LANES), BF16)
    parts = pl.pallas_call(
        functools.partial(_cumsum_kernel, chunks=chunks),
        out_shape=(part, part, part),
        compiler_params=pltpu.CompilerParams(
            vmem_limit_bytes=_vmem_limit(2 * n * LANES * 4, 6 * n * n)),
        name="cumsum",
    )(v.reshape(n, LANES))
    return tuple(p.reshape(rows, s) for p in parts)


def _flash_scratch(rows, tile, acc_lanes):
    return ([pltpu.VMEM((rows, LANES), F32), pltpu.VMEM((rows, acc_lanes), F32)]
            + [pltpu.VMEM((rows, tile), F32)] * 2
            + [pltpu.VMEM((rows, tile), BF16)] * 2
            + [pltpu.VMEM((rows, LANES), F32)] * 2)


def _flash_scratch_bytes(rows, tile, acc_lanes):
    return rows * (LANES * 4 + acc_lanes * 4 + 2 * tile * 4 + 2 * tile * 2 + 2 * LANES * 4)


def _causal_flash(n_full, tile, q_ref, load_k, pv, causal_mask, epilogue, scratch):
    m_ref, acc_ref, s0, s1, p0, p1, a0, a1 = scratch
    s_bufs, p_bufs, a_bufs = (s0, s1), (p0, p1), (a0, a1)

    def logits(j):
        return _dot_nt(q_ref[...], load_k(pl.multiple_of(j * tile, tile)))

    def softmax(s):
        m_prev = m_ref[...]
        m_next = jnp.maximum(m_prev, jnp.max(s, axis=1, keepdims=True))
        m_ref[...] = m_next
        p = jnp.exp2(s - jnp.tile(m_next, (1, tile // LANES)))
        return p.astype(BF16), jnp.exp2(m_prev - m_next)

    def accumulate(p, alpha, j):
        acc_ref[...] = acc_ref[...] * alpha + pv(p, pl.multiple_of(j * tile, tile))

    def stage(j, cur):
        s_bufs[1 - cur][...] = logits(j + 1)
        p_bufs[cur][...], a_bufs[cur][...] = softmax(s_bufs[cur][...])
        accumulate(p_bufs[1 - cur][...], a_bufs[1 - cur][...], j - 1)

    def run_stages(first, count):
        for i in range(count):
            stage(first + i, (1 + i) % 2)

    def finish(cur):
        p, alpha = softmax(causal_mask(s_bufs[cur][...], n_full * tile))
        accumulate(p_bufs[1 - cur][...], a_bufs[1 - cur][...], n_full - 1)
        accumulate(p, alpha, n_full)

    m_ref[...] = jnp.full(m_ref.shape, MASK_VALUE, F32)
    acc_ref[...] = jnp.zeros(acc_ref.shape, F32)
    s_first = logits(0)
    s_bufs[1][...] = logits(jnp.minimum(n_full, 1))
    p_bufs[0][...], a_bufs[0][...] = softmax(causal_mask(s_first, 0))

    @pl.when(n_full == 0)
    def _():
        accumulate(p_bufs[0][...], a_bufs[0][...], 0)
        epilogue()

    @pl.when(n_full > 0)
    def _():
        later = n_full - 1

        def unrolled(jj, carry):
            run_stages(1 + FLASH_UNROLL * jj, FLASH_UNROLL)
            return carry

        lax.fori_loop(0, later // FLASH_UNROLL, unrolled, 0)
        rest = later % FLASH_UNROLL

        def tail(count):
            run_stages(n_full - count, count)
            finish((1 + count) % 2)
            epilogue()

        for count in range(FLASH_UNROLL):
            pl.when(rest == count)(functools.partial(tail, count))


def _fox_aug(cum_split):
    b, h, s = cum_split[0].shape
    terms = jnp.concatenate(list(cum_split) + [jnp.ones((b, 1, s), BF16)], axis=1)
    terms = jnp.transpose(terms, (0, 2, 1))
    place_q = np.zeros((3 * h + 1, h * HEAD_DIM), np.float32)
    place_k = np.zeros((3 * h + 1, h * HEAD_DIM), np.float32)
    half = FOX_AUG // 2
    for head in range(h):
        for part in range(half):
            place_q[part * h + head, head * HEAD_DIM + part] = 1.0
            place_q[3 * h, head * HEAD_DIM + half + part] = 1.0
            place_k[3 * h, head * HEAD_DIM + part] = 1.0
            place_k[part * h + head, head * HEAD_DIM + half + part] = -1.0
    scatter = lambda place: jnp.einsum("bsr,rl->bsl", terms, jnp.asarray(place, BF16),
                                       preferred_element_type=F32).astype(BF16)
    return scatter(place_q), scatter(place_k)


def _fox_kernel(q_ref, qa_ref, k_ref, ka_ref, v_ref, z_ref, o_ref, qm_ref, *flash):
    t = FOX_TILE
    qi = pl.program_id(2)
    q = q_ref[0]
    qa = qa_ref[0]
    lane = lax.broadcasted_iota(jnp.int32, q.shape, 1)
    for hh in range(2):
        r = slice(hh * t, (hh + 1) * t)
        own = (lane < HEAD_DIM) if hh == 0 else (lane >= HEAD_DIM)
        qm_ref[r, 0:LANES] = jnp.where(own, q, jnp.zeros_like(q))
        qm_ref[r, LANES:2 * LANES] = jnp.where(own, qa, jnp.zeros_like(qa))

    def load_k(k0):
        return jnp.concatenate([k_ref[0, pl.ds(k0, t), :], ka_ref[0, pl.ds(k0, t), :]], axis=1)

    def pv(p, k0):
        return jnp.concatenate(
            [jnp.dot(p[hh * t:(hh + 1) * t], v_ref[0, hh, pl.ds(k0, t), :], preferred_element_type=F32)
             for hh in range(2)], axis=0)

    def causal_mask(s, k0):
        t_row = qi * t + (lax.broadcasted_iota(jnp.int32, s.shape, 0) & (t - 1))
        pos = k0 + lax.broadcasted_iota(jnp.int32, s.shape, 1)
        return jnp.where(pos <= t_row, s, MASK_VALUE)

    def epilogue():
        acc_ref = flash[1]
        o0 = acc_ref[0:t, :] * (1.0 / acc_ref[0:t, HEAD_DIM:HEAD_DIM + 1])
        o1 = acc_ref[t:2 * t, :] * (1.0 / acc_ref[t:2 * t, HEAD_DIM:HEAD_DIM + 1])
        o = jnp.where(lane < HEAD_DIM, o0, pltpu.roll(o1, HEAD_DIM, 1))
        o_ref[0] = (o * z_ref[0].astype(F32)).astype(BF16)

    _causal_flash(qi, t, qm_ref, load_k, pv, causal_mask, epilogue, flash)


def _fox(fq, fk, fv, cum_split, fz):
    b, s, w = fq.shape
    t = FOX_TILE
    pairs = w // LANES
    qa, ka = _fox_aug(cum_split)
    tile = pl.BlockSpec((1, t, LANES), lambda bi, hp, i: (bi, i, hp))
    full = pl.BlockSpec((1, s, LANES), lambda bi, hp, i: (bi, 0, hp))
    resident = 4 * s * LANES * 2
    scratch = 2 * t * 2 * LANES * 2 + _flash_scratch_bytes(2 * t, t, LANES)
    return pl.pallas_call(
        _fox_kernel,
        grid=(b, pairs, s // t),
        in_specs=[tile, tile, full, full,
                  pl.BlockSpec((1, 2, s, LANES), lambda bi, hp, i: (bi, hp, 0, 0)),
                  tile],
        out_specs=tile,
        out_shape=jax.ShapeDtypeStruct((b, s, w), BF16),
        scratch_shapes=[pltpu.VMEM((2 * t, 2 * LANES), BF16)] + _flash_scratch(2 * t, t, LANES),
        compiler_params=pltpu.CompilerParams(
            dimension_semantics=("arbitrary", "arbitrary", "arbitrary"),
            vmem_limit_bytes=_vmem_limit(resident + 4 * t * LANES * 2, scratch + 4 * 2 * t * t * 4)),
        name="fox",
    )(fq, qa, fk, ka, fv, fz)


def _compress_kernel(t_ref, w1_ref, w2_ref, pe_ref, hi_ref, lo_ref):
    hp = lax.Precision.HIGHEST
    half = CMP_STRIDE * HEAD_DIM
    tt = t_ref[0, 0, 0]
    n = tt.shape[0]
    first = jnp.dot(tt, w1_ref[0, 0:half, :], precision=hp, preferred_element_type=F32)
    second = jnp.dot(tt, w1_ref[0, half:2 * half, :], precision=hp, preferred_element_type=F32)
    pe_term = jnp.dot(pe_ref[0], w1_ref[0], precision=hp, preferred_element_type=F32)[0:1, :]
    hidden = first + pltpu.roll(second, n - 1, 0) + pe_term
    out = jnp.dot(_silu(hidden), w2_ref[0], precision=hp, preferred_element_type=F32)
    hi = out.astype(BF16)
    hi_ref[0, 0, 0] = hi
    lo_ref[0, 0, 0] = (out - hi.astype(F32)).astype(BF16)


def _compress(t, w1, w2, pe):
    kinds, b, g, n, flat = t.shape
    width = w2.shape[-1]
    out_spec = pl.BlockSpec((1, 1, 1, n, width), lambda a, bi, gi: (a, bi, gi, 0, 0))
    out_part = jax.ShapeDtypeStruct((kinds, b, g, n, width), BF16)
    return pl.pallas_call(
        _compress_kernel,
        grid=(kinds, b, g),
        in_specs=[pl.BlockSpec((1, 1, 1, n, flat), lambda a, bi, gi: (a, bi, gi, 0, 0)),
                  pl.BlockSpec((1, 2 * flat, CMP_HIDDEN), lambda a, bi, gi: (a, 0, 0)),
                  pl.BlockSpec((1, CMP_HIDDEN, width), lambda a, bi, gi: (a, 0, 0)),
                  pl.BlockSpec((1, SUBLANES, 2 * flat), lambda a, bi, gi: (a, 0, 0))],
        out_specs=(out_spec, out_spec),
        out_shape=(out_part, out_part),
        compiler_params=pltpu.CompilerParams(
            dimension_semantics=("arbitrary", "arbitrary", "arbitrary"),
            vmem_limit_bytes=_vmem_limit(n * flat * 4 + 2 * flat * CMP_HIDDEN * 4, 8 * 1024 * 1024)),
        name="compress",
    )(t, w1, w2, pe)


def _softmax_numerator(s, bias):
    s = s + bias
    m = jnp.max(s, axis=1, keepdims=True)
    m = jnp.where(m == -jnp.inf, 0.0, m)
    return jnp.exp2(s - m)


def _stack_heads(q):
    tq = q.shape[0]
    lane = lax.broadcasted_iota(jnp.int32, (tq, LANES), 1)
    rows = []
    for h in range(NSA_GROUP):
        slab = q[:, (h // 2) * LANES:(h // 2 + 1) * LANES]
        own = (lane < HEAD_DIM) if h % 2 == 0 else (lane >= HEAD_DIM)
        rows.append(jnp.where(own, slab, jnp.zeros_like(slab)))
    return jnp.concatenate(rows, axis=0)


def _group_gates(misc, gi):
    gates = _sigmoid(misc)
    return jnp.where(gi == 0, gates, pltpu.roll(gates, LANES - 3 * NSA_GROUP, 1))


def _gate_lanes(gates, h, j):
    c = MISC_GATE0 + 3 * h + j
    return jnp.broadcast_to(gates[:, c:c + 1], gates.shape)


def _pair_slabs(per_head):
    lane = lax.broadcasted_iota(jnp.int32, per_head[0].shape, 1)
    return [jnp.where(lane < HEAD_DIM, per_head[2 * pp], per_head[2 * pp + 1])
            for pp in range(NSA_GROUP // 2)]


def _cmpwin_kernel(q_ref, kc_hi_ref, kc_lo_ref, vc_ref, overlap_ref, kw_ref, vw_ref, misc_ref,
                   part_ref, selb_ref):
    gi = pl.program_id(1)
    q0 = pl.multiple_of(pl.program_id(2) * NSA_Q, NSA_Q)
    quarter = kc_hi_ref.shape[3] // CMP_QUARTERS
    reach = q0 // (quarter * CMP_STRIDE)
    refs = (q_ref, kc_hi_ref, kc_lo_ref, vc_ref, overlap_ref, kw_ref, vw_ref, misc_ref,
            part_ref, selb_ref)
    for quarters in range(1, CMP_QUARTERS + 1):
        pl.when(reach == quarters - 1)(
            functools.partial(_cmpwin_body, refs, gi, q0, quarters * quarter))


def _cmpwin_body(refs, gi, q0, n_cmp):
    (q_ref, kc_hi_ref, kc_lo_ref, vc_ref, overlap_ref, kw_ref, vw_ref, misc_ref,
     part_ref, selb_ref) = refs
    tq = NSA_Q
    q4 = _stack_heads(q_ref[0])
    n_blk = n_cmp * CMP_STRIDE // SLC_BLOCK
    head_rows = [slice(h * tq, (h + 1) * tq) for h in range(NSA_GROUP)]

    s = (_dot_nt(q4, kc_hi_ref[0, 0, 0, 0:n_cmp, :])
         + _dot_nt(q4, kc_lo_ref[0, 0, 0, 0:n_cmp, :]))
    col = lax.broadcasted_iota(jnp.int32, (tq, n_cmp), 1)
    t_row = q0 + lax.broadcasted_iota(jnp.int32, (tq, n_cmp), 0)
    cmp_bias = jnp.where(col * CMP_STRIDE + (CMP_BLOCK - 1) <= t_row, 0.0, -jnp.inf)
    pcs = []
    for r in head_rows:
        p = _softmax_numerator(s[r], cmp_bias)
        pcs.append(p * (1.0 / jnp.maximum(jnp.sum(p, axis=1, keepdims=True), 1e-30)))
    oc = jnp.dot(jnp.concatenate(pcs, axis=0).astype(BF16), vc_ref[0, 0, 0, 0:n_cmp, :],
                 preferred_element_type=F32)

    pc_sum = pcs[0] + pcs[1] + pcs[2] + pcs[3]
    overlap = overlap_ref[0:n_blk, 0:n_cmp]
    imp = sum(_dot_nt(overlap, p) for p in _split3(pc_sum))
    blk = lax.broadcasted_iota(jnp.int32, (n_blk, tq), 0)
    cur = (q0 + lax.broadcasted_iota(jnp.int32, (n_blk, tq), 1)) // SLC_BLOCK
    forced = jnp.logical_or(blk == 0, jnp.logical_or(blk == cur, blk == cur - 1))
    imp = jnp.where(forced, jnp.inf, jnp.where(blk > cur, -jnp.inf, imp))

    def pick_one(_, carry):
        rem, sel = carry
        best = jnp.max(rem, axis=0, keepdims=True)
        first = jnp.min(jnp.where(rem == best, blk, n_blk), axis=0, keepdims=True)
        hit = blk == first
        return jnp.where(hit, -jnp.inf, rem), jnp.where(hit, 1.0, sel)

    _, sel = lax.fori_loop(0, min(N_SELECT, n_blk), pick_one, (imp, jnp.zeros_like(imp)),
                           unroll=True)
    all_blk = selb_ref.shape[3]
    if n_blk < all_blk:
        sel = jnp.concatenate([sel, jnp.zeros((all_blk - n_blk, tq), F32)], axis=0)
    selb_ref[0, 0] =jnp.where(sel.T > 0.5, 0.0, MASK_VALUE).astype(BF16)

    start = pl.multiple_of(jnp.maximum(q0 - WINDOW, 0), tq)
    kw = kw_ref[0, 0, pl.ds(start, WIN_KEYS), :]
    vw = vw_ref[0, 0, pl.ds(start, WIN_KEYS), :]
    sw = _dot_nt(q4, kw)
    pos = start + lax.broadcasted_iota(jnp.int32, (tq, WIN_KEYS), 1)
    dist = q0 + lax.broadcasted_iota(jnp.int32, (tq, WIN_KEYS), 0) - pos
    win_bias = jnp.where(jnp.logical_and(dist >= 0, dist < WINDOW), 0.0, -jnp.inf)
    pw = jnp.concatenate([_softmax_numerator(sw[r], win_bias) for r in head_rows],
                         axis=0).astype(BF16)
    ow = jnp.dot(pw, vw, preferred_element_type=F32)
    lw = jnp.dot(pw, jnp.ones((WIN_KEYS, LANES), BF16), preferred_element_type=F32)
    ow = ow * (1.0 / jnp.maximum(lw, 1e-30))

    gates = _group_gates(misc_ref[0], gi)
    gated = [_gate_lanes(gates, h, 0) * oc[r] + _gate_lanes(gates, h, 2) * ow[r]
             for h, r in enumerate(head_rows)]
    for pp, slab in enumerate(_pair_slabs(gated)):
        part_ref[0, :, pp * LANES:(pp + 1) * LANES] = slab


def _block_overlap(n_blk, n_cmp):
    ratio = SLC_BLOCK // CMP_STRIDE
    lo = np.arange(n_blk)[:, None] * ratio - (CMP_BLOCK // CMP_STRIDE - 1)
    i = np.arange(n_cmp)[None, :]
    n_overlap = (SLC_BLOCK + CMP_BLOCK) // CMP_STRIDE - 1
    return ((i >= lo) & (i < lo + n_overlap) & (i < n_cmp - 1)).astype(np.float32)


def _cmpwin(nq, cmp_hi, cmp_lo, kw, vw_ones, misc):
    b, s, w = nq.shape
    g = NSA_KV_HEADS
    tq = NSA_Q
    gw = NSA_GROUP * HEAD_DIM
    n_cmp = cmp_hi.shape[3]
    n_blk = s // SLC_BLOCK
    q_spec = pl.BlockSpec((1, tq, gw), lambda bi, gi, i: (bi, i, gi))
    key_spec = pl.BlockSpec((1, 1, 1, n_cmp, LANES), lambda bi, gi, i: (0, bi, gi, 0, 0))
    val_spec = pl.BlockSpec((1, 1, 1, n_cmp, LANES), lambda bi, gi, i: (1, bi, gi, 0, 0))
    seq_spec = pl.BlockSpec((1, 1, s, LANES), lambda bi, gi, i: (bi, gi, 0, 0))
    resident = 2 * s * LANES * 2 + 3 * n_cmp * LANES * 2 + n_blk * n_cmp * 2
    rows = NSA_GROUP * tq
    return pl.pallas_call(
        _cmpwin_kernel,
        grid=(b, g, s // tq),
        in_specs=[q_spec, key_spec, key_spec, val_spec,
                  pl.BlockSpec((n_blk, n_cmp), lambda bi, gi, i: (0, 0)),
                  seq_spec, seq_spec,
                  pl.BlockSpec((1, tq, LANES), lambda bi, gi, i: (bi, i, 0))],
        out_specs=(q_spec, pl.BlockSpec((1, 1, tq, n_blk), lambda bi, gi, i: (bi, gi, i, 0))),
        out_shape=(jax.ShapeDtypeStruct((b, s, w), F32),
                   jax.ShapeDtypeStruct((b, g, s, n_blk), BF16)),
        compiler_params=pltpu.CompilerParams(
            dimension_semantics=("arbitrary", "arbitrary", "arbitrary"),
            vmem_limit_bytes=_vmem_limit(resident + tq * gw * 8, 12 * rows * WIN_KEYS * 4)),
        name="cmpwin",
    )(nq, cmp_hi, cmp_lo, cmp_hi, jnp.asarray(_block_overlap(n_blk, n_cmp), BF16), kw, vw_ones, misc)


def _sel_kernel(q_ref, k_ref, v_ref, selb_ref, part_ref, misc_ref, z_ref, o_ref,
                qa_ref, gate_ref, *flash):
    tq = SEL_Q
    gi = pl.program_id(1)
    qi = pl.program_id(2)
    q0 = pl.multiple_of(qi * tq, tq)
    q = q_ref[0]
    selb = selb_ref[0, 0]
    lane = lax.broadcasted_iota(jnp.int32, (tq, LANES), 1)
    for h in range(NSA_GROUP):
        r = slice(h * tq, (h + 1) * tq)
        slab = q[:, (h // 2) * LANES:(h // 2 + 1) * LANES]
        own = (lane < HEAD_DIM) if h % 2 == 0 else (lane >= HEAD_DIM)
        qa_ref[r, 0:LANES] = jnp.where(own, slab, jnp.zeros_like(slab))
        qa_ref[r, LANES:2 * LANES] = selb
    gates = _group_gates(misc_ref[0], gi)
    for h in range(NSA_GROUP):
        gate_ref[h] = _gate_lanes(gates, h, 1)

    n_full = q0 // SEL_KV

    def causal_mask(s, k0):
        t_row = q0 + (lax.broadcasted_iota(jnp.int32, s.shape, 0) & (tq - 1))
        pos = k0 + lax.broadcasted_iota(jnp.int32, s.shape, 1)
        return jnp.where(pos <= t_row, s, MASK_VALUE)

    def epilogue():
        acc_ref = flash[1]
        gated = []
        for h in range(NSA_GROUP):
            acc = acc_ref[h * tq:(h + 1) * tq, :]
            scaled = acc * (gate_ref[h] * (1.0 / acc[:, HEAD_DIM:HEAD_DIM + 1]))
            gated.append(scaled if h % 2 == 0 else pltpu.roll(scaled, HEAD_DIM, 1))
        for pp, slab in enumerate(_pair_slabs(gated)):
            c = slice(pp * LANES, (pp + 1) * LANES)
            o_ref[0, :, c] = ((part_ref[0, :, c] + slab) * z_ref[0, :, c].astype(F32)).astype(BF16)

    _causal_flash(n_full, SEL_KV, qa_ref,
                  lambda k0: k_ref[0, 0, pl.ds(k0, SEL_KV), :],
                  lambda p, k0: jnp.dot(p, v_ref[0, 0, pl.ds(k0, SEL_KV), :],
                                        preferred_element_type=F32),
                  causal_mask, epilogue, flash)


def _sel(nq, k_aug, v_ones, selb, part, misc, nz):
    b, s, w = nq.shape
    g = NSA_KV_HEADS
    tq = SEL_Q
    gw = NSA_GROUP * HEAD_DIM
    n_blk = s // SLC_BLOCK
    rows = NSA_GROUP * tq
    q_spec = pl.BlockSpec((1, tq, gw), lambda bi, gi, i: (bi, i, gi))
    resident = s * 3 * LANES * 2
    return pl.pallas_call(
        _sel_kernel,
        grid=(b, g, s // tq),
        in_specs=[q_spec,
                  pl.BlockSpec((1, 1, s, 2 * LANES), lambda bi, gi, i: (bi, gi, 0, 0)),
                  pl.BlockSpec((1, 1, s, LANES), lambda bi, gi, i: (bi, gi, 0, 0)),
                  pl.BlockSpec((1, 1, tq, n_blk), lambda bi, gi, i: (bi, gi, i, 0)),
                  q_spec,
                  pl.BlockSpec((1, tq, LANES), lambda bi, gi, i: (bi, i, 0)),
                  q_spec],
        out_specs=q_spec,
        out_shape=jax.ShapeDtypeStruct((b, s, w), BF16),
        scratch_shapes=[pltpu.VMEM((rows, 2 * LANES), BF16),
                        pltpu.VMEM((NSA_GROUP, tq, LANES), F32)] + _flash_scratch(rows, SEL_KV, LANES),
        compiler_params=pltpu.CompilerParams(
            dimension_semantics=("arbitrary", "arbitrary", "arbitrary"),
            vmem_limit_bytes=_vmem_limit(resident + tq * gw * 10,
                                         _flash_scratch_bytes(rows, SEL_KV, LANES) + 4 * rows * SEL_KV * 4)),
        name="sel",
    )(nq, k_aug, v_ones, selb, part, misc, nz)


def _out_kernel(yf_ref, yn_ref, w_ref, g_ref, mod_ref, x_ref, o_ref):
    y = (jnp.dot(yf_ref[0], w_ref[0:FOX_WIDTH, :], preferred_element_type=F32)
         + jnp.dot(yn_ref[0], w_ref[FOX_WIDTH:, :], preferred_element_type=F32))
    yn = y * lax.rsqrt(jnp.mean(y * y, axis=-1, keepdims=True) + RMS_EPS)
    o_ref[0] = x_ref[0] + mod_ref[0, 2:3, :] * (yn * g_ref[...])


def _out(y_fox, y_nsa, w_out, g_post, mod, x):
    b, s, d = x.shape
    tm = PROJ_ROWS
    row = lambda bi, i: (bi, i, 0)
    half = pl.BlockSpec((1, tm, FOX_WIDTH), row)
    pipelined = 2 * tm * FOX_WIDTH * 2 + 2 * tm * d * 4 + d * d * 2
    return pl.pallas_call(
        _out_kernel,
        grid=(b, s // tm),
        in_specs=[half, half,
                  pl.BlockSpec((d, d), lambda bi, i: (0, 0)),
                  pl.BlockSpec((1, d), lambda bi, i: (0, 0)),
                  pl.BlockSpec((1, 3, d), lambda bi, i: (bi, 0, 0)),
                  pl.BlockSpec((1, tm, d), row)],
        out_specs=pl.BlockSpec((1, tm, d), row),
        out_shape=jax.ShapeDtypeStruct((b, s, d), F32),
        compiler_params=pltpu.CompilerParams(
            dimension_semantics=("arbitrary", "arbitrary"),
            vmem_limit_bytes=_vmem_limit(pipelined, 6 * tm * d * 4)),
        name="out",
    )(y_fox, y_nsa, w_out, g_post, mod, x)


def _rope_slabs(seq_len):
    inv = 1.0 / (ROPE_THETA ** (jnp.arange(0, HEAD_DIM, 2, dtype=F32) / HEAD_DIM))
    ang = jnp.arange(seq_len, dtype=F32)[:, None] * inv[None, :]
    cos, sin = jnp.cos(ang), jnp.sin(ang)
    reps = LANES // (HEAD_DIM // 2)
    sign = jnp.tile(jnp.concatenate([-jnp.ones((HEAD_DIM // 2,), F32), jnp.ones((HEAD_DIM // 2,), F32)]),
                    LANES // HEAD_DIM)
    return jnp.tile(cos, (1, reps)), jnp.tile(sin, (1, reps)) * sign[None, :]


def _reorder_w_in(w_in):
    fw, kv = FOX_WIDTH, NSA_KV_WIDTH
    o = 0
    cols = {}
    for name, n in (("fq", fw), ("fk", fw), ("fv", fw), ("ff", FOX_HEADS), ("fz", fw), ("nq", NSA_WIDTH),
                    ("kc", kv), ("vc", kv), ("ks", kv), ("vs", kv), ("kw", kv), ("vw", kv),
                    ("ng", 3 * NSA_HEADS), ("nz", NSA_WIDTH)):
        cols[name] = w_in[:, o:o + n]
        o += n
    pad = jnp.zeros((w_in.shape[0], LANES - FOX_HEADS - 3 * NSA_HEADS), w_in.dtype)
    order = ("fq", "fk", "fv", "fz", "nq", "kc", "vc", "ks", "vs", "kw", "vw", "nz", "ff", "ng")
    return jnp.concatenate([cols[k] for k in order] + [pad], axis=1).astype(BF16)


def _layer(x, c8, g_pre, g_post, w_ada, b_ada, w_in, b_forget, w_cmp_k1, w_cmp_k2,
           w_cmp_v1, w_cmp_v2, pe_cmp_k, pe_cmp_v, w_out, cos128, sin128):
    b, s, d = x.shape
    mod = _ada(c8, w_ada, b_ada)[:b].reshape(b, 3, d)
    bf128 = jnp.pad(b_forget, (0, LANES - FOX_HEADS)).reshape(1, LANES)
    (fq, fk, fv, fz, nq, cmp_in, ks_aug, vs_ones, kw2, vw_ones, nz, misc, log_f) = _proj(
        x, mod, g_pre.reshape(1, d), _reorder_w_in(w_in), cos128, sin128, bf128)

    cum_split = tuple(p.reshape(b, FOX_HEADS, s)
                      for p in _cumsum_lanes_split(log_f.reshape(b * FOX_HEADS, s)))
    y_fox = _fox(fq, fk, fv, cum_split, fz)

    flat = CMP_STRIDE * HEAD_DIM
    t = cmp_in.reshape(2, b, NSA_KV_HEADS, s // CMP_STRIDE, flat)
    pe = jnp.stack([pe_cmp_k.reshape(1, 2 * flat), pe_cmp_v.reshape(1, 2 * flat)])
    w2 = jnp.stack([w_cmp_k2, w_cmp_v2])
    cmp_hi, cmp_lo = _compress(t, jnp.stack([w_cmp_k1, w_cmp_v1]), jnp.concatenate([w2, w2], axis=-1),
                               jnp.broadcast_to(pe, (2, SUBLANES, 2 * flat)))
    part, selb = _cmpwin(nq, cmp_hi, cmp_lo, kw2, vw_ones, misc)
    y_nsa = _sel(nq, ks_aug, vs_ones, selb, part, misc, nz)

    return _out(y_fox, y_nsa, w_out.astype(BF16), g_post.reshape(1, d), mod, x)


def kernel(x, c, g_pre, g_post, w_ada, b_ada, w_in, b_forget, w_cmp_k1, w_cmp_k2,
           w_cmp_v1, w_cmp_v2, pe_cmp_k, pe_cmp_v, w_out):
    cos128, sin128 = _rope_slabs(x.shape[1])
    c8 = jnp.pad(c, ((0, SUBLANES - c.shape[0]), (0, 0)))
    for layer in range(g_pre.shape[0]):
        x = _layer(x, c8, g_pre[layer], g_post[layer], w_ada[layer], b_ada[layer], w_in[layer],
                   b_forget[layer], w_cmp_k1[layer], w_cmp_k2[layer], w_cmp_v1[layer],
                   w_cmp_v2[layer], pe_cmp_k[layer], pe_cmp_v[layer], w_out[layer], cos128, sin128)
    return x
```

```python
import functools

import jax
import jax.numpy as jnp
import numpy as np
from jax import lax
from jax.experimental import pallas as pl
from jax.experimental.pallas import tpu as pltpu

F32 = jnp.float32
BF16 = jnp.bfloat16

D_MODEL = 1024
HEAD_DIM = 64
FOX_WIDTH = 512
NSA_WIDTH = 512
FOX_HEADS = 8
NSA_HEADS = 8
NSA_KV_HEADS = 2
NSA_GROUP = 4
NSA_KV_WIDTH = 128
CMP_BLOCK = 32
CMP_STRIDE = 16
CMP_HIDDEN = 128
SLC_BLOCK = 64
N_SELECT = 16
WINDOW = 512
ROPE_THETA = 10000.0
RMS_EPS = 1e-6
LOG2E = 1.4426950408889634
QK_SCALE = HEAD_DIM ** -0.5 * LOG2E

LANES = 128
SUBLANES = 8
V7X_VMEM_BYTES = 64 * 1024 * 1024
MASK_VALUE = -1e30

PROJ_ROWS = 512
FOX_TILE = 512
NSA_Q = 256
SEL_Q = 128
SEL_KV = 512
WIN_KEYS = WINDOW + NSA_Q
FLASH_UNROLL = 6
CMP_QUARTERS = 4
FOX_AUG = 6

C_FQ, C_FK, C_FV, C_FZ, C_NQ = 0, 512, 1024, 1536, 2048
C_KC, C_VC, C_KS, C_VS, C_KW, C_VW = 2560, 2688, 2816, 2944, 3072, 3200
C_NZ, C_MISC, PROJ_COLS = 3328, 3840, 3968
MISC_GATE0 = FOX_HEADS


def _vmem_limit(pipelined_bytes, resident_bytes):
    need = 2 * pipelined_bytes + resident_bytes
    return int(min(max(need, 16 * 1024 * 1024), V7X_VMEM_BYTES - 8 * 1024 * 1024))


def _sigmoid(v):
    return 1.0 / (1.0 + jnp.exp(-v))


def _silu(v):
    return v * _sigmoid(v)


def _dot_nt(a, b):
    return lax.dot_general(a, b, (((1,), (1,)), ((), ())), preferred_element_type=F32)


def _split3(v):
    hi = v.astype(BF16)
    r1 = v - hi.astype(F32)
    mid = r1.astype(BF16)
    lo = (r1 - mid.astype(F32)).astype(BF16)
    return hi, mid, lo


def _ada_kernel(c_ref, w_ref, b_ref, o_ref):
    a = _silu(c_ref[...])
    o_ref[...] = jnp.dot(a, w_ref[...], precision=lax.Precision.HIGHEST,
                         preferred_element_type=F32) + b_ref[...]


def _ada(c8, w_ada, b_ada):
    n = w_ada.shape[1]
    blk = D_MODEL
    return pl.pallas_call(
        _ada_kernel,
        grid=(n // blk,),
        in_specs=[pl.BlockSpec((SUBLANES, D_MODEL), lambda j: (0, 0)),
                  pl.BlockSpec((D_MODEL, blk), lambda j: (0, j)),
                  pl.BlockSpec((1, blk), lambda j: (0, j))],
        out_specs=pl.BlockSpec((SUBLANES, blk), lambda j: (0, j)),
        out_shape=jax.ShapeDtypeStruct((SUBLANES, n), F32),
        compiler_params=pltpu.CompilerParams(
            dimension_semantics=("arbitrary",),
            vmem_limit_bytes=_vmem_limit(D_MODEL * blk * 4, 4 * 1024 * 1024)),
        name="ada",
    )(c8, w_ada, b_ada.reshape(1, n))


def _rope128(t, cos, sin_signed):
    lane = lax.broadcasted_iota(jnp.int32, t.shape, 1)
    first_half = (lane & (HEAD_DIM - 1)) < HEAD_DIM // 2
    partner = jnp.where(first_half,
                        pltpu.roll(t, LANES - HEAD_DIM // 2, 1),
                        pltpu.roll(t, HEAD_DIM // 2, 1))
    return t * cos + partner * sin_signed


def _proj_kernel(x_ref, mod_ref, g_ref, w_ref, cos_ref, sin_ref, bf_ref,
                 fq_ref, fk_ref, fv_ref, fz_ref, nq_ref, cmp_ref,
                 ks_ref, vs_ref, kw_ref, vw_ref, nz_ref, misc_ref, lf_ref):
    tm = x_ref.shape[1]
    x = x_ref[0]
    y = x * lax.rsqrt(jnp.mean(x * x, axis=-1, keepdims=True) + RMS_EPS)
    y = y * g_ref[...]
    h = (y * (1.0 + mod_ref[0, 1:2, :]) + mod_ref[0, 0:1, :]).astype(BF16)
    cos = cos_ref[...]
    sin = sin_ref[...]

    def mm(lo, n):
        return jnp.dot(h, w_ref[:, lo:lo + n], preferred_element_type=F32)

    def slab_pair(lo):
        both = mm(lo, 2 * LANES)
        return both[:, 0:LANES], both[:, LANES:2 * LANES]

    lane = lax.broadcasted_iota(jnp.int32, (tm, LANES), 1)
    low = lane < HEAD_DIM
    ones_col = jnp.where(lane == HEAD_DIM, 1.0, 0.0).astype(BF16)

    def doubled(slab):
        swapped = pltpu.roll(slab, HEAD_DIM, 1)
        return jnp.where(low, slab, swapped), jnp.where(low, swapped, slab)

    def with_ones(slab):
        swapped = pltpu.roll(slab, HEAD_DIM, 1)
        return jnp.where(low, slab, ones_col), jnp.where(low, swapped, ones_col)

    fq_ref[0] = (mm(C_FQ, FOX_WIDTH) * QK_SCALE).astype(BF16)
    fk_ref[0] = mm(C_FK, FOX_WIDTH).astype(BF16)
    fv = mm(C_FV, FOX_WIDTH).astype(BF16)
    for pair in range(FOX_WIDTH // LANES):
        fv_ref[0, 2 * pair], fv_ref[0, 2 * pair + 1] = with_ones(fv[:, pair * LANES:(pair + 1) * LANES])
    fz_ref[0] = _silu(mm(C_FZ, FOX_WIDTH)).astype(BF16)
    for j in range(NSA_WIDTH // (2 * LANES)):
        for i, t in enumerate(slab_pair(C_NQ + 2 * j * LANES)):
            c = (2 * j + i) * LANES
            nq_ref[0, :, c:c + LANES] = (_rope128(t, cos, sin) * QK_SCALE).astype(BF16)

    kc, vc = slab_pair(C_KC)
    for kind, slab in enumerate((_rope128(kc, cos, sin), vc)):
        for g in range(NSA_KV_HEADS):
            cmp_ref[kind, 0, g] = slab[:, g * HEAD_DIM:(g + 1) * HEAD_DIM]
    pos = pl.program_id(1) * tm + lax.broadcasted_iota(jnp.int32, (tm, LANES), 0)
    onehot = jnp.where(pos // SLC_BLOCK == lane, 1.0, 0.0).astype(BF16)
    ks, vs = slab_pair(C_KS)
    for g, k2 in enumerate(doubled(_rope128(ks, cos, sin).astype(BF16))):
        ks_ref[0, g, :, 0:LANES] = k2
        ks_ref[0, g, :, LANES:2 * LANES] = onehot
    vs_ref[0, 0], vs_ref[0, 1] = with_ones(vs.astype(BF16))
    kw, vw = slab_pair(C_KW)
    kw_ref[0, 0], kw_ref[0, 1] = doubled(_rope128(kw, cos, sin).astype(BF16))
    vw_ref[0, 0], vw_ref[0, 1] = doubled(vw.astype(BF16))
    nz_ref[0] = _silu(mm(C_NZ, NSA_WIDTH)).astype(BF16)
    misc = mm(C_MISC, LANES)
    misc_ref[0] = misc
    z = misc + bf_ref[...]
    log_f = jnp.minimum(z, 0.0) - jnp.log1p(jnp.exp(-jnp.abs(z)))
    lf_ref[0] = log_f.T[0:FOX_HEADS, :]


def _proj(x, mod, g_pre, w_cat, cos128, sin128, bf128):
    b, s, d = x.shape
    tm = PROJ_ROWS
    row = lambda bi, i: (bi, i, 0)
    wide = lambda dt: jax.ShapeDtypeStruct((b, s, FOX_WIDTH), dt)
    g = NSA_KV_HEADS
    grouped = lambda lanes: jax.ShapeDtypeStruct((b, g, s, lanes), BF16)
    grouped_spec = lambda lanes: pl.BlockSpec((1, g, tm, lanes), lambda bi, i: (bi, 0, i, 0))
    out_shape = (wide(BF16), wide(BF16), jax.ShapeDtypeStruct((b, FOX_HEADS, s, LANES), BF16),
                 wide(BF16), wide(BF16),
                 jax.ShapeDtypeStruct((2, b, g, s, HEAD_DIM), F32),
                 grouped(2 * LANES), grouped(LANES), grouped(LANES), grouped(LANES),
                 wide(BF16), jax.ShapeDtypeStruct((b, s, LANES), F32),
                 jax.ShapeDtypeStruct((b, FOX_HEADS, s), F32))
    wide_spec = pl.BlockSpec((1, tm, FOX_WIDTH), row)
    out_specs = (wide_spec, wide_spec,
                 pl.BlockSpec((1, FOX_HEADS, tm, LANES), lambda bi, i: (bi, 0, i, 0)),
                 wide_spec, wide_spec) + (
        pl.BlockSpec((2, 1, g, tm, HEAD_DIM), lambda bi, i: (0, bi, 0, i, 0)),
        grouped_spec(2 * LANES), grouped_spec(LANES), grouped_spec(LANES), grouped_spec(LANES),
        wide_spec, pl.BlockSpec((1, tm, LANES), row),
        pl.BlockSpec((1, FOX_HEADS, tm), lambda bi, i: (bi, 0, i)))
    pipelined = (tm * d * 4 + tm * (7 * FOX_WIDTH * 2 + 2 * g * LANES * 4 + 5 * g * LANES * 2 + LANES * 4)
                 + d * PROJ_COLS * 2)
    return pl.pallas_call(
        _proj_kernel,
        grid=(b, s // tm),
        in_specs=[pl.BlockSpec((1, tm, d), row),
                  pl.BlockSpec((1, 3, d), lambda bi, i: (bi, 0, 0)),
                  pl.BlockSpec((1, d), lambda bi, i: (0, 0)),
                  pl.BlockSpec((d, PROJ_COLS), lambda bi, i: (0, 0)),
                  pl.BlockSpec((tm, LANES), lambda bi, i: (i, 0)),
                  pl.BlockSpec((tm, LANES), lambda bi, i: (i, 0)),
                  pl.BlockSpec((1, LANES), lambda bi, i: (0, 0))],
        out_specs=out_specs,
        out_shape=out_shape,
        compiler_params=pltpu.CompilerParams(
            dimension_semantics=("arbitrary", "arbitrary"),
            vmem_limit_bytes=_vmem_limit(pipelined, 8 * 1024 * 1024)),
        name="proj",
    )(x, mod, g_pre, w_cat, cos128, sin128, bf128)


def _cumsum_kernel(x_ref, hi_ref, mid_ref, lo_ref, *, chunks):
    x = x_ref[...]
    n = x.shape[0]
    parts = _split3(x)
    r = lax.broadcasted_iota(jnp.int32, (LANES, LANES), 0)
    c = lax.broadcasted_iota(jnp.int32, (LANES, LANES), 1)
    tri = (r <= c).astype(BF16)
    rr = lax.broadcasted_iota(jnp.int32, (n, n), 0)
    cc = lax.broadcasted_iota(jnp.int32, (n, n), 1)
    earlier = jnp.logical_and(cc < rr, (cc // chunks) == (rr // chunks)).astype(BF16)
    within = sum(jnp.dot(p, tri, preferred_element_type=F32) for p in parts)
    before = sum(jnp.dot(earlier, p, preferred_element_type=F32) for p in parts)
    total = (within + jnp.sum(before, axis=-1, keepdims=True)) * LOG2E
    hi_ref[...], mid_ref[...], lo_ref[...] = _split3(total)


def _cumsum_lanes_split(v):
    rows, s = v.shape
    chunks = s // LANES
    n = rows * chunks
    part = jax.ShapeDtypeStruct((n, LANES), BF16)
    parts = pl.pallas_call(
        functools.partial(_cumsum_kernel, chunks=chunks),
        out_shape=(part, part, part),
        compiler_params=pltpu.CompilerParams(
            vmem_limit_bytes=_vmem_limit(2 * n * LANES * 4, 6 * n * n)),
        name="cumsum",
    )(v.reshape(n, LANES))
    return tuple(p.reshape(rows, s) for p in parts)


def _flash_scratch(rows, tile, acc_lanes):
    return ([pltpu.VMEM((rows, LANES), F32), pltpu.VMEM((rows, acc_lanes), F32)]
            + [pltpu.VMEM((rows, tile), F32)] * 2
            + [pltpu.VMEM((rows, tile), BF16)] * 2
            + [pltpu.VMEM((rows, LANES), F32)] * 2)


def _flash_scratch_bytes(rows, tile, acc_lanes):
    return rows * (LANES * 4 + acc_lanes * 4 + 2 * tile * 4 + 2 * tile * 2 + 2 * LANES * 4)


def _causal_flash(n_full, tile, q_ref, load_k, pv, causal_mask, prologue, epilogue, scratch):
    m_ref, acc_ref, s0, s1, p0, p1, a0, a1 = scratch
    s_bufs, p_bufs, a_bufs = (s0, s1), (p0, p1), (a0, a1)

    def logits(j):
        return _dot_nt(q_ref[...], load_k(pl.multiple_of(j * tile, tile)))

    def softmax(s):
        m_prev = m_ref[...]
        m_next = jnp.maximum(m_prev, jnp.max(s, axis=1, keepdims=True))
        m_ref[...] = m_next
        p = jnp.exp2(s - jnp.tile(m_next, (1, tile // LANES)))
        return p.astype(BF16), jnp.exp2(m_prev - m_next)

    def accumulate(p, alpha, j):
        acc_ref[...] = acc_ref[...] * alpha + pv(p, pl.multiple_of(j * tile, tile))

    def stage(j, cur):
        s_bufs[1 - cur][...] = logits(j + 1)
        p_bufs[cur][...], a_bufs[cur][...] = softmax(s_bufs[cur][...])
        accumulate(p_bufs[1 - cur][...], a_bufs[1 - cur][...], j - 1)

    def run_stages(first, count):
        for i in range(count):
            stage(first + i, (1 + i) % 2)

    def finish(cur):
        p, alpha = softmax(causal_mask(s_bufs[cur][...], n_full * tile))
        accumulate(p_bufs[1 - cur][...], a_bufs[1 - cur][...], n_full - 1)
        accumulate(p, alpha, n_full)

    def start():
        prologue()
        m_ref[...] = jnp.full(m_ref.shape, MASK_VALUE, F32)
        acc_ref[...] = jnp.zeros(acc_ref.shape, F32)

    @pl.when(n_full == 0)
    def _():
        start()
        p, alpha = softmax(causal_mask(logits(0), 0))
        accumulate(p, alpha, 0)
        epilogue()

    @pl.when(n_full > 0)
    def _():
        start()
        s_first = logits(0)
        s_bufs[1][...] = logits(1)
        p_bufs[0][...], a_bufs[0][...] = softmax(s_first)
        later = n_full - 1

        def unrolled(jj, carry):
            run_stages(1 + FLASH_UNROLL * jj, FLASH_UNROLL)
            return carry

        lax.fori_loop(0, later // FLASH_UNROLL, unrolled, 0)
        rest = later % FLASH_UNROLL

        def tail(count):
            run_stages(n_full - count, count)
            finish((1 + count) % 2)
            epilogue()

        for count in range(FLASH_UNROLL):
            pl.when(rest == count)(functools.partial(tail, count))


def _fox_aug(cum_split):
    b, h, s = cum_split[0].shape
    terms = jnp.concatenate(list(cum_split) + [jnp.ones((b, 1, s), BF16)], axis=1)
    terms = jnp.transpose(terms, (0, 2, 1))
    place_q = np.zeros((3 * h + 1, h * HEAD_DIM), np.float32)
    place_k = np.zeros((3 * h + 1, h * HEAD_DIM), np.float32)
    half = FOX_AUG // 2
    for head in range(h):
        for part in range(half):
            place_q[part * h + head, head * HEAD_DIM + part] = 1.0
            place_q[3 * h, head * HEAD_DIM + half + part] = 1.0
            place_k[3 * h, head * HEAD_DIM + part] = 1.0
            place_k[part * h + head, head * HEAD_DIM + half + part] = -1.0
    scatter = lambda place: jnp.einsum("bsr,rl->bsl", terms, jnp.asarray(place, BF16),
                                       preferred_element_type=F32).astype(BF16)
    return scatter(place_q), scatter(place_k)


def _fox_kernel(q_ref, qa_ref, k_ref, ka_ref, v_ref, z_ref, o_ref, qm_ref, *flash):
    t = FOX_TILE
    qi = pl.program_id(2)

    def prologue():
        q = q_ref[0]
        qa = qa_ref[0]
        lane = lax.broadcasted_iota(jnp.int32, q.shape, 1)
        for hh in range(2):
            r = slice(hh * t, (hh + 1) * t)
            own = (lane < HEAD_DIM) if hh == 0 else (lane >= HEAD_DIM)
            qm_ref[r, 0:LANES] = jnp.where(own, q, jnp.zeros_like(q))
            qm_ref[r, LANES:2 * LANES] = jnp.where(own, qa, jnp.zeros_like(qa))

    def load_k(k0):
        return jnp.concatenate([k_ref[0, pl.ds(k0, t), :], ka_ref[0, pl.ds(k0, t), :]], axis=1)

    def pv(p, k0):
        return jnp.concatenate(
            [jnp.dot(p[hh * t:(hh + 1) * t], v_ref[0, hh, pl.ds(k0, t), :], preferred_element_type=F32)
             for hh in range(2)], axis=0)

    def causal_mask(s, k0):
        t_row = qi * t + (lax.broadcasted_iota(jnp.int32, s.shape, 0) & (t - 1))
        pos = k0 + lax.broadcasted_iota(jnp.int32, s.shape, 1)
        return jnp.where(pos <= t_row, s, MASK_VALUE)

    def epilogue():
        acc_ref = flash[1]
        o0 = acc_ref[0:t, :] * (1.0 / acc_ref[0:t, HEAD_DIM:HEAD_DIM + 1])
        o1 = acc_ref[t:2 * t, :] * (1.0 / acc_ref[t:2 * t, HEAD_DIM:HEAD_DIM + 1])
        lane = lax.broadcasted_iota(jnp.int32, o0.shape, 1)
        o = jnp.where(lane < HEAD_DIM, o0, pltpu.roll(o1, HEAD_DIM, 1))
        o_ref[0] = (o * z_ref[0].astype(F32)).astype(BF16)

    _causal_flash(qi, t, qm_ref, load_k, pv, causal_mask, prologue, epilogue, flash)


def _fox(fq, fk, fv, cum_split, fz):
    b, s, w = fq.shape
    t = FOX_TILE
    pairs = w // LANES
    qa, ka = _fox_aug(cum_split)
    tile = pl.BlockSpec((1, t, LANES), lambda bi, hp, i: (bi, i, hp))
    full = pl.BlockSpec((1, s, LANES), lambda bi, hp, i: (bi, 0, hp))
    resident = 4 * s * LANES * 2
    scratch = 2 * t * 2 * LANES * 2 + _flash_scratch_bytes(2 * t, t, LANES)
    return pl.pallas_call(
        _fox_kernel,
        grid=(b, pairs, s // t),
        in_specs=[tile, tile, full, full,
                  pl.BlockSpec((1, 2, s, LANES), lambda bi, hp, i: (bi, hp, 0, 0)),
                  tile],
        out_specs=tile,
        out_shape=jax.ShapeDtypeStruct((b, s, w), BF16),
        scratch_shapes=[pltpu.VMEM((2 * t, 2 * LANES), BF16)] + _flash_scratch(2 * t, t, LANES),
        compiler_params=pltpu.CompilerParams(
            dimension_semantics=("arbitrary", "arbitrary", "arbitrary"),
            vmem_limit_bytes=_vmem_limit(resident + 4 * t * LANES * 2, scratch + 4 * 2 * t * t * 4)),
        name="fox",
    )(fq, qa, fk, ka, fv, fz)


def _compress_kernel(t_ref, w1_ref, w2_ref, pe_ref, hi_ref, lo_ref):
    hp = lax.Precision.HIGHEST
    half = CMP_STRIDE * HEAD_DIM
    tt = t_ref[0, 0, 0]
    n = tt.shape[0]
    first = jnp.dot(tt, w1_ref[0, 0:half, :], precision=hp, preferred_element_type=F32)
    second = jnp.dot(tt, w1_ref[0, half:2 * half, :], precision=hp, preferred_element_type=F32)
    pe_term = jnp.dot(pe_ref[0], w1_ref[0], precision=hp, preferred_element_type=F32)[0:1, :]
    hidden = first + pltpu.roll(second, n - 1, 0) + pe_term
    out = jnp.dot(_silu(hidden), w2_ref[0], precision=hp, preferred_element_type=F32)
    hi = out.astype(BF16)
    hi_ref[0, 0, 0] = hi
    lo_ref[0, 0, 0] = (out - hi.astype(F32)).astype(BF16)


def _compress(t, w1, w2, pe):
    kinds, b, g, n, flat = t.shape
    width = w2.shape[-1]
    out_spec = pl.BlockSpec((1, 1, 1, n, width), lambda a, bi, gi: (a, bi, gi, 0, 0))
    out_part = jax.ShapeDtypeStruct((kinds, b, g, n, width), BF16)
    return pl.pallas_call(
        _compress_kernel,
        grid=(kinds, b, g),
        in_specs=[pl.BlockSpec((1, 1, 1, n, flat), lambda a, bi, gi: (a, bi, gi, 0, 0)),
                  pl.BlockSpec((1, 2 * flat, CMP_HIDDEN), lambda a, bi, gi: (a, 0, 0)),
                  pl.BlockSpec((1, CMP_HIDDEN, width), lambda a, bi, gi: (a, 0, 0)),
                  pl.BlockSpec((1, SUBLANES, 2 * flat), lambda a, bi, gi: (a, 0, 0))],
        out_specs=(out_spec, out_spec),
        out_shape=(out_part, out_part),
        compiler_params=pltpu.CompilerParams(
            dimension_semantics=("arbitrary", "arbitrary", "arbitrary"),
            vmem_limit_bytes=_vmem_limit(n * flat * 4 + 2 * flat * CMP_HIDDEN * 4, 8 * 1024 * 1024)),
        name="compress",
    )(t, w1, w2, pe)


def _softmax_numerator(s, bias):
    s = s + bias
    m = jnp.max(s, axis=1, keepdims=True)
    m = jnp.where(m == -jnp.inf, 0.0, m)
    return jnp.exp2(s - m)


def _stack_heads(q):
    tq = q.shape[0]
    lane = lax.broadcasted_iota(jnp.int32, (tq, LANES), 1)
    rows = []
    for h in range(NSA_GROUP):
        slab = q[:, (h // 2) * LANES:(h // 2 + 1) * LANES]
        own = (lane < HEAD_DIM) if h % 2 == 0 else (lane >= HEAD_DIM)
        rows.append(jnp.where(own, slab, jnp.zeros_like(slab)))
    return jnp.concatenate(rows, axis=0)


def _group_gates(misc, gi):
    gates = _sigmoid(misc)
    return jnp.where(gi == 0, gates, pltpu.roll(gates, LANES - 3 * NSA_GROUP, 1))


def _gate_lanes(gates, h, j):
    c = MISC_GATE0 + 3 * h + j
    return jnp.broadcast_to(gates[:, c:c + 1], gates.shape)


def _pair_slabs(per_head):
    lane = lax.broadcasted_iota(jnp.int32, per_head[0].shape, 1)
    return [jnp.where(lane < HEAD_DIM, per_head[2 * pp], per_head[2 * pp + 1])
            for pp in range(NSA_GROUP // 2)]


def _cmpwin_kernel(q_ref, kc_hi_ref, kc_lo_ref, vc_ref, overlap_ref, kw_ref, vw_ref, misc_ref,
                   part_ref, selb_ref):
    gi = pl.program_id(1)
    q0 = pl.multiple_of(pl.program_id(2) * NSA_Q, NSA_Q)
    quarter = kc_hi_ref.shape[3] // CMP_QUARTERS
    reach = q0 // (quarter * CMP_STRIDE)
    refs = (q_ref, kc_hi_ref, kc_lo_ref, vc_ref, overlap_ref, kw_ref, vw_ref, misc_ref,
            part_ref, selb_ref)
    for quarters in range(1, CMP_QUARTERS + 1):
        pl.when(reach == quarters - 1)(
            functools.partial(_cmpwin_body, refs, gi, q0, quarters * quarter))


def _cmpwin_body(refs, gi, q0, n_cmp):
    (q_ref, kc_hi_ref, kc_lo_ref, vc_ref, overlap_ref, kw_ref, vw_ref, misc_ref,
     part_ref, selb_ref) = refs
    tq = NSA_Q
    q4 = _stack_heads(q_ref[0])
    n_blk = n_cmp * CMP_STRIDE // SLC_BLOCK
    head_rows = [slice(h * tq, (h + 1) * tq) for h in range(NSA_GROUP)]

    s = (_dot_nt(q4, kc_hi_ref[0, 0, 0, 0:n_cmp, :])
         + _dot_nt(q4, kc_lo_ref[0, 0, 0, 0:n_cmp, :]))
    col = lax.broadcasted_iota(jnp.int32, (tq, n_cmp), 1)
    t_row = q0 + lax.broadcasted_iota(jnp.int32, (tq, n_cmp), 0)
    cmp_bias = jnp.where(col * CMP_STRIDE + (CMP_BLOCK - 1) <= t_row, 0.0, -jnp.inf)
    pcs = []
    for r in head_rows:
        p = _softmax_numerator(s[r], cmp_bias)
        pcs.append(p * (1.0 / jnp.maximum(jnp.sum(p, axis=1, keepdims=True), 1e-30)))
    oc = jnp.dot(jnp.concatenate(pcs, axis=0).astype(BF16), vc_ref[0, 0, 0, 0:n_cmp, :],
                 preferred_element_type=F32)

    pc_sum = pcs[0] + pcs[1] + pcs[2] + pcs[3]
    overlap = overlap_ref[0:n_blk, 0:n_cmp]
    imp = sum(_dot_nt(overlap, p) for p in _split3(pc_sum))
    blk = lax.broadcasted_iota(jnp.int32, (n_blk, tq), 0)
    cur = (q0 + lax.broadcasted_iota(jnp.int32, (n_blk, tq), 1)) // SLC_BLOCK
    forced = jnp.logical_or(blk == 0, jnp.logical_or(blk == cur, blk == cur - 1))
    imp = jnp.where(forced, jnp.inf, jnp.where(blk > cur, -jnp.inf, imp))

    def pick_one(_, carry):
        rem, sel = carry
        best = jnp.max(rem, axis=0, keepdims=True)
        first = jnp.min(jnp.where(rem == best, blk, n_blk), axis=0, keepdims=True)
        hit = blk == first
        return jnp.where(hit, -jnp.inf, rem), jnp.where(hit, 1.0, sel)

    _, sel = lax.fori_loop(0, min(N_SELECT, n_blk), pick_one, (imp, jnp.zeros_like(imp)),
                           unroll=True)
    all_blk = selb_ref.shape[3]
    if n_blk < all_blk:
        sel = jnp.concatenate([sel, jnp.zeros((all_blk - n_blk, tq), F32)], axis=0)
    selb_ref[0, 0] =jnp.where(sel.T > 0.5, 0.0, MASK_VALUE).astype(BF16)

    start = pl.multiple_of(jnp.maximum(q0 - WINDOW, 0), tq)
    kw = kw_ref[0, 0, pl.ds(start, WIN_KEYS), :]
    vw = vw_ref[0, 0, pl.ds(start, WIN_KEYS), :]
    sw = _dot_nt(q4, kw)
    pos = start + lax.broadcasted_iota(jnp.int32, (tq, WIN_KEYS), 1)
    dist = q0 + lax.broadcasted_iota(jnp.int32, (tq, WIN_KEYS), 0) - pos
    win_bias = jnp.where(jnp.logical_and(dist >= 0, dist < WINDOW), 0.0, -jnp.inf)
    pw = jnp.concatenate([_softmax_numerator(sw[r], win_bias) for r in head_rows],
                         axis=0).astype(BF16)
    ow = jnp.dot(pw, vw, preferred_element_type=F32)
    lw = jnp.dot(pw, jnp.ones((WIN_KEYS, LANES), BF16), preferred_element_type=F32)
    ow = ow * (1.0 / jnp.maximum(lw, 1e-30))

    gates = _group_gates(misc_ref[0], gi)
    gated = [_gate_lanes(gates, h, 0) * oc[r] + _gate_lanes(gates, h, 2) * ow[r]
             for h, r in enumerate(head_rows)]
    for pp, slab in enumerate(_pair_slabs(gated)):
        part_ref[0, :, pp * LANES:(pp + 1) * LANES] = slab


def _block_overlap(n_blk, n_cmp):
    ratio = SLC_BLOCK // CMP_STRIDE
    lo = np.arange(n_blk)[:, None] * ratio - (CMP_BLOCK // CMP_STRIDE - 1)
    i = np.arange(n_cmp)[None, :]
    n_overlap = (SLC_BLOCK + CMP_BLOCK) // CMP_STRIDE - 1
    return ((i >= lo) & (i < lo + n_overlap) & (i < n_cmp - 1)).astype(np.float32)


def _cmpwin(nq, cmp_hi, cmp_lo, kw, vw_ones, misc):
    b, s, w = nq.shape
    g = NSA_KV_HEADS
    tq = NSA_Q
    gw = NSA_GROUP * HEAD_DIM
    n_cmp = cmp_hi.shape[3]
    n_blk = s // SLC_BLOCK
    q_spec = pl.BlockSpec((1, tq, gw), lambda bi, gi, i: (bi, i, gi))
    key_spec = pl.BlockSpec((1, 1, 1, n_cmp, LANES), lambda bi, gi, i: (0, bi, gi, 0, 0))
    val_spec = pl.BlockSpec((1, 1, 1, n_cmp, LANES), lambda bi, gi, i: (1, bi, gi, 0, 0))
    seq_spec = pl.BlockSpec((1, 1, s, LANES), lambda bi, gi, i: (bi, gi, 0, 0))
    resident = 2 * s * LANES * 2 + 3 * n_cmp * LANES * 2 + n_blk * n_cmp * 2
    rows = NSA_GROUP * tq
    return pl.pallas_call(
        _cmpwin_kernel,
        grid=(b, g, s // tq),
        in_specs=[q_spec, key_spec, key_spec, val_spec,
                  pl.BlockSpec((n_blk, n_cmp), lambda bi, gi, i: (0, 0)),
                  seq_spec, seq_spec,
                  pl.BlockSpec((1, tq, LANES), lambda bi, gi, i: (bi, i, 0))],
        out_specs=(q_spec, pl.BlockSpec((1, 1, tq, n_blk), lambda bi, gi, i: (bi, gi, i, 0))),
        out_shape=(jax.ShapeDtypeStruct((b, s, w), F32),
                   jax.ShapeDtypeStruct((b, g, s, n_blk), BF16)),
        compiler_params=pltpu.CompilerParams(
            dimension_semantics=("arbitrary", "arbitrary", "arbitrary"),
            vmem_limit_bytes=_vmem_limit(resident + tq * gw * 8, 12 * rows * WIN_KEYS * 4)),
        name="cmpwin",
    )(nq, cmp_hi, cmp_lo, cmp_hi, jnp.asarray(_block_overlap(n_blk, n_cmp), BF16), kw, vw_ones, misc)


def _sel_kernel(q_ref, k_ref, v_ref, selb_ref, part_ref, misc_ref, z_ref, o_ref,
                qa_ref, gate_ref, *flash):
    tq = SEL_Q
    gi = pl.program_id(1)
    qi = pl.program_id(2)
    q0 = pl.multiple_of(qi * tq, tq)
    n_full = q0 // SEL_KV

    def prologue():
        q = q_ref[0]
        selb = selb_ref[0, 0]
        lane = lax.broadcasted_iota(jnp.int32, (tq, LANES), 1)
        for h in range(NSA_GROUP):
            r = slice(h * tq, (h + 1) * tq)
            slab = q[:, (h // 2) * LANES:(h // 2 + 1) * LANES]
            own = (lane < HEAD_DIM) if h % 2 == 0 else (lane >= HEAD_DIM)
            qa_ref[r, 0:LANES] = jnp.where(own, slab, jnp.zeros_like(slab))
            qa_ref[r, LANES:2 * LANES] = selb
        gates = _group_gates(misc_ref[0], gi)
        for h in range(NSA_GROUP):
            gate_ref[h] = _gate_lanes(gates, h, 1)

    def causal_mask(s, k0):
        t_row = q0 + (lax.broadcasted_iota(jnp.int32, s.shape, 0) & (tq - 1))
        pos = k0 + lax.broadcasted_iota(jnp.int32, s.shape, 1)
        return jnp.where(pos <= t_row, s, MASK_VALUE)

    def epilogue():
        acc_ref = flash[1]
        gated = []
        for h in range(NSA_GROUP):
            acc = acc_ref[h * tq:(h + 1) * tq, :]
            scaled = acc * (gate_ref[h] * (1.0 / acc[:, HEAD_DIM:HEAD_DIM + 1]))
            gated.append(scaled if h % 2 == 0 else pltpu.roll(scaled, HEAD_DIM, 1))
        for pp, slab in enumerate(_pair_slabs(gated)):
            c = slice(pp * LANES, (pp + 1) * LANES)
            o_ref[0, :, c] = ((part_ref[0, :, c] + slab) * z_ref[0, :, c].astype(F32)).astype(BF16)

    _causal_flash(n_full, SEL_KV, qa_ref,
                  lambda k0: k_ref[0, 0, pl.ds(k0, SEL_KV), :],
                  lambda p, k0: jnp.dot(p, v_ref[0, 0, pl.ds(k0, SEL_KV), :],
                                        preferred_element_type=F32),
                  causal_mask, prologue, epilogue, flash)


def _sel(nq, k_aug, v_ones, selb, part, misc, nz):
    b, s, w = nq.shape
    g = NSA_KV_HEADS
    tq = SEL_Q
    gw = NSA_GROUP * HEAD_DIM
    n_blk = s // SLC_BLOCK
    rows = NSA_GROUP * tq
    q_spec = pl.BlockSpec((1, tq, gw), lambda bi, gi, i: (bi, i, gi))
    resident = s * 3 * LANES * 2
    return pl.pallas_call(
        _sel_kernel,
        grid=(b, g, s // tq),
        in_specs=[q_spec,
                  pl.BlockSpec((1, 1, s, 2 * LANES), lambda bi, gi, i: (bi, gi, 0, 0)),
                  pl.BlockSpec((1, 1, s, LANES), lambda bi, gi, i: (bi, gi, 0, 0)),
                  pl.BlockSpec((1, 1, tq, n_blk), lambda bi, gi, i: (bi, gi, i, 0)),
                  q_spec,
                  pl.BlockSpec((1, tq, LANES), lambda bi, gi, i: (bi, i, 0)),
                  q_spec],
        out_specs=q_spec,
        out_shape=jax.ShapeDtypeStruct((b, s, w), BF16),
        scratch_shapes=[pltpu.VMEM((rows, 2 * LANES), BF16),
                        pltpu.VMEM((NSA_GROUP, tq, LANES), F32)] + _flash_scratch(rows, SEL_KV, LANES),
        compiler_params=pltpu.CompilerParams(
            dimension_semantics=("arbitrary", "arbitrary", "arbitrary"),
            vmem_limit_bytes=_vmem_limit(resident + tq * gw * 10,
                                         _flash_scratch_bytes(rows, SEL_KV, LANES) + 4 * rows * SEL_KV * 4)),
        name="sel",
    )(nq, k_aug, v_ones, selb, part, misc, nz)


def _out_kernel(yf_ref, yn_ref, w_ref, g_ref, mod_ref, x_ref, o_ref):
    y = (jnp.dot(yf_ref[0], w_ref[0:FOX_WIDTH, :], preferred_element_type=F32)
         + jnp.dot(yn_ref[0], w_ref[FOX_WIDTH:, :], preferred_element_type=F32))
    yn = y * lax.rsqrt(jnp.mean(y * y, axis=-1, keepdims=True) + RMS_EPS)
    o_ref[0] = x_ref[0] + mod_ref[0, 2:3, :] * (yn * g_ref[...])


def _out(y_fox, y_nsa, w_out, g_post, mod, x):
    b, s, d = x.shape
    tm = PROJ_ROWS
    row = lambda bi, i: (bi, i, 0)
    half = pl.BlockSpec((1, tm, FOX_WIDTH), row)
    pipelined = 2 * tm * FOX_WIDTH * 2 + 2 * tm * d * 4 + d * d * 2
    return pl.pallas_call(
        _out_kernel,
        grid=(b, s // tm),
        in_specs=[half, half,
                  pl.BlockSpec((d, d), lambda bi, i: (0, 0)),
                  pl.BlockSpec((1, d), lambda bi, i: (0, 0)),
                  pl.BlockSpec((1, 3, d), lambda bi, i: (bi, 0, 0)),
                  pl.BlockSpec((1, tm, d), row)],
        out_specs=pl.BlockSpec((1, tm, d), row),
        out_shape=jax.ShapeDtypeStruct((b, s, d), F32),
        compiler_params=pltpu.CompilerParams(
            dimension_semantics=("arbitrary", "arbitrary"),
            vmem_limit_bytes=_vmem_limit(pipelined, 6 * tm * d * 4)),
        name="out",
    )(y_fox, y_nsa, w_out, g_post, mod, x)


def _rope_slabs(seq_len):
    inv = 1.0 / (ROPE_THETA ** (jnp.arange(0, HEAD_DIM, 2, dtype=F32) / HEAD_DIM))
    ang = jnp.arange(seq_len, dtype=F32)[:, None] * inv[None, :]
    cos, sin = jnp.cos(ang), jnp.sin(ang)
    reps = LANES // (HEAD_DIM // 2)
    sign = jnp.tile(jnp.concatenate([-jnp.ones((HEAD_DIM // 2,), F32), jnp.ones((HEAD_DIM // 2,), F32)]),
                    LANES // HEAD_DIM)
    return jnp.tile(cos, (1, reps)), jnp.tile(sin, (1, reps)) * sign[None, :]


def _reorder_w_in(w_in):
    fw, kv = FOX_WIDTH, NSA_KV_WIDTH
    o = 0
    cols = {}
    for name, n in (("fq", fw), ("fk", fw), ("fv", fw), ("ff", FOX_HEADS), ("fz", fw), ("nq", NSA_WIDTH),
                    ("kc", kv), ("vc", kv), ("ks", kv), ("vs", kv), ("kw", kv), ("vw", kv),
                    ("ng", 3 * NSA_HEADS), ("nz", NSA_WIDTH)):
        cols[name] = w_in[:, o:o + n]
        o += n
    pad = jnp.zeros((w_in.shape[0], LANES - FOX_HEADS - 3 * NSA_HEADS), w_in.dtype)
    order = ("fq", "fk", "fv", "fz", "nq", "kc", "vc", "ks", "vs", "kw", "vw", "nz", "ff", "ng")
    return jnp.concatenate([cols[k] for k in order] + [pad], axis=1).astype(BF16)


def _layer(x, c8, g_pre, g_post, w_ada, b_ada, w_in, b_forget, w_cmp_k1, w_cmp_k2,
           w_cmp_v1, w_cmp_v2, pe_cmp_k, pe_cmp_v, w_out, cos128, sin128):
    b, s, d = x.shape
    mod = _ada(c8, w_ada, b_ada)[:b].reshape(b, 3, d)
    bf128 = jnp.pad(b_forget, (0, LANES - FOX_HEADS)).reshape(1, LANES)
    (fq, fk, fv, fz, nq, cmp_in, ks_aug, vs_ones, kw2, vw_ones, nz, misc, log_f) = _proj(
        x, mod, g_pre.reshape(1, d), _reorder_w_in(w_in), cos128, sin128, bf128)

    cum_split = tuple(p.reshape(b, FOX_HEADS, s)
                      for p in _cumsum_lanes_split(log_f.reshape(b * FOX_HEADS, s)))
    y_fox = _fox(fq, fk, fv, cum_split, fz)

    flat = CMP_STRIDE * HEAD_DIM
    t = cmp_in.reshape(2, b, NSA_KV_HEADS, s // CMP_STRIDE, flat)
    pe = jnp.stack([pe_cmp_k.reshape(1, 2 * flat), pe_cmp_v.reshape(1, 2 * flat)])
    w2 = jnp.stack([w_cmp_k2, w_cmp_v2])
    cmp_hi, cmp_lo = _compress(t, jnp.stack([w_cmp_k1, w_cmp_v1]), jnp.concatenate([w2, w2], axis=-1),
                               jnp.broadcast_to(pe, (2, SUBLANES, 2 * flat)))
    part, selb = _cmpwin(nq, cmp_hi, cmp_lo, kw2, vw_ones, misc)
    y_nsa = _sel(nq, ks_aug, vs_ones, selb, part, misc, nz)

    return _out(y_fox, y_nsa, w_out.astype(BF16), g_post.reshape(1, d), mod, x)


def kernel(x, c, g_pre, g_post, w_ada, b_ada, w_in, b_forget, w_cmp_k1, w_cmp_k2,
           w_cmp_v1, w_cmp_v2, pe_cmp_k, pe_cmp_v, w_out):
    cos128, sin128 = _rope_slabs(x.shape[1])
    c8 = jnp.pad(c, ((0, SUBLANES - c.shape[0]), (0, 0)))
    for layer in range(g_pre.shape[0]):
        x = _layer(x, c8, g_pre[layer], g_post[layer], w_ada[layer], b_ada[layer], w_in[layer],
                   b_forget[layer], w_cmp_k1[layer], w_cmp_k2[layer], w_cmp_v1[layer],
                   w_cmp_v2[layer], pe_cmp_k[layer], pe_cmp_v[layer], w_out[layer], cos128, sin128)
    return x
```

```python
import functools

import jax
import jax.numpy as jnp
import numpy as np
from jax import lax
from jax.experimental import pallas as pl
from jax.experimental.pallas import tpu as pltpu

F32 = jnp.float32
BF16 = jnp.bfloat16

D_MODEL = 1024
HEAD_DIM = 64
FOX_WIDTH = 512
NSA_WIDTH = 512
FOX_HEADS = 8
NSA_HEADS = 8
NSA_KV_HEADS = 2
NSA_GROUP = 4
NSA_KV_WIDTH = 128
CMP_BLOCK = 32
CMP_STRIDE = 16
CMP_HIDDEN = 128
SLC_BLOCK = 64
N_SELECT = 16
N_FORCED = 3
WINDOW = 512
ROPE_THETA = 10000.0
RMS_EPS = 1e-6
LOG2E = 1.4426950408889634
QK_SCALE = HEAD_DIM ** -0.5 * LOG2E

LANES = 128
SUBLANES = 8
V7X_VMEM_BYTES = 64 * 1024 * 1024
MASK_VALUE = -1e30

PROJ_ROWS = 512
FOX_TILE = 512
NSA_Q = 256
SEL_Q = 128
SEL_KV = 512
WIN_KEYS = WINDOW + NSA_Q
FLASH_UNROLL = 6
CMP_QUARTERS = 4
FOX_AUG = 6

C_FQ, C_FK, C_FV, C_FZ, C_NQ = 0, 512, 1024, 1536, 2048
C_KC, C_VC, C_KS, C_VS, C_KW, C_VW = 2560, 2688, 2816, 2944, 3072, 3200
C_NZ, C_MISC, PROJ_COLS = 3328, 3840, 3968
MISC_GATE0 = FOX_HEADS


def _vmem_limit(pipelined_bytes, resident_bytes):
    need = 2 * pipelined_bytes + resident_bytes
    return int(min(max(need, 16 * 1024 * 1024), V7X_VMEM_BYTES - 8 * 1024 * 1024))


def _sigmoid(v):
    return 1.0 / (1.0 + jnp.exp(-v))


def _silu(v):
    return v * _sigmoid(v)


def _dot_nt(a, b):
    return lax.dot_general(a, b, (((1,), (1,)), ((), ())), preferred_element_type=F32)


def _split3(v):
    hi = v.astype(BF16)
    r1 = v - hi.astype(F32)
    mid = r1.astype(BF16)
    lo = (r1 - mid.astype(F32)).astype(BF16)
    return hi, mid, lo


def _ada_kernel(c_ref, w_ref, b_ref, o_ref):
    a = _silu(c_ref[...])
    o_ref[...] = jnp.dot(a, w_ref[...], precision=lax.Precision.HIGHEST,
                         preferred_element_type=F32) + b_ref[...]


def _ada(c8, w_ada, b_ada):
    n = w_ada.shape[1]
    blk = D_MODEL
    return pl.pallas_call(
        _ada_kernel,
        grid=(n // blk,),
        in_specs=[pl.BlockSpec((SUBLANES, D_MODEL), lambda j: (0, 0)),
                  pl.BlockSpec((D_MODEL, blk), lambda j: (0, j)),
                  pl.BlockSpec((1, blk), lambda j: (0, j))],
        out_specs=pl.BlockSpec((SUBLANES, blk), lambda j: (0, j)),
        out_shape=jax.ShapeDtypeStruct((SUBLANES, n), F32),
        compiler_params=pltpu.CompilerParams(
            dimension_semantics=("arbitrary",),
            vmem_limit_bytes=_vmem_limit(D_MODEL * blk * 4, 4 * 1024 * 1024)),
        name="ada",
    )(c8, w_ada, b_ada.reshape(1, n))


def _rope128(t, cos, sin_signed):
    lane = lax.broadcasted_iota(jnp.int32, t.shape, 1)
    first_half = (lane & (HEAD_DIM - 1)) < HEAD_DIM // 2
    partner = jnp.where(first_half,
                        pltpu.roll(t, LANES - HEAD_DIM // 2, 1),
                        pltpu.roll(t, HEAD_DIM // 2, 1))
    return t * cos + partner * sin_signed


def _proj_kernel(x_ref, mod_ref, g_ref, w_ref, cos_ref, sin_ref, bf_ref,
                 fq_ref, fk_ref, fv_ref, fz_ref, nq_ref, cmp_ref,
                 ks_ref, vs_ref, kw_ref, vw_ref, nz_ref, misc_ref, lf_ref):
    tm = x_ref.shape[1]
    x = x_ref[0]
    y = x * lax.rsqrt(jnp.mean(x * x, axis=-1, keepdims=True) + RMS_EPS)
    y = y * g_ref[...]
    h = (y * (1.0 + mod_ref[0, 1:2, :]) + mod_ref[0, 0:1, :]).astype(BF16)
    cos = cos_ref[...]
    sin = sin_ref[...]

    def mm(lo, n):
        return jnp.dot(h, w_ref[:, lo:lo + n], preferred_element_type=F32)

    def slab_pair(lo):
        both = mm(lo, 2 * LANES)
        return both[:, 0:LANES], both[:, LANES:2 * LANES]

    lane = lax.broadcasted_iota(jnp.int32, (tm, LANES), 1)
    low = lane < HEAD_DIM
    ones_col = jnp.where(lane == HEAD_DIM, 1.0, 0.0).astype(BF16)

    def doubled(slab):
        swapped = pltpu.roll(slab, HEAD_DIM, 1)
        return jnp.where(low, slab, swapped), jnp.where(low, swapped, slab)

    def with_ones(slab):
        swapped = pltpu.roll(slab, HEAD_DIM, 1)
        return jnp.where(low, slab, ones_col), jnp.where(low, swapped, ones_col)

    fq_ref[0] = (mm(C_FQ, FOX_WIDTH) * QK_SCALE).astype(BF16)
    fk_ref[0] = mm(C_FK, FOX_WIDTH).astype(BF16)
    fv = mm(C_FV, FOX_WIDTH).astype(BF16)
    for pair in range(FOX_WIDTH // LANES):
        fv_ref[0, 2 * pair], fv_ref[0, 2 * pair + 1] = with_ones(fv[:, pair * LANES:(pair + 1) * LANES])
    fz_ref[0] = _silu(mm(C_FZ, FOX_WIDTH)).astype(BF16)
    for j in range(NSA_WIDTH // (2 * LANES)):
        for i, t in enumerate(slab_pair(C_NQ + 2 * j * LANES)):
            c = (2 * j + i) * LANES
            nq_ref[0, :, c:c + LANES] = (_rope128(t, cos, sin) * QK_SCALE).astype(BF16)

    kc, vc = slab_pair(C_KC)
    for kind, slab in enumerate((_rope128(kc, cos, sin), vc)):
        for g in range(NSA_KV_HEADS):
            cmp_ref[kind, 0, g] = slab[:, g * HEAD_DIM:(g + 1) * HEAD_DIM]
    pos = pl.program_id(1) * tm + lax.broadcasted_iota(jnp.int32, (tm, LANES), 0)
    onehot = jnp.where(pos // SLC_BLOCK == lane, 1.0, 0.0).astype(BF16)
    ks, vs = slab_pair(C_KS)
    for g, k2 in enumerate(doubled(_rope128(ks, cos, sin).astype(BF16))):
        ks_ref[0, g, :, 0:LANES] = k2
        ks_ref[0, g, :, LANES:2 * LANES] = onehot
    vs_ref[0, 0], vs_ref[0, 1] = with_ones(vs.astype(BF16))
    kw, vw = slab_pair(C_KW)
    kw_ref[0, 0], kw_ref[0, 1] = doubled(_rope128(kw, cos, sin).astype(BF16))
    vw_ref[0, 0], vw_ref[0, 1] = doubled(vw.astype(BF16))
    nz_ref[0] = _silu(mm(C_NZ, NSA_WIDTH)).astype(BF16)
    misc = mm(C_MISC, LANES)
    misc_ref[0] = misc
    z = misc + bf_ref[...]
    log_f = jnp.minimum(z, 0.0) - jnp.log1p(jnp.exp(-jnp.abs(z)))
    lf_ref[0] = log_f.T[0:FOX_HEADS, :]


def _proj(x, mod, g_pre, w_cat, cos128, sin128, bf128):
    b, s, d = x.shape
    tm = PROJ_ROWS
    row = lambda bi, i: (bi, i, 0)
    wide = lambda dt: jax.ShapeDtypeStruct((b, s, FOX_WIDTH), dt)
    g = NSA_KV_HEADS
    grouped = lambda lanes: jax.ShapeDtypeStruct((b, g, s, lanes), BF16)
    grouped_spec = lambda lanes: pl.BlockSpec((1, g, tm, lanes), lambda bi, i: (bi, 0, i, 0))
    out_shape = (wide(BF16), wide(BF16), jax.ShapeDtypeStruct((b, FOX_HEADS, s, LANES), BF16),
                 wide(BF16), wide(BF16),
                 jax.ShapeDtypeStruct((2, b, g, s, HEAD_DIM), F32),
                 grouped(2 * LANES), grouped(LANES), grouped(LANES), grouped(LANES),
                 wide(BF16), jax.ShapeDtypeStruct((b, s, LANES), F32),
                 jax.ShapeDtypeStruct((b, FOX_HEADS, s), F32))
    wide_spec = pl.BlockSpec((1, tm, FOX_WIDTH), row)
    out_specs = (wide_spec, wide_spec,
                 pl.BlockSpec((1, FOX_HEADS, tm, LANES), lambda bi, i: (bi, 0, i, 0)),
                 wide_spec, wide_spec) + (
        pl.BlockSpec((2, 1, g, tm, HEAD_DIM), lambda bi, i: (0, bi, 0, i, 0)),
        grouped_spec(2 * LANES), grouped_spec(LANES), grouped_spec(LANES), grouped_spec(LANES),
        wide_spec, pl.BlockSpec((1, tm, LANES), row),
        pl.BlockSpec((1, FOX_HEADS, tm), lambda bi, i: (bi, 0, i)))
    pipelined = (tm * d * 4 + tm * (7 * FOX_WIDTH * 2 + 2 * g * LANES * 4 + 5 * g * LANES * 2 + LANES * 4)
                 + d * PROJ_COLS * 2)
    return pl.pallas_call(
        _proj_kernel,
        grid=(b, s // tm),
        in_specs=[pl.BlockSpec((1, tm, d), row),
                  pl.BlockSpec((1, 3, d), lambda bi, i: (bi, 0, 0)),
                  pl.BlockSpec((1, d), lambda bi, i: (0, 0)),
                  pl.BlockSpec((d, PROJ_COLS), lambda bi, i: (0, 0)),
                  pl.BlockSpec((tm, LANES), lambda bi, i: (i, 0)),
                  pl.BlockSpec((tm, LANES), lambda bi, i: (i, 0)),
                  pl.BlockSpec((1, LANES), lambda bi, i: (0, 0))],
        out_specs=out_specs,
        out_shape=out_shape,
        compiler_params=pltpu.CompilerParams(
            dimension_semantics=("arbitrary", "arbitrary"),
            vmem_limit_bytes=_vmem_limit(pipelined, 8 * 1024 * 1024)),
        name="proj",
    )(x, mod, g_pre, w_cat, cos128, sin128, bf128)


def _cumsum_kernel(x_ref, hi_ref, mid_ref, lo_ref, *, chunks):
    x = x_ref[...]
    n = x.shape[0]
    parts = _split3(x)
    r = lax.broadcasted_iota(jnp.int32, (LANES, LANES), 0)
    c = lax.broadcasted_iota(jnp.int32, (LANES, LANES), 1)
    tri = (r <= c).astype(BF16)
    rr = lax.broadcasted_iota(jnp.int32, (n, n), 0)
    cc = lax.broadcasted_iota(jnp.int32, (n, n), 1)
    earlier = jnp.logical_and(cc < rr, (cc // chunks) == (rr // chunks)).astype(BF16)
    within = sum(jnp.dot(p, tri, preferred_element_type=F32) for p in parts)
    before = sum(jnp.dot(earlier, p, preferred_element_type=F32) for p in parts)
    total = (within + jnp.sum(before, axis=-1, keepdims=True)) * LOG2E
    hi_ref[...], mid_ref[...], lo_ref[...] = _split3(total)


def _cumsum_lanes_split(v):
    rows, s = v.shape
    chunks = s // LANES
    n = rows * chunks
    part = jax.ShapeDtypeStruct((n, LANES), BF16)
    parts = pl.pallas_call(
        functools.partial(_cumsum_kernel, chunks=chunks),
        out_shape=(part, part, part),
        compiler_params=pltpu.CompilerParams(
            vmem_limit_bytes=_vmem_limit(2 * n * LANES * 4, 6 * n * n)),
        name="cumsum",
    )(v.reshape(n, LANES))
    return tuple(p.reshape(rows, s) for p in parts)


def _flash_scratch(rows, tile, acc_lanes):
    return ([pltpu.VMEM((rows, LANES), F32), pltpu.VMEM((rows, acc_lanes), F32)]
            + [pltpu.VMEM((rows, tile), F32)] * 2
            + [pltpu.VMEM((rows, tile), BF16)] * 2
            + [pltpu.VMEM((rows, LANES), F32)] * 2)


def _flash_scratch_bytes(rows, tile, acc_lanes):
    return rows * (LANES * 4 + acc_lanes * 4 + 2 * tile * 4 + 2 * tile * 2 + 2 * LANES * 4)


def _causal_flash(n_full, tile, q_ref, load_k, pv, causal_mask, prologue, epilogue, scratch):
    m_ref, acc_ref, s0, s1, p0, p1, a0, a1 = scratch
    s_bufs, p_bufs, a_bufs = (s0, s1), (p0, p1), (a0, a1)

    def logits(j):
        return _dot_nt(q_ref[...], load_k(pl.multiple_of(j * tile, tile)))

    def softmax(s):
        m_prev = m_ref[...]
        m_next = jnp.maximum(m_prev, jnp.max(s, axis=1, keepdims=True))
        m_ref[...] = m_next
        p = jnp.exp2(s - jnp.tile(m_next, (1, tile // LANES)))
        return p.astype(BF16), jnp.exp2(m_prev - m_next)

    def accumulate(p, alpha, j):
        acc_ref[...] = acc_ref[...] * alpha + pv(p, pl.multiple_of(j * tile, tile))

    def stage(j, cur):
        s_bufs[1 - cur][...] = logits(j + 1)
        p_bufs[cur][...], a_bufs[cur][...] = softmax(s_bufs[cur][...])
        accumulate(p_bufs[1 - cur][...], a_bufs[1 - cur][...], j - 1)

    def run_stages(first, count):
        for i in range(count):
            stage(first + i, (1 + i) % 2)

    def finish(cur):
        p, alpha = softmax(causal_mask(s_bufs[cur][...], n_full * tile))
        accumulate(p_bufs[1 - cur][...], a_bufs[1 - cur][...], n_full - 1)
        accumulate(p, alpha, n_full)

    def start():
        prologue()
        m_ref[...] = jnp.full(m_ref.shape, MASK_VALUE, F32)
        acc_ref[...] = jnp.zeros(acc_ref.shape, F32)

    @pl.when(n_full == 0)
    def _():
        start()
        p, alpha = softmax(causal_mask(logits(0), 0))
        accumulate(p, alpha, 0)
        epilogue()

    @pl.when(n_full > 0)
    def _():
        start()
        s_first = logits(0)
        s_bufs[1][...] = logits(1)
        p_bufs[0][...], a_bufs[0][...] = softmax(s_first)
        later = n_full - 1

        def unrolled(jj, carry):
            run_stages(1 + FLASH_UNROLL * jj, FLASH_UNROLL)
            return carry

        lax.fori_loop(0, later // FLASH_UNROLL, unrolled, 0)
        rest = later % FLASH_UNROLL

        def tail(count):
            run_stages(n_full - count, count)
            finish((1 + count) % 2)
            epilogue()

        for count in range(FLASH_UNROLL):
            pl.when(rest == count)(functools.partial(tail, count))


def _fox_aug(cum_split):
    b, h, s = cum_split[0].shape
    terms = jnp.concatenate(list(cum_split) + [jnp.ones((b, 1, s), BF16)], axis=1)
    terms = jnp.transpose(terms, (0, 2, 1))
    place_q = np.zeros((3 * h + 1, h * HEAD_DIM), np.float32)
    place_k = np.zeros((3 * h + 1, h * HEAD_DIM), np.float32)
    half = FOX_AUG // 2
    for head in range(h):
        for part in range(half):
            place_q[part * h + head, head * HEAD_DIM + part] = 1.0
            place_q[3 * h, head * HEAD_DIM + half + part] = 1.0
            place_k[3 * h, head * HEAD_DIM + part] = 1.0
            place_k[part * h + head, head * HEAD_DIM + half + part] = -1.0
    scatter = lambda place: jnp.einsum("bsr,rl->bsl", terms, jnp.asarray(place, BF16),
                                       preferred_element_type=F32).astype(BF16)
    return scatter(place_q), scatter(place_k)


def _fox_kernel(q_ref, qa_ref, k_ref, ka_ref, v_ref, z_ref, o_ref, qm_ref, *flash):
    t = FOX_TILE
    qi = pl.program_id(2)

    def prologue():
        q = q_ref[0]
        qa = qa_ref[0]
        lane = lax.broadcasted_iota(jnp.int32, q.shape, 1)
        for hh in range(2):
            r = slice(hh * t, (hh + 1) * t)
            own = (lane < HEAD_DIM) if hh == 0 else (lane >= HEAD_DIM)
            qm_ref[r, 0:LANES] = jnp.where(own, q, jnp.zeros_like(q))
            qm_ref[r, LANES:2 * LANES] = jnp.where(own, qa, jnp.zeros_like(qa))

    def load_k(k0):
        return jnp.concatenate([k_ref[0, pl.ds(k0, t), :], ka_ref[0, pl.ds(k0, t), :]], axis=1)

    def pv(p, k0):
        return jnp.concatenate(
            [jnp.dot(p[hh * t:(hh + 1) * t], v_ref[0, hh, pl.ds(k0, t), :], preferred_element_type=F32)
             for hh in range(2)], axis=0)

    def causal_mask(s, k0):
        t_row = qi * t + (lax.broadcasted_iota(jnp.int32, s.shape, 0) & (t - 1))
        pos = k0 + lax.broadcasted_iota(jnp.int32, s.shape, 1)
        return jnp.where(pos <= t_row, s, MASK_VALUE)

    def epilogue():
        acc_ref = flash[1]
        o0 = acc_ref[0:t, :] * (1.0 / acc_ref[0:t, HEAD_DIM:HEAD_DIM + 1])
        o1 = acc_ref[t:2 * t, :] * (1.0 / acc_ref[t:2 * t, HEAD_DIM:HEAD_DIM + 1])
        lane = lax.broadcasted_iota(jnp.int32, o0.shape, 1)
        o = jnp.where(lane < HEAD_DIM, o0, pltpu.roll(o1, HEAD_DIM, 1))
        o_ref[0] = (o * z_ref[0].astype(F32)).astype(BF16)

    _causal_flash(qi, t, qm_ref, load_k, pv, causal_mask, prologue, epilogue, flash)


def _fox(fq, fk, fv, cum_split, fz):
    b, s, w = fq.shape
    t = FOX_TILE
    pairs = w // LANES
    qa, ka = _fox_aug(cum_split)
    tile = pl.BlockSpec((1, t, LANES), lambda bi, hp, i: (bi, i, hp))
    full = pl.BlockSpec((1, s, LANES), lambda bi, hp, i: (bi, 0, hp))
    resident = 4 * s * LANES * 2
    scratch = 2 * t * 2 * LANES * 2 + _flash_scratch_bytes(2 * t, t, LANES)
    return pl.pallas_call(
        _fox_kernel,
        grid=(b, pairs, s // t),
        in_specs=[tile, tile, full, full,
                  pl.BlockSpec((1, 2, s, LANES), lambda bi, hp, i: (bi, hp, 0, 0)),
                  tile],
        out_specs=tile,
        out_shape=jax.ShapeDtypeStruct((b, s, w), BF16),
        scratch_shapes=[pltpu.VMEM((2 * t, 2 * LANES), BF16)] + _flash_scratch(2 * t, t, LANES),
        compiler_params=pltpu.CompilerParams(
            dimension_semantics=("arbitrary", "arbitrary", "arbitrary"),
            vmem_limit_bytes=_vmem_limit(resident + 4 * t * LANES * 2, scratch + 4 * 2 * t * t * 4)),
        name="fox",
    )(fq, qa, fk, ka, fv, fz)


def _compress_kernel(t_ref, w1_ref, w2_ref, pe_ref, hi_ref, lo_ref):
    hp = lax.Precision.HIGHEST
    half = CMP_STRIDE * HEAD_DIM
    tt = t_ref[0, 0, 0]
    n = tt.shape[0]
    first = jnp.dot(tt, w1_ref[0, 0:half, :], precision=hp, preferred_element_type=F32)
    second = jnp.dot(tt, w1_ref[0, half:2 * half, :], precision=hp, preferred_element_type=F32)
    pe_term = jnp.dot(pe_ref[0], w1_ref[0], precision=hp, preferred_element_type=F32)[0:1, :]
    hidden = first + pltpu.roll(second, n - 1, 0) + pe_term
    out = jnp.dot(_silu(hidden), w2_ref[0], precision=hp, preferred_element_type=F32)
    hi = out.astype(BF16)
    hi_ref[0, 0, 0] = hi
    lo_ref[0, 0, 0] = (out - hi.astype(F32)).astype(BF16)


def _compress(t, w1, w2, pe):
    kinds, b, g, n, flat = t.shape
    width = w2.shape[-1]
    out_spec = pl.BlockSpec((1, 1, 1, n, width), lambda a, bi, gi: (a, bi, gi, 0, 0))
    out_part = jax.ShapeDtypeStruct((kinds, b, g, n, width), BF16)
    return pl.pallas_call(
        _compress_kernel,
        grid=(kinds, b, g),
        in_specs=[pl.BlockSpec((1, 1, 1, n, flat), lambda a, bi, gi: (a, bi, gi, 0, 0)),
                  pl.BlockSpec((1, 2 * flat, CMP_HIDDEN), lambda a, bi, gi: (a, 0, 0)),
                  pl.BlockSpec((1, CMP_HIDDEN, width), lambda a, bi, gi: (a, 0, 0)),
                  pl.BlockSpec((1, SUBLANES, 2 * flat), lambda a, bi, gi: (a, 0, 0))],
        out_specs=(out_spec, out_spec),
        out_shape=(out_part, out_part),
        compiler_params=pltpu.CompilerParams(
            dimension_semantics=("arbitrary", "arbitrary", "arbitrary"),
            vmem_limit_bytes=_vmem_limit(n * flat * 4 + 2 * flat * CMP_HIDDEN * 4, 8 * 1024 * 1024)),
        name="compress",
    )(t, w1, w2, pe)


def _softmax_numerator(s, bias):
    s = s + bias
    m = jnp.max(s, axis=1, keepdims=True)
    m = jnp.where(m == -jnp.inf, 0.0, m)
    return jnp.exp2(s - m)


def _stack_heads(q):
    tq = q.shape[0]
    lane = lax.broadcasted_iota(jnp.int32, (tq, LANES), 1)
    rows = []
    for h in range(NSA_GROUP):
        slab = q[:, (h // 2) * LANES:(h // 2 + 1) * LANES]
        own = (lane < HEAD_DIM) if h % 2 == 0 else (lane >= HEAD_DIM)
        rows.append(jnp.where(own, slab, jnp.zeros_like(slab)))
    return jnp.concatenate(rows, axis=0)


def _group_gates(misc, gi):
    gates = _sigmoid(misc)
    return jnp.where(gi == 0, gates, pltpu.roll(gates, LANES - 3 * NSA_GROUP, 1))


def _gate_lanes(gates, h, j):
    c = MISC_GATE0 + 3 * h + j
    return jnp.broadcast_to(gates[:, c:c + 1], gates.shape)


def _pair_slabs(per_head):
    lane = lax.broadcasted_iota(jnp.int32, per_head[0].shape, 1)
    return [jnp.where(lane < HEAD_DIM, per_head[2 * pp], per_head[2 * pp + 1])
            for pp in range(NSA_GROUP // 2)]


def _cmpwin_kernel(q_ref, kc_hi_ref, kc_lo_ref, vc_ref, overlap_ref, kw_ref, vw_ref, misc_ref,
                   part_ref, selb_ref):
    gi = pl.program_id(1)
    q0 = pl.multiple_of(pl.program_id(2) * NSA_Q, NSA_Q)
    quarter = kc_hi_ref.shape[3] // CMP_QUARTERS
    reach = q0 // (quarter * CMP_STRIDE)
    refs = (q_ref, kc_hi_ref, kc_lo_ref, vc_ref, overlap_ref, kw_ref, vw_ref, misc_ref,
            part_ref, selb_ref)
    for quarters in range(1, CMP_QUARTERS + 1):
        pl.when(reach == quarters - 1)(
            functools.partial(_cmpwin_body, refs, gi, q0, quarters * quarter))


def _cmpwin_body(refs, gi, q0, n_cmp):
    (q_ref, kc_hi_ref, kc_lo_ref, vc_ref, overlap_ref, kw_ref, vw_ref, misc_ref,
     part_ref, selb_ref) = refs
    tq = NSA_Q
    q4 = _stack_heads(q_ref[0])
    n_blk = n_cmp * CMP_STRIDE // SLC_BLOCK
    head_rows = [slice(h * tq, (h + 1) * tq) for h in range(NSA_GROUP)]

    s = (_dot_nt(q4, kc_hi_ref[0, 0, 0, 0:n_cmp, :])
         + _dot_nt(q4, kc_lo_ref[0, 0, 0, 0:n_cmp, :]))
    col = lax.broadcasted_iota(jnp.int32, (tq, n_cmp), 1)
    t_row = q0 + lax.broadcasted_iota(jnp.int32, (tq, n_cmp), 0)
    cmp_bias = jnp.where(col * CMP_STRIDE + (CMP_BLOCK - 1) <= t_row, 0.0, -jnp.inf)
    pcs = []
    for r in head_rows:
        p = _softmax_numerator(s[r], cmp_bias)
        pcs.append(p * (1.0 / jnp.maximum(jnp.sum(p, axis=1, keepdims=True), 1e-30)))
    oc = jnp.dot(jnp.concatenate(pcs, axis=0).astype(BF16), vc_ref[0, 0, 0, 0:n_cmp, :],
                 preferred_element_type=F32)

    pc_sum = pcs[0] + pcs[1] + pcs[2] + pcs[3]
    overlap = overlap_ref[0:n_blk, 0:n_cmp]
    imp = sum(_dot_nt(overlap, p) for p in _split3(pc_sum))
    blk = lax.broadcasted_iota(jnp.int32, (n_blk, tq), 0)
    cur = (q0 + lax.broadcasted_iota(jnp.int32, (n_blk, tq), 1)) // SLC_BLOCK
    forced = jnp.logical_or(blk == 0, jnp.logical_or(blk == cur, blk == cur - 1))

    def pick_one(_, carry):
        rem, sel = carry
        best = jnp.max(rem, axis=0, keepdims=True)
        first = jnp.min(jnp.where(rem == best, blk, n_blk), axis=0, keepdims=True)
        hit = blk == first
        return jnp.where(hit, -jnp.inf, rem), jnp.where(hit, 1.0, sel)

    candidates = jnp.where(jnp.logical_or(forced, blk > cur), -jnp.inf, imp)
    _, sel = lax.fori_loop(0, min(N_SELECT, n_blk) - N_FORCED, pick_one,
                           (candidates, jnp.where(forced, 1.0, 0.0)), unroll=True)
    all_blk = selb_ref.shape[3]
    if n_blk < all_blk:
        sel = jnp.concatenate([sel, jnp.zeros((all_blk - n_blk, tq), F32)], axis=0)
    selb_ref[0, 0] =jnp.where(sel.T > 0.5, 0.0, MASK_VALUE).astype(BF16)

    start = pl.multiple_of(jnp.maximum(q0 - WINDOW, 0), tq)
    kw = kw_ref[0, 0, pl.ds(start, WIN_KEYS), :]
    vw = vw_ref[0, 0, pl.ds(start, WIN_KEYS), :]
    sw = _dot_nt(q4, kw)
    pos = start + lax.broadcasted_iota(jnp.int32, (tq, WIN_KEYS), 1)
    dist = q0 + lax.broadcasted_iota(jnp.int32, (tq, WIN_KEYS), 0) - pos
    win_bias = jnp.where(jnp.logical_and(dist >= 0, dist < WINDOW), 0.0, -jnp.inf)
    pw = jnp.concatenate([_softmax_numerator(sw[r], win_bias) for r in head_rows],
                         axis=0).astype(BF16)
    ow = jnp.dot(pw, vw, preferred_element_type=F32)
    lw = jnp.dot(pw, jnp.ones((WIN_KEYS, LANES), BF16), preferred_element_type=F32)
    ow = ow * (1.0 / jnp.maximum(lw, 1e-30))

    gates = _group_gates(misc_ref[0], gi)
    gated = [_gate_lanes(gates, h, 0) * oc[r] + _gate_lanes(gates, h, 2) * ow[r]
             for h, r in enumerate(head_rows)]
    for pp, slab in enumerate(_pair_slabs(gated)):
        part_ref[0, :, pp * LANES:(pp + 1) * LANES] = slab


def _block_overlap(n_blk, n_cmp):
    ratio = SLC_BLOCK // CMP_STRIDE
    lo = np.arange(n_blk)[:, None] * ratio - (CMP_BLOCK // CMP_STRIDE - 1)
    i = np.arange(n_cmp)[None, :]
    n_overlap = (SLC_BLOCK + CMP_BLOCK) // CMP_STRIDE - 1
    return ((i >= lo) & (i < lo + n_overlap) & (i < n_cmp - 1)).astype(np.float32)


def _cmpwin(nq, cmp_hi, cmp_lo, kw, vw_ones, misc):
    b, s, w = nq.shape
    g = NSA_KV_HEADS
    tq = NSA_Q
    gw = NSA_GROUP * HEAD_DIM
    n_cmp = cmp_hi.shape[3]
    n_blk = s // SLC_BLOCK
    q_spec = pl.BlockSpec((1, tq, gw), lambda bi, gi, i: (bi, i, gi))
    key_spec = pl.BlockSpec((1, 1, 1, n_cmp, LANES), lambda bi, gi, i: (0, bi, gi, 0, 0))
    val_spec = pl.BlockSpec((1, 1, 1, n_cmp, LANES), lambda bi, gi, i: (1, bi, gi, 0, 0))
    seq_spec = pl.BlockSpec((1, 1, s, LANES), lambda bi, gi, i: (bi, gi, 0, 0))
    resident = 2 * s * LANES * 2 + 3 * n_cmp * LANES * 2 + n_blk * n_cmp * 2
    rows = NSA_GROUP * tq
    return pl.pallas_call(
        _cmpwin_kernel,
        grid=(b, g, s // tq),
        in_specs=[q_spec, key_spec, key_spec, val_spec,
                  pl.BlockSpec((n_blk, n_cmp), lambda bi, gi, i: (0, 0)),
                  seq_spec, seq_spec,
                  pl.BlockSpec((1, tq, LANES), lambda bi, gi, i: (bi, i, 0))],
        out_specs=(q_spec, pl.BlockSpec((1, 1, tq, n_blk), lambda bi, gi, i: (bi, gi, i, 0))),
        out_shape=(jax.ShapeDtypeStruct((b, s, w), F32),
                   jax.ShapeDtypeStruct((b, g, s, n_blk), BF16)),
        compiler_params=pltpu.CompilerParams(
            dimension_semantics=("arbitrary", "arbitrary", "arbitrary"),
            vmem_limit_bytes=_vmem_limit(resident + tq * gw * 8, 12 * rows * WIN_KEYS * 4)),
        name="cmpwin",
    )(nq, cmp_hi, cmp_lo, cmp_hi, jnp.asarray(_block_overlap(n_blk, n_cmp), BF16), kw, vw_ones, misc)


def _sel_kernel(q_ref, k_ref, v_ref, selb_ref, part_ref, misc_ref, z_ref, o_ref,
                qa_ref, gate_ref, *flash):
    tq = SEL_Q
    gi = pl.program_id(1)
    qi = pl.program_id(2)
    q0 = pl.multiple_of(qi * tq, tq)
    n_full = q0 // SEL_KV

    def prologue():
        q = q_ref[0]
        selb = selb_ref[0, 0]
        lane = lax.broadcasted_iota(jnp.int32, (tq, LANES), 1)
        for h in range(NSA_GROUP):
            r = slice(h * tq, (h + 1) * tq)
            slab = q[:, (h // 2) * LANES:(h // 2 + 1) * LANES]
            own = (lane < HEAD_DIM) if h % 2 == 0 else (lane >= HEAD_DIM)
            qa_ref[r, 0:LANES] = jnp.where(own, slab, jnp.zeros_like(slab))
            qa_ref[r, LANES:2 * LANES] = selb
        gates = _group_gates(misc_ref[0], gi)
        for h in range(NSA_GROUP):
            gate_ref[h] = _gate_lanes(gates, h, 1)

    def causal_mask(s, k0):
        t_row = q0 + (lax.broadcasted_iota(jnp.int32, s.shape, 0) & (tq - 1))
        pos = k0 + lax.broadcasted_iota(jnp.int32, s.shape, 1)
        return jnp.where(pos <= t_row, s, MASK_VALUE)

    def epilogue():
        acc_ref = flash[1]
        gated = []
        for h in range(NSA_GROUP):
            acc = acc_ref[h * tq:(h + 1) * tq, :]
            scaled = acc * (gate_ref[h] * (1.0 / acc[:, HEAD_DIM:HEAD_DIM + 1]))
            gated.append(scaled if h % 2 == 0 else pltpu.roll(scaled, HEAD_DIM, 1))
        for pp, slab in enumerate(_pair_slabs(gated)):
            c = slice(pp * LANES, (pp + 1) * LANES)
            o_ref[0, :, c] = ((part_ref[0, :, c] + slab) * z_ref[0, :, c].astype(F32)).astype(BF16)

    _causal_flash(n_full, SEL_KV, qa_ref,
                  lambda k0: k_ref[0, 0, pl.ds(k0, SEL_KV), :],
                  lambda p, k0: jnp.dot(p, v_ref[0, 0, pl.ds(k0, SEL_KV), :],
                                        preferred_element_type=F32),
                  causal_mask, prologue, epilogue, flash)


def _sel(nq, k_aug, v_ones, selb, part, misc, nz):
    b, s, w = nq.shape
    g = NSA_KV_HEADS
    tq = SEL_Q
    gw = NSA_GROUP * HEAD_DIM
    n_blk = s // SLC_BLOCK
    rows = NSA_GROUP * tq
    q_spec = pl.BlockSpec((1, tq, gw), lambda bi, gi, i: (bi, i, gi))
    resident = s * 3 * LANES * 2
    return pl.pallas_call(
        _sel_kernel,
        grid=(b, g, s // tq),
        in_specs=[q_spec,
                  pl.BlockSpec((1, 1, s, 2 * LANES), lambda bi, gi, i: (bi, gi, 0, 0)),
                  pl.BlockSpec((1, 1, s, LANES), lambda bi, gi, i: (bi, gi, 0, 0)),
                  pl.BlockSpec((1, 1, tq, n_blk), lambda bi, gi, i: (bi, gi, i, 0)),
                  q_spec,
                  pl.BlockSpec((1, tq, LANES), lambda bi, gi, i: (bi, i, 0)),
                  q_spec],
        out_specs=q_spec,
        out_shape=jax.ShapeDtypeStruct((b, s, w), BF16),
        scratch_shapes=[pltpu.VMEM((rows, 2 * LANES), BF16),
                        pltpu.VMEM((NSA_GROUP, tq, LANES), F32)] + _flash_scratch(rows, SEL_KV, LANES),
        compiler_params=pltpu.CompilerParams(
            dimension_semantics=("arbitrary", "arbitrary", "arbitrary"),
            vmem_limit_bytes=_vmem_limit(resident + tq * gw * 10,
                                         _flash_scratch_bytes(rows, SEL_KV, LANES) + 4 * rows * SEL_KV * 4)),
        name="sel",
    )(nq, k_aug, v_ones, selb, part, misc, nz)


def _out_kernel(yf_ref, yn_ref, w_ref, g_ref, mod_ref, x_ref, o_ref):
    y = (jnp.dot(yf_ref[0], w_ref[0:FOX_WIDTH, :], preferred_element_type=F32)
         + jnp.dot(yn_ref[0], w_ref[FOX_WIDTH:, :], preferred_element_type=F32))
    yn = y * lax.rsqrt(jnp.mean(y * y, axis=-1, keepdims=True) + RMS_EPS)
    o_ref[0] = x_ref[0] + mod_ref[0, 2:3, :] * (yn * g_ref[...])


def _out(y_fox, y_nsa, w_out, g_post, mod, x):
    b, s, d = x.shape
    tm = PROJ_ROWS
    row = lambda bi, i: (bi, i, 0)
    half = pl.BlockSpec((1, tm, FOX_WIDTH), row)
    pipelined = 2 * tm * FOX_WIDTH * 2 + 2 * tm * d * 4 + d * d * 2
    return pl.pallas_call(
        _out_kernel,
        grid=(b, s // tm),
        in_specs=[half, half,
                  pl.BlockSpec((d, d), lambda bi, i: (0, 0)),
                  pl.BlockSpec((1, d), lambda bi, i: (0, 0)),
                  pl.BlockSpec((1, 3, d), lambda bi, i: (bi, 0, 0)),
                  pl.BlockSpec((1, tm, d), row)],
        out_specs=pl.BlockSpec((1, tm, d), row),
        out_shape=jax.ShapeDtypeStruct((b, s, d), F32),
        compiler_params=pltpu.CompilerParams(
            dimension_semantics=("arbitrary", "arbitrary"),
            vmem_limit_bytes=_vmem_limit(pipelined, 6 * tm * d * 4)),
        name="out",
    )(y_fox, y_nsa, w_out, g_post, mod, x)


def _rope_slabs(seq_len):
    inv = 1.0 / (ROPE_THETA ** (jnp.arange(0, HEAD_DIM, 2, dtype=F32) / HEAD_DIM))
    ang = jnp.arange(seq_len, dtype=F32)[:, None] * inv[None, :]
    cos, sin = jnp.cos(ang), jnp.sin(ang)
    reps = LANES // (HEAD_DIM // 2)
    sign = jnp.tile(jnp.concatenate([-jnp.ones((HEAD_DIM // 2,), F32), jnp.ones((HEAD_DIM // 2,), F32)]),
                    LANES // HEAD_DIM)
    return jnp.tile(cos, (1, reps)), jnp.tile(sin, (1, reps)) * sign[None, :]


def _reorder_w_in(w_in):
    fw, kv = FOX_WIDTH, NSA_KV_WIDTH
    o = 0
    cols = {}
    for name, n in (("fq", fw), ("fk", fw), ("fv", fw), ("ff", FOX_HEADS), ("fz", fw), ("nq", NSA_WIDTH),
                    ("kc", kv), ("vc", kv), ("ks", kv), ("vs", kv), ("kw", kv), ("vw", kv),
                    ("ng", 3 * NSA_HEADS), ("nz", NSA_WIDTH)):
        cols[name] = w_in[:, o:o + n]
        o += n
    pad = jnp.zeros((w_in.shape[0], LANES - FOX_HEADS - 3 * NSA_HEADS), w_in.dtype)
    order = ("fq", "fk", "fv", "fz", "nq", "kc", "vc", "ks", "vs", "kw", "vw", "nz", "ff", "ng")
    return jnp.concatenate([cols[k] for k in order] + [pad], axis=1).astype(BF16)


def _layer(x, c8, g_pre, g_post, w_ada, b_ada, w_in, b_forget, w_cmp_k1, w_cmp_k2,
           w_cmp_v1, w_cmp_v2, pe_cmp_k, pe_cmp_v, w_out, cos128, sin128):
    b, s, d = x.shape
    mod = _ada(c8, w_ada, b_ada)[:b].reshape(b, 3, d)
    bf128 = jnp.pad(b_forget, (0, LANES - FOX_HEADS)).reshape(1, LANES)
    (fq, fk, fv, fz, nq, cmp_in, ks_aug, vs_ones, kw2, vw_ones, nz, misc, log_f) = _proj(
        x, mod, g_pre.reshape(1, d), _reorder_w_in(w_in), cos128, sin128, bf128)

    cum_split = tuple(p.reshape(b, FOX_HEADS, s)
                      for p in _cumsum_lanes_split(log_f.reshape(b * FOX_HEADS, s)))
    y_fox = _fox(fq, fk, fv, cum_split, fz)

    flat = CMP_STRIDE * HEAD_DIM
    t = cmp_in.reshape(2, b, NSA_KV_HEADS, s // CMP_STRIDE, flat)
    pe = jnp.stack([pe_cmp_k.reshape(1, 2 * flat), pe_cmp_v.reshape(1, 2 * flat)])
    w2 = jnp.stack([w_cmp_k2, w_cmp_v2])
    cmp_hi, cmp_lo = _compress(t, jnp.stack([w_cmp_k1, w_cmp_v1]), jnp.concatenate([w2, w2], axis=-1),
                               jnp.broadcast_to(pe, (2, SUBLANES, 2 * flat)))
    part, selb = _cmpwin(nq, cmp_hi, cmp_lo, kw2, vw_ones, misc)
    y_nsa = _sel(nq, ks_aug, vs_ones, selb, part, misc, nz)

    return _out(y_fox, y_nsa, w_out.astype(BF16), g_post.reshape(1, d), mod, x)


def kernel(x, c, g_pre, g_post, w_ada, b_ada, w_in, b_forget, w_cmp_k1, w_cmp_k2,
           w_cmp_v1, w_cmp_v2, pe_cmp_k, pe_cmp_v, w_out):
    cos128, sin128 = _rope_slabs(x.shape[1])
    c8 = jnp.pad(c, ((0, SUBLANES - c.shape[0]), (0, 0)))
    for layer in range(g_pre.shape[0]):
        x = _layer(x, c8, g_pre[layer], g_post[layer], w_ada[layer], b_ada[layer], w_in[layer],
                   b_forget[layer], w_cmp_k1[layer], w_cmp_k2[layer], w_cmp_v1[layer],
                   w_cmp_v2[layer], pe_cmp_k[layer], pe_cmp_v[layer], w_out[layer], cos128, sin128)
    return x
```

```python
import functools

import jax
import jax.numpy as jnp
import numpy as np
from jax import lax
from jax.experimental import pallas as pl
from jax.experimental.pallas import tpu as pltpu

F32 = jnp.float32
BF16 = jnp.bfloat16

D_MODEL = 1024
HEAD_DIM = 64
FOX_WIDTH = 512
NSA_WIDTH = 512
FOX_HEADS = 8
NSA_HEADS = 8
NSA_KV_HEADS = 2
NSA_GROUP = 4
NSA_KV_WIDTH = 128
CMP_BLOCK = 32
CMP_STRIDE = 16
CMP_HIDDEN = 128
SLC_BLOCK = 64
N_SELECT = 16
N_FORCED = 3
WINDOW = 512
ROPE_THETA = 10000.0
RMS_EPS = 1e-6
LOG2E = 1.4426950408889634
QK_SCALE = HEAD_DIM ** -0.5 * LOG2E

LANES = 128
SUBLANES = 8
V7X_VMEM_BYTES = 64 * 1024 * 1024
MASK_VALUE = -1e30

PROJ_ROWS = 512
FOX_TILE = 512
NSA_Q = 256
SEL_Q = 128
SEL_KV = 512
WIN_KEYS = WINDOW + NSA_Q
FLASH_UNROLL = 6
CMP_QUARTERS = 4
FOX_AUG = 6

C_FQ, C_FK, C_FV, C_FZ, C_NQ = 0, 512, 1024, 1536, 2048
C_KC, C_VC, C_KS, C_VS, C_KW, C_VW = 2560, 2688, 2816, 2944, 3072, 3200
C_NZ, C_MISC, PROJ_COLS = 3328, 3840, 3968
MISC_GATE0 = FOX_HEADS


def _vmem_limit(pipelined_bytes, resident_bytes):
    need = 2 * pipelined_bytes + resident_bytes
    return int(min(max(need, 16 * 1024 * 1024), V7X_VMEM_BYTES - 8 * 1024 * 1024))


def _sigmoid(v):
    return 1.0 / (1.0 + jnp.exp(-v))


def _silu(v):
    return v * _sigmoid(v)


def _dot_nt(a, b):
    return lax.dot_general(a, b, (((1,), (1,)), ((), ())), preferred_element_type=F32)


def _split3(v):
    hi = v.astype(BF16)
    r1 = v - hi.astype(F32)
    mid = r1.astype(BF16)
    lo = (r1 - mid.astype(F32)).astype(BF16)
    return hi, mid, lo


def _ada_kernel(c_ref, w_ref, b_ref, o_ref):
    a = _silu(c_ref[...])
    o_ref[...] = jnp.dot(a, w_ref[...], precision=lax.Precision.HIGHEST,
                         preferred_element_type=F32) + b_ref[...]


def _ada(c8, w_ada, b_ada):
    n = w_ada.shape[1]
    blk = D_MODEL
    return pl.pallas_call(
        _ada_kernel,
        grid=(n // blk,),
        in_specs=[pl.BlockSpec((SUBLANES, D_MODEL), lambda j: (0, 0)),
                  pl.BlockSpec((D_MODEL, blk), lambda j: (0, j)),
                  pl.BlockSpec((1, blk), lambda j: (0, j))],
        out_specs=pl.BlockSpec((SUBLANES, blk), lambda j: (0, j)),
        out_shape=jax.ShapeDtypeStruct((SUBLANES, n), F32),
        compiler_params=pltpu.CompilerParams(
            dimension_semantics=("arbitrary",),
            vmem_limit_bytes=_vmem_limit(D_MODEL * blk * 4, 4 * 1024 * 1024)),
        name="ada",
    )(c8, w_ada, b_ada.reshape(1, n))


def _rope128(t, cos, sin_signed):
    lane = lax.broadcasted_iota(jnp.int32, t.shape, 1)
    first_half = (lane & (HEAD_DIM - 1)) < HEAD_DIM // 2
    partner = jnp.where(first_half,
                        pltpu.roll(t, LANES - HEAD_DIM // 2, 1),
                        pltpu.roll(t, HEAD_DIM // 2, 1))
    return t * cos + partner * sin_signed


def _proj_kernel(x_ref, mod_ref, g_ref, w_ref, cos_ref, sin_ref, bf_ref,
                 fq_ref, fk_ref, fv_ref, fz_ref, nq_ref, cmp_ref,
                 ks_ref, vs_ref, kw_ref, vw_ref, nz_ref, misc_ref, lf_ref):
    tm = x_ref.shape[1]
    x = x_ref[0]
    y = x * lax.rsqrt(jnp.mean(x * x, axis=-1, keepdims=True) + RMS_EPS)
    y = y * g_ref[...]
    h = (y * (1.0 + mod_ref[0, 1:2, :]) + mod_ref[0, 0:1, :]).astype(BF16)
    cos = cos_ref[...]
    sin = sin_ref[...]

    def mm(lo, n):
        return jnp.dot(h, w_ref[:, lo:lo + n], preferred_element_type=F32)

    def slab_pair(lo):
        both = mm(lo, 2 * LANES)
        return both[:, 0:LANES], both[:, LANES:2 * LANES]

    lane = lax.broadcasted_iota(jnp.int32, (tm, LANES), 1)
    low = lane < HEAD_DIM
    ones_col = jnp.where(lane == HEAD_DIM, 1.0, 0.0).astype(BF16)

    def doubled(slab):
        swapped = pltpu.roll(slab, HEAD_DIM, 1)
        return jnp.where(low, slab, swapped), jnp.where(low, swapped, slab)

    def with_ones(slab):
        swapped = pltpu.roll(slab, HEAD_DIM, 1)
        return jnp.where(low, slab, ones_col), jnp.where(low, swapped, ones_col)

    fq_ref[0] = (mm(C_FQ, FOX_WIDTH) * QK_SCALE).astype(BF16)
    fk_ref[0] = mm(C_FK, FOX_WIDTH).astype(BF16)
    fv = mm(C_FV, FOX_WIDTH).astype(BF16)
    for pair in range(FOX_WIDTH // LANES):
        fv_ref[0, 2 * pair], fv_ref[0, 2 * pair + 1] = with_ones(fv[:, pair * LANES:(pair + 1) * LANES])
    fz_ref[0] = _silu(mm(C_FZ, FOX_WIDTH)).astype(BF16)
    for j in range(NSA_WIDTH // (2 * LANES)):
        for i, t in enumerate(slab_pair(C_NQ + 2 * j * LANES)):
            c = (2 * j + i) * LANES
            nq_ref[0, :, c:c + LANES] = (_rope128(t, cos, sin) * QK_SCALE).astype(BF16)

    kc, vc = slab_pair(C_KC)
    for kind, slab in enumerate((_rope128(kc, cos, sin), vc)):
        for g in range(NSA_KV_HEADS):
            cmp_ref[kind, 0, g] = slab[:, g * HEAD_DIM:(g + 1) * HEAD_DIM]
    pos = pl.program_id(1) * tm + lax.broadcasted_iota(jnp.int32, (tm, LANES), 0)
    onehot = jnp.where(pos // SLC_BLOCK == lane, 1.0, 0.0).astype(BF16)
    ks, vs = slab_pair(C_KS)
    for g, k2 in enumerate(doubled(_rope128(ks, cos, sin).astype(BF16))):
        ks_ref[0, g, :, 0:LANES] = k2
        ks_ref[0, g, :, LANES:2 * LANES] = onehot
    vs_ref[0, 0], vs_ref[0, 1] = with_ones(vs.astype(BF16))
    kw, vw = slab_pair(C_KW)
    kw_ref[0, 0], kw_ref[0, 1] = doubled(_rope128(kw, cos, sin).astype(BF16))
    vw_ref[0, 0], vw_ref[0, 1] = doubled(vw.astype(BF16))
    nz_ref[0] = _silu(mm(C_NZ, NSA_WIDTH)).astype(BF16)
    misc = mm(C_MISC, LANES)
    misc_ref[0] = misc
    z = misc + bf_ref[...]
    log_f = jnp.minimum(z, 0.0) - jnp.log1p(jnp.exp(-jnp.abs(z)))
    lf_ref[0] = log_f.T[0:FOX_HEADS, :]


def _proj(x, mod, g_pre, w_cat, cos128, sin128, bf128):
    b, s, d = x.shape
    tm = PROJ_ROWS
    row = lambda bi, i: (bi, i, 0)
    wide = lambda dt: jax.ShapeDtypeStruct((b, s, FOX_WIDTH), dt)
    g = NSA_KV_HEADS
    grouped = lambda lanes: jax.ShapeDtypeStruct((b, g, s, lanes), BF16)
    grouped_spec = lambda lanes: pl.BlockSpec((1, g, tm, lanes), lambda bi, i: (bi, 0, i, 0))
    out_shape = (wide(BF16), wide(BF16), jax.ShapeDtypeStruct((b, FOX_HEADS, s, LANES), BF16),
                 wide(BF16), wide(BF16),
                 jax.ShapeDtypeStruct((2, b, g, s, HEAD_DIM), F32),
                 grouped(2 * LANES), grouped(LANES), grouped(LANES), grouped(LANES),
                 wide(BF16), jax.ShapeDtypeStruct((b, s, LANES), F32),
                 jax.ShapeDtypeStruct((b, FOX_HEADS, s), F32))
    wide_spec = pl.BlockSpec((1, tm, FOX_WIDTH), row)
    out_specs = (wide_spec, wide_spec,
                 pl.BlockSpec((1, FOX_HEADS, tm, LANES), lambda bi, i: (bi, 0, i, 0)),
                 wide_spec, wide_spec) + (
        pl.BlockSpec((2, 1, g, tm, HEAD_DIM), lambda bi, i: (0, bi, 0, i, 0)),
        grouped_spec(2 * LANES), grouped_spec(LANES), grouped_spec(LANES), grouped_spec(LANES),
        wide_spec, pl.BlockSpec((1, tm, LANES), row),
        pl.BlockSpec((1, FOX_HEADS, tm), lambda bi, i: (bi, 0, i)))
    pipelined = (tm * d * 4 + tm * (7 * FOX_WIDTH * 2 + 2 * g * LANES * 4 + 5 * g * LANES * 2 + LANES * 4)
                 + d * PROJ_COLS * 2)
    return pl.pallas_call(
        _proj_kernel,
        grid=(b, s // tm),
        in_specs=[pl.BlockSpec((1, tm, d), row),
                  pl.BlockSpec((1, 3, d), lambda bi, i: (bi, 0, 0)),
                  pl.BlockSpec((1, d), lambda bi, i: (0, 0)),
                  pl.BlockSpec((d, PROJ_COLS), lambda bi, i: (0, 0)),
                  pl.BlockSpec((tm, LANES), lambda bi, i: (i, 0)),
                  pl.BlockSpec((tm, LANES), lambda bi, i: (i, 0)),
                  pl.BlockSpec((1, LANES), lambda bi, i: (0, 0))],
        out_specs=out_specs,
        out_shape=out_shape,
        compiler_params=pltpu.CompilerParams(
            dimension_semantics=("arbitrary", "arbitrary"),
            vmem_limit_bytes=_vmem_limit(pipelined, 8 * 1024 * 1024)),
        name="proj",
    )(x, mod, g_pre, w_cat, cos128, sin128, bf128)


def _cumsum_kernel(x_ref, hi_ref, mid_ref, lo_ref, *, chunks):
    x = x_ref[...]
    n = x.shape[0]
    parts = _split3(x)
    r = lax.broadcasted_iota(jnp.int32, (LANES, LANES), 0)
    c = lax.broadcasted_iota(jnp.int32, (LANES, LANES), 1)
    tri = (r <= c).astype(BF16)
    rr = lax.broadcasted_iota(jnp.int32, (n, n), 0)
    cc = lax.broadcasted_iota(jnp.int32, (n, n), 1)
    earlier = jnp.logical_and(cc < rr, (cc // chunks) == (rr // chunks)).astype(BF16)
    within = sum(jnp.dot(p, tri, preferred_element_type=F32) for p in parts)
    before = sum(jnp.dot(earlier, p, preferred_element_type=F32) for p in parts)
    total = (within + jnp.sum(before, axis=-1, keepdims=True)) * LOG2E
    hi_ref[...], mid_ref[...], lo_ref[...] = _split3(total)


def _cumsum_lanes_split(v):
    rows, s = v.shape
    chunks = s // LANES
    n = rows * chunks
    part = jax.ShapeDtypeStruct((n, LANES), BF16)
    parts = pl.pallas_call(
        functools.partial(_cumsum_kernel, chunks=chunks),
        out_shape=(part, part, part),
        compiler_params=pltpu.CompilerParams(
            vmem_limit_bytes=_vmem_limit(2 * n * LANES * 4, 6 * n * n)),
        name="cumsum",
    )(v.reshape(n, LANES))
    return tuple(p.reshape(rows, s) for p in parts)


def _flash_scratch(rows, tile, acc_lanes):
    return ([pltpu.VMEM((rows, LANES), F32), pltpu.VMEM((rows, acc_lanes), F32)]
            + [pltpu.VMEM((rows, tile), F32)] * 2
            + [pltpu.VMEM((rows, tile), BF16)] * 2
            + [pltpu.VMEM((rows, LANES), F32)] * 2)


def _flash_scratch_bytes(rows, tile, acc_lanes):
    return rows * (LANES * 4 + acc_lanes * 4 + 2 * tile * 4 + 2 * tile * 2 + 2 * LANES * 4)


def _causal_flash(n_full, tile, q_ref, load_k, pv, causal_mask, prologue, epilogue, scratch):
    m_ref, acc_ref, s0, s1, p0, p1, a0, a1 = scratch
    s_bufs, p_bufs, a_bufs = (s0, s1), (p0, p1), (a0, a1)

    def logits(j):
        return _dot_nt(q_ref[...], load_k(pl.multiple_of(j * tile, tile)))

    def softmax(s):
        m_prev = m_ref[...]
        m_next = jnp.maximum(m_prev, jnp.max(s, axis=1, keepdims=True))
        m_ref[...] = m_next
        p = jnp.exp2(s - jnp.tile(m_next, (1, tile // LANES)))
        return p.astype(BF16), jnp.exp2(m_prev - m_next)

    def accumulate(p, alpha, j):
        acc_ref[...] = acc_ref[...] * alpha + pv(p, pl.multiple_of(j * tile, tile))

    def stage(j, cur):
        s_bufs[1 - cur][...] = logits(j + 1)
        p_bufs[cur][...], a_bufs[cur][...] = softmax(s_bufs[cur][...])
        accumulate(p_bufs[1 - cur][...], a_bufs[1 - cur][...], j - 1)

    def run_stages(first, count):
        for i in range(count):
            stage(first + i, (1 + i) % 2)

    def finish(cur):
        p, alpha = softmax(causal_mask(s_bufs[cur][...], n_full * tile))
        accumulate(p_bufs[1 - cur][...], a_bufs[1 - cur][...], n_full - 1)
        accumulate(p, alpha, n_full)

    def start():
        prologue()
        m_ref[...] = jnp.full(m_ref.shape, MASK_VALUE, F32)
        acc_ref[...] = jnp.zeros(acc_ref.shape, F32)

    @pl.when(n_full == 0)
    def _():
        start()
        p, alpha = softmax(causal_mask(logits(0), 0))
        accumulate(p, alpha, 0)
        epilogue()

    @pl.when(n_full > 0)
    def _():
        start()
        s_first = logits(0)
        s_bufs[1][...] = logits(1)
        p_bufs[0][...], a_bufs[0][...] = softmax(s_first)
        later = n_full - 1

        def unrolled(jj, carry):
            run_stages(1 + FLASH_UNROLL * jj, FLASH_UNROLL)
            return carry

        lax.fori_loop(0, later // FLASH_UNROLL, unrolled, 0)
        rest = later % FLASH_UNROLL

        def tail(count):
            run_stages(n_full - count, count)
            finish((1 + count) % 2)
            epilogue()

        for count in range(FLASH_UNROLL):
            pl.when(rest == count)(functools.partial(tail, count))


def _fox_aug(cum_split):
    b, h, s = cum_split[0].shape
    terms = jnp.concatenate(list(cum_split) + [jnp.ones((b, 1, s), BF16)], axis=1)
    terms = jnp.transpose(terms, (0, 2, 1))
    place_q = np.zeros((3 * h + 1, h * HEAD_DIM), np.float32)
    place_k = np.zeros((3 * h + 1, h * HEAD_DIM), np.float32)
    half = FOX_AUG // 2
    for head in range(h):
        for part in range(half):
            place_q[part * h + head, head * HEAD_DIM + part] = 1.0
            place_q[3 * h, head * HEAD_DIM + half + part] = 1.0
            place_k[3 * h, head * HEAD_DIM + part] = 1.0
            place_k[part * h + head, head * HEAD_DIM + half + part] = -1.0
    scatter = lambda place: jnp.einsum("bsr,rl->bsl", terms, jnp.asarray(place, BF16),
                                       preferred_element_type=F32).astype(BF16)
    return scatter(place_q), scatter(place_k)


def _fox_kernel(q_ref, qa_ref, k_ref, ka_ref, v_ref, z_ref, o_ref, qm_ref, *flash):
    t = FOX_TILE
    qi = pl.program_id(2)

    def prologue():
        q = q_ref[0]
        qa = qa_ref[0]
        lane = lax.broadcasted_iota(jnp.int32, q.shape, 1)
        for hh in range(2):
            r = slice(hh * t, (hh + 1) * t)
            own = (lane < HEAD_DIM) if hh == 0 else (lane >= HEAD_DIM)
            qm_ref[r, 0:LANES] = jnp.where(own, q, jnp.zeros_like(q))
            qm_ref[r, LANES:2 * LANES] = jnp.where(own, qa, jnp.zeros_like(qa))

    def load_k(k0):
        return jnp.concatenate([k_ref[0, pl.ds(k0, t), :], ka_ref[0, pl.ds(k0, t), :]], axis=1)

    def pv(p, k0):
        return jnp.concatenate(
            [jnp.dot(p[hh * t:(hh + 1) * t], v_ref[0, hh, pl.ds(k0, t), :], preferred_element_type=F32)
             for hh in range(2)], axis=0)

    def causal_mask(s, k0):
        t_row = qi * t + (lax.broadcasted_iota(jnp.int32, s.shape, 0) & (t - 1))
        pos = k0 + lax.broadcasted_iota(jnp.int32, s.shape, 1)
        return jnp.where(pos <= t_row, s, MASK_VALUE)

    def epilogue():
        acc_ref = flash[1]
        o0 = acc_ref[0:t, :] * (1.0 / acc_ref[0:t, HEAD_DIM:HEAD_DIM + 1])
        o1 = acc_ref[t:2 * t, :] * (1.0 / acc_ref[t:2 * t, HEAD_DIM:HEAD_DIM + 1])
        lane = lax.broadcasted_iota(jnp.int32, o0.shape, 1)
        o = jnp.where(lane < HEAD_DIM, o0, pltpu.roll(o1, HEAD_DIM, 1))
        o_ref[0] = (o * z_ref[0].astype(F32)).astype(BF16)

    _causal_flash(qi, t, qm_ref, load_k, pv, causal_mask, prologue, epilogue, flash)


def _fox(fq, fk, fv, cum_split, fz):
    b, s, w = fq.shape
    t = FOX_TILE
    pairs = w // LANES
    qa, ka = _fox_aug(cum_split)
    tile = pl.BlockSpec((1, t, LANES), lambda bi, hp, i: (bi, i, hp))
    full = pl.BlockSpec((1, s, LANES), lambda bi, hp, i: (bi, 0, hp))
    resident = 4 * s * LANES * 2
    scratch = 2 * t * 2 * LANES * 2 + _flash_scratch_bytes(2 * t, t, LANES)
    return pl.pallas_call(
        _fox_kernel,
        grid=(b, pairs, s // t),
        in_specs=[tile, tile, full, full,
                  pl.BlockSpec((1, 2, s, LANES), lambda bi, hp, i: (bi, hp, 0, 0)),
                  tile],
        out_specs=tile,
        out_shape=jax.ShapeDtypeStruct((b, s, w), BF16),
        scratch_shapes=[pltpu.VMEM((2 * t, 2 * LANES), BF16)] + _flash_scratch(2 * t, t, LANES),
        compiler_params=pltpu.CompilerParams(
            dimension_semantics=("arbitrary", "arbitrary", "arbitrary"),
            vmem_limit_bytes=_vmem_limit(resident + 4 * t * LANES * 2, scratch + 4 * 2 * t * t * 4)),
        name="fox",
    )(fq, qa, fk, ka, fv, fz)


def _compress_kernel(t_ref, w1_ref, w2_ref, pe_ref, hi_ref, lo_ref):
    hp = lax.Precision.HIGHEST
    half = CMP_STRIDE * HEAD_DIM
    hid = CMP_HIDDEN
    tt = t_ref[0, 0, 0]
    n = tt.shape[0]
    w1 = w1_ref[0]
    both = jnp.dot(tt, w1, precision=hp, preferred_element_type=F32)
    pe_first = jnp.dot(pe_ref[0, :, 0:half], w1, precision=hp, preferred_element_type=F32)
    pe_second = jnp.dot(pe_ref[0, :, half:2 * half], w1, precision=hp, preferred_element_type=F32)
    pe_term = pe_first[0:1, 0:hid] + pe_second[0:1, hid:2 * hid]
    hidden = both[:, 0:hid] + pltpu.roll(both[:, hid:2 * hid], n - 1, 0) + pe_term
    out = jnp.dot(_silu(hidden), w2_ref[0], precision=hp, preferred_element_type=F32)
    hi = out.astype(BF16)
    hi_ref[0, 0, 0] = hi
    lo_ref[0, 0, 0] = (out - hi.astype(F32)).astype(BF16)


def _compress(t, w1, w2, pe):
    kinds, b, g, n, flat = t.shape
    width = w2.shape[-1]
    out_spec = pl.BlockSpec((1, 1, 1, n, width), lambda a, bi, gi: (a, bi, gi, 0, 0))
    out_part = jax.ShapeDtypeStruct((kinds, b, g, n, width), BF16)
    return pl.pallas_call(
        _compress_kernel,
        grid=(kinds, b, g),
        in_specs=[pl.BlockSpec((1, 1, 1, n, flat), lambda a, bi, gi: (a, bi, gi, 0, 0)),
                  pl.BlockSpec((1, flat, 2 * CMP_HIDDEN), lambda a, bi, gi: (a, 0, 0)),
                  pl.BlockSpec((1, CMP_HIDDEN, width), lambda a, bi, gi: (a, 0, 0)),
                  pl.BlockSpec((1, SUBLANES, 2 * flat), lambda a, bi, gi: (a, 0, 0))],
        out_specs=(out_spec, out_spec),
        out_shape=(out_part, out_part),
        compiler_params=pltpu.CompilerParams(
            dimension_semantics=("arbitrary", "arbitrary", "arbitrary"),
            vmem_limit_bytes=_vmem_limit(n * flat * 4 + 2 * flat * CMP_HIDDEN * 4, 8 * 1024 * 1024)),
        name="compress",
    )(t, w1, w2, pe)


def _softmax_numerator(s, bias):
    s = s + bias
    m = jnp.max(s, axis=1, keepdims=True)
    m = jnp.where(m == -jnp.inf, 0.0, m)
    return jnp.exp2(s - m)


def _stack_heads(q):
    tq = q.shape[0]
    lane = lax.broadcasted_iota(jnp.int32, (tq, LANES), 1)
    rows = []
    for h in range(NSA_GROUP):
        slab = q[:, (h // 2) * LANES:(h // 2 + 1) * LANES]
        own = (lane < HEAD_DIM) if h % 2 == 0 else (lane >= HEAD_DIM)
        rows.append(jnp.where(own, slab, jnp.zeros_like(slab)))
    return jnp.concatenate(rows, axis=0)


def _group_gates(misc, gi):
    gates = _sigmoid(misc)
    return jnp.where(gi == 0, gates, pltpu.roll(gates, LANES - 3 * NSA_GROUP, 1))


def _gate_lanes(gates, h, j):
    c = MISC_GATE0 + 3 * h + j
    return jnp.broadcast_to(gates[:, c:c + 1], gates.shape)


def _pair_slabs(per_head):
    lane = lax.broadcasted_iota(jnp.int32, per_head[0].shape, 1)
    return [jnp.where(lane < HEAD_DIM, per_head[2 * pp], per_head[2 * pp + 1])
            for pp in range(NSA_GROUP // 2)]


def _cmpwin_kernel(q_ref, kc_hi_ref, kc_lo_ref, vc_ref, overlap_ref, kw_ref, vw_ref, misc_ref,
                   part_ref, selb_ref):
    gi = pl.program_id(1)
    q0 = pl.multiple_of(pl.program_id(2) * NSA_Q, NSA_Q)
    quarter = kc_hi_ref.shape[3] // CMP_QUARTERS
    reach = q0 // (quarter * CMP_STRIDE)
    refs = (q_ref, kc_hi_ref, kc_lo_ref, vc_ref, overlap_ref, kw_ref, vw_ref, misc_ref,
            part_ref, selb_ref)
    for quarters in range(1, CMP_QUARTERS + 1):
        pl.when(reach == quarters - 1)(
            functools.partial(_cmpwin_body, refs, gi, q0, quarters * quarter))


def _cmpwin_body(refs, gi, q0, n_cmp):
    (q_ref, kc_hi_ref, kc_lo_ref, vc_ref, overlap_ref, kw_ref, vw_ref, misc_ref,
     part_ref, selb_ref) = refs
    tq = NSA_Q
    q4 = _stack_heads(q_ref[0])
    n_blk = n_cmp * CMP_STRIDE // SLC_BLOCK
    head_rows = [slice(h * tq, (h + 1) * tq) for h in range(NSA_GROUP)]

    kc = jnp.concatenate([kc_hi_ref[0, 0, 0, 0:n_cmp, :], kc_lo_ref[0, 0, 0, 0:n_cmp, :]], axis=1)
    s = _dot_nt(jnp.concatenate([q4, q4], axis=1), kc)
    col = lax.broadcasted_iota(jnp.int32, (tq, n_cmp), 1)
    t_row = q0 + lax.broadcasted_iota(jnp.int32, (tq, n_cmp), 0)
    cmp_bias = jnp.where(col * CMP_STRIDE + (CMP_BLOCK - 1) <= t_row, 0.0, -jnp.inf)
    pcs = []
    for r in head_rows:
        p = _softmax_numerator(s[r], cmp_bias)
        pcs.append(p * (1.0 / jnp.maximum(jnp.sum(p, axis=1, keepdims=True), 1e-30)))
    oc = jnp.dot(jnp.concatenate(pcs, axis=0).astype(BF16), vc_ref[0, 0, 0, 0:n_cmp, :],
                 preferred_element_type=F32)

    pc_sum = pcs[0] + pcs[1] + pcs[2] + pcs[3]
    overlap = overlap_ref[0:n_blk, 0:n_cmp]
    imp = sum(_dot_nt(overlap, p) for p in _split3(pc_sum))
    blk = lax.broadcasted_iota(jnp.int32, (n_blk, tq), 0)
    cur = (q0 + lax.broadcasted_iota(jnp.int32, (n_blk, tq), 1)) // SLC_BLOCK
    forced = jnp.logical_or(blk == 0, jnp.logical_or(blk == cur, blk == cur - 1))

    def pick_one(_, carry):
        rem, sel = carry
        best = jnp.max(rem, axis=0, keepdims=True)
        first = jnp.min(jnp.where(rem == best, blk, n_blk), axis=0, keepdims=True)
        hit = blk == first
        return jnp.where(hit, -jnp.inf, rem), jnp.where(hit, 1.0, sel)

    candidates = jnp.where(jnp.logical_or(forced, blk > cur), -jnp.inf, imp)
    _, sel = lax.fori_loop(0, min(N_SELECT, n_blk) - N_FORCED, pick_one,
                           (candidates, jnp.where(forced, 1.0, 0.0)), unroll=True)
    all_blk = selb_ref.shape[3]
    if n_blk < all_blk:
        sel = jnp.concatenate([sel, jnp.zeros((all_blk - n_blk, tq), F32)], axis=0)
    selb_ref[0, 0] =jnp.where(sel.T > 0.5, 0.0, MASK_VALUE).astype(BF16)

    start = pl.multiple_of(jnp.maximum(q0 - WINDOW, 0), tq)
    kw = kw_ref[0, 0, pl.ds(start, WIN_KEYS), :]
    vw = vw_ref[0, 0, pl.ds(start, WIN_KEYS), :]
    sw = _dot_nt(q4, kw)
    pos = start + lax.broadcasted_iota(jnp.int32, (tq, WIN_KEYS), 1)
    dist = q0 + lax.broadcasted_iota(jnp.int32, (tq, WIN_KEYS), 0) - pos
    win_bias = jnp.where(jnp.logical_and(dist >= 0, dist < WINDOW), 0.0, -jnp.inf)
    pw = jnp.concatenate([_softmax_numerator(sw[r], win_bias) for r in head_rows],
                         axis=0).astype(BF16)
    ow = jnp.dot(pw, vw, preferred_element_type=F32)
    lw = jnp.dot(pw, jnp.ones((WIN_KEYS, LANES), BF16), preferred_element_type=F32)
    ow = ow * (1.0 / jnp.maximum(lw, 1e-30))

    gates = _group_gates(misc_ref[0], gi)
    gated = [_gate_lanes(gates, h, 0) * oc[r] + _gate_lanes(gates, h, 2) * ow[r]
             for h, r in enumerate(head_rows)]
    for pp, slab in enumerate(_pair_slabs(gated)):
        part_ref[0, :, pp * LANES:(pp + 1) * LANES] = slab


def _block_overlap(n_blk, n_cmp):
    ratio = SLC_BLOCK // CMP_STRIDE
    lo = np.arange(n_blk)[:, None] * ratio - (CMP_BLOCK // CMP_STRIDE - 1)
    i = np.arange(n_cmp)[None, :]
    n_overlap = (SLC_BLOCK + CMP_BLOCK) // CMP_STRIDE - 1
    return ((i >= lo) & (i < lo + n_overlap) & (i < n_cmp - 1)).astype(np.float32)


def _cmpwin(nq, cmp_hi, cmp_lo, kw, vw_ones, misc):
    b, s, w = nq.shape
    g = NSA_KV_HEADS
    tq = NSA_Q
    gw = NSA_GROUP * HEAD_DIM
    n_cmp = cmp_hi.shape[3]
    n_blk = s // SLC_BLOCK
    q_spec = pl.BlockSpec((1, tq, gw), lambda bi, gi, i: (bi, i, gi))
    key_spec = pl.BlockSpec((1, 1, 1, n_cmp, LANES), lambda bi, gi, i: (0, bi, gi, 0, 0))
    val_spec = pl.BlockSpec((1, 1, 1, n_cmp, LANES), lambda bi, gi, i: (1, bi, gi, 0, 0))
    seq_spec = pl.BlockSpec((1, 1, s, LANES), lambda bi, gi, i: (bi, gi, 0, 0))
    resident = 2 * s * LANES * 2 + 3 * n_cmp * LANES * 2 + n_blk * n_cmp * 2
    rows = NSA_GROUP * tq
    return pl.pallas_call(
        _cmpwin_kernel,
        grid=(b, g, s // tq),
        in_specs=[q_spec, key_spec, key_spec, val_spec,
                  pl.BlockSpec((n_blk, n_cmp), lambda bi, gi, i: (0, 0)),
                  seq_spec, seq_spec,
                  pl.BlockSpec((1, tq, LANES), lambda bi, gi, i: (bi, i, 0))],
        out_specs=(q_spec, pl.BlockSpec((1, 1, tq, n_blk), lambda bi, gi, i: (bi, gi, i, 0))),
        out_shape=(jax.ShapeDtypeStruct((b, s, w), F32),
                   jax.ShapeDtypeStruct((b, g, s, n_blk), BF16)),
        compiler_params=pltpu.CompilerParams(
            dimension_semantics=("arbitrary", "arbitrary", "arbitrary"),
            vmem_limit_bytes=_vmem_limit(resident + tq * gw * 8, 12 * rows * WIN_KEYS * 4)),
        name="cmpwin",
    )(nq, cmp_hi, cmp_lo, cmp_hi, jnp.asarray(_block_overlap(n_blk, n_cmp), BF16), kw, vw_ones, misc)


def _sel_kernel(q_ref, k_ref, v_ref, selb_ref, part_ref, misc_ref, z_ref, o_ref,
                qa_ref, gate_ref, *flash):
    tq = SEL_Q
    gi = pl.program_id(1)
    qi = pl.program_id(2)
    q0 = pl.multiple_of(qi * tq, tq)
    n_full = q0 // SEL_KV

    def prologue():
        q = q_ref[0]
        selb = selb_ref[0, 0]
        lane = lax.broadcasted_iota(jnp.int32, (tq, LANES), 1)
        for h in range(NSA_GROUP):
            r = slice(h * tq, (h + 1) * tq)
            slab = q[:, (h // 2) * LANES:(h // 2 + 1) * LANES]
            own = (lane < HEAD_DIM) if h % 2 == 0 else (lane >= HEAD_DIM)
            qa_ref[r, 0:LANES] = jnp.where(own, slab, jnp.zeros_like(slab))
            qa_ref[r, LANES:2 * LANES] = selb
        gates = _group_gates(misc_ref[0], gi)
        for h in range(NSA_GROUP):
            gate_ref[h] = _gate_lanes(gates, h, 1)

    def causal_mask(s, k0):
        t_row = q0 + (lax.broadcasted_iota(jnp.int32, s.shape, 0) & (tq - 1))
        pos = k0 + lax.broadcasted_iota(jnp.int32, s.shape, 1)
        return jnp.where(pos <= t_row, s, MASK_VALUE)

    def epilogue():
        acc_ref = flash[1]
        gated = []
        for h in range(NSA_GROUP):
            acc = acc_ref[h * tq:(h + 1) * tq, :]
            scaled = acc * (gate_ref[h] * (1.0 / acc[:, HEAD_DIM:HEAD_DIM + 1]))
            gated.append(scaled if h % 2 == 0 else pltpu.roll(scaled, HEAD_DIM, 1))
        for pp, slab in enumerate(_pair_slabs(gated)):
            c = slice(pp * LANES, (pp + 1) * LANES)
            o_ref[0, :, c] = ((part_ref[0, :, c] + slab) * z_ref[0, :, c].astype(F32)).astype(BF16)

    _causal_flash(n_full, SEL_KV, qa_ref,
                  lambda k0: k_ref[0, 0, pl.ds(k0, SEL_KV), :],
                  lambda p, k0: jnp.dot(p, v_ref[0, 0, pl.ds(k0, SEL_KV), :],
                                        preferred_element_type=F32),
                  causal_mask, prologue, epilogue, flash)


def _sel(nq, k_aug, v_ones, selb, part, misc, nz):
    b, s, w = nq.shape
    g = NSA_KV_HEADS
    tq = SEL_Q
    gw = NSA_GROUP * HEAD_DIM
    n_blk = s // SLC_BLOCK
    rows = NSA_GROUP * tq
    q_spec = pl.BlockSpec((1, tq, gw), lambda bi, gi, i: (bi, i, gi))
    resident = s * 3 * LANES * 2
    return pl.pallas_call(
        _sel_kernel,
        grid=(b, g, s // tq),
        in_specs=[q_spec,
                  pl.BlockSpec((1, 1, s, 2 * LANES), lambda bi, gi, i: (bi, gi, 0, 0)),
                  pl.BlockSpec((1, 1, s, LANES), lambda bi, gi, i: (bi, gi, 0, 0)),
                  pl.BlockSpec((1, 1, tq, n_blk), lambda bi, gi, i: (bi, gi, i, 0)),
                  q_spec,
                  pl.BlockSpec((1, tq, LANES), lambda bi, gi, i: (bi, i, 0)),
                  q_spec],
        out_specs=q_spec,
        out_shape=jax.ShapeDtypeStruct((b, s, w), BF16),
        scratch_shapes=[pltpu.VMEM((rows, 2 * LANES), BF16),
                        pltpu.VMEM((NSA_GROUP, tq, LANES), F32)] + _flash_scratch(rows, SEL_KV, LANES),
        compiler_params=pltpu.CompilerParams(
            dimension_semantics=("arbitrary", "arbitrary", "arbitrary"),
            vmem_limit_bytes=_vmem_limit(resident + tq * gw * 10,
                                         _flash_scratch_bytes(rows, SEL_KV, LANES) + 4 * rows * SEL_KV * 4)),
        name="sel",
    )(nq, k_aug, v_ones, selb, part, misc, nz)


def _out_kernel(yf_ref, yn_ref, w_ref, g_ref, mod_ref, x_ref, o_ref):
    y = (jnp.dot(yf_ref[0], w_ref[0:FOX_WIDTH, :], preferred_element_type=F32)
         + jnp.dot(yn_ref[0], w_ref[FOX_WIDTH:, :], preferred_element_type=F32))
    yn = y * lax.rsqrt(jnp.mean(y * y, axis=-1, keepdims=True) + RMS_EPS)
    o_ref[0] = x_ref[0] + mod_ref[0, 2:3, :] * (yn * g_ref[...])


def _out(y_fox, y_nsa, w_out, g_post, mod, x):
    b, s, d = x.shape
    tm = PROJ_ROWS
    row = lambda bi, i: (bi, i, 0)
    half = pl.BlockSpec((1, tm, FOX_WIDTH), row)
    pipelined = 2 * tm * FOX_WIDTH * 2 + 2 * tm * d * 4 + d * d * 2
    return pl.pallas_call(
        _out_kernel,
        grid=(b, s // tm),
        in_specs=[half, half,
                  pl.BlockSpec((d, d), lambda bi, i: (0, 0)),
                  pl.BlockSpec((1, d), lambda bi, i: (0, 0)),
                  pl.BlockSpec((1, 3, d), lambda bi, i: (bi, 0, 0)),
                  pl.BlockSpec((1, tm, d), row)],
        out_specs=pl.BlockSpec((1, tm, d), row),
        out_shape=jax.ShapeDtypeStruct((b, s, d), F32),
        compiler_params=pltpu.CompilerParams(
            dimension_semantics=("arbitrary", "arbitrary"),
            vmem_limit_bytes=_vmem_limit(pipelined, 6 * tm * d * 4)),
        name="out",
    )(y_fox, y_nsa, w_out, g_post, mod, x)


def _rope_slabs(seq_len):
    inv = 1.0 / (ROPE_THETA ** (jnp.arange(0, HEAD_DIM, 2, dtype=F32) / HEAD_DIM))
    ang = jnp.arange(seq_len, dtype=F32)[:, None] * inv[None, :]
    cos, sin = jnp.cos(ang), jnp.sin(ang)
    reps = LANES // (HEAD_DIM // 2)
    sign = jnp.tile(jnp.concatenate([-jnp.ones((HEAD_DIM // 2,), F32), jnp.ones((HEAD_DIM // 2,), F32)]),
                    LANES // HEAD_DIM)
    return jnp.tile(cos, (1, reps)), jnp.tile(sin, (1, reps)) * sign[None, :]


def _reorder_w_in(w_in):
    fw, kv = FOX_WIDTH, NSA_KV_WIDTH
    o = 0
    cols = {}
    for name, n in (("fq", fw), ("fk", fw), ("fv", fw), ("ff", FOX_HEADS), ("fz", fw), ("nq", NSA_WIDTH),
                    ("kc", kv), ("vc", kv), ("ks", kv), ("vs", kv), ("kw", kv), ("vw", kv),
                    ("ng", 3 * NSA_HEADS), ("nz", NSA_WIDTH)):
        cols[name] = w_in[:, o:o + n]
        o += n
    pad = jnp.zeros((w_in.shape[0], LANES - FOX_HEADS - 3 * NSA_HEADS), w_in.dtype)
    order = ("fq", "fk", "fv", "fz", "nq", "kc", "vc", "ks", "vs", "kw", "vw", "nz", "ff", "ng")
    return jnp.concatenate([cols[k] for k in order] + [pad], axis=1).astype(BF16)


def _layer(x, c8, g_pre, g_post, w_ada, b_ada, w_in, b_forget, w_cmp_k1, w_cmp_k2,
           w_cmp_v1, w_cmp_v2, pe_cmp_k, pe_cmp_v, w_out, cos128, sin128):
    b, s, d = x.shape
    mod = _ada(c8, w_ada, b_ada)[:b].reshape(b, 3, d)
    bf128 = jnp.pad(b_forget, (0, LANES - FOX_HEADS)).reshape(1, LANES)
    (fq, fk, fv, fz, nq, cmp_in, ks_aug, vs_ones, kw2, vw_ones, nz, misc, log_f) = _proj(
        x, mod, g_pre.reshape(1, d), _reorder_w_in(w_in), cos128, sin128, bf128)

    cum_split = tuple(p.reshape(b, FOX_HEADS, s)
                      for p in _cumsum_lanes_split(log_f.reshape(b * FOX_HEADS, s)))
    y_fox = _fox(fq, fk, fv, cum_split, fz)

    flat = CMP_STRIDE * HEAD_DIM
    t = cmp_in.reshape(2, b, NSA_KV_HEADS, s // CMP_STRIDE, flat)
    pe = jnp.stack([pe_cmp_k.reshape(1, 2 * flat), pe_cmp_v.reshape(1, 2 * flat)])
    w1 = jnp.stack([w_cmp_k1, w_cmp_v1])
    w2 = jnp.stack([w_cmp_k2, w_cmp_v2])
    cmp_hi, cmp_lo = _compress(t, jnp.concatenate([w1[:, :flat], w1[:, flat:]], axis=-1),
                               jnp.concatenate([w2, w2], axis=-1),
                               jnp.broadcast_to(pe, (2, SUBLANES, 2 * flat)))
    part, selb = _cmpwin(nq, cmp_hi, cmp_lo, kw2, vw_ones, misc)
    y_nsa = _sel(nq, ks_aug, vs_ones, selb, part, misc, nz)

    return _out(y_fox, y_nsa, w_out.astype(BF16), g_post.reshape(1, d), mod, x)


def kernel(x, c, g_pre, g_post, w_ada, b_ada, w_in, b_forget, w_cmp_k1, w_cmp_k2,
           w_cmp_v1, w_cmp_v2, pe_cmp_k, pe_cmp_v, w_out):
    cos128, sin128 = _rope_slabs(x.shape[1])
    c8 = jnp.pad(c, ((0, SUBLANES - c.shape[0]), (0, 0)))
    for layer in range(g_pre.shape[0]):
        x = _layer(x, c8, g_pre[layer], g_post[layer], w_ada[layer], b_ada[layer], w_in[layer],
                   b_forget[layer], w_cmp_k1[layer], w_cmp_k2[layer], w_cmp_v1[layer],
                   w_cmp_v2[layer], pe_cmp_k[layer], pe_cmp_v[layer], w_out[layer], cos128, sin128)
    return x
```

```python
import functools

import jax
import jax.numpy as jnp
import numpy as np
from jax import lax
from jax.experimental import pallas as pl
from jax.experimental.pallas import tpu as pltpu

F32 = jnp.float32
BF16 = jnp.bfloat16

D_MODEL = 1024
HEAD_DIM = 64
FOX_WIDTH = 512
NSA_WIDTH = 512
FOX_HEADS = 8
NSA_HEADS = 8
NSA_KV_HEADS = 2
NSA_GROUP = 4
NSA_KV_WIDTH = 128
CMP_BLOCK = 32
CMP_STRIDE = 16
CMP_HIDDEN = 128
SLC_BLOCK = 64
N_SELECT = 16
N_FORCED = 3
WINDOW = 512
ROPE_THETA = 10000.0
RMS_EPS = 1e-6
LOG2E = 1.4426950408889634
QK_SCALE = HEAD_DIM ** -0.5 * LOG2E

LANES = 128
SUBLANES = 8
MIB = 1024 * 1024
V7X_VMEM_BYTES = 64 * MIB
VMEM_LIMIT_FLOOR = 16 * MIB
VMEM_COMPILER_RESERVE = 8 * MIB
MASK_VALUE = -1e30

PROJ_ROWS = 512
FOX_TILE = 512
NSA_Q = 256
SEL_Q = 128
SEL_KV = 512
WIN_KEYS = WINDOW + NSA_Q
FLASH_UNROLL = 6
CMP_QUARTERS = 4
FOX_AUG = 6

C_FQ, C_FK, C_FV, C_FZ, C_NQ = 0, 512, 1024, 1536, 2048
C_KC, C_VC, C_KS, C_VS, C_KW, C_VW = 2560, 2688, 2816, 2944, 3072, 3200
C_NZ, C_MISC, PROJ_COLS = 3328, 3840, 3968
MISC_GATE0 = FOX_HEADS


def _vmem_limit(pipelined_bytes, resident_bytes):
    need = 2 * pipelined_bytes + resident_bytes
    return int(min(max(need, VMEM_LIMIT_FLOOR), V7X_VMEM_BYTES - VMEM_COMPILER_RESERVE))


def _sigmoid(v):
    return 1.0 / (1.0 + jnp.exp(-v))


def _silu(v):
    return v * _sigmoid(v)


def _dot_nt(a, b):
    return lax.dot_general(a, b, (((1,), (1,)), ((), ())), preferred_element_type=F32)


def _split3(v):
    hi = v.astype(BF16)
    r1 = v - hi.astype(F32)
    mid = r1.astype(BF16)
    lo = (r1 - mid.astype(F32)).astype(BF16)
    return hi, mid, lo


def _ada_kernel(c_ref, w_ref, b_ref, o_ref):
    a = _silu(c_ref[...])
    o_ref[...] = jnp.dot(a, w_ref[...], precision=lax.Precision.HIGHEST,
                         preferred_element_type=F32) + b_ref[...]


def _ada(c8, w_ada, b_ada):
    n = w_ada.shape[1]
    blk = D_MODEL
    return pl.pallas_call(
        _ada_kernel,
        grid=(n // blk,),
        in_specs=[pl.BlockSpec((SUBLANES, D_MODEL), lambda j: (0, 0)),
                  pl.BlockSpec((D_MODEL, blk), lambda j: (0, j)),
                  pl.BlockSpec((1, blk), lambda j: (0, j))],
        out_specs=pl.BlockSpec((SUBLANES, blk), lambda j: (0, j)),
        out_shape=jax.ShapeDtypeStruct((SUBLANES, n), F32),
        compiler_params=pltpu.CompilerParams(
            dimension_semantics=("arbitrary",),
            vmem_limit_bytes=_vmem_limit(D_MODEL * blk * 4, D_MODEL * blk * 4)),
        name="ada",
    )(c8, w_ada, b_ada.reshape(1, n))


def _rope128(t, cos, sin_signed):
    lane = lax.broadcasted_iota(jnp.int32, t.shape, 1)
    first_half = (lane & (HEAD_DIM - 1)) < HEAD_DIM // 2
    partner = jnp.where(first_half,
                        pltpu.roll(t, LANES - HEAD_DIM // 2, 1),
                        pltpu.roll(t, HEAD_DIM // 2, 1))
    return t * cos + partner * sin_signed


def _proj_kernel(x_ref, mod_ref, g_ref, w_ref, cos_ref, sin_ref, bf_ref,
                 fq_ref, fk_ref, fv_ref, fz_ref, nq_ref, cmp_ref,
                 ks_ref, vs_ref, kw_ref, vw_ref, nz_ref, misc_ref, lf_ref):
    tm = x_ref.shape[1]
    x = x_ref[0]
    y = x * lax.rsqrt(jnp.mean(x * x, axis=-1, keepdims=True) + RMS_EPS)
    y = y * g_ref[...]
    h = (y * (1.0 + mod_ref[0, 1:2, :]) + mod_ref[0, 0:1, :]).astype(BF16)
    cos = cos_ref[...]
    sin = sin_ref[...]

    def mm(lo, n):
        return jnp.dot(h, w_ref[:, lo:lo + n], preferred_element_type=F32)

    def slab_pair(lo):
        both = mm(lo, 2 * LANES)
        return both[:, 0:LANES], both[:, LANES:2 * LANES]

    lane = lax.broadcasted_iota(jnp.int32, (tm, LANES), 1)
    low = lane < HEAD_DIM
    ones_col = jnp.where(lane == HEAD_DIM, 1.0, 0.0).astype(BF16)

    def doubled(slab):
        swapped = pltpu.roll(slab, HEAD_DIM, 1)
        return jnp.where(low, slab, swapped), jnp.where(low, swapped, slab)

    def with_ones(slab):
        swapped = pltpu.roll(slab, HEAD_DIM, 1)
        return jnp.where(low, slab, ones_col), jnp.where(low, swapped, ones_col)

    fq_ref[0] = (mm(C_FQ, FOX_WIDTH) * QK_SCALE).astype(BF16)
    fk_ref[0] = mm(C_FK, FOX_WIDTH).astype(BF16)
    fv = mm(C_FV, FOX_WIDTH).astype(BF16)
    for pair in range(FOX_WIDTH // LANES):
        fv_ref[0, 2 * pair], fv_ref[0, 2 * pair + 1] = with_ones(fv[:, pair * LANES:(pair + 1) * LANES])
    fz_ref[0] = _silu(mm(C_FZ, FOX_WIDTH)).astype(BF16)
    for j in range(NSA_WIDTH // (2 * LANES)):
        for i, t in enumerate(slab_pair(C_NQ + 2 * j * LANES)):
            c = (2 * j + i) * LANES
            nq_ref[0, :, c:c + LANES] = (_rope128(t, cos, sin) * QK_SCALE).astype(BF16)

    kc, vc = slab_pair(C_KC)
    for kind, slab in enumerate((_rope128(kc, cos, sin), vc)):
        for g in range(NSA_KV_HEADS):
            cmp_ref[kind, 0, g] = slab[:, g * HEAD_DIM:(g + 1) * HEAD_DIM]
    pos = pl.program_id(1) * tm + lax.broadcasted_iota(jnp.int32, (tm, LANES), 0)
    onehot = jnp.where(pos // SLC_BLOCK == lane, 1.0, 0.0).astype(BF16)
    ks, vs = slab_pair(C_KS)
    for g, k2 in enumerate(doubled(_rope128(ks, cos, sin).astype(BF16))):
        ks_ref[0, g, :, 0:LANES] = k2
        ks_ref[0, g, :, LANES:2 * LANES] = onehot
    vs_ref[0, 0], vs_ref[0, 1] = with_ones(vs.astype(BF16))
    kw, vw = slab_pair(C_KW)
    kw_ref[0, 0], kw_ref[0, 1] = doubled(_rope128(kw, cos, sin).astype(BF16))
    vw_ref[0, 0], vw_ref[0, 1] = doubled(vw.astype(BF16))
    nz_ref[0] = _silu(mm(C_NZ, NSA_WIDTH)).astype(BF16)
    misc = mm(C_MISC, LANES)
    misc_ref[0] = misc
    z = misc + bf_ref[...]
    log_f = jnp.minimum(z, 0.0) - jnp.log1p(jnp.exp(-jnp.abs(z)))
    lf_ref[0] = log_f.T[0:FOX_HEADS, :]


def _proj(x, mod, g_pre, w_cat, cos128, sin128, bf128):
    b, s, d = x.shape
    tm = PROJ_ROWS
    row = lambda bi, i: (bi, i, 0)
    wide = lambda dt: jax.ShapeDtypeStruct((b, s, FOX_WIDTH), dt)
    g = NSA_KV_HEADS
    grouped = lambda lanes: jax.ShapeDtypeStruct((b, g, s, lanes), BF16)
    grouped_spec = lambda lanes: pl.BlockSpec((1, g, tm, lanes), lambda bi, i: (bi, 0, i, 0))
    out_shape = (wide(BF16), wide(BF16), jax.ShapeDtypeStruct((b, FOX_HEADS, s, LANES), BF16),
                 wide(BF16), wide(BF16),
                 jax.ShapeDtypeStruct((2, b, g, s, HEAD_DIM), F32),
                 grouped(2 * LANES), grouped(LANES), grouped(LANES), grouped(LANES),
                 wide(BF16), jax.ShapeDtypeStruct((b, s, LANES), F32),
                 jax.ShapeDtypeStruct((b, FOX_HEADS, s), F32))
    wide_spec = pl.BlockSpec((1, tm, FOX_WIDTH), row)
    out_specs = (wide_spec, wide_spec,
                 pl.BlockSpec((1, FOX_HEADS, tm, LANES), lambda bi, i: (bi, 0, i, 0)),
                 wide_spec, wide_spec) + (
        pl.BlockSpec((2, 1, g, tm, HEAD_DIM), lambda bi, i: (0, bi, 0, i, 0)),
        grouped_spec(2 * LANES), grouped_spec(LANES), grouped_spec(LANES), grouped_spec(LANES),
        wide_spec, pl.BlockSpec((1, tm, LANES), row),
        pl.BlockSpec((1, FOX_HEADS, tm), lambda bi, i: (bi, 0, i)))
    pipelined = (tm * d * 4 + tm * (7 * FOX_WIDTH * 2 + 2 * g * LANES * 4 + 5 * g * LANES * 2 + LANES * 4)
                 + d * PROJ_COLS * 2)
    return pl.pallas_call(
        _proj_kernel,
        grid=(b, s // tm),
        in_specs=[pl.BlockSpec((1, tm, d), row),
                  pl.BlockSpec((1, 3, d), lambda bi, i: (bi, 0, 0)),
                  pl.BlockSpec((1, d), lambda bi, i: (0, 0)),
                  pl.BlockSpec((d, PROJ_COLS), lambda bi, i: (0, 0)),
                  pl.BlockSpec((tm, LANES), lambda bi, i: (i, 0)),
                  pl.BlockSpec((tm, LANES), lambda bi, i: (i, 0)),
                  pl.BlockSpec((1, LANES), lambda bi, i: (0, 0))],
        out_specs=out_specs,
        out_shape=out_shape,
        compiler_params=pltpu.CompilerParams(
            dimension_semantics=("arbitrary", "arbitrary"),
            vmem_limit_bytes=_vmem_limit(pipelined, 4 * tm * d * 4)),
        name="proj",
    )(x, mod, g_pre, w_cat, cos128, sin128, bf128)


def _cumsum_kernel(x_ref, hi_ref, mid_ref, lo_ref, *, chunks):
    x = x_ref[...]
    n = x.shape[0]
    parts = _split3(x)
    r = lax.broadcasted_iota(jnp.int32, (LANES, LANES), 0)
    c = lax.broadcasted_iota(jnp.int32, (LANES, LANES), 1)
    tri = (r <= c).astype(BF16)
    rr = lax.broadcasted_iota(jnp.int32, (n, n), 0)
    cc = lax.broadcasted_iota(jnp.int32, (n, n), 1)
    earlier = jnp.logical_and(cc < rr, (cc // chunks) == (rr // chunks)).astype(BF16)
    within = sum(jnp.dot(p, tri, preferred_element_type=F32) for p in parts)
    before = sum(jnp.dot(earlier, p, preferred_element_type=F32) for p in parts)
    total = (within + jnp.sum(before, axis=-1, keepdims=True)) * LOG2E
    hi_ref[...], mid_ref[...], lo_ref[...] = _split3(total)


def _cumsum_lanes_split(v):
    rows, s = v.shape
    chunks = s // LANES
    n = rows * chunks
    part = jax.ShapeDtypeStruct((n, LANES), BF16)
    parts = pl.pallas_call(
        functools.partial(_cumsum_kernel, chunks=chunks),
        out_shape=(part, part, part),
        compiler_params=pltpu.CompilerParams(
            vmem_limit_bytes=_vmem_limit(2 * n * LANES * 4, 6 * n * n)),
        name="cumsum",
    )(v.reshape(n, LANES))
    return tuple(p.reshape(rows, s) for p in parts)


def _flash_scratch(rows, tile):
    return ([pltpu.VMEM((rows, LANES), F32)] * 2
            + [pltpu.VMEM((rows, tile), F32)] * 2
            + [pltpu.VMEM((rows, tile), BF16)] * 2
            + [pltpu.VMEM((rows, LANES), F32)] * 2)


def _flash_scratch_bytes(rows, tile):
    return rows * (2 * LANES * 4 + 2 * tile * 4 + 2 * tile * 2 + 2 * LANES * 4)


def _causal_flash(n_full, tile, q_ref, load_k, pv, causal_mask, prologue, epilogue, scratch):
    m_ref, acc_ref, s0, s1, p0, p1, a0, a1 = scratch
    s_bufs, p_bufs, a_bufs = (s0, s1), (p0, p1), (a0, a1)

    def logits(j):
        return _dot_nt(q_ref[...], load_k(pl.multiple_of(j * tile, tile)))

    def softmax(s):
        m_prev = m_ref[...]
        m_next = jnp.maximum(m_prev, jnp.max(s, axis=1, keepdims=True))
        m_ref[...] = m_next
        p = jnp.exp2(s - jnp.tile(m_next, (1, tile // LANES)))
        return p.astype(BF16), jnp.exp2(m_prev - m_next)

    def accumulate(p, alpha, j):
        acc_ref[...] = acc_ref[...] * alpha + pv(p, pl.multiple_of(j * tile, tile))

    def stage(j, cur):
        s_bufs[1 - cur][...] = logits(j + 1)
        p_bufs[cur][...], a_bufs[cur][...] = softmax(s_bufs[cur][...])
        accumulate(p_bufs[1 - cur][...], a_bufs[1 - cur][...], j - 1)

    def run_stages(first, count):
        for i in range(count):
            stage(first + i, (1 + i) % 2)

    def finish(cur):
        p, alpha = softmax(causal_mask(s_bufs[cur][...], n_full * tile))
        accumulate(p_bufs[1 - cur][...], a_bufs[1 - cur][...], n_full - 1)
        accumulate(p, alpha, n_full)

    def start():
        prologue()
        m_ref[...] = jnp.full(m_ref.shape, MASK_VALUE, F32)
        acc_ref[...] = jnp.zeros(acc_ref.shape, F32)

    @pl.when(n_full == 0)
    def _():
        start()
        p, alpha = softmax(causal_mask(logits(0), 0))
        accumulate(p, alpha, 0)
        epilogue()

    @pl.when(n_full > 0)
    def _():
        start()
        s_first = logits(0)
        s_bufs[1][...] = logits(1)
        p_bufs[0][...], a_bufs[0][...] = softmax(s_first)
        later = n_full - 1

        def unrolled(jj, carry):
            run_stages(1 + FLASH_UNROLL * jj, FLASH_UNROLL)
            return carry

        lax.fori_loop(0, later // FLASH_UNROLL, unrolled, 0)
        rest = later % FLASH_UNROLL

        def tail(count):
            run_stages(n_full - count, count)
            finish((1 + count) % 2)
            epilogue()

        for count in range(FLASH_UNROLL):
            pl.when(rest == count)(functools.partial(tail, count))


def _fox_aug(cum_split):
    b, h, s = cum_split[0].shape
    terms = jnp.concatenate(list(cum_split) + [jnp.ones((b, 1, s), BF16)], axis=1)
    terms = jnp.transpose(terms, (0, 2, 1))
    place_q = np.zeros((3 * h + 1, h * HEAD_DIM), np.float32)
    place_k = np.zeros((3 * h + 1, h * HEAD_DIM), np.float32)
    half = FOX_AUG // 2
    for head in range(h):
        for part in range(half):
            place_q[part * h + head, head * HEAD_DIM + part] = 1.0
            place_q[3 * h, head * HEAD_DIM + half + part] = 1.0
            place_k[3 * h, head * HEAD_DIM + part] = 1.0
            place_k[part * h + head, head * HEAD_DIM + half + part] = -1.0
    scatter = lambda place: jnp.einsum("bsr,rl->bsl", terms, jnp.asarray(place, BF16),
                                       preferred_element_type=F32).astype(BF16)
    return scatter(place_q), scatter(place_k)


def _fox_kernel(q_ref, qa_ref, k_ref, ka_ref, v_ref, z_ref, o_ref, qm_ref, *flash):
    t = FOX_TILE
    qi = pl.program_id(2)

    def prologue():
        q = q_ref[0]
        qa = qa_ref[0]
        lane = lax.broadcasted_iota(jnp.int32, q.shape, 1)
        for hh in range(2):
            r = slice(hh * t, (hh + 1) * t)
            own = (lane < HEAD_DIM) if hh == 0 else (lane >= HEAD_DIM)
            qm_ref[r, 0:LANES] = jnp.where(own, q, jnp.zeros_like(q))
            qm_ref[r, LANES:2 * LANES] = jnp.where(own, qa, jnp.zeros_like(qa))

    def load_k(k0):
        return jnp.concatenate([k_ref[0, pl.ds(k0, t), :], ka_ref[0, pl.ds(k0, t), :]], axis=1)

    def pv(p, k0):
        return jnp.concatenate(
            [jnp.dot(p[hh * t:(hh + 1) * t], v_ref[0, hh, pl.ds(k0, t), :], preferred_element_type=F32)
             for hh in range(2)], axis=0)

    def causal_mask(s, k0):
        t_row = qi * t + (lax.broadcasted_iota(jnp.int32, s.shape, 0) & (t - 1))
        pos = k0 + lax.broadcasted_iota(jnp.int32, s.shape, 1)
        return jnp.where(pos <= t_row, s, MASK_VALUE)

    def epilogue():
        acc_ref = flash[1]
        o0 = acc_ref[0:t, :] * (1.0 / acc_ref[0:t, HEAD_DIM:HEAD_DIM + 1])
        o1 = acc_ref[t:2 * t, :] * (1.0 / acc_ref[t:2 * t, HEAD_DIM:HEAD_DIM + 1])
        lane = lax.broadcasted_iota(jnp.int32, o0.shape, 1)
        o = jnp.where(lane < HEAD_DIM, o0, pltpu.roll(o1, HEAD_DIM, 1))
        o_ref[0] = (o * z_ref[0].astype(F32)).astype(BF16)

    _causal_flash(qi, t, qm_ref, load_k, pv, causal_mask, prologue, epilogue, flash)


def _fox(fq, fk, fv, cum_split, fz):
    b, s, w = fq.shape
    t = FOX_TILE
    pairs = w // LANES
    qa, ka = _fox_aug(cum_split)
    tile = pl.BlockSpec((1, t, LANES), lambda bi, hp, i: (bi, i, hp))
    full = pl.BlockSpec((1, s, LANES), lambda bi, hp, i: (bi, 0, hp))
    resident = 4 * s * LANES * 2
    scratch = 2 * t * 2 * LANES * 2 + _flash_scratch_bytes(2 * t, t)
    return pl.pallas_call(
        _fox_kernel,
        grid=(b, pairs, s // t),
        in_specs=[tile, tile, full, full,
                  pl.BlockSpec((1, 2, s, LANES), lambda bi, hp, i: (bi, hp, 0, 0)),
                  tile],
        out_specs=tile,
        out_shape=jax.ShapeDtypeStruct((b, s, w), BF16),
        scratch_shapes=[pltpu.VMEM((2 * t, 2 * LANES), BF16)] + _flash_scratch(2 * t, t),
        compiler_params=pltpu.CompilerParams(
            dimension_semantics=("arbitrary", "arbitrary", "arbitrary"),
            vmem_limit_bytes=_vmem_limit(resident + 4 * t * LANES * 2, scratch + 4 * 2 * t * t * 4)),
        name="fox",
    )(fq, qa, fk, ka, fv, fz)


def _compress_kernel(t_ref, w1_ref, w2_ref, pe_ref, hi_ref, lo_ref):
    hp = lax.Precision.HIGHEST
    half = CMP_STRIDE * HEAD_DIM
    hid = CMP_HIDDEN
    tt = t_ref[0, 0, 0]
    n = tt.shape[0]
    w1 = w1_ref[0]
    both = jnp.dot(tt, w1, precision=hp, preferred_element_type=F32)
    pe_first = jnp.dot(pe_ref[0, :, 0:half], w1, precision=hp, preferred_element_type=F32)
    pe_second = jnp.dot(pe_ref[0, :, half:2 * half], w1, precision=hp, preferred_element_type=F32)
    pe_term = pe_first[0:1, 0:hid] + pe_second[0:1, hid:2 * hid]
    hidden = both[:, 0:hid] + pltpu.roll(both[:, hid:2 * hid], n - 1, 0) + pe_term
    out = jnp.dot(_silu(hidden), w2_ref[0], precision=hp, preferred_element_type=F32)
    hi = out.astype(BF16)
    hi_ref[0, 0, 0] = hi
    lo_ref[0, 0, 0] = (out - hi.astype(F32)).astype(BF16)


def _compress(t, w1, w2, pe):
    kinds, b, g, n, flat = t.shape
    width = w2.shape[-1]
    out_spec = pl.BlockSpec((1, 1, 1, n, width), lambda a, bi, gi: (a, bi, gi, 0, 0))
    out_part = jax.ShapeDtypeStruct((kinds, b, g, n, width), BF16)
    return pl.pallas_call(
        _compress_kernel,
        grid=(kinds, b, g),
        in_specs=[pl.BlockSpec((1, 1, 1, n, flat), lambda a, bi, gi: (a, bi, gi, 0, 0)),
                  pl.BlockSpec((1, flat, 2 * CMP_HIDDEN), lambda a, bi, gi: (a, 0, 0)),
                  pl.BlockSpec((1, CMP_HIDDEN, width), lambda a, bi, gi: (a, 0, 0)),
                  pl.BlockSpec((1, SUBLANES, 2 * flat), lambda a, bi, gi: (a, 0, 0))],
        out_specs=(out_spec, out_spec),
        out_shape=(out_part, out_part),
        compiler_params=pltpu.CompilerParams(
            dimension_semantics=("arbitrary", "arbitrary", "arbitrary"),
            vmem_limit_bytes=_vmem_limit(n * flat * 4 + 2 * flat * CMP_HIDDEN * 4, 4 * n * flat * 4)),
        name="compress",
    )(t, w1, w2, pe)


def _softmax_numerator(s, bias):
    s = s + bias
    m = jnp.max(s, axis=1, keepdims=True)
    m = jnp.where(m == -jnp.inf, 0.0, m)
    return jnp.exp2(s - m)


def _stack_heads(q):
    tq = q.shape[0]
    lane = lax.broadcasted_iota(jnp.int32, (tq, LANES), 1)
    rows = []
    for h in range(NSA_GROUP):
        slab = q[:, (h // 2) * LANES:(h // 2 + 1) * LANES]
        own = (lane < HEAD_DIM) if h % 2 == 0 else (lane >= HEAD_DIM)
        rows.append(jnp.where(own, slab, jnp.zeros_like(slab)))
    return jnp.concatenate(rows, axis=0)


def _group_gates(misc, gi):
    gates = _sigmoid(misc)
    return jnp.where(gi == 0, gates, pltpu.roll(gates, LANES - 3 * NSA_GROUP, 1))


def _gate_lanes(gates, h, j):
    c = MISC_GATE0 + 3 * h + j
    return jnp.broadcast_to(gates[:, c:c + 1], gates.shape)


def _pair_slabs(per_head):
    lane = lax.broadcasted_iota(jnp.int32, per_head[0].shape, 1)
    return [jnp.where(lane < HEAD_DIM, per_head[2 * pp], per_head[2 * pp + 1])
            for pp in range(NSA_GROUP // 2)]


def _cmpwin_kernel(q_ref, kc_hi_ref, kc_lo_ref, vc_ref, overlap_ref, kw_ref, vw_ref, misc_ref,
                   part_ref, selb_ref):
    gi = pl.program_id(1)
    q0 = pl.multiple_of(pl.program_id(2) * NSA_Q, NSA_Q)
    quarter = kc_hi_ref.shape[3] // CMP_QUARTERS
    reach = q0 // (quarter * CMP_STRIDE)
    refs = (q_ref, kc_hi_ref, kc_lo_ref, vc_ref, overlap_ref, kw_ref, vw_ref, misc_ref,
            part_ref, selb_ref)
    for quarters in range(1, CMP_QUARTERS + 1):
        pl.when(reach == quarters - 1)(
            functools.partial(_cmpwin_body, refs, gi, q0, quarters * quarter))


def _cmpwin_body(refs, gi, q0, n_cmp):
    (q_ref, kc_hi_ref, kc_lo_ref, vc_ref, overlap_ref, kw_ref, vw_ref, misc_ref,
     part_ref, selb_ref) = refs
    tq = NSA_Q
    q4 = _stack_heads(q_ref[0])
    n_blk = n_cmp * CMP_STRIDE // SLC_BLOCK
    head_rows = [slice(h * tq, (h + 1) * tq) for h in range(NSA_GROUP)]

    kc = jnp.concatenate([kc_hi_ref[0, 0, 0, 0:n_cmp, :], kc_lo_ref[0, 0, 0, 0:n_cmp, :]], axis=1)
    s = _dot_nt(jnp.concatenate([q4, q4], axis=1), kc)
    col = lax.broadcasted_iota(jnp.int32, (tq, n_cmp), 1)
    t_row = q0 + lax.broadcasted_iota(jnp.int32, (tq, n_cmp), 0)
    cmp_bias = jnp.where(col * CMP_STRIDE + (CMP_BLOCK - 1) <= t_row, 0.0, -jnp.inf)
    pcs = []
    for r in head_rows:
        p = _softmax_numerator(s[r], cmp_bias)
        pcs.append(p * (1.0 / jnp.maximum(jnp.sum(p, axis=1, keepdims=True), 1e-30)))
    oc = jnp.dot(jnp.concatenate(pcs, axis=0).astype(BF16), vc_ref[0, 0, 0, 0:n_cmp, :],
                 preferred_element_type=F32)

    pc_sum = pcs[0] + pcs[1] + pcs[2] + pcs[3]
    overlap = overlap_ref[0:n_blk, 0:n_cmp]
    imp = sum(_dot_nt(overlap, p) for p in _split3(pc_sum))
    blk = lax.broadcasted_iota(jnp.int32, (n_blk, tq), 0)
    cur = (q0 + lax.broadcasted_iota(jnp.int32, (n_blk, tq), 1)) // SLC_BLOCK
    forced = jnp.logical_or(blk == 0, jnp.logical_or(blk == cur, blk == cur - 1))

    def pick_one(_, carry):
        rem, sel = carry
        best = jnp.max(rem, axis=0, keepdims=True)
        first = jnp.min(jnp.where(rem == best, blk, n_blk), axis=0, keepdims=True)
        hit = blk == first
        return jnp.where(hit, -jnp.inf, rem), jnp.where(hit, 1.0, sel)

    candidates = jnp.where(jnp.logical_or(forced, blk > cur), -jnp.inf, imp)
    _, sel = lax.fori_loop(0, min(N_SELECT, n_blk) - N_FORCED, pick_one,
                           (candidates, jnp.where(forced, 1.0, 0.0)), unroll=True)
    all_blk = selb_ref.shape[3]
    if n_blk < all_blk:
        sel = jnp.concatenate([sel, jnp.zeros((all_blk - n_blk, tq), F32)], axis=0)
    selb_ref[0, 0] =jnp.where(sel.T > 0.5, 0.0, MASK_VALUE).astype(BF16)

    start = pl.multiple_of(jnp.maximum(q0 - WINDOW, 0), tq)
    kw = kw_ref[0, 0, pl.ds(start, WIN_KEYS), :]
    vw = vw_ref[0, 0, pl.ds(start, WIN_KEYS), :]
    sw = _dot_nt(q4, kw)
    pos = start + lax.broadcasted_iota(jnp.int32, (tq, WIN_KEYS), 1)
    dist = q0 + lax.broadcasted_iota(jnp.int32, (tq, WIN_KEYS), 0) - pos
    win_bias = jnp.where(jnp.logical_and(dist >= 0, dist < WINDOW), 0.0, -jnp.inf)
    pw = jnp.concatenate([_softmax_numerator(sw[r], win_bias) for r in head_rows],
                         axis=0).astype(BF16)
    ow = jnp.dot(pw, vw, preferred_element_type=F32)
    lw = jnp.dot(pw, jnp.ones((WIN_KEYS, LANES), BF16), preferred_element_type=F32)
    ow = ow * (1.0 / jnp.maximum(lw, 1e-30))

    gates = _group_gates(misc_ref[0], gi)
    gated = [_gate_lanes(gates, h, 0) * oc[r] + _gate_lanes(gates, h, 2) * ow[r]
             for h, r in enumerate(head_rows)]
    for pp, slab in enumerate(_pair_slabs(gated)):
        part_ref[0, :, pp * LANES:(pp + 1) * LANES] = slab


def _block_overlap(n_blk, n_cmp):
    ratio = SLC_BLOCK // CMP_STRIDE
    lo = np.arange(n_blk)[:, None] * ratio - (CMP_BLOCK // CMP_STRIDE - 1)
    i = np.arange(n_cmp)[None, :]
    n_overlap = (SLC_BLOCK + CMP_BLOCK) // CMP_STRIDE - 1
    return ((i >= lo) & (i < lo + n_overlap) & (i < n_cmp - 1)).astype(np.float32)


def _cmpwin(nq, cmp_hi, cmp_lo, kw, vw_ones, misc):
    b, s, w = nq.shape
    g = NSA_KV_HEADS
    tq = NSA_Q
    gw = NSA_GROUP * HEAD_DIM
    n_cmp = cmp_hi.shape[3]
    n_blk = s // SLC_BLOCK
    q_spec = pl.BlockSpec((1, tq, gw), lambda bi, gi, i: (bi, i, gi))
    key_spec = pl.BlockSpec((1, 1, 1, n_cmp, LANES), lambda bi, gi, i: (0, bi, gi, 0, 0))
    val_spec = pl.BlockSpec((1, 1, 1, n_cmp, LANES), lambda bi, gi, i: (1, bi, gi, 0, 0))
    seq_spec = pl.BlockSpec((1, 1, s, LANES), lambda bi, gi, i: (bi, gi, 0, 0))
    resident = 2 * s * LANES * 2 + 3 * n_cmp * LANES * 2 + n_blk * n_cmp * 2
    rows = NSA_GROUP * tq
    return pl.pallas_call(
        _cmpwin_kernel,
        grid=(b, g, s // tq),
        in_specs=[q_spec, key_spec, key_spec, val_spec,
                  pl.BlockSpec((n_blk, n_cmp), lambda bi, gi, i: (0, 0)),
                  seq_spec, seq_spec,
                  pl.BlockSpec((1, tq, LANES), lambda bi, gi, i: (bi, i, 0))],
        out_specs=(q_spec, pl.BlockSpec((1, 1, tq, n_blk), lambda bi, gi, i: (bi, gi, i, 0))),
        out_shape=(jax.ShapeDtypeStruct((b, s, w), F32),
                   jax.ShapeDtypeStruct((b, g, s, n_blk), BF16)),
        compiler_params=pltpu.CompilerParams(
            dimension_semantics=("arbitrary", "arbitrary", "arbitrary"),
            vmem_limit_bytes=_vmem_limit(resident + tq * gw * 8, 12 * rows * WIN_KEYS * 4)),
        name="cmpwin",
    )(nq, cmp_hi, cmp_lo, cmp_hi, jnp.asarray(_block_overlap(n_blk, n_cmp), BF16), kw, vw_ones, misc)


def _sel_kernel(q_ref, k_ref, v_ref, selb_ref, part_ref, misc_ref, z_ref, o_ref,
                qa_ref, gate_ref, *flash):
    tq = SEL_Q
    gi = pl.program_id(1)
    qi = pl.program_id(2)
    q0 = pl.multiple_of(qi * tq, tq)
    n_full = q0 // SEL_KV

    def prologue():
        q = q_ref[0]
        selb = selb_ref[0, 0]
        lane = lax.broadcasted_iota(jnp.int32, (tq, LANES), 1)
        for h in range(NSA_GROUP):
            r = slice(h * tq, (h + 1) * tq)
            slab = q[:, (h // 2) * LANES:(h // 2 + 1) * LANES]
            own = (lane < HEAD_DIM) if h % 2 == 0 else (lane >= HEAD_DIM)
            qa_ref[r, 0:LANES] = jnp.where(own, slab, jnp.zeros_like(slab))
            qa_ref[r, LANES:2 * LANES] = selb
        gates = _group_gates(misc_ref[0], gi)
        for h in range(NSA_GROUP):
            gate_ref[h] = _gate_lanes(gates, h, 1)

    def causal_mask(s, k0):
        t_row = q0 + (lax.broadcasted_iota(jnp.int32, s.shape, 0) & (tq - 1))
        pos = k0 + lax.broadcasted_iota(jnp.int32, s.shape, 1)
        return jnp.where(pos <= t_row, s, MASK_VALUE)

    def epilogue():
        acc_ref = flash[1]
        gated = []
        for h in range(NSA_GROUP):
            acc = acc_ref[h * tq:(h + 1) * tq, :]
            scaled = acc * (gate_ref[h] * (1.0 / acc[:, HEAD_DIM:HEAD_DIM + 1]))
            gated.append(scaled if h % 2 == 0 else pltpu.roll(scaled, HEAD_DIM, 1))
        for pp, slab in enumerate(_pair_slabs(gated)):
            c = slice(pp * LANES, (pp + 1) * LANES)
            o_ref[0, :, c] = ((part_ref[0, :, c] + slab) * z_ref[0, :, c].astype(F32)).astype(BF16)

    _causal_flash(n_full, SEL_KV, qa_ref,
                  lambda k0: k_ref[0, 0, pl.ds(k0, SEL_KV), :],
                  lambda p, k0: jnp.dot(p, v_ref[0, 0, pl.ds(k0, SEL_KV), :],
                                        preferred_element_type=F32),
                  causal_mask, prologue, epilogue, flash)


def _sel(nq, k_aug, v_ones, selb, part, misc, nz):
    b, s, w = nq.shape
    g = NSA_KV_HEADS
    tq = SEL_Q
    gw = NSA_GROUP * HEAD_DIM
    n_blk = s // SLC_BLOCK
    rows = NSA_GROUP * tq
    q_spec = pl.BlockSpec((1, tq, gw), lambda bi, gi, i: (bi, i, gi))
    resident = s * 3 * LANES * 2
    return pl.pallas_call(
        _sel_kernel,
        grid=(b, g, s // tq),
        in_specs=[q_spec,
                  pl.BlockSpec((1, 1, s, 2 * LANES), lambda bi, gi, i: (bi, gi, 0, 0)),
                  pl.BlockSpec((1, 1, s, LANES), lambda bi, gi, i: (bi, gi, 0, 0)),
                  pl.BlockSpec((1, 1, tq, n_blk), lambda bi, gi, i: (bi, gi, i, 0)),
                  q_spec,
                  pl.BlockSpec((1, tq, LANES), lambda bi, gi, i: (bi, i, 0)),
                  q_spec],
        out_specs=q_spec,
        out_shape=jax.ShapeDtypeStruct((b, s, w), BF16),
        scratch_shapes=[pltpu.VMEM((rows, 2 * LANES), BF16),
                        pltpu.VMEM((NSA_GROUP, tq, LANES), F32)] + _flash_scratch(rows, SEL_KV),
        compiler_params=pltpu.CompilerParams(
            dimension_semantics=("arbitrary", "arbitrary", "arbitrary"),
            vmem_limit_bytes=_vmem_limit(resident + tq * gw * 10,
                                         _flash_scratch_bytes(rows, SEL_KV) + 4 * rows * SEL_KV * 4)),
        name="sel",
    )(nq, k_aug, v_ones, selb, part, misc, nz)


def _out_kernel(yf_ref, yn_ref, w_ref, g_ref, mod_ref, x_ref, o_ref):
    y = (jnp.dot(yf_ref[0], w_ref[0:FOX_WIDTH, :], preferred_element_type=F32)
         + jnp.dot(yn_ref[0], w_ref[FOX_WIDTH:, :], preferred_element_type=F32))
    yn = y * lax.rsqrt(jnp.mean(y * y, axis=-1, keepdims=True) + RMS_EPS)
    o_ref[0] = x_ref[0] + mod_ref[0, 2:3, :] * (yn * g_ref[...])


def _out(y_fox, y_nsa, w_out, g_post, mod, x):
    b, s, d = x.shape
    tm = PROJ_ROWS
    row = lambda bi, i: (bi, i, 0)
    half = pl.BlockSpec((1, tm, FOX_WIDTH), row)
    pipelined = 2 * tm * FOX_WIDTH * 2 + 2 * tm * d * 4 + d * d * 2
    return pl.pallas_call(
        _out_kernel,
        grid=(b, s // tm),
        in_specs=[half, half,
                  pl.BlockSpec((d, d), lambda bi, i: (0, 0)),
                  pl.BlockSpec((1, d), lambda bi, i: (0, 0)),
                  pl.BlockSpec((1, 3, d), lambda bi, i: (bi, 0, 0)),
                  pl.BlockSpec((1, tm, d), row)],
        out_specs=pl.BlockSpec((1, tm, d), row),
        out_shape=jax.ShapeDtypeStruct((b, s, d), F32),
        compiler_params=pltpu.CompilerParams(
            dimension_semantics=("arbitrary", "arbitrary"),
            vmem_limit_bytes=_vmem_limit(pipelined, 6 * tm * d * 4)),
        name="out",
    )(y_fox, y_nsa, w_out, g_post, mod, x)


def _rope_slabs(seq_len):
    inv = 1.0 / (ROPE_THETA ** (jnp.arange(0, HEAD_DIM, 2, dtype=F32) / HEAD_DIM))
    ang = jnp.arange(seq_len, dtype=F32)[:, None] * inv[None, :]
    cos, sin = jnp.cos(ang), jnp.sin(ang)
    reps = LANES // (HEAD_DIM // 2)
    sign = jnp.tile(jnp.concatenate([-jnp.ones((HEAD_DIM // 2,), F32), jnp.ones((HEAD_DIM // 2,), F32)]),
                    LANES // HEAD_DIM)
    return jnp.tile(cos, (1, reps)), jnp.tile(sin, (1, reps)) * sign[None, :]


def _reorder_w_in(w_in):
    fw, kv = FOX_WIDTH, NSA_KV_WIDTH
    o = 0
    cols = {}
    for name, n in (("fq", fw), ("fk", fw), ("fv", fw), ("ff", FOX_HEADS), ("fz", fw), ("nq", NSA_WIDTH),
                    ("kc", kv), ("vc", kv), ("ks", kv), ("vs", kv), ("kw", kv), ("vw", kv),
                    ("ng", 3 * NSA_HEADS), ("nz", NSA_WIDTH)):
        cols[name] = w_in[:, o:o + n]
        o += n
    pad = jnp.zeros((w_in.shape[0], LANES - FOX_HEADS - 3 * NSA_HEADS), w_in.dtype)
    order = ("fq", "fk", "fv", "fz", "nq", "kc", "vc", "ks", "vs", "kw", "vw", "nz", "ff", "ng")
    return jnp.concatenate([cols[k] for k in order] + [pad], axis=1).astype(BF16)


def _layer(x, c8, g_pre, g_post, w_ada, b_ada, w_in, b_forget, w_cmp_k1, w_cmp_k2,
           w_cmp_v1, w_cmp_v2, pe_cmp_k, pe_cmp_v, w_out, cos128, sin128):
    b, s, d = x.shape
    mod = _ada(c8, w_ada, b_ada)[:b].reshape(b, 3, d)
    bf128 = jnp.pad(b_forget, (0, LANES - FOX_HEADS)).reshape(1, LANES)
    (fq, fk, fv, fz, nq, cmp_in, ks_aug, vs_ones, kw2, vw_ones, nz, misc, log_f) = _proj(
        x, mod, g_pre.reshape(1, d), _reorder_w_in(w_in), cos128, sin128, bf128)

    cum_split = tuple(p.reshape(b, FOX_HEADS, s)
                      for p in _cumsum_lanes_split(log_f.reshape(b * FOX_HEADS, s)))
    y_fox = _fox(fq, fk, fv, cum_split, fz)

    flat = CMP_STRIDE * HEAD_DIM
    t = cmp_in.reshape(2, b, NSA_KV_HEADS, s // CMP_STRIDE, flat)
    pe = jnp.stack([pe_cmp_k.reshape(1, 2 * flat), pe_cmp_v.reshape(1, 2 * flat)])
    w1 = jnp.stack([w_cmp_k1, w_cmp_v1])
    w2 = jnp.stack([w_cmp_k2, w_cmp_v2])
    cmp_hi, cmp_lo = _compress(t, jnp.concatenate([w1[:, :flat], w1[:, flat:]], axis=-1),
                               jnp.concatenate([w2, w2], axis=-1),
                               jnp.broadcast_to(pe, (2, SUBLANES, 2 * flat)))
    part, selb = _cmpwin(nq, cmp_hi, cmp_lo, kw2, vw_ones, misc)
    y_nsa = _sel(nq, ks_aug, vs_ones, selb, part, misc, nz)

    return _out(y_fox, y_nsa, w_out.astype(BF16), g_post.reshape(1, d), mod, x)


def kernel(x, c, g_pre, g_post, w_ada, b_ada, w_in, b_forget, w_cmp_k1, w_cmp_k2,
           w_cmp_v1, w_cmp_v2, pe_cmp_k, pe_cmp_v, w_out):
    cos128, sin128 = _rope_slabs(x.shape[1])
    c8 = jnp.pad(c, ((0, SUBLANES - c.shape[0]), (0, 0)))
    for layer in range(g_pre.shape[0]):
        x = _layer(x, c8, g_pre[layer], g_post[layer], w_ada[layer], b_ada[layer], w_in[layer],
                   b_forget[layer], w_cmp_k1[layer], w_cmp_k2[layer], w_cmp_v1[layer],
                   w_cmp_v2[layer], pe_cmp_k[layer], pe_cmp_v[layer], w_out[layer], cos128, sin128)
    return x
```

```python
import functools

import jax
import jax.numpy as jnp
import numpy as np
from jax import lax
from jax.experimental import pallas as pl
from jax.experimental.pallas import tpu as pltpu

F32 = jnp.float32
BF16 = jnp.bfloat16

D_MODEL = 1024
HEAD_DIM = 64
FOX_WIDTH = 512
NSA_WIDTH = 512
FOX_HEADS = 8
NSA_HEADS = 8
NSA_KV_HEADS = 2
NSA_GROUP = 4
NSA_KV_WIDTH = 128
CMP_BLOCK = 32
CMP_STRIDE = 16
CMP_HIDDEN = 128
SLC_BLOCK = 64
N_SELECT = 16
N_FORCED = 3
WINDOW = 512
ROPE_THETA = 10000.0
RMS_EPS = 1e-6
LOG2E = 1.4426950408889634
QK_SCALE = HEAD_DIM ** -0.5 * LOG2E

LANES = 128
SUBLANES = 8
MIB = 1024 * 1024
V7X_VMEM_BYTES = 64 * MIB
VMEM_LIMIT_FLOOR = 16 * MIB
VMEM_COMPILER_RESERVE = 8 * MIB
MASK_VALUE = -1e30

PROJ_ROWS = 512
FOX_TILE = 512
NSA_Q = 256
SEL_Q = 128
SEL_KV = 512
WIN_KEYS = WINDOW + NSA_Q
FLASH_UNROLL = 6
CMP_QUARTERS = 4
FOX_AUG = 6

C_FQ, C_FK, C_FV, C_FZ, C_NQ = 0, 512, 1024, 1536, 2048
C_KC, C_VC, C_KS, C_VS, C_KW, C_VW = 2560, 2688, 2816, 2944, 3072, 3200
C_NZ, C_MISC, PROJ_COLS = 3328, 3840, 3968
MISC_GATE0 = FOX_HEADS


def _vmem_limit(pipelined_bytes, resident_bytes):
    need = 2 * pipelined_bytes + resident_bytes
    return int(min(max(need, VMEM_LIMIT_FLOOR), V7X_VMEM_BYTES - VMEM_COMPILER_RESERVE))


def _sigmoid(v):
    return 1.0 / (1.0 + jnp.exp(-v))


def _silu(v):
    return v * _sigmoid(v)


def _dot_nt(a, b):
    return lax.dot_general(a, b, (((1,), (1,)), ((), ())), preferred_element_type=F32)


def _split3(v):
    hi = v.astype(BF16)
    r1 = v - hi.astype(F32)
    mid = r1.astype(BF16)
    lo = (r1 - mid.astype(F32)).astype(BF16)
    return hi, mid, lo


def _ada_kernel(c_ref, w_ref, b_ref, o_ref):
    a = _silu(c_ref[...])
    o_ref[...] = jnp.dot(a, w_ref[...], precision=lax.Precision.HIGHEST,
                         preferred_element_type=F32) + b_ref[...]


def _ada(c8, w_ada, b_ada):
    n = w_ada.shape[1]
    blk = D_MODEL
    return pl.pallas_call(
        _ada_kernel,
        grid=(n // blk,),
        in_specs=[pl.BlockSpec((SUBLANES, D_MODEL), lambda j: (0, 0)),
                  pl.BlockSpec((D_MODEL, blk), lambda j: (0, j)),
                  pl.BlockSpec((1, blk), lambda j: (0, j))],
        out_specs=pl.BlockSpec((SUBLANES, blk), lambda j: (0, j)),
        out_shape=jax.ShapeDtypeStruct((SUBLANES, n), F32),
        compiler_params=pltpu.CompilerParams(
            dimension_semantics=("arbitrary",),
            vmem_limit_bytes=_vmem_limit(D_MODEL * blk * 4, D_MODEL * blk * 4)),
        name="ada",
    )(c8, w_ada, b_ada.reshape(1, n))


def _rope128(t, cos, sin_signed):
    lane = lax.broadcasted_iota(jnp.int32, t.shape, 1)
    first_half = (lane & (HEAD_DIM - 1)) < HEAD_DIM // 2
    partner = jnp.where(first_half,
                        pltpu.roll(t, LANES - HEAD_DIM // 2, 1),
                        pltpu.roll(t, HEAD_DIM // 2, 1))
    return t * cos + partner * sin_signed


def _proj_kernel(x_ref, mod_ref, g_ref, w_ref, cos_ref, sin_ref, bf_ref,
                 fq_ref, fk_ref, fv_ref, fz_ref, nq_ref, cmp_ref,
                 ks_ref, vs_ref, kw_ref, vw_ref, nz_ref, misc_ref, lf_ref):
    tm = x_ref.shape[1]
    x = x_ref[0]
    y = x * lax.rsqrt(jnp.mean(x * x, axis=-1, keepdims=True) + RMS_EPS)
    y = y * g_ref[...]
    h = (y * (1.0 + mod_ref[0, 1:2, :]) + mod_ref[0, 0:1, :]).astype(BF16)
    cos = cos_ref[...]
    sin = sin_ref[...]

    def mm(lo, n):
        return jnp.dot(h, w_ref[:, lo:lo + n], preferred_element_type=F32)

    def slab_pair(lo):
        both = mm(lo, 2 * LANES)
        return both[:, 0:LANES], both[:, LANES:2 * LANES]

    lane = lax.broadcasted_iota(jnp.int32, (tm, LANES), 1)
    low = lane < HEAD_DIM
    ones_col = jnp.where(lane == HEAD_DIM, 1.0, 0.0).astype(BF16)

    def doubled(slab):
        swapped = pltpu.roll(slab, HEAD_DIM, 1)
        return jnp.where(low, slab, swapped), jnp.where(low, swapped, slab)

    def with_ones(slab):
        swapped = pltpu.roll(slab, HEAD_DIM, 1)
        return jnp.where(low, slab, ones_col), jnp.where(low, swapped, ones_col)

    fq_ref[0] = (mm(C_FQ, FOX_WIDTH) * QK_SCALE).astype(BF16)
    fk_ref[0] = mm(C_FK, FOX_WIDTH).astype(BF16)
    fv = mm(C_FV, FOX_WIDTH).astype(BF16)
    for pair in range(FOX_WIDTH // LANES):
        fv_ref[0, 2 * pair], fv_ref[0, 2 * pair + 1] = with_ones(fv[:, pair * LANES:(pair + 1) * LANES])
    fz_ref[0] = _silu(mm(C_FZ, FOX_WIDTH)).astype(BF16)
    for j in range(NSA_WIDTH // (2 * LANES)):
        for i, t in enumerate(slab_pair(C_NQ + 2 * j * LANES)):
            c = (2 * j + i) * LANES
            nq_ref[0, :, c:c + LANES] = (_rope128(t, cos, sin) * QK_SCALE).astype(BF16)

    kc, vc = slab_pair(C_KC)
    for kind, slab in enumerate((_rope128(kc, cos, sin), vc)):
        for g in range(NSA_KV_HEADS):
            cmp_ref[kind, 0, g] = slab[:, g * HEAD_DIM:(g + 1) * HEAD_DIM]
    pos = pl.program_id(1) * tm + lax.broadcasted_iota(jnp.int32, (tm, LANES), 0)
    onehot = jnp.where(pos // SLC_BLOCK == lane, 1.0, 0.0).astype(BF16)
    ks, vs = slab_pair(C_KS)
    for g, k2 in enumerate(doubled(_rope128(ks, cos, sin).astype(BF16))):
        ks_ref[0, g, :, 0:LANES] = k2
        ks_ref[0, g, :, LANES:2 * LANES] = onehot
    vs_ref[0, 0], vs_ref[0, 1] = with_ones(vs.astype(BF16))
    kw, vw = slab_pair(C_KW)
    kw_ref[0, 0], kw_ref[0, 1] = doubled(_rope128(kw, cos, sin).astype(BF16))
    vw_ref[0, 0], vw_ref[0, 1] = doubled(vw.astype(BF16))
    nz_ref[0] = _silu(mm(C_NZ, NSA_WIDTH)).astype(BF16)
    misc = mm(C_MISC, LANES)
    misc_ref[0] = misc
    z = misc + bf_ref[...]
    log_f = jnp.minimum(z, 0.0) - jnp.log1p(jnp.exp(-jnp.abs(z)))
    lf_ref[0] = log_f.T[0:FOX_HEADS, :]


def _proj(x, mod, g_pre, w_cat, cos128, sin128, bf128):
    b, s, d = x.shape
    tm = PROJ_ROWS
    row = lambda bi, i: (bi, i, 0)
    wide = lambda dt: jax.ShapeDtypeStruct((b, s, FOX_WIDTH), dt)
    g = NSA_KV_HEADS
    grouped = lambda lanes: jax.ShapeDtypeStruct((b, g, s, lanes), BF16)
    grouped_spec = lambda lanes: pl.BlockSpec((1, g, tm, lanes), lambda bi, i: (bi, 0, i, 0))
    out_shape = (wide(BF16), wide(BF16), jax.ShapeDtypeStruct((b, FOX_HEADS, s, LANES), BF16),
                 wide(BF16), wide(BF16),
                 jax.ShapeDtypeStruct((2, b, g, s, HEAD_DIM), F32),
                 grouped(2 * LANES), grouped(LANES), grouped(LANES), grouped(LANES),
                 wide(BF16), jax.ShapeDtypeStruct((b, s, LANES), F32),
                 jax.ShapeDtypeStruct((b, FOX_HEADS, s), F32))
    wide_spec = pl.BlockSpec((1, tm, FOX_WIDTH), row)
    out_specs = (wide_spec, wide_spec,
                 pl.BlockSpec((1, FOX_HEADS, tm, LANES), lambda bi, i: (bi, 0, i, 0)),
                 wide_spec, wide_spec) + (
        pl.BlockSpec((2, 1, g, tm, HEAD_DIM), lambda bi, i: (0, bi, 0, i, 0)),
        grouped_spec(2 * LANES), grouped_spec(LANES), grouped_spec(LANES), grouped_spec(LANES),
        wide_spec, pl.BlockSpec((1, tm, LANES), row),
        pl.BlockSpec((1, FOX_HEADS, tm), lambda bi, i: (bi, 0, i)))
    pipelined = (tm * d * 4 + tm * (7 * FOX_WIDTH * 2 + 2 * g * LANES * 4 + 5 * g * LANES * 2 + LANES * 4)
                 + d * PROJ_COLS * 2)
    return pl.pallas_call(
        _proj_kernel,
        grid=(b, s // tm),
        in_specs=[pl.BlockSpec((1, tm, d), row),
                  pl.BlockSpec((1, 3, d), lambda bi, i: (bi, 0, 0)),
                  pl.BlockSpec((1, d), lambda bi, i: (0, 0)),
                  pl.BlockSpec((d, PROJ_COLS), lambda bi, i: (0, 0)),
                  pl.BlockSpec((tm, LANES), lambda bi, i: (i, 0)),
                  pl.BlockSpec((tm, LANES), lambda bi, i: (i, 0)),
                  pl.BlockSpec((1, LANES), lambda bi, i: (0, 0))],
        out_specs=out_specs,
        out_shape=out_shape,
        compiler_params=pltpu.CompilerParams(
            dimension_semantics=("arbitrary", "arbitrary"),
            vmem_limit_bytes=_vmem_limit(pipelined, 4 * tm * d * 4)),
        name="proj",
    )(x, mod, g_pre, w_cat, cos128, sin128, bf128)


def _cumsum_kernel(x_ref, hi_ref, mid_ref, lo_ref, *, chunks):
    x = x_ref[...]
    n = x.shape[0]
    parts = _split3(x)
    r = lax.broadcasted_iota(jnp.int32, (LANES, LANES), 0)
    c = lax.broadcasted_iota(jnp.int32, (LANES, LANES), 1)
    tri = (r <= c).astype(BF16)
    rr = lax.broadcasted_iota(jnp.int32, (n, n), 0)
    cc = lax.broadcasted_iota(jnp.int32, (n, n), 1)
    earlier = jnp.logical_and(cc < rr, (cc // chunks) == (rr // chunks)).astype(BF16)
    within = sum(jnp.dot(p, tri, preferred_element_type=F32) for p in parts)
    before = sum(jnp.dot(earlier, p, preferred_element_type=F32) for p in parts)
    total = (within + jnp.sum(before, axis=-1, keepdims=True)) * LOG2E
    hi_ref[...], mid_ref[...], lo_ref[...] = _split3(total)


def _cumsum_lanes_split(v):
    rows, s = v.shape
    chunks = s // LANES
    n = rows * chunks
    part = jax.ShapeDtypeStruct((n, LANES), BF16)
    parts = pl.pallas_call(
        functools.partial(_cumsum_kernel, chunks=chunks),
        out_shape=(part, part, part),
        compiler_params=pltpu.CompilerParams(
            vmem_limit_bytes=_vmem_limit(2 * n * LANES * 4, 6 * n * n)),
        name="cumsum",
    )(v.reshape(n, LANES))
    return tuple(p.reshape(rows, s) for p in parts)


def _flash_scratch(rows, tile):
    return ([pltpu.VMEM((rows, LANES), F32)] * 2
            + [pltpu.VMEM((rows, tile), F32)] * 2
            + [pltpu.VMEM((rows, tile), BF16)] * 2
            + [pltpu.VMEM((rows, LANES), F32)] * 2)


def _flash_scratch_bytes(rows, tile):
    return rows * (2 * LANES * 4 + 2 * tile * 4 + 2 * tile * 2 + 2 * LANES * 4)


def _causal_flash(n_full, tile, q_ref, load_k, pv, causal_mask, prologue, epilogue, scratch):
    m_ref, acc_ref, s0, s1, p0, p1, a0, a1 = scratch
    s_bufs, p_bufs, a_bufs = (s0, s1), (p0, p1), (a0, a1)

    def logits(j):
        return _dot_nt(q_ref[...], load_k(pl.multiple_of(j * tile, tile)))

    def softmax(s):
        m_prev = m_ref[...]
        m_next = jnp.maximum(m_prev, jnp.max(s, axis=1, keepdims=True))
        m_ref[...] = m_next
        p = jnp.exp2(s - jnp.tile(m_next, (1, tile // LANES)))
        return p.astype(BF16), jnp.exp2(m_prev - m_next)

    def accumulate(p, alpha, j):
        acc_ref[...] = acc_ref[...] * alpha + pv(p, pl.multiple_of(j * tile, tile))

    def stage(j, cur):
        s_bufs[1 - cur][...] = logits(j + 1)
        p_bufs[cur][...], a_bufs[cur][...] = softmax(s_bufs[cur][...])
        accumulate(p_bufs[1 - cur][...], a_bufs[1 - cur][...], j - 1)

    def run_stages(first, count):
        for i in range(count):
            stage(first + i, (1 + i) % 2)

    def finish(cur):
        p, alpha = softmax(causal_mask(s_bufs[cur][...], n_full * tile))
        accumulate(p_bufs[1 - cur][...], a_bufs[1 - cur][...], n_full - 1)
        accumulate(p, alpha, n_full)

    def start():
        prologue()
        m_ref[...] = jnp.full(m_ref.shape, MASK_VALUE, F32)
        acc_ref[...] = jnp.zeros(acc_ref.shape, F32)

    @pl.when(n_full == 0)
    def _():
        start()
        p, alpha = softmax(causal_mask(logits(0), 0))
        accumulate(p, alpha, 0)
        epilogue()

    @pl.when(n_full > 0)
    def _():
        start()
        s_first = logits(0)
        s_bufs[1][...] = logits(1)
        p_bufs[0][...], a_bufs[0][...] = softmax(s_first)
        later = n_full - 1

        def unrolled(jj, carry):
            run_stages(1 + FLASH_UNROLL * jj, FLASH_UNROLL)
            return carry

        lax.fori_loop(0, later // FLASH_UNROLL, unrolled, 0)
        rest = later % FLASH_UNROLL

        def tail(count):
            run_stages(n_full - count, count)
            finish((1 + count) % 2)
            epilogue()

        for count in range(FLASH_UNROLL):
            pl.when(rest == count)(functools.partial(tail, count))


def _fox_aug(cum_split):
    b, h, s = cum_split[0].shape
    terms = jnp.concatenate(list(cum_split) + [jnp.ones((b, 1, s), BF16)], axis=1)
    terms = jnp.transpose(terms, (0, 2, 1))
    place_q = np.zeros((3 * h + 1, h * HEAD_DIM), np.float32)
    place_k = np.zeros((3 * h + 1, h * HEAD_DIM), np.float32)
    half = FOX_AUG // 2
    for head in range(h):
        for part in range(half):
            place_q[part * h + head, head * HEAD_DIM + part] = 1.0
            place_q[3 * h, head * HEAD_DIM + half + part] = 1.0
            place_k[3 * h, head * HEAD_DIM + part] = 1.0
            place_k[part * h + head, head * HEAD_DIM + half + part] = -1.0
    scatter = lambda place: jnp.einsum("bsr,rl->bsl", terms, jnp.asarray(place, BF16),
                                       preferred_element_type=F32).astype(BF16)
    return scatter(place_q), scatter(place_k)


def _fox_kernel(q_ref, qa_ref, k_ref, ka_ref, v_ref, z_ref, o_ref, qm_ref, *flash):
    t = FOX_TILE
    qi = pl.program_id(2)

    def prologue():
        q = q_ref[0]
        qa = qa_ref[0]
        lane = lax.broadcasted_iota(jnp.int32, q.shape, 1)
        for hh in range(2):
            r = slice(hh * t, (hh + 1) * t)
            own = (lane < HEAD_DIM) if hh == 0 else (lane >= HEAD_DIM)
            qm_ref[r, 0:LANES] = jnp.where(own, q, jnp.zeros_like(q))
            qm_ref[r, LANES:2 * LANES] = jnp.where(own, qa, jnp.zeros_like(qa))

    def load_k(k0):
        return jnp.concatenate([k_ref[0, pl.ds(k0, t), :], ka_ref[0, pl.ds(k0, t), :]], axis=1)

    def pv(p, k0):
        return jnp.concatenate(
            [jnp.dot(p[hh * t:(hh + 1) * t], v_ref[0, hh, pl.ds(k0, t), :], preferred_element_type=F32)
             for hh in range(2)], axis=0)

    def causal_mask(s, k0):
        t_row = qi * t + (lax.broadcasted_iota(jnp.int32, s.shape, 0) & (t - 1))
        pos = k0 + lax.broadcasted_iota(jnp.int32, s.shape, 1)
        return jnp.where(pos <= t_row, s, MASK_VALUE)

    def epilogue():
        acc_ref = flash[1]
        o0 = acc_ref[0:t, :] * (1.0 / acc_ref[0:t, HEAD_DIM:HEAD_DIM + 1])
        o1 = acc_ref[t:2 * t, :] * (1.0 / acc_ref[t:2 * t, HEAD_DIM:HEAD_DIM + 1])
        lane = lax.broadcasted_iota(jnp.int32, o0.shape, 1)
        o = jnp.where(lane < HEAD_DIM, o0, pltpu.roll(o1, HEAD_DIM, 1))
        o_ref[0] = (o * z_ref[0].astype(F32)).astype(BF16)

    _causal_flash(qi, t, qm_ref, load_k, pv, causal_mask, prologue, epilogue, flash)


def _fox(fq, fk, fv, cum_split, fz):
    b, s, w = fq.shape
    t = FOX_TILE
    pairs = w // LANES
    qa, ka = _fox_aug(cum_split)
    tile = pl.BlockSpec((1, t, LANES), lambda bi, hp, i: (bi, i, hp))
    full = pl.BlockSpec((1, s, LANES), lambda bi, hp, i: (bi, 0, hp))
    resident = 4 * s * LANES * 2
    scratch = 2 * t * 2 * LANES * 2 + _flash_scratch_bytes(2 * t, t)
    return pl.pallas_call(
        _fox_kernel,
        grid=(b, pairs, s // t),
        in_specs=[tile, tile, full, full,
                  pl.BlockSpec((1, 2, s, LANES), lambda bi, hp, i: (bi, hp, 0, 0)),
                  tile],
        out_specs=tile,
        out_shape=jax.ShapeDtypeStruct((b, s, w), BF16),
        scratch_shapes=[pltpu.VMEM((2 * t, 2 * LANES), BF16)] + _flash_scratch(2 * t, t),
        compiler_params=pltpu.CompilerParams(
            dimension_semantics=("arbitrary", "arbitrary", "arbitrary"),
            vmem_limit_bytes=_vmem_limit(resident + 4 * t * LANES * 2, scratch + 4 * 2 * t * t * 4)),
        name="fox",
    )(fq, qa, fk, ka, fv, fz)


def _compress_kernel(t_ref, w1_ref, w2_ref, pe_ref, hi_ref, lo_ref):
    hp = lax.Precision.HIGHEST
    half = CMP_STRIDE * HEAD_DIM
    hid = CMP_HIDDEN
    tt = t_ref[0, 0, 0]
    n = tt.shape[0]
    w1 = w1_ref[0]
    both = jnp.dot(tt, w1, precision=hp, preferred_element_type=F32)
    pe_first = jnp.dot(pe_ref[0, :, 0:half], w1, precision=hp, preferred_element_type=F32)
    pe_second = jnp.dot(pe_ref[0, :, half:2 * half], w1, precision=hp, preferred_element_type=F32)
    pe_term = pe_first[0:1, 0:hid] + pe_second[0:1, hid:2 * hid]
    hidden = both[:, 0:hid] + pltpu.roll(both[:, hid:2 * hid], n - 1, 0) + pe_term
    out = jnp.dot(_silu(hidden), w2_ref[0], precision=hp, preferred_element_type=F32)
    hi = out.astype(BF16)
    hi_ref[0, 0, 0] = hi
    lo_ref[0, 0, 0] = (out - hi.astype(F32)).astype(BF16)


def _compress(t, w1, w2, pe):
    kinds, b, g, n, flat = t.shape
    width = w2.shape[-1]
    out_spec = pl.BlockSpec((1, 1, 1, n, width), lambda a, bi, gi: (a, bi, gi, 0, 0))
    out_part = jax.ShapeDtypeStruct((kinds, b, g, n, width), BF16)
    return pl.pallas_call(
        _compress_kernel,
        grid=(kinds, b, g),
        in_specs=[pl.BlockSpec((1, 1, 1, n, flat), lambda a, bi, gi: (a, bi, gi, 0, 0)),
                  pl.BlockSpec((1, flat, 2 * CMP_HIDDEN), lambda a, bi, gi: (a, 0, 0)),
                  pl.BlockSpec((1, CMP_HIDDEN, width), lambda a, bi, gi: (a, 0, 0)),
                  pl.BlockSpec((1, SUBLANES, 2 * flat), lambda a, bi, gi: (a, 0, 0))],
        out_specs=(out_spec, out_spec),
        out_shape=(out_part, out_part),
        compiler_params=pltpu.CompilerParams(
            dimension_semantics=("arbitrary", "arbitrary", "arbitrary"),
            vmem_limit_bytes=_vmem_limit(n * flat * 4 + 2 * flat * CMP_HIDDEN * 4, 4 * n * flat * 4)),
        name="compress",
    )(t, w1, w2, pe)


def _softmax_numerator(s, bias):
    s = s + bias
    m = jnp.max(s, axis=1, keepdims=True)
    m = jnp.where(m == -jnp.inf, 0.0, m)
    return jnp.exp2(s - m)


def _stack_heads(q):
    tq = q.shape[0]
    lane = lax.broadcasted_iota(jnp.int32, (tq, LANES), 1)
    rows = []
    for h in range(NSA_GROUP):
        slab = q[:, (h // 2) * LANES:(h // 2 + 1) * LANES]
        own = (lane < HEAD_DIM) if h % 2 == 0 else (lane >= HEAD_DIM)
        rows.append(jnp.where(own, slab, jnp.zeros_like(slab)))
    return jnp.concatenate(rows, axis=0)


def _group_gates(misc, gi):
    gates = _sigmoid(misc)
    return jnp.where(gi == 0, gates, pltpu.roll(gates, LANES - 3 * NSA_GROUP, 1))


def _gate_lanes(gates, h, j):
    c = MISC_GATE0 + 3 * h + j
    return jnp.broadcast_to(gates[:, c:c + 1], gates.shape)


def _pair_slabs(per_head):
    lane = lax.broadcasted_iota(jnp.int32, per_head[0].shape, 1)
    return [jnp.where(lane < HEAD_DIM, per_head[2 * pp], per_head[2 * pp + 1])
            for pp in range(NSA_GROUP // 2)]


def _cmpwin_kernel(q_ref, kc_hi_ref, kc_lo_ref, vc_ref, overlap_ref, kw_ref, vw_ref, misc_ref,
                   part_ref, selb_ref):
    gi = pl.program_id(1)
    q0 = pl.multiple_of(pl.program_id(2) * NSA_Q, NSA_Q)
    quarter = kc_hi_ref.shape[3] // CMP_QUARTERS
    reach = q0 // (quarter * CMP_STRIDE)
    refs = (q_ref, kc_hi_ref, kc_lo_ref, vc_ref, overlap_ref, kw_ref, vw_ref, misc_ref,
            part_ref, selb_ref)
    for quarters in range(1, CMP_QUARTERS + 1):
        pl.when(reach == quarters - 1)(
            functools.partial(_cmpwin_body, refs, gi, q0, quarters * quarter))


def _cmpwin_body(refs, gi, q0, n_cmp):
    (q_ref, kc_hi_ref, kc_lo_ref, vc_ref, overlap_ref, kw_ref, vw_ref, misc_ref,
     part_ref, selb_ref) = refs
    tq = NSA_Q
    q4 = _stack_heads(q_ref[0])
    n_blk = n_cmp * CMP_STRIDE // SLC_BLOCK
    head_rows = [slice(h * tq, (h + 1) * tq) for h in range(NSA_GROUP)]

    kc = jnp.concatenate([kc_hi_ref[0, 0, 0, 0:n_cmp, :], kc_lo_ref[0, 0, 0, 0:n_cmp, :]], axis=1)
    s = _dot_nt(jnp.concatenate([q4, q4], axis=1), kc)
    col = lax.broadcasted_iota(jnp.int32, (tq, n_cmp), 1)
    t_row = q0 + lax.broadcasted_iota(jnp.int32, (tq, n_cmp), 0)
    cmp_bias = jnp.where(col * CMP_STRIDE + (CMP_BLOCK - 1) <= t_row, 0.0, -jnp.inf)
    pcs = []
    for r in head_rows:
        p = _softmax_numerator(s[r], cmp_bias)
        pcs.append(p * (1.0 / jnp.maximum(jnp.sum(p, axis=1, keepdims=True), 1e-30)))
    oc = jnp.dot(jnp.concatenate(pcs, axis=0).astype(BF16), vc_ref[0, 0, 0, 0:n_cmp, :],
                 preferred_element_type=F32)

    pc_sum = pcs[0] + pcs[1] + pcs[2] + pcs[3]
    split = jnp.concatenate(_split3(pc_sum), axis=0)
    terms = jnp.dot(split, overlap_ref[0:n_cmp, :], preferred_element_type=F32)
    imp = (terms[0:tq] + terms[tq:2 * tq] + terms[2 * tq:3 * tq]).T[0:n_blk, :]
    blk = lax.broadcasted_iota(jnp.int32, (n_blk, tq), 0)
    cur = (q0 + lax.broadcasted_iota(jnp.int32, (n_blk, tq), 1)) // SLC_BLOCK
    forced = jnp.logical_or(blk == 0, jnp.logical_or(blk == cur, blk == cur - 1))

    def pick_one(_, carry):
        rem, sel = carry
        best = jnp.max(rem, axis=0, keepdims=True)
        first = jnp.min(jnp.where(rem == best, blk, n_blk), axis=0, keepdims=True)
        hit = blk == first
        return jnp.where(hit, -jnp.inf, rem), jnp.where(hit, 1.0, sel)

    candidates = jnp.where(jnp.logical_or(forced, blk > cur), -jnp.inf, imp)
    _, sel = lax.fori_loop(0, min(N_SELECT, n_blk) - N_FORCED, pick_one,
                           (candidates, jnp.where(forced, 1.0, 0.0)), unroll=True)
    all_blk = selb_ref.shape[3]
    if n_blk < all_blk:
        sel = jnp.concatenate([sel, jnp.zeros((all_blk - n_blk, tq), F32)], axis=0)
    selb_ref[0, 0] =jnp.where(sel.T > 0.5, 0.0, MASK_VALUE).astype(BF16)

    start = pl.multiple_of(jnp.maximum(q0 - WINDOW, 0), tq)
    kw = kw_ref[0, 0, pl.ds(start, WIN_KEYS), :]
    vw = vw_ref[0, 0, pl.ds(start, WIN_KEYS), :]
    sw = _dot_nt(q4, kw)
    pos = start + lax.broadcasted_iota(jnp.int32, (tq, WIN_KEYS), 1)
    dist = q0 + lax.broadcasted_iota(jnp.int32, (tq, WIN_KEYS), 0) - pos
    win_bias = jnp.where(jnp.logical_and(dist >= 0, dist < WINDOW), 0.0, -jnp.inf)
    pw = jnp.concatenate([_softmax_numerator(sw[r], win_bias) for r in head_rows],
                         axis=0).astype(BF16)
    ow = jnp.dot(pw, vw, preferred_element_type=F32)
    lw = jnp.dot(pw, jnp.ones((WIN_KEYS, LANES), BF16), preferred_element_type=F32)
    ow = ow * (1.0 / jnp.maximum(lw, 1e-30))

    gates = _group_gates(misc_ref[0], gi)
    gated = [_gate_lanes(gates, h, 0) * oc[r] + _gate_lanes(gates, h, 2) * ow[r]
             for h, r in enumerate(head_rows)]
    for pp, slab in enumerate(_pair_slabs(gated)):
        part_ref[0, :, pp * LANES:(pp + 1) * LANES] = slab


def _block_overlap(n_blk, n_cmp):
    ratio = SLC_BLOCK // CMP_STRIDE
    lo = np.arange(n_blk)[:, None] * ratio - (CMP_BLOCK // CMP_STRIDE - 1)
    i = np.arange(n_cmp)[None, :]
    n_overlap = (SLC_BLOCK + CMP_BLOCK) // CMP_STRIDE - 1
    return ((i >= lo) & (i < lo + n_overlap) & (i < n_cmp - 1)).astype(np.float32)


def _cmpwin(nq, cmp_hi, cmp_lo, kw, vw_ones, misc):
    b, s, w = nq.shape
    g = NSA_KV_HEADS
    tq = NSA_Q
    gw = NSA_GROUP * HEAD_DIM
    n_cmp = cmp_hi.shape[3]
    n_blk = s // SLC_BLOCK
    q_spec = pl.BlockSpec((1, tq, gw), lambda bi, gi, i: (bi, i, gi))
    key_spec = pl.BlockSpec((1, 1, 1, n_cmp, LANES), lambda bi, gi, i: (0, bi, gi, 0, 0))
    val_spec = pl.BlockSpec((1, 1, 1, n_cmp, LANES), lambda bi, gi, i: (1, bi, gi, 0, 0))
    seq_spec = pl.BlockSpec((1, 1, s, LANES), lambda bi, gi, i: (bi, gi, 0, 0))
    resident = 2 * s * LANES * 2 + 3 * n_cmp * LANES * 2 + n_blk * n_cmp * 2
    rows = NSA_GROUP * tq
    return pl.pallas_call(
        _cmpwin_kernel,
        grid=(b, g, s // tq),
        in_specs=[q_spec, key_spec, key_spec, val_spec,
                  pl.BlockSpec((n_cmp, n_blk), lambda bi, gi, i: (0, 0)),
                  seq_spec, seq_spec,
                  pl.BlockSpec((1, tq, LANES), lambda bi, gi, i: (bi, i, 0))],
        out_specs=(q_spec, pl.BlockSpec((1, 1, tq, n_blk), lambda bi, gi, i: (bi, gi, i, 0))),
        out_shape=(jax.ShapeDtypeStruct((b, s, w), F32),
                   jax.ShapeDtypeStruct((b, g, s, n_blk), BF16)),
        compiler_params=pltpu.CompilerParams(
            dimension_semantics=("arbitrary", "arbitrary", "arbitrary"),
            vmem_limit_bytes=_vmem_limit(resident + tq * gw * 8, 12 * rows * WIN_KEYS * 4)),
        name="cmpwin",
    )(nq, cmp_hi, cmp_lo, cmp_hi, jnp.asarray(_block_overlap(n_blk, n_cmp).T, BF16), kw, vw_ones, misc)


def _sel_kernel(q_ref, k_ref, v_ref, selb_ref, part_ref, misc_ref, z_ref, o_ref,
                qa_ref, gate_ref, *flash):
    tq = SEL_Q
    gi = pl.program_id(1)
    qi = pl.program_id(2)
    q0 = pl.multiple_of(qi * tq, tq)
    n_full = q0 // SEL_KV

    def prologue():
        q = q_ref[0]
        selb = selb_ref[0, 0]
        lane = lax.broadcasted_iota(jnp.int32, (tq, LANES), 1)
        for h in range(NSA_GROUP):
            r = slice(h * tq, (h + 1) * tq)
            slab = q[:, (h // 2) * LANES:(h // 2 + 1) * LANES]
            own = (lane < HEAD_DIM) if h % 2 == 0 else (lane >= HEAD_DIM)
            qa_ref[r, 0:LANES] = jnp.where(own, slab, jnp.zeros_like(slab))
            qa_ref[r, LANES:2 * LANES] = selb
        gates = _group_gates(misc_ref[0], gi)
        for h in range(NSA_GROUP):
            gate_ref[h] = _gate_lanes(gates, h, 1)

    def causal_mask(s, k0):
        t_row = q0 + (lax.broadcasted_iota(jnp.int32, s.shape, 0) & (tq - 1))
        pos = k0 + lax.broadcasted_iota(jnp.int32, s.shape, 1)
        return jnp.where(pos <= t_row, s, MASK_VALUE)

    def epilogue():
        acc_ref = flash[1]
        gated = []
        for h in range(NSA_GROUP):
            acc = acc_ref[h * tq:(h + 1) * tq, :]
            scaled = acc * (gate_ref[h] * (1.0 / acc[:, HEAD_DIM:HEAD_DIM + 1]))
            gated.append(scaled if h % 2 == 0 else pltpu.roll(scaled, HEAD_DIM, 1))
        for pp, slab in enumerate(_pair_slabs(gated)):
            c = slice(pp * LANES, (pp + 1) * LANES)
            o_ref[0, :, c] = ((part_ref[0, :, c] + slab) * z_ref[0, :, c].astype(F32)).astype(BF16)

    _causal_flash(n_full, SEL_KV, qa_ref,
                  lambda k0: k_ref[0, 0, pl.ds(k0, SEL_KV), :],
                  lambda p, k0: jnp.dot(p, v_ref[0, 0, pl.ds(k0, SEL_KV), :],
                                        preferred_element_type=F32),
                  causal_mask, prologue, epilogue, flash)


def _sel(nq, k_aug, v_ones, selb, part, misc, nz):
    b, s, w = nq.shape
    g = NSA_KV_HEADS
    tq = SEL_Q
    gw = NSA_GROUP * HEAD_DIM
    n_blk = s // SLC_BLOCK
    rows = NSA_GROUP * tq
    q_spec = pl.BlockSpec((1, tq, gw), lambda bi, gi, i: (bi, i, gi))
    resident = s * 3 * LANES * 2
    return pl.pallas_call(
        _sel_kernel,
        grid=(b, g, s // tq),
        in_specs=[q_spec,
                  pl.BlockSpec((1, 1, s, 2 * LANES), lambda bi, gi, i: (bi, gi, 0, 0)),
                  pl.BlockSpec((1, 1, s, LANES), lambda bi, gi, i: (bi, gi, 0, 0)),
                  pl.BlockSpec((1, 1, tq, n_blk), lambda bi, gi, i: (bi, gi, i, 0)),
                  q_spec,
                  pl.BlockSpec((1, tq, LANES), lambda bi, gi, i: (bi, i, 0)),
                  q_spec],
        out_specs=q_spec,
        out_shape=jax.ShapeDtypeStruct((b, s, w), BF16),
        scratch_shapes=[pltpu.VMEM((rows, 2 * LANES), BF16),
                        pltpu.VMEM((NSA_GROUP, tq, LANES), F32)] + _flash_scratch(rows, SEL_KV),
        compiler_params=pltpu.CompilerParams(
            dimension_semantics=("arbitrary", "arbitrary", "arbitrary"),
            vmem_limit_bytes=_vmem_limit(resident + tq * gw * 10,
                                         _flash_scratch_bytes(rows, SEL_KV) + 4 * rows * SEL_KV * 4)),
        name="sel",
    )(nq, k_aug, v_ones, selb, part, misc, nz)


def _out_kernel(yf_ref, yn_ref, w_ref, g_ref, mod_ref, x_ref, o_ref):
    y = (jnp.dot(yf_ref[0], w_ref[0:FOX_WIDTH, :], preferred_element_type=F32)
         + jnp.dot(yn_ref[0], w_ref[FOX_WIDTH:, :], preferred_element_type=F32))
    yn = y * lax.rsqrt(jnp.mean(y * y, axis=-1, keepdims=True) + RMS_EPS)
    o_ref[0] = x_ref[0] + mod_ref[0, 2:3, :] * (yn * g_ref[...])


def _out(y_fox, y_nsa, w_out, g_post, mod, x):
    b, s, d = x.shape
    tm = PROJ_ROWS
    row = lambda bi, i: (bi, i, 0)
    half = pl.BlockSpec((1, tm, FOX_WIDTH), row)
    pipelined = 2 * tm * FOX_WIDTH * 2 + 2 * tm * d * 4 + d * d * 2
    return pl.pallas_call(
        _out_kernel,
        grid=(b, s // tm),
        in_specs=[half, half,
                  pl.BlockSpec((d, d), lambda bi, i: (0, 0)),
                  pl.BlockSpec((1, d), lambda bi, i: (0, 0)),
                  pl.BlockSpec((1, 3, d), lambda bi, i: (bi, 0, 0)),
                  pl.BlockSpec((1, tm, d), row)],
        out_specs=pl.BlockSpec((1, tm, d), row),
        out_shape=jax.ShapeDtypeStruct((b, s, d), F32),
        compiler_params=pltpu.CompilerParams(
            dimension_semantics=("arbitrary", "arbitrary"),
            vmem_limit_bytes=_vmem_limit(pipelined, 6 * tm * d * 4)),
        name="out",
    )(y_fox, y_nsa, w_out, g_post, mod, x)


def _rope_slabs(seq_len):
    inv = 1.0 / (ROPE_THETA ** (jnp.arange(0, HEAD_DIM, 2, dtype=F32) / HEAD_DIM))
    ang = jnp.arange(seq_len, dtype=F32)[:, None] * inv[None, :]
    cos, sin = jnp.cos(ang), jnp.sin(ang)
    reps = LANES // (HEAD_DIM // 2)
    sign = jnp.tile(jnp.concatenate([-jnp.ones((HEAD_DIM // 2,), F32), jnp.ones((HEAD_DIM // 2,), F32)]),
                    LANES // HEAD_DIM)
    return jnp.tile(cos, (1, reps)), jnp.tile(sin, (1, reps)) * sign[None, :]


def _reorder_w_in(w_in):
    fw, kv = FOX_WIDTH, NSA_KV_WIDTH
    o = 0
    cols = {}
    for name, n in (("fq", fw), ("fk", fw), ("fv", fw), ("ff", FOX_HEADS), ("fz", fw), ("nq", NSA_WIDTH),
                    ("kc", kv), ("vc", kv), ("ks", kv), ("vs", kv), ("kw", kv), ("vw", kv),
                    ("ng", 3 * NSA_HEADS), ("nz", NSA_WIDTH)):
        cols[name] = w_in[:, o:o + n]
        o += n
    pad = jnp.zeros((w_in.shape[0], LANES - FOX_HEADS - 3 * NSA_HEADS), w_in.dtype)
    order = ("fq", "fk", "fv", "fz", "nq", "kc", "vc", "ks", "vs", "kw", "vw", "nz", "ff", "ng")
    return jnp.concatenate([cols[k] for k in order] + [pad], axis=1).astype(BF16)


def _layer(x, c8, g_pre, g_post, w_ada, b_ada, w_in, b_forget, w_cmp_k1, w_cmp_k2,
           w_cmp_v1, w_cmp_v2, pe_cmp_k, pe_cmp_v, w_out, cos128, sin128):
    b, s, d = x.shape
    mod = _ada(c8, w_ada, b_ada)[:b].reshape(b, 3, d)
    bf128 = jnp.pad(b_forget, (0, LANES - FOX_HEADS)).reshape(1, LANES)
    (fq, fk, fv, fz, nq, cmp_in, ks_aug, vs_ones, kw2, vw_ones, nz, misc, log_f) = _proj(
        x, mod, g_pre.reshape(1, d), _reorder_w_in(w_in), cos128, sin128, bf128)

    cum_split = tuple(p.reshape(b, FOX_HEADS, s)
                      for p in _cumsum_lanes_split(log_f.reshape(b * FOX_HEADS, s)))
    y_fox = _fox(fq, fk, fv, cum_split, fz)

    flat = CMP_STRIDE * HEAD_DIM
    t = cmp_in.reshape(2, b, NSA_KV_HEADS, s // CMP_STRIDE, flat)
    pe = jnp.stack([pe_cmp_k.reshape(1, 2 * flat), pe_cmp_v.reshape(1, 2 * flat)])
    w1 = jnp.stack([w_cmp_k1, w_cmp_v1])
    w2 = jnp.stack([w_cmp_k2, w_cmp_v2])
    cmp_hi, cmp_lo = _compress(t, jnp.concatenate([w1[:, :flat], w1[:, flat:]], axis=-1),
                               jnp.concatenate([w2, w2], axis=-1),
                               jnp.broadcast_to(pe, (2, SUBLANES, 2 * flat)))
    part, selb = _cmpwin(nq, cmp_hi, cmp_lo, kw2, vw_ones, misc)
    y_nsa = _sel(nq, ks_aug, vs_ones, selb, part, misc, nz)

    return _out(y_fox, y_nsa, w_out.astype(BF16), g_post.reshape(1, d), mod, x)


def kernel(x, c, g_pre, g_post, w_ada, b_ada, w_in, b_forget, w_cmp_k1, w_cmp_k2,
           w_cmp_v1, w_cmp_v2, pe_cmp_k, pe_cmp_v, w_out):
    cos128, sin128 = _rope_slabs(x.shape[1])
    c8 = jnp.pad(c, ((0, SUBLANES - c.shape[0]), (0, 0)))
    for layer in range(g_pre.shape[0]):
        x = _layer(x, c8, g_pre[layer], g_post[layer], w_ada[layer], b_ada[layer], w_in[layer],
                   b_forget[layer], w_cmp_k1[layer], w_cmp_k2[layer], w_cmp_v1[layer],
                   w_cmp_v2[layer], pe_cmp_k[layer], pe_cmp_v[layer], w_out[layer], cos128, sin128)
    return x
```

```python
import functools

import jax
import jax.numpy as jnp
import numpy as np
from jax import lax
from jax.experimental import pallas as pl
from jax.experimental.pallas import tpu as pltpu

F32 = jnp.float32
BF16 = jnp.bfloat16

D_MODEL = 1024
HEAD_DIM = 64
FOX_WIDTH = 512
NSA_WIDTH = 512
FOX_HEADS = 8
NSA_HEADS = 8
NSA_KV_HEADS = 2
NSA_GROUP = 4
NSA_KV_WIDTH = 128
CMP_BLOCK = 32
CMP_STRIDE = 16
CMP_HIDDEN = 128
SLC_BLOCK = 64
N_SELECT = 16
N_FORCED = 3
WINDOW = 512
ROPE_THETA = 10000.0
RMS_EPS = 1e-6
LOG2E = 1.4426950408889634
QK_SCALE = HEAD_DIM ** -0.5 * LOG2E

LANES = 128
SUBLANES = 8
MIB = 1024 * 1024
V7X_VMEM_BYTES = 64 * MIB
VMEM_LIMIT_FLOOR = 16 * MIB
VMEM_COMPILER_RESERVE = 8 * MIB
MASK_VALUE = -1e30

PROJ_ROWS = 512
FOX_TILE = 512
NSA_Q = 256
SEL_Q = 128
SEL_KV = 512
WIN_KEYS = WINDOW + NSA_Q
FLASH_UNROLL = 8
CMP_QUARTERS = 4
FOX_AUG = 6

C_FQ, C_FK, C_FV, C_FZ, C_NQ = 0, 512, 1024, 1536, 2048
C_KC, C_VC, C_KS, C_VS, C_KW, C_VW = 2560, 2688, 2816, 2944, 3072, 3200
C_NZ, C_MISC, PROJ_COLS = 3328, 3840, 3968
MISC_GATE0 = FOX_HEADS


def _vmem_limit(pipelined_bytes, resident_bytes):
    need = 2 * pipelined_bytes + resident_bytes
    return int(min(max(need, VMEM_LIMIT_FLOOR), V7X_VMEM_BYTES - VMEM_COMPILER_RESERVE))


def _sigmoid(v):
    return 1.0 / (1.0 + jnp.exp(-v))


def _silu(v):
    return v * _sigmoid(v)


def _dot_nt(a, b):
    return lax.dot_general(a, b, (((1,), (1,)), ((), ())), preferred_element_type=F32)


def _split3(v):
    hi = v.astype(BF16)
    r1 = v - hi.astype(F32)
    mid = r1.astype(BF16)
    lo = (r1 - mid.astype(F32)).astype(BF16)
    return hi, mid, lo


def _ada_kernel(c_ref, w_ref, b_ref, o_ref):
    a = _silu(c_ref[...])
    o_ref[...] = jnp.dot(a, w_ref[...], precision=lax.Precision.HIGHEST,
                         preferred_element_type=F32) + b_ref[...]


def _ada(c8, w_ada, b_ada):
    n = w_ada.shape[1]
    blk = D_MODEL
    return pl.pallas_call(
        _ada_kernel,
        grid=(n // blk,),
        in_specs=[pl.BlockSpec((SUBLANES, D_MODEL), lambda j: (0, 0)),
                  pl.BlockSpec((D_MODEL, blk), lambda j: (0, j)),
                  pl.BlockSpec((1, blk), lambda j: (0, j))],
        out_specs=pl.BlockSpec((SUBLANES, blk), lambda j: (0, j)),
        out_shape=jax.ShapeDtypeStruct((SUBLANES, n), F32),
        compiler_params=pltpu.CompilerParams(
            dimension_semantics=("arbitrary",),
            vmem_limit_bytes=_vmem_limit(D_MODEL * blk * 4, D_MODEL * blk * 4)),
        name="ada",
    )(c8, w_ada, b_ada.reshape(1, n))


def _rope128(t, cos, sin_signed):
    lane = lax.broadcasted_iota(jnp.int32, t.shape, 1)
    first_half = (lane & (HEAD_DIM - 1)) < HEAD_DIM // 2
    partner = jnp.where(first_half,
                        pltpu.roll(t, LANES - HEAD_DIM // 2, 1),
                        pltpu.roll(t, HEAD_DIM // 2, 1))
    return t * cos + partner * sin_signed


def _proj_kernel(x_ref, mod_ref, g_ref, w_ref, cos_ref, sin_ref, bf_ref,
                 fq_ref, fk_ref, fv_ref, fz_ref, nq_ref, cmp_ref,
                 ks_ref, vs_ref, kw_ref, vw_ref, nz_ref, misc_ref, lf_ref):
    tm = x_ref.shape[1]
    x = x_ref[0]
    y = x * lax.rsqrt(jnp.mean(x * x, axis=-1, keepdims=True) + RMS_EPS)
    y = y * g_ref[...]
    h = (y * (1.0 + mod_ref[0, 1:2, :]) + mod_ref[0, 0:1, :]).astype(BF16)
    cos = cos_ref[...]
    sin = sin_ref[...]

    def mm(lo, n):
        return jnp.dot(h, w_ref[:, lo:lo + n], preferred_element_type=F32)

    def slab_pair(lo):
        both = mm(lo, 2 * LANES)
        return both[:, 0:LANES], both[:, LANES:2 * LANES]

    lane = lax.broadcasted_iota(jnp.int32, (tm, LANES), 1)
    low = lane < HEAD_DIM
    ones_col = jnp.where(lane == HEAD_DIM, 1.0, 0.0).astype(BF16)

    def doubled(slab):
        swapped = pltpu.roll(slab, HEAD_DIM, 1)
        return jnp.where(low, slab, swapped), jnp.where(low, swapped, slab)

    def with_ones(slab):
        swapped = pltpu.roll(slab, HEAD_DIM, 1)
        return jnp.where(low, slab, ones_col), jnp.where(low, swapped, ones_col)

    fq_ref[0] = (mm(C_FQ, FOX_WIDTH) * QK_SCALE).astype(BF16)
    fk_ref[0] = mm(C_FK, FOX_WIDTH).astype(BF16)
    fv = mm(C_FV, FOX_WIDTH).astype(BF16)
    for pair in range(FOX_WIDTH // LANES):
        fv_ref[0, 2 * pair], fv_ref[0, 2 * pair + 1] = with_ones(fv[:, pair * LANES:(pair + 1) * LANES])
    fz_ref[0] = _silu(mm(C_FZ, FOX_WIDTH)).astype(BF16)
    for j in range(NSA_WIDTH // (2 * LANES)):
        for i, t in enumerate(slab_pair(C_NQ + 2 * j * LANES)):
            c = (2 * j + i) * LANES
            nq_ref[0, :, c:c + LANES] = (_rope128(t, cos, sin) * QK_SCALE).astype(BF16)

    kc, vc = slab_pair(C_KC)
    for kind, slab in enumerate((_rope128(kc, cos, sin), vc)):
        for g in range(NSA_KV_HEADS):
            cmp_ref[kind, 0, g] = slab[:, g * HEAD_DIM:(g + 1) * HEAD_DIM]
    pos = pl.program_id(1) * tm + lax.broadcasted_iota(jnp.int32, (tm, LANES), 0)
    onehot = jnp.where(pos // SLC_BLOCK == lane, 1.0, 0.0).astype(BF16)
    ks, vs = slab_pair(C_KS)
    for g, k2 in enumerate(doubled(_rope128(ks, cos, sin).astype(BF16))):
        ks_ref[0, g, :, 0:LANES] = k2
        ks_ref[0, g, :, LANES:2 * LANES] = onehot
    vs_ref[0, 0], vs_ref[0, 1] = with_ones(vs.astype(BF16))
    kw, vw = slab_pair(C_KW)
    kw_ref[0, 0], kw_ref[0, 1] = doubled(_rope128(kw, cos, sin).astype(BF16))
    vw_ref[0, 0], vw_ref[0, 1] = doubled(vw.astype(BF16))
    nz_ref[0] = _silu(mm(C_NZ, NSA_WIDTH)).astype(BF16)
    misc = mm(C_MISC, LANES)
    misc_ref[0] = misc
    z = misc + bf_ref[...]
    log_f = jnp.minimum(z, 0.0) - jnp.log1p(jnp.exp(-jnp.abs(z)))
    lf_ref[0] = log_f.T[0:FOX_HEADS, :]


def _proj(x, mod, g_pre, w_cat, cos128, sin128, bf128):
    b, s, d = x.shape
    tm = PROJ_ROWS
    row = lambda bi, i: (bi, i, 0)
    wide = lambda dt: jax.ShapeDtypeStruct((b, s, FOX_WIDTH), dt)
    g = NSA_KV_HEADS
    grouped = lambda lanes: jax.ShapeDtypeStruct((b, g, s, lanes), BF16)
    grouped_spec = lambda lanes: pl.BlockSpec((1, g, tm, lanes), lambda bi, i: (bi, 0, i, 0))
    out_shape = (wide(BF16), wide(BF16), jax.ShapeDtypeStruct((b, FOX_HEADS, s, LANES), BF16),
                 wide(BF16), wide(BF16),
                 jax.ShapeDtypeStruct((2, b, g, s, HEAD_DIM), F32),
                 grouped(2 * LANES), grouped(LANES), grouped(LANES), grouped(LANES),
                 wide(BF16), jax.ShapeDtypeStruct((b, s, LANES), F32),
                 jax.ShapeDtypeStruct((b, FOX_HEADS, s), F32))
    wide_spec = pl.BlockSpec((1, tm, FOX_WIDTH), row)
    out_specs = (wide_spec, wide_spec,
                 pl.BlockSpec((1, FOX_HEADS, tm, LANES), lambda bi, i: (bi, 0, i, 0)),
                 wide_spec, wide_spec) + (
        pl.BlockSpec((2, 1, g, tm, HEAD_DIM), lambda bi, i: (0, bi, 0, i, 0)),
        grouped_spec(2 * LANES), grouped_spec(LANES), grouped_spec(LANES), grouped_spec(LANES),
        wide_spec, pl.BlockSpec((1, tm, LANES), row),
        pl.BlockSpec((1, FOX_HEADS, tm), lambda bi, i: (bi, 0, i)))
    pipelined = (tm * d * 4 + tm * (7 * FOX_WIDTH * 2 + 2 * g * LANES * 4 + 5 * g * LANES * 2 + LANES * 4)
                 + d * PROJ_COLS * 2)
    return pl.pallas_call(
        _proj_kernel,
        grid=(b, s // tm),
        in_specs=[pl.BlockSpec((1, tm, d), row),
                  pl.BlockSpec((1, 3, d), lambda bi, i: (bi, 0, 0)),
                  pl.BlockSpec((1, d), lambda bi, i: (0, 0)),
                  pl.BlockSpec((d, PROJ_COLS), lambda bi, i: (0, 0)),
                  pl.BlockSpec((tm, LANES), lambda bi, i: (i, 0)),
                  pl.BlockSpec((tm, LANES), lambda bi, i: (i, 0)),
                  pl.BlockSpec((1, LANES), lambda bi, i: (0, 0))],
        out_specs=out_specs,
        out_shape=out_shape,
        compiler_params=pltpu.CompilerParams(
            dimension_semantics=("arbitrary", "arbitrary"),
            vmem_limit_bytes=_vmem_limit(pipelined, 4 * tm * d * 4)),
        name="proj",
    )(x, mod, g_pre, w_cat, cos128, sin128, bf128)


def _cumsum_kernel(x_ref, hi_ref, mid_ref, lo_ref, *, chunks):
    x = x_ref[...]
    n = x.shape[0]
    parts = _split3(x)
    r = lax.broadcasted_iota(jnp.int32, (LANES, LANES), 0)
    c = lax.broadcasted_iota(jnp.int32, (LANES, LANES), 1)
    tri = (r <= c).astype(BF16)
    rr = lax.broadcasted_iota(jnp.int32, (n, n), 0)
    cc = lax.broadcasted_iota(jnp.int32, (n, n), 1)
    earlier = jnp.logical_and(cc < rr, (cc // chunks) == (rr // chunks)).astype(BF16)
    within = sum(jnp.dot(p, tri, preferred_element_type=F32) for p in parts)
    before = sum(jnp.dot(earlier, p, preferred_element_type=F32) for p in parts)
    total = (within + jnp.sum(before, axis=-1, keepdims=True)) * LOG2E
    hi_ref[...], mid_ref[...], lo_ref[...] = _split3(total)


def _cumsum_lanes_split(v):
    rows, s = v.shape
    chunks = s // LANES
    n = rows * chunks
    part = jax.ShapeDtypeStruct((n, LANES), BF16)
    parts = pl.pallas_call(
        functools.partial(_cumsum_kernel, chunks=chunks),
        out_shape=(part, part, part),
        compiler_params=pltpu.CompilerParams(
            vmem_limit_bytes=_vmem_limit(2 * n * LANES * 4, 6 * n * n)),
        name="cumsum",
    )(v.reshape(n, LANES))
    return tuple(p.reshape(rows, s) for p in parts)


def _flash_scratch(rows, tile):
    return ([pltpu.VMEM((rows, LANES), F32)] * 2
            + [pltpu.VMEM((rows, tile), F32)] * 2
            + [pltpu.VMEM((rows, tile), BF16)] * 2
            + [pltpu.VMEM((rows, LANES), F32)] * 2)


def _flash_scratch_bytes(rows, tile):
    return rows * (2 * LANES * 4 + 2 * tile * 4 + 2 * tile * 2 + 2 * LANES * 4)


def _causal_flash(n_full, tile, q_ref, load_k, pv, causal_mask, prologue, epilogue, scratch):
    m_ref, acc_ref, s0, s1, p0, p1, a0, a1 = scratch
    s_bufs, p_bufs, a_bufs = (s0, s1), (p0, p1), (a0, a1)

    def logits(j):
        return _dot_nt(q_ref[...], load_k(pl.multiple_of(j * tile, tile)))

    def softmax(s):
        m_prev = m_ref[...]
        m_next = jnp.maximum(m_prev, jnp.max(s, axis=1, keepdims=True))
        m_ref[...] = m_next
        p = jnp.exp2(s - jnp.tile(m_next, (1, tile // LANES)))
        return p.astype(BF16), jnp.exp2(m_prev - m_next)

    def accumulate(p, alpha, j):
        acc_ref[...] = acc_ref[...] * alpha + pv(p, pl.multiple_of(j * tile, tile))

    def stage(j, cur):
        s_bufs[1 - cur][...] = logits(j + 1)
        p_bufs[cur][...], a_bufs[cur][...] = softmax(s_bufs[cur][...])
        accumulate(p_bufs[1 - cur][...], a_bufs[1 - cur][...], j - 1)

    def run_stages(first, count):
        for i in range(count):
            stage(first + i, (1 + i) % 2)

    def finish(cur):
        p, alpha = softmax(causal_mask(s_bufs[cur][...], n_full * tile))
        accumulate(p_bufs[1 - cur][...], a_bufs[1 - cur][...], n_full - 1)
        accumulate(p, alpha, n_full)

    def start():
        prologue()
        m_ref[...] = jnp.full(m_ref.shape, MASK_VALUE, F32)
        acc_ref[...] = jnp.zeros(acc_ref.shape, F32)

    @pl.when(n_full == 0)
    def _():
        start()
        p, alpha = softmax(causal_mask(logits(0), 0))
        accumulate(p, alpha, 0)
        epilogue()

    @pl.when(n_full > 0)
    def _():
        start()
        s_first = logits(0)
        s_bufs[1][...] = logits(1)
        p_bufs[0][...], a_bufs[0][...] = softmax(s_first)
        later = n_full - 1

        def unrolled(jj, carry):
            run_stages(1 + FLASH_UNROLL * jj, FLASH_UNROLL)
            return carry

        lax.fori_loop(0, later // FLASH_UNROLL, unrolled, 0)
        rest = later % FLASH_UNROLL

        def tail(count):
            run_stages(n_full - count, count)
            finish((1 + count) % 2)
            epilogue()

        for count in range(FLASH_UNROLL):
            pl.when(rest == count)(functools.partial(tail, count))


def _fox_aug(cum_split):
    b, h, s = cum_split[0].shape
    terms = jnp.concatenate(list(cum_split) + [jnp.ones((b, 1, s), BF16)], axis=1)
    terms = jnp.transpose(terms, (0, 2, 1))
    place_q = np.zeros((3 * h + 1, h * HEAD_DIM), np.float32)
    place_k = np.zeros((3 * h + 1, h * HEAD_DIM), np.float32)
    half = FOX_AUG // 2
    for head in range(h):
        for part in range(half):
            place_q[part * h + head, head * HEAD_DIM + part] = 1.0
            place_q[3 * h, head * HEAD_DIM + half + part] = 1.0
            place_k[3 * h, head * HEAD_DIM + part] = 1.0
            place_k[part * h + head, head * HEAD_DIM + half + part] = -1.0
    scatter = lambda place: jnp.einsum("bsr,rl->bsl", terms, jnp.asarray(place, BF16),
                                       preferred_element_type=F32).astype(BF16)
    return scatter(place_q), scatter(place_k)


def _fox_kernel(q_ref, qa_ref, k_ref, ka_ref, v_ref, z_ref, o_ref, qm_ref, *flash):
    t = FOX_TILE
    qi = pl.program_id(2)

    def prologue():
        q = q_ref[0]
        qa = qa_ref[0]
        lane = lax.broadcasted_iota(jnp.int32, q.shape, 1)
        for hh in range(2):
            r = slice(hh * t, (hh + 1) * t)
            own = (lane < HEAD_DIM) if hh == 0 else (lane >= HEAD_DIM)
            qm_ref[r, 0:LANES] = jnp.where(own, q, jnp.zeros_like(q))
            qm_ref[r, LANES:2 * LANES] = jnp.where(own, qa, jnp.zeros_like(qa))

    def load_k(k0):
        return jnp.concatenate([k_ref[0, pl.ds(k0, t), :], ka_ref[0, pl.ds(k0, t), :]], axis=1)

    def pv(p, k0):
        return jnp.concatenate(
            [jnp.dot(p[hh * t:(hh + 1) * t], v_ref[0, hh, pl.ds(k0, t), :], preferred_element_type=F32)
             for hh in range(2)], axis=0)

    def causal_mask(s, k0):
        t_row = qi * t + (lax.broadcasted_iota(jnp.int32, s.shape, 0) & (t - 1))
        pos = k0 + lax.broadcasted_iota(jnp.int32, s.shape, 1)
        return jnp.where(pos <= t_row, s, MASK_VALUE)

    def epilogue():
        acc_ref = flash[1]
        o0 = acc_ref[0:t, :] * (1.0 / acc_ref[0:t, HEAD_DIM:HEAD_DIM + 1])
        o1 = acc_ref[t:2 * t, :] * (1.0 / acc_ref[t:2 * t, HEAD_DIM:HEAD_DIM + 1])
        lane = lax.broadcasted_iota(jnp.int32, o0.shape, 1)
        o = jnp.where(lane < HEAD_DIM, o0, pltpu.roll(o1, HEAD_DIM, 1))
        o_ref[0] = (o * z_ref[0].astype(F32)).astype(BF16)

    _causal_flash(qi, t, qm_ref, load_k, pv, causal_mask, prologue, epilogue, flash)


def _fox(fq, fk, fv, cum_split, fz):
    b, s, w = fq.shape
    t = FOX_TILE
    pairs = w // LANES
    qa, ka = _fox_aug(cum_split)
    tile = pl.BlockSpec((1, t, LANES), lambda bi, hp, i: (bi, i, hp))
    full = pl.BlockSpec((1, s, LANES), lambda bi, hp, i: (bi, 0, hp))
    resident = 4 * s * LANES * 2
    scratch = 2 * t * 2 * LANES * 2 + _flash_scratch_bytes(2 * t, t)
    return pl.pallas_call(
        _fox_kernel,
        grid=(b, pairs, s // t),
        in_specs=[tile, tile, full, full,
                  pl.BlockSpec((1, 2, s, LANES), lambda bi, hp, i: (bi, hp, 0, 0)),
                  tile],
        out_specs=tile,
        out_shape=jax.ShapeDtypeStruct((b, s, w), BF16),
        scratch_shapes=[pltpu.VMEM((2 * t, 2 * LANES), BF16)] + _flash_scratch(2 * t, t),
        compiler_params=pltpu.CompilerParams(
            dimension_semantics=("arbitrary", "arbitrary", "arbitrary"),
            vmem_limit_bytes=_vmem_limit(resident + 4 * t * LANES * 2, scratch + 4 * 2 * t * t * 4)),
        name="fox",
    )(fq, qa, fk, ka, fv, fz)


def _compress_kernel(t_ref, w1_ref, w2_ref, pe_ref, hi_ref, lo_ref):
    hp = lax.Precision.HIGHEST
    half = CMP_STRIDE * HEAD_DIM
    hid = CMP_HIDDEN
    tt = t_ref[0, 0, 0]
    n = tt.shape[0]
    w1 = w1_ref[0]
    both = jnp.dot(tt, w1, precision=hp, preferred_element_type=F32)
    pe_first = jnp.dot(pe_ref[0, :, 0:half], w1, precision=hp, preferred_element_type=F32)
    pe_second = jnp.dot(pe_ref[0, :, half:2 * half], w1, precision=hp, preferred_element_type=F32)
    pe_term = pe_first[0:1, 0:hid] + pe_second[0:1, hid:2 * hid]
    hidden = both[:, 0:hid] + pltpu.roll(both[:, hid:2 * hid], n - 1, 0) + pe_term
    out = jnp.dot(_silu(hidden), w2_ref[0], precision=hp, preferred_element_type=F32)
    hi = out.astype(BF16)
    hi_ref[0, 0, 0] = hi
    lo_ref[0, 0, 0] = (out - hi.astype(F32)).astype(BF16)


def _compress(t, w1, w2, pe):
    kinds, b, g, n, flat = t.shape
    width = w2.shape[-1]
    out_spec = pl.BlockSpec((1, 1, 1, n, width), lambda a, bi, gi: (a, bi, gi, 0, 0))
    out_part = jax.ShapeDtypeStruct((kinds, b, g, n, width), BF16)
    return pl.pallas_call(
        _compress_kernel,
        grid=(kinds, b, g),
        in_specs=[pl.BlockSpec((1, 1, 1, n, flat), lambda a, bi, gi: (a, bi, gi, 0, 0)),
                  pl.BlockSpec((1, flat, 2 * CMP_HIDDEN), lambda a, bi, gi: (a, 0, 0)),
                  pl.BlockSpec((1, CMP_HIDDEN, width), lambda a, bi, gi: (a, 0, 0)),
                  pl.BlockSpec((1, SUBLANES, 2 * flat), lambda a, bi, gi: (a, 0, 0))],
        out_specs=(out_spec, out_spec),
        out_shape=(out_part, out_part),
        compiler_params=pltpu.CompilerParams(
            dimension_semantics=("arbitrary", "arbitrary", "arbitrary"),
            vmem_limit_bytes=_vmem_limit(n * flat * 4 + 2 * flat * CMP_HIDDEN * 4, 4 * n * flat * 4)),
        name="compress",
    )(t, w1, w2, pe)


def _softmax_numerator(s, bias):
    s = s + bias
    m = jnp.max(s, axis=1, keepdims=True)
    m = jnp.where(m == -jnp.inf, 0.0, m)
    return jnp.exp2(s - m)


def _stack_heads(q):
    tq = q.shape[0]
    lane = lax.broadcasted_iota(jnp.int32, (tq, LANES), 1)
    rows = []
    for h in range(NSA_GROUP):
        slab = q[:, (h // 2) * LANES:(h // 2 + 1) * LANES]
        own = (lane < HEAD_DIM) if h % 2 == 0 else (lane >= HEAD_DIM)
        rows.append(jnp.where(own, slab, jnp.zeros_like(slab)))
    return jnp.concatenate(rows, axis=0)


def _group_gates(misc, gi):
    gates = _sigmoid(misc)
    return jnp.where(gi == 0, gates, pltpu.roll(gates, LANES - 3 * NSA_GROUP, 1))


def _gate_lanes(gates, h, j):
    c = MISC_GATE0 + 3 * h + j
    return jnp.broadcast_to(gates[:, c:c + 1], gates.shape)


def _pair_slabs(per_head):
    lane = lax.broadcasted_iota(jnp.int32, per_head[0].shape, 1)
    return [jnp.where(lane < HEAD_DIM, per_head[2 * pp], per_head[2 * pp + 1])
            for pp in range(NSA_GROUP // 2)]


def _cmpwin_kernel(q_ref, kc_hi_ref, kc_lo_ref, vc_ref, overlap_ref, kw_ref, vw_ref, misc_ref,
                   part_ref, selb_ref):
    gi = pl.program_id(1)
    q0 = pl.multiple_of(pl.program_id(2) * NSA_Q, NSA_Q)
    quarter = kc_hi_ref.shape[3] // CMP_QUARTERS
    reach = q0 // (quarter * CMP_STRIDE)
    refs = (q_ref, kc_hi_ref, kc_lo_ref, vc_ref, overlap_ref, kw_ref, vw_ref, misc_ref,
            part_ref, selb_ref)
    for quarters in range(1, CMP_QUARTERS + 1):
        pl.when(reach == quarters - 1)(
            functools.partial(_cmpwin_body, refs, gi, q0, quarters * quarter))


def _cmpwin_body(refs, gi, q0, n_cmp):
    (q_ref, kc_hi_ref, kc_lo_ref, vc_ref, overlap_ref, kw_ref, vw_ref, misc_ref,
     part_ref, selb_ref) = refs
    tq = NSA_Q
    q4 = _stack_heads(q_ref[0])
    n_blk = n_cmp * CMP_STRIDE // SLC_BLOCK
    head_rows = [slice(h * tq, (h + 1) * tq) for h in range(NSA_GROUP)]

    kc = jnp.concatenate([kc_hi_ref[0, 0, 0, 0:n_cmp, :], kc_lo_ref[0, 0, 0, 0:n_cmp, :]], axis=1)
    s = _dot_nt(jnp.concatenate([q4, q4], axis=1), kc)
    col = lax.broadcasted_iota(jnp.int32, (tq, n_cmp), 1)
    t_row = q0 + lax.broadcasted_iota(jnp.int32, (tq, n_cmp), 0)
    cmp_bias = jnp.where(col * CMP_STRIDE + (CMP_BLOCK - 1) <= t_row, 0.0, -jnp.inf)
    pcs = []
    for r in head_rows:
        p = _softmax_numerator(s[r], cmp_bias)
        pcs.append(p * (1.0 / jnp.maximum(jnp.sum(p, axis=1, keepdims=True), 1e-30)))
    oc = jnp.dot(jnp.concatenate(pcs, axis=0).astype(BF16), vc_ref[0, 0, 0, 0:n_cmp, :],
                 preferred_element_type=F32)

    pc_sum = pcs[0] + pcs[1] + pcs[2] + pcs[3]
    split = jnp.concatenate(_split3(pc_sum), axis=0)
    terms = jnp.dot(split, overlap_ref[0:n_cmp, :], preferred_element_type=F32)
    imp = (terms[0:tq] + terms[tq:2 * tq] + terms[2 * tq:3 * tq]).T[0:n_blk, :]
    blk = lax.broadcasted_iota(jnp.int32, (n_blk, tq), 0)
    cur = (q0 + lax.broadcasted_iota(jnp.int32, (n_blk, tq), 1)) // SLC_BLOCK
    forced = jnp.logical_or(blk == 0, jnp.logical_or(blk == cur, blk == cur - 1))

    def pick_one(_, carry):
        rem, sel = carry
        best = jnp.max(rem, axis=0, keepdims=True)
        first = jnp.min(jnp.where(rem == best, blk, n_blk), axis=0, keepdims=True)
        hit = blk == first
        return jnp.where(hit, -jnp.inf, rem), jnp.where(hit, 1.0, sel)

    candidates = jnp.where(jnp.logical_or(forced, blk > cur), -jnp.inf, imp)
    _, sel = lax.fori_loop(0, min(N_SELECT, n_blk) - N_FORCED, pick_one,
                           (candidates, jnp.where(forced, 1.0, 0.0)), unroll=True)
    all_blk = selb_ref.shape[3]
    if n_blk < all_blk:
        sel = jnp.concatenate([sel, jnp.zeros((all_blk - n_blk, tq), F32)], axis=0)
    selb_ref[0, 0] =jnp.where(sel.T > 0.5, 0.0, MASK_VALUE).astype(BF16)

    start = pl.multiple_of(jnp.maximum(q0 - WINDOW, 0), tq)
    kw = kw_ref[0, 0, pl.ds(start, WIN_KEYS), :]
    vw = vw_ref[0, 0, pl.ds(start, WIN_KEYS), :]
    sw = _dot_nt(q4, kw)
    pos = start + lax.broadcasted_iota(jnp.int32, (tq, WIN_KEYS), 1)
    dist = q0 + lax.broadcasted_iota(jnp.int32, (tq, WIN_KEYS), 0) - pos
    win_bias = jnp.where(jnp.logical_and(dist >= 0, dist < WINDOW), 0.0, -jnp.inf)
    pw = jnp.concatenate([_softmax_numerator(sw[r], win_bias) for r in head_rows],
                         axis=0).astype(BF16)
    ow = jnp.dot(pw, vw, preferred_element_type=F32)
    lw = jnp.dot(pw, jnp.ones((WIN_KEYS, LANES), BF16), preferred_element_type=F32)
    ow = ow * (1.0 / jnp.maximum(lw, 1e-30))

    gates = _group_gates(misc_ref[0], gi)
    gated = [_gate_lanes(gates, h, 0) * oc[r] + _gate_lanes(gates, h, 2) * ow[r]
             for h, r in enumerate(head_rows)]
    for pp, slab in enumerate(_pair_slabs(gated)):
        part_ref[0, :, pp * LANES:(pp + 1) * LANES] = slab


def _block_overlap(n_blk, n_cmp):
    ratio = SLC_BLOCK // CMP_STRIDE
    lo = np.arange(n_blk)[:, None] * ratio - (CMP_BLOCK // CMP_STRIDE - 1)
    i = np.arange(n_cmp)[None, :]
    n_overlap = (SLC_BLOCK + CMP_BLOCK) // CMP_STRIDE - 1
    return ((i >= lo) & (i < lo + n_overlap) & (i < n_cmp - 1)).astype(np.float32)


def _cmpwin(nq, cmp_hi, cmp_lo, kw, vw_ones, misc):
    b, s, w = nq.shape
    g = NSA_KV_HEADS
    tq = NSA_Q
    gw = NSA_GROUP * HEAD_DIM
    n_cmp = cmp_hi.shape[3]
    n_blk = s // SLC_BLOCK
    q_spec = pl.BlockSpec((1, tq, gw), lambda bi, gi, i: (bi, i, gi))
    key_spec = pl.BlockSpec((1, 1, 1, n_cmp, LANES), lambda bi, gi, i: (0, bi, gi, 0, 0))
    val_spec = pl.BlockSpec((1, 1, 1, n_cmp, LANES), lambda bi, gi, i: (1, bi, gi, 0, 0))
    seq_spec = pl.BlockSpec((1, 1, s, LANES), lambda bi, gi, i: (bi, gi, 0, 0))
    resident = 2 * s * LANES * 2 + 3 * n_cmp * LANES * 2 + n_blk * n_cmp * 2
    rows = NSA_GROUP * tq
    return pl.pallas_call(
        _cmpwin_kernel,
        grid=(b, g, s // tq),
        in_specs=[q_spec, key_spec, key_spec, val_spec,
                  pl.BlockSpec((n_cmp, n_blk), lambda bi, gi, i: (0, 0)),
                  seq_spec, seq_spec,
                  pl.BlockSpec((1, tq, LANES), lambda bi, gi, i: (bi, i, 0))],
        out_specs=(q_spec, pl.BlockSpec((1, 1, tq, n_blk), lambda bi, gi, i: (bi, gi, i, 0))),
        out_shape=(jax.ShapeDtypeStruct((b, s, w), F32),
                   jax.ShapeDtypeStruct((b, g, s, n_blk), BF16)),
        compiler_params=pltpu.CompilerParams(
            dimension_semantics=("arbitrary", "arbitrary", "arbitrary"),
            vmem_limit_bytes=_vmem_limit(resident + tq * gw * 8, 12 * rows * WIN_KEYS * 4)),
        name="cmpwin",
    )(nq, cmp_hi, cmp_lo, cmp_hi, jnp.asarray(_block_overlap(n_blk, n_cmp).T, BF16), kw, vw_ones, misc)


def _sel_kernel(q_ref, k_ref, v_ref, selb_ref, part_ref, misc_ref, z_ref, o_ref,
                qa_ref, gate_ref, *flash):
    tq = SEL_Q
    gi = pl.program_id(1)
    qi = pl.program_id(2)
    q0 = pl.multiple_of(qi * tq, tq)
    n_full = q0 // SEL_KV

    def prologue():
        q = q_ref[0]
        selb = selb_ref[0, 0]
        lane = lax.broadcasted_iota(jnp.int32, (tq, LANES), 1)
        for h in range(NSA_GROUP):
            r = slice(h * tq, (h + 1) * tq)
            slab = q[:, (h // 2) * LANES:(h // 2 + 1) * LANES]
            own = (lane < HEAD_DIM) if h % 2 == 0 else (lane >= HEAD_DIM)
            qa_ref[r, 0:LANES] = jnp.where(own, slab, jnp.zeros_like(slab))
            qa_ref[r, LANES:2 * LANES] = selb
        gates = _group_gates(misc_ref[0], gi)
        for h in range(NSA_GROUP):
            gate_ref[h] = _gate_lanes(gates, h, 1)

    def causal_mask(s, k0):
        t_row = q0 + (lax.broadcasted_iota(jnp.int32, s.shape, 0) & (tq - 1))
        pos = k0 + lax.broadcasted_iota(jnp.int32, s.shape, 1)
        return jnp.where(pos <= t_row, s, MASK_VALUE)

    def epilogue():
        acc_ref = flash[1]
        gated = []
        for h in range(NSA_GROUP):
            acc = acc_ref[h * tq:(h + 1) * tq, :]
            scaled = acc * (gate_ref[h] * (1.0 / acc[:, HEAD_DIM:HEAD_DIM + 1]))
            gated.append(scaled if h % 2 == 0 else pltpu.roll(scaled, HEAD_DIM, 1))
        for pp, slab in enumerate(_pair_slabs(gated)):
            c = slice(pp * LANES, (pp + 1) * LANES)
            o_ref[0, :, c] = ((part_ref[0, :, c] + slab) * z_ref[0, :, c].astype(F32)).astype(BF16)

    _causal_flash(n_full, SEL_KV, qa_ref,
                  lambda k0: k_ref[0, 0, pl.ds(k0, SEL_KV), :],
                  lambda p, k0: jnp.dot(p, v_ref[0, 0, pl.ds(k0, SEL_KV), :],
                                        preferred_element_type=F32),
                  causal_mask, prologue, epilogue, flash)


def _sel(nq, k_aug, v_ones, selb, part, misc, nz):
    b, s, w = nq.shape
    g = NSA_KV_HEADS
    tq = SEL_Q
    gw = NSA_GROUP * HEAD_DIM
    n_blk = s // SLC_BLOCK
    rows = NSA_GROUP * tq
    q_spec = pl.BlockSpec((1, tq, gw), lambda bi, gi, i: (bi, i, gi))
    resident = s * 3 * LANES * 2
    return pl.pallas_call(
        _sel_kernel,
        grid=(b, g, s // tq),
        in_specs=[q_spec,
                  pl.BlockSpec((1, 1, s, 2 * LANES), lambda bi, gi, i: (bi, gi, 0, 0)),
                  pl.BlockSpec((1, 1, s, LANES), lambda bi, gi, i: (bi, gi, 0, 0)),
                  pl.BlockSpec((1, 1, tq, n_blk), lambda bi, gi, i: (bi, gi, i, 0)),
                  q_spec,
                  pl.BlockSpec((1, tq, LANES), lambda bi, gi, i: (bi, i, 0)),
                  q_spec],
        out_specs=q_spec,
        out_shape=jax.ShapeDtypeStruct((b, s, w), BF16),
        scratch_shapes=[pltpu.VMEM((rows, 2 * LANES), BF16),
                        pltpu.VMEM((NSA_GROUP, tq, LANES), F32)] + _flash_scratch(rows, SEL_KV),
        compiler_params=pltpu.CompilerParams(
            dimension_semantics=("arbitrary", "arbitrary", "arbitrary"),
            vmem_limit_bytes=_vmem_limit(resident + tq * gw * 10,
                                         _flash_scratch_bytes(rows, SEL_KV) + 4 * rows * SEL_KV * 4)),
        name="sel",
    )(nq, k_aug, v_ones, selb, part, misc, nz)


def _out_kernel(yf_ref, yn_ref, w_ref, g_ref, mod_ref, x_ref, o_ref):
    y = (jnp.dot(yf_ref[0], w_ref[0:FOX_WIDTH, :], preferred_element_type=F32)
         + jnp.dot(yn_ref[0], w_ref[FOX_WIDTH:, :], preferred_element_type=F32))
    yn = y * lax.rsqrt(jnp.mean(y * y, axis=-1, keepdims=True) + RMS_EPS)
    o_ref[0] = x_ref[0] + mod_ref[0, 2:3, :] * (yn * g_ref[...])


def _out(y_fox, y_nsa, w_out, g_post, mod, x):
    b, s, d = x.shape
    tm = PROJ_ROWS
    row = lambda bi, i: (bi, i, 0)
    half = pl.BlockSpec((1, tm, FOX_WIDTH), row)
    pipelined = 2 * tm * FOX_WIDTH * 2 + 2 * tm * d * 4 + d * d * 2
    return pl.pallas_call(
        _out_kernel,
        grid=(b, s // tm),
        in_specs=[half, half,
                  pl.BlockSpec((d, d), lambda bi, i: (0, 0)),
                  pl.BlockSpec((1, d), lambda bi, i: (0, 0)),
                  pl.BlockSpec((1, 3, d), lambda bi, i: (bi, 0, 0)),
                  pl.BlockSpec((1, tm, d), row)],
        out_specs=pl.BlockSpec((1, tm, d), row),
        out_shape=jax.ShapeDtypeStruct((b, s, d), F32),
        compiler_params=pltpu.CompilerParams(
            dimension_semantics=("arbitrary", "arbitrary"),
            vmem_limit_bytes=_vmem_limit(pipelined, 6 * tm * d * 4)),
        name="out",
    )(y_fox, y_nsa, w_out, g_post, mod, x)


def _rope_slabs(seq_len):
    inv = 1.0 / (ROPE_THETA ** (jnp.arange(0, HEAD_DIM, 2, dtype=F32) / HEAD_DIM))
    ang = jnp.arange(seq_len, dtype=F32)[:, None] * inv[None, :]
    cos, sin = jnp.cos(ang), jnp.sin(ang)
    reps = LANES // (HEAD_DIM // 2)
    sign = jnp.tile(jnp.concatenate([-jnp.ones((HEAD_DIM // 2,), F32), jnp.ones((HEAD_DIM // 2,), F32)]),
                    LANES // HEAD_DIM)
    return jnp.tile(cos, (1, reps)), jnp.tile(sin, (1, reps)) * sign[None, :]


def _reorder_w_in(w_in):
    fw, kv = FOX_WIDTH, NSA_KV_WIDTH
    o = 0
    cols = {}
    for name, n in (("fq", fw), ("fk", fw), ("fv", fw), ("ff", FOX_HEADS), ("fz", fw), ("nq", NSA_WIDTH),
                    ("kc", kv), ("vc", kv), ("ks", kv), ("vs", kv), ("kw", kv), ("vw", kv),
                    ("ng", 3 * NSA_HEADS), ("nz", NSA_WIDTH)):
        cols[name] = w_in[:, o:o + n]
        o += n
    pad = jnp.zeros((w_in.shape[0], LANES - FOX_HEADS - 3 * NSA_HEADS), w_in.dtype)
    order = ("fq", "fk", "fv", "fz", "nq", "kc", "vc", "ks", "vs", "kw", "vw", "nz", "ff", "ng")
    return jnp.concatenate([cols[k] for k in order] + [pad], axis=1).astype(BF16)


def _layer(x, c8, g_pre, g_post, w_ada, b_ada, w_in, b_forget, w_cmp_k1, w_cmp_k2,
           w_cmp_v1, w_cmp_v2, pe_cmp_k, pe_cmp_v, w_out, cos128, sin128):
    b, s, d = x.shape
    mod = _ada(c8, w_ada, b_ada)[:b].reshape(b, 3, d)
    bf128 = jnp.pad(b_forget, (0, LANES - FOX_HEADS)).reshape(1, LANES)
    (fq, fk, fv, fz, nq, cmp_in, ks_aug, vs_ones, kw2, vw_ones, nz, misc, log_f) = _proj(
        x, mod, g_pre.reshape(1, d), _reorder_w_in(w_in), cos128, sin128, bf128)

    cum_split = tuple(p.reshape(b, FOX_HEADS, s)
                      for p in _cumsum_lanes_split(log_f.reshape(b * FOX_HEADS, s)))
    y_fox = _fox(fq, fk, fv, cum_split, fz)

    flat = CMP_STRIDE * HEAD_DIM
    t = cmp_in.reshape(2, b, NSA_KV_HEADS, s // CMP_STRIDE, flat)
    pe = jnp.stack([pe_cmp_k.reshape(1, 2 * flat), pe_cmp_v.reshape(1, 2 * flat)])
    w1 = jnp.stack([w_cmp_k1, w_cmp_v1])
    w2 = jnp.stack([w_cmp_k2, w_cmp_v2])
    cmp_hi, cmp_lo = _compress(t, jnp.concatenate([w1[:, :flat], w1[:, flat:]], axis=-1),
                               jnp.concatenate([w2, w2], axis=-1),
                               jnp.broadcast_to(pe, (2, SUBLANES, 2 * flat)))
    part, selb = _cmpwin(nq, cmp_hi, cmp_lo, kw2, vw_ones, misc)
    y_nsa = _sel(nq, ks_aug, vs_ones, selb, part, misc, nz)

    return _out(y_fox, y_nsa, w_out.astype(BF16), g_post.reshape(1, d), mod, x)


def kernel(x, c, g_pre, g_post, w_ada, b_ada, w_in, b_forget, w_cmp_k1, w_cmp_k2,
           w_cmp_v1, w_cmp_v2, pe_cmp_k, pe_cmp_v, w_out):
    cos128, sin128 = _rope_slabs(x.shape[1])
    c8 = jnp.pad(c, ((0, SUBLANES - c.shape[0]), (0, 0)))
    for layer in range(g_pre.shape[0]):
        x = _layer(x, c8, g_pre[layer], g_post[layer], w_ada[layer], b_ada[layer], w_in[layer],
                   b_forget[layer], w_cmp_k1[layer], w_cmp_k2[layer], w_cmp_v1[layer],
                   w_cmp_v2[layer], pe_cmp_k[layer], pe_cmp_v[layer], w_out[layer], cos128, sin128)
    return x
```

```python
import functools

import jax
import jax.numpy as jnp
import numpy as np
from jax import lax
from jax.experimental import pallas as pl
from jax.experimental.pallas import tpu as pltpu

F32 = jnp.float32
BF16 = jnp.bfloat16

D_MODEL = 1024
HEAD_DIM = 64
FOX_WIDTH = 512
NSA_WIDTH = 512
FOX_HEADS = 8
NSA_HEADS = 8
NSA_KV_HEADS = 2
NSA_GROUP = 4
NSA_KV_WIDTH = 128
CMP_BLOCK = 32
CMP_STRIDE = 16
CMP_HIDDEN = 128
SLC_BLOCK = 64
N_SELECT = 16
N_FORCED = 3
WINDOW = 512
ROPE_THETA = 10000.0
RMS_EPS = 1e-6
LOG2E = 1.4426950408889634
QK_SCALE = HEAD_DIM ** -0.5 * LOG2E

LANES = 128
SUBLANES = 8
MIB = 1024 * 1024
V7X_VMEM_BYTES = 64 * MIB
VMEM_LIMIT_FLOOR = 16 * MIB
VMEM_COMPILER_RESERVE = 8 * MIB
MASK_VALUE = -1e30

PROJ_ROWS = 512
FOX_TILE = 512
NSA_Q = 256
SEL_Q = 128
SEL_KV = 512
WIN_KEYS = WINDOW + NSA_Q
FOX_UNROLL = 6
SEL_UNROLL = 8
CMP_QUARTERS = 4
FOX_AUG = 6

C_FQ, C_FK, C_FV, C_FZ, C_NQ = 0, 512, 1024, 1536, 2048
C_KC, C_VC, C_KS, C_VS, C_KW, C_VW = 2560, 2688, 2816, 2944, 3072, 3200
C_NZ, C_MISC, PROJ_COLS = 3328, 3840, 3968
MISC_GATE0 = FOX_HEADS


def _vmem_limit(pipelined_bytes, resident_bytes):
    need = 2 * pipelined_bytes + resident_bytes
    return int(min(max(need, VMEM_LIMIT_FLOOR), V7X_VMEM_BYTES - VMEM_COMPILER_RESERVE))


def _sigmoid(v):
    return 1.0 / (1.0 + jnp.exp(-v))


def _silu(v):
    return v * _sigmoid(v)


def _dot_nt(a, b):
    return lax.dot_general(a, b, (((1,), (1,)), ((), ())), preferred_element_type=F32)


def _split3(v):
    hi = v.astype(BF16)
    r1 = v - hi.astype(F32)
    mid = r1.astype(BF16)
    lo = (r1 - mid.astype(F32)).astype(BF16)
    return hi, mid, lo


def _ada_kernel(c_ref, w_ref, b_ref, o_ref):
    a = _silu(c_ref[...])
    o_ref[...] = jnp.dot(a, w_ref[...], precision=lax.Precision.HIGHEST,
                         preferred_element_type=F32) + b_ref[...]


def _ada(c8, w_ada, b_ada):
    n = w_ada.shape[1]
    blk = D_MODEL
    return pl.pallas_call(
        _ada_kernel,
        grid=(n // blk,),
        in_specs=[pl.BlockSpec((SUBLANES, D_MODEL), lambda j: (0, 0)),
                  pl.BlockSpec((D_MODEL, blk), lambda j: (0, j)),
                  pl.BlockSpec((1, blk), lambda j: (0, j))],
        out_specs=pl.BlockSpec((SUBLANES, blk), lambda j: (0, j)),
        out_shape=jax.ShapeDtypeStruct((SUBLANES, n), F32),
        compiler_params=pltpu.CompilerParams(
            dimension_semantics=("arbitrary",),
            vmem_limit_bytes=_vmem_limit(D_MODEL * blk * 4, D_MODEL * blk * 4)),
        name="ada",
    )(c8, w_ada, b_ada.reshape(1, n))


def _rope128(t, cos, sin_signed):
    lane = lax.broadcasted_iota(jnp.int32, t.shape, 1)
    first_half = (lane & (HEAD_DIM - 1)) < HEAD_DIM // 2
    partner = jnp.where(first_half,
                        pltpu.roll(t, LANES - HEAD_DIM // 2, 1),
                        pltpu.roll(t, HEAD_DIM // 2, 1))
    return t * cos + partner * sin_signed


def _proj_kernel(x_ref, mod_ref, g_ref, w_ref, cos_ref, sin_ref, bf_ref,
                 fq_ref, fk_ref, fv_ref, fz_ref, nq_ref, cmp_ref,
                 ks_ref, vs_ref, kw_ref, vw_ref, nz_ref, misc_ref, lf_ref):
    tm = x_ref.shape[1]
    x = x_ref[0]
    y = x * lax.rsqrt(jnp.mean(x * x, axis=-1, keepdims=True) + RMS_EPS)
    y = y * g_ref[...]
    h = (y * (1.0 + mod_ref[0, 1:2, :]) + mod_ref[0, 0:1, :]).astype(BF16)
    cos = cos_ref[...]
    sin = sin_ref[...]

    def mm(lo, n):
        return jnp.dot(h, w_ref[:, lo:lo + n], preferred_element_type=F32)

    def slab_pair(lo):
        both = mm(lo, 2 * LANES)
        return both[:, 0:LANES], both[:, LANES:2 * LANES]

    lane = lax.broadcasted_iota(jnp.int32, (tm, LANES), 1)
    low = lane < HEAD_DIM
    ones_col = jnp.where(lane == HEAD_DIM, 1.0, 0.0).astype(BF16)

    def doubled(slab):
        swapped = pltpu.roll(slab, HEAD_DIM, 1)
        return jnp.where(low, slab, swapped), jnp.where(low, swapped, slab)

    def with_ones(slab):
        swapped = pltpu.roll(slab, HEAD_DIM, 1)
        return jnp.where(low, slab, ones_col), jnp.where(low, swapped, ones_col)

    fq_ref[0] = (mm(C_FQ, FOX_WIDTH) * QK_SCALE).astype(BF16)
    fk_ref[0] = mm(C_FK, FOX_WIDTH).astype(BF16)
    fv = mm(C_FV, FOX_WIDTH).astype(BF16)
    for pair in range(FOX_WIDTH // LANES):
        fv_ref[0, 2 * pair], fv_ref[0, 2 * pair + 1] = with_ones(fv[:, pair * LANES:(pair + 1) * LANES])
    fz_ref[0] = _silu(mm(C_FZ, FOX_WIDTH)).astype(BF16)
    for j in range(NSA_WIDTH // (2 * LANES)):
        for i, t in enumerate(slab_pair(C_NQ + 2 * j * LANES)):
            c = (2 * j + i) * LANES
            nq_ref[0, :, c:c + LANES] = (_rope128(t, cos, sin) * QK_SCALE).astype(BF16)

    kc, vc = slab_pair(C_KC)
    for kind, slab in enumerate((_rope128(kc, cos, sin), vc)):
        for g in range(NSA_KV_HEADS):
            cmp_ref[kind, 0, g] = slab[:, g * HEAD_DIM:(g + 1) * HEAD_DIM]
    pos = pl.program_id(1) * tm + lax.broadcasted_iota(jnp.int32, (tm, LANES), 0)
    onehot = jnp.where(pos // SLC_BLOCK == lane, 1.0, 0.0).astype(BF16)
    ks, vs = slab_pair(C_KS)
    for g, k2 in enumerate(doubled(_rope128(ks, cos, sin).astype(BF16))):
        ks_ref[0, g, :, 0:LANES] = k2
        ks_ref[0, g, :, LANES:2 * LANES] = onehot
    vs_ref[0, 0], vs_ref[0, 1] = with_ones(vs.astype(BF16))
    kw, vw = slab_pair(C_KW)
    kw_ref[0, 0], kw_ref[0, 1] = doubled(_rope128(kw, cos, sin).astype(BF16))
    vw_ref[0, 0], vw_ref[0, 1] = doubled(vw.astype(BF16))
    nz_ref[0] = _silu(mm(C_NZ, NSA_WIDTH)).astype(BF16)
    misc = mm(C_MISC, LANES)
    misc_ref[0] = misc
    z = misc + bf_ref[...]
    log_f = jnp.minimum(z, 0.0) - jnp.log1p(jnp.exp(-jnp.abs(z)))
    lf_ref[0] = log_f.T[0:FOX_HEADS, :]


def _proj(x, mod, g_pre, w_cat, cos128, sin128, bf128):
    b, s, d = x.shape
    tm = PROJ_ROWS
    row = lambda bi, i: (bi, i, 0)
    wide = lambda dt: jax.ShapeDtypeStruct((b, s, FOX_WIDTH), dt)
    g = NSA_KV_HEADS
    grouped = lambda lanes: jax.ShapeDtypeStruct((b, g, s, lanes), BF16)
    grouped_spec = lambda lanes: pl.BlockSpec((1, g, tm, lanes), lambda bi, i: (bi, 0, i, 0))
    out_shape = (wide(BF16), wide(BF16), jax.ShapeDtypeStruct((b, FOX_HEADS, s, LANES), BF16),
                 wide(BF16), wide(BF16),
                 jax.ShapeDtypeStruct((2, b, g, s, HEAD_DIM), F32),
                 grouped(2 * LANES), grouped(LANES), grouped(LANES), grouped(LANES),
                 wide(BF16), jax.ShapeDtypeStruct((b, s, LANES), F32),
                 jax.ShapeDtypeStruct((b, FOX_HEADS, s), F32))
    wide_spec = pl.BlockSpec((1, tm, FOX_WIDTH), row)
    out_specs = (wide_spec, wide_spec,
                 pl.BlockSpec((1, FOX_HEADS, tm, LANES), lambda bi, i: (bi, 0, i, 0)),
                 wide_spec, wide_spec) + (
        pl.BlockSpec((2, 1, g, tm, HEAD_DIM), lambda bi, i: (0, bi, 0, i, 0)),
        grouped_spec(2 * LANES), grouped_spec(LANES), grouped_spec(LANES), grouped_spec(LANES),
        wide_spec, pl.BlockSpec((1, tm, LANES), row),
        pl.BlockSpec((1, FOX_HEADS, tm), lambda bi, i: (bi, 0, i)))
    pipelined = (tm * d * 4 + tm * (7 * FOX_WIDTH * 2 + 2 * g * LANES * 4 + 5 * g * LANES * 2 + LANES * 4)
                 + d * PROJ_COLS * 2)
    return pl.pallas_call(
        _proj_kernel,
        grid=(b, s // tm),
        in_specs=[pl.BlockSpec((1, tm, d), row),
                  pl.BlockSpec((1, 3, d), lambda bi, i: (bi, 0, 0)),
                  pl.BlockSpec((1, d), lambda bi, i: (0, 0)),
                  pl.BlockSpec((d, PROJ_COLS), lambda bi, i: (0, 0)),
                  pl.BlockSpec((tm, LANES), lambda bi, i: (i, 0)),
                  pl.BlockSpec((tm, LANES), lambda bi, i: (i, 0)),
                  pl.BlockSpec((1, LANES), lambda bi, i: (0, 0))],
        out_specs=out_specs,
        out_shape=out_shape,
        compiler_params=pltpu.CompilerParams(
            dimension_semantics=("arbitrary", "arbitrary"),
            vmem_limit_bytes=_vmem_limit(pipelined, 4 * tm * d * 4)),
        name="proj",
    )(x, mod, g_pre, w_cat, cos128, sin128, bf128)


def _cumsum_kernel(x_ref, hi_ref, mid_ref, lo_ref, *, chunks):
    x = x_ref[...]
    n = x.shape[0]
    parts = _split3(x)
    r = lax.broadcasted_iota(jnp.int32, (LANES, LANES), 0)
    c = lax.broadcasted_iota(jnp.int32, (LANES, LANES), 1)
    tri = (r <= c).astype(BF16)
    rr = lax.broadcasted_iota(jnp.int32, (n, n), 0)
    cc = lax.broadcasted_iota(jnp.int32, (n, n), 1)
    earlier = jnp.logical_and(cc < rr, (cc // chunks) == (rr // chunks)).astype(BF16)
    within = sum(jnp.dot(p, tri, preferred_element_type=F32) for p in parts)
    before = sum(jnp.dot(earlier, p, preferred_element_type=F32) for p in parts)
    total = (within + jnp.sum(before, axis=-1, keepdims=True)) * LOG2E
    hi_ref[...], mid_ref[...], lo_ref[...] = _split3(total)


def _cumsum_lanes_split(v):
    rows, s = v.shape
    chunks = s // LANES
    n = rows * chunks
    part = jax.ShapeDtypeStruct((n, LANES), BF16)
    parts = pl.pallas_call(
        functools.partial(_cumsum_kernel, chunks=chunks),
        out_shape=(part, part, part),
        compiler_params=pltpu.CompilerParams(
            vmem_limit_bytes=_vmem_limit(2 * n * LANES * 4, 6 * n * n)),
        name="cumsum",
    )(v.reshape(n, LANES))
    return tuple(p.reshape(rows, s) for p in parts)


def _flash_scratch(rows, tile):
    return ([pltpu.VMEM((rows, LANES), F32)] * 2
            + [pltpu.VMEM((rows, tile), F32)] * 2
            + [pltpu.VMEM((rows, tile), BF16)] * 2
            + [pltpu.VMEM((rows, LANES), F32)] * 2)


def _flash_scratch_bytes(rows, tile):
    return rows * (2 * LANES * 4 + 2 * tile * 4 + 2 * tile * 2 + 2 * LANES * 4)


def _causal_flash(n_full, tile, unroll, q_ref, load_k, pv, causal_mask, prologue, epilogue, scratch):
    m_ref, acc_ref, s0, s1, p0, p1, a0, a1 = scratch
    s_bufs, p_bufs, a_bufs = (s0, s1), (p0, p1), (a0, a1)

    def logits(j):
        return _dot_nt(q_ref[...], load_k(pl.multiple_of(j * tile, tile)))

    def softmax(s):
        m_prev = m_ref[...]
        m_next = jnp.maximum(m_prev, jnp.max(s, axis=1, keepdims=True))
        m_ref[...] = m_next
        p = jnp.exp2(s - jnp.tile(m_next, (1, tile // LANES)))
        return p.astype(BF16), jnp.exp2(m_prev - m_next)

    def accumulate(p, alpha, j):
        acc_ref[...] = acc_ref[...] * alpha + pv(p, pl.multiple_of(j * tile, tile))

    def stage(j, cur):
        s_bufs[1 - cur][...] = logits(j + 1)
        p_bufs[cur][...], a_bufs[cur][...] = softmax(s_bufs[cur][...])
        accumulate(p_bufs[1 - cur][...], a_bufs[1 - cur][...], j - 1)

    def run_stages(first, count):
        for i in range(count):
            stage(first + i, (1 + i) % 2)

    def finish(cur):
        p, alpha = softmax(causal_mask(s_bufs[cur][...], n_full * tile))
        accumulate(p_bufs[1 - cur][...], a_bufs[1 - cur][...], n_full - 1)
        accumulate(p, alpha, n_full)

    def start():
        prologue()
        m_ref[...] = jnp.full(m_ref.shape, MASK_VALUE, F32)
        acc_ref[...] = jnp.zeros(acc_ref.shape, F32)

    @pl.when(n_full == 0)
    def _():
        start()
        p, alpha = softmax(causal_mask(logits(0), 0))
        accumulate(p, alpha, 0)
        epilogue()

    @pl.when(n_full > 0)
    def _():
        start()
        s_first = logits(0)
        s_bufs[1][...] = logits(1)
        p_bufs[0][...], a_bufs[0][...] = softmax(s_first)
        later = n_full - 1

        def unrolled(jj, carry):
            run_stages(1 + unroll * jj, unroll)
            return carry

        lax.fori_loop(0, later // unroll, unrolled, 0)
        rest = later % unroll

        def tail(count):
            run_stages(n_full - count, count)
            finish((1 + count) % 2)
            epilogue()

        for count in range(unroll):
            pl.when(rest == count)(functools.partial(tail, count))


def _fox_aug(cum_split):
    b, h, s = cum_split[0].shape
    terms = jnp.concatenate(list(cum_split) + [jnp.ones((b, 1, s), BF16)], axis=1)
    terms = jnp.transpose(terms, (0, 2, 1))
    place_q = np.zeros((3 * h + 1, h * HEAD_DIM), np.float32)
    place_k = np.zeros((3 * h + 1, h * HEAD_DIM), np.float32)
    half = FOX_AUG // 2
    for head in range(h):
        for part in range(half):
            place_q[part * h + head, head * HEAD_DIM + part] = 1.0
            place_q[3 * h, head * HEAD_DIM + half + part] = 1.0
            place_k[3 * h, head * HEAD_DIM + part] = 1.0
            place_k[part * h + head, head * HEAD_DIM + half + part] = -1.0
    scatter = lambda place: jnp.einsum("bsr,rl->bsl", terms, jnp.asarray(place, BF16),
                                       preferred_element_type=F32).astype(BF16)
    return scatter(place_q), scatter(place_k)


def _fox_kernel(q_ref, qa_ref, k_ref, ka_ref, v_ref, z_ref, o_ref, qm_ref, *flash):
    t = FOX_TILE
    qi = pl.program_id(2)

    def prologue():
        q = q_ref[0]
        qa = qa_ref[0]
        lane = lax.broadcasted_iota(jnp.int32, q.shape, 1)
        for hh in range(2):
            r = slice(hh * t, (hh + 1) * t)
            own = (lane < HEAD_DIM) if hh == 0 else (lane >= HEAD_DIM)
            qm_ref[r, 0:LANES] = jnp.where(own, q, jnp.zeros_like(q))
            qm_ref[r, LANES:2 * LANES] = jnp.where(own, qa, jnp.zeros_like(qa))

    def load_k(k0):
        return jnp.concatenate([k_ref[0, pl.ds(k0, t), :], ka_ref[0, pl.ds(k0, t), :]], axis=1)

    def pv(p, k0):
        return jnp.concatenate(
            [jnp.dot(p[hh * t:(hh + 1) * t], v_ref[0, hh, pl.ds(k0, t), :], preferred_element_type=F32)
             for hh in range(2)], axis=0)

    def causal_mask(s, k0):
        t_row = qi * t + (lax.broadcasted_iota(jnp.int32, s.shape, 0) & (t - 1))
        pos = k0 + lax.broadcasted_iota(jnp.int32, s.shape, 1)
        return jnp.where(pos <= t_row, s, MASK_VALUE)

    def epilogue():
        acc_ref = flash[1]
        o0 = acc_ref[0:t, :] * (1.0 / acc_ref[0:t, HEAD_DIM:HEAD_DIM + 1])
        o1 = acc_ref[t:2 * t, :] * (1.0 / acc_ref[t:2 * t, HEAD_DIM:HEAD_DIM + 1])
        lane = lax.broadcasted_iota(jnp.int32, o0.shape, 1)
        o = jnp.where(lane < HEAD_DIM, o0, pltpu.roll(o1, HEAD_DIM, 1))
        o_ref[0] = (o * z_ref[0].astype(F32)).astype(BF16)

    _causal_flash(qi, t, FOX_UNROLL, qm_ref, load_k, pv, causal_mask, prologue, epilogue, flash)


def _fox(fq, fk, fv, cum_split, fz):
    b, s, w = fq.shape
    t = FOX_TILE
    pairs = w // LANES
    qa, ka = _fox_aug(cum_split)
    tile = pl.BlockSpec((1, t, LANES), lambda bi, hp, i: (bi, i, hp))
    full = pl.BlockSpec((1, s, LANES), lambda bi, hp, i: (bi, 0, hp))
    resident = 4 * s * LANES * 2
    scratch = 2 * t * 2 * LANES * 2 + _flash_scratch_bytes(2 * t, t)
    return pl.pallas_call(
        _fox_kernel,
        grid=(b, pairs, s // t),
        in_specs=[tile, tile, full, full,
                  pl.BlockSpec((1, 2, s, LANES), lambda bi, hp, i: (bi, hp, 0, 0)),
                  tile],
        out_specs=tile,
        out_shape=jax.ShapeDtypeStruct((b, s, w), BF16),
        scratch_shapes=[pltpu.VMEM((2 * t, 2 * LANES), BF16)] + _flash_scratch(2 * t, t),
        compiler_params=pltpu.CompilerParams(
            dimension_semantics=("arbitrary", "arbitrary", "arbitrary"),
            vmem_limit_bytes=_vmem_limit(resident + 4 * t * LANES * 2, scratch + 4 * 2 * t * t * 4)),
        name="fox",
    )(fq, qa, fk, ka, fv, fz)


def _compress_kernel(t_ref, w1_ref, w2_ref, pe_ref, hi_ref, lo_ref):
    hp = lax.Precision.HIGHEST
    half = CMP_STRIDE * HEAD_DIM
    hid = CMP_HIDDEN
    tt = t_ref[0, 0, 0]
    n = tt.shape[0]
    w1 = w1_ref[0]
    both = jnp.dot(tt, w1, precision=hp, preferred_element_type=F32)
    pe_first = jnp.dot(pe_ref[0, :, 0:half], w1, precision=hp, preferred_element_type=F32)
    pe_second = jnp.dot(pe_ref[0, :, half:2 * half], w1, precision=hp, preferred_element_type=F32)
    pe_term = pe_first[0:1, 0:hid] + pe_second[0:1, hid:2 * hid]
    hidden = both[:, 0:hid] + pltpu.roll(both[:, hid:2 * hid], n - 1, 0) + pe_term
    out = jnp.dot(_silu(hidden), w2_ref[0], precision=hp, preferred_element_type=F32)
    hi = out.astype(BF16)
    hi_ref[0, 0, 0] = hi
    lo_ref[0, 0, 0] = (out - hi.astype(F32)).astype(BF16)


def _compress(t, w1, w2, pe):
    kinds, b, g, n, flat = t.shape
    width = w2.shape[-1]
    out_spec = pl.BlockSpec((1, 1, 1, n, width), lambda a, bi, gi: (a, bi, gi, 0, 0))
    out_part = jax.ShapeDtypeStruct((kinds, b, g, n, width), BF16)
    return pl.pallas_call(
        _compress_kernel,
        grid=(kinds, b, g),
        in_specs=[pl.BlockSpec((1, 1, 1, n, flat), lambda a, bi, gi: (a, bi, gi, 0, 0)),
                  pl.BlockSpec((1, flat, 2 * CMP_HIDDEN), lambda a, bi, gi: (a, 0, 0)),
                  pl.BlockSpec((1, CMP_HIDDEN, width), lambda a, bi, gi: (a, 0, 0)),
                  pl.BlockSpec((1, SUBLANES, 2 * flat), lambda a, bi, gi: (a, 0, 0))],
        out_specs=(out_spec, out_spec),
        out_shape=(out_part, out_part),
        compiler_params=pltpu.CompilerParams(
            dimension_semantics=("arbitrary", "arbitrary", "arbitrary"),
            vmem_limit_bytes=_vmem_limit(n * flat * 4 + 2 * flat * CMP_HIDDEN * 4, 4 * n * flat * 4)),
        name="compress",
    )(t, w1, w2, pe)


def _softmax_numerator(s, bias):
    s = s + bias
    m = jnp.max(s, axis=1, keepdims=True)
    m = jnp.where(m == -jnp.inf, 0.0, m)
    return jnp.exp2(s - m)


def _stack_heads(q):
    tq = q.shape[0]
    lane = lax.broadcasted_iota(jnp.int32, (tq, LANES), 1)
    rows = []
    for h in range(NSA_GROUP):
        slab = q[:, (h // 2) * LANES:(h // 2 + 1) * LANES]
        own = (lane < HEAD_DIM) if h % 2 == 0 else (lane >= HEAD_DIM)
        rows.append(jnp.where(own, slab, jnp.zeros_like(slab)))
    return jnp.concatenate(rows, axis=0)


def _group_gates(misc, gi):
    gates = _sigmoid(misc)
    return jnp.where(gi == 0, gates, pltpu.roll(gates, LANES - 3 * NSA_GROUP, 1))


def _gate_lanes(gates, h, j):
    c = MISC_GATE0 + 3 * h + j
    return jnp.broadcast_to(gates[:, c:c + 1], gates.shape)


def _pair_slabs(per_head):
    lane = lax.broadcasted_iota(jnp.int32, per_head[0].shape, 1)
    return [jnp.where(lane < HEAD_DIM, per_head[2 * pp], per_head[2 * pp + 1])
            for pp in range(NSA_GROUP // 2)]


def _cmpwin_kernel(q_ref, kc_hi_ref, kc_lo_ref, vc_ref, overlap_ref, kw_ref, vw_ref, misc_ref,
                   part_ref, selb_ref):
    gi = pl.program_id(1)
    q0 = pl.multiple_of(pl.program_id(2) * NSA_Q, NSA_Q)
    quarter = kc_hi_ref.shape[3] // CMP_QUARTERS
    reach = q0 // (quarter * CMP_STRIDE)
    refs = (q_ref, kc_hi_ref, kc_lo_ref, vc_ref, overlap_ref, kw_ref, vw_ref, misc_ref,
            part_ref, selb_ref)
    for quarters in range(1, CMP_QUARTERS + 1):
        pl.when(reach == quarters - 1)(
            functools.partial(_cmpwin_body, refs, gi, q0, quarters * quarter))


def _cmpwin_body(refs, gi, q0, n_cmp):
    (q_ref, kc_hi_ref, kc_lo_ref, vc_ref, overlap_ref, kw_ref, vw_ref, misc_ref,
     part_ref, selb_ref) = refs
    tq = NSA_Q
    q4 = _stack_heads(q_ref[0])
    n_blk = n_cmp * CMP_STRIDE // SLC_BLOCK
    head_rows = [slice(h * tq, (h + 1) * tq) for h in range(NSA_GROUP)]

    kc = jnp.concatenate([kc_hi_ref[0, 0, 0, 0:n_cmp, :], kc_lo_ref[0, 0, 0, 0:n_cmp, :]], axis=1)
    s = _dot_nt(jnp.concatenate([q4, q4], axis=1), kc)
    col = lax.broadcasted_iota(jnp.int32, (tq, n_cmp), 1)
    t_row = q0 + lax.broadcasted_iota(jnp.int32, (tq, n_cmp), 0)
    cmp_bias = jnp.where(col * CMP_STRIDE + (CMP_BLOCK - 1) <= t_row, 0.0, -jnp.inf)
    pcs = []
    for r in head_rows:
        p = _softmax_numerator(s[r], cmp_bias)
        pcs.append(p * (1.0 / jnp.maximum(jnp.sum(p, axis=1, keepdims=True), 1e-30)))
    oc = jnp.dot(jnp.concatenate(pcs, axis=0).astype(BF16), vc_ref[0, 0, 0, 0:n_cmp, :],
                 preferred_element_type=F32)

    pc_sum = pcs[0] + pcs[1] + pcs[2] + pcs[3]
    split = jnp.concatenate(_split3(pc_sum), axis=0)
    terms = jnp.dot(split, overlap_ref[0:n_cmp, :], preferred_element_type=F32)
    imp = (terms[0:tq] + terms[tq:2 * tq] + terms[2 * tq:3 * tq]).T[0:n_blk, :]
    blk = lax.broadcasted_iota(jnp.int32, (n_blk, tq), 0)
    cur = (q0 + lax.broadcasted_iota(jnp.int32, (n_blk, tq), 1)) // SLC_BLOCK
    forced = jnp.logical_or(blk == 0, jnp.logical_or(blk == cur, blk == cur - 1))

    def pick_one(_, carry):
        rem, sel = carry
        best = jnp.max(rem, axis=0, keepdims=True)
        first = jnp.min(jnp.where(rem == best, blk, n_blk), axis=0, keepdims=True)
        hit = blk == first
        return jnp.where(hit, -jnp.inf, rem), jnp.where(hit, 1.0, sel)

    candidates = jnp.where(jnp.logical_or(forced, blk > cur), -jnp.inf, imp)
    _, sel = lax.fori_loop(0, min(N_SELECT, n_blk) - N_FORCED, pick_one,
                           (candidates, jnp.where(forced, 1.0, 0.0)), unroll=True)
    all_blk = selb_ref.shape[3]
    if n_blk < all_blk:
        sel = jnp.concatenate([sel, jnp.zeros((all_blk - n_blk, tq), F32)], axis=0)
    selb_ref[0, 0] =jnp.where(sel.T > 0.5, 0.0, MASK_VALUE).astype(BF16)

    start = pl.multiple_of(jnp.maximum(q0 - WINDOW, 0), tq)
    kw = kw_ref[0, 0, pl.ds(start, WIN_KEYS), :]
    vw = vw_ref[0, 0, pl.ds(start, WIN_KEYS), :]
    sw = _dot_nt(q4, kw)
    pos = start + lax.broadcasted_iota(jnp.int32, (tq, WIN_KEYS), 1)
    dist = q0 + lax.broadcasted_iota(jnp.int32, (tq, WIN_KEYS), 0) - pos
    win_bias = jnp.where(jnp.logical_and(dist >= 0, dist < WINDOW), 0.0, -jnp.inf)
    pw = jnp.concatenate([_softmax_numerator(sw[r], win_bias) for r in head_rows],
                         axis=0).astype(BF16)
    ow = jnp.dot(pw, vw, preferred_element_type=F32)
    lw = jnp.dot(pw, jnp.ones((WIN_KEYS, LANES), BF16), preferred_element_type=F32)
    ow = ow * (1.0 / jnp.maximum(lw, 1e-30))

    gates = _group_gates(misc_ref[0], gi)
    gated = [_gate_lanes(gates, h, 0) * oc[r] + _gate_lanes(gates, h, 2) * ow[r]
             for h, r in enumerate(head_rows)]
    for pp, slab in enumerate(_pair_slabs(gated)):
        part_ref[0, :, pp * LANES:(pp + 1) * LANES] = slab


def _block_overlap(n_blk, n_cmp):
    ratio = SLC_BLOCK // CMP_STRIDE
    lo = np.arange(n_blk)[:, None] * ratio - (CMP_BLOCK // CMP_STRIDE - 1)
    i = np.arange(n_cmp)[None, :]
    n_overlap = (SLC_BLOCK + CMP_BLOCK) // CMP_STRIDE - 1
    return ((i >= lo) & (i < lo + n_overlap) & (i < n_cmp - 1)).astype(np.float32)


def _cmpwin(nq, cmp_hi, cmp_lo, kw, vw_ones, misc):
    b, s, w = nq.shape
    g = NSA_KV_HEADS
    tq = NSA_Q
    gw = NSA_GROUP * HEAD_DIM
    n_cmp = cmp_hi.shape[3]
    n_blk = s // SLC_BLOCK
    q_spec = pl.BlockSpec((1, tq, gw), lambda bi, gi, i: (bi, i, gi))
    key_spec = pl.BlockSpec((1, 1, 1, n_cmp, LANES), lambda bi, gi, i: (0, bi, gi, 0, 0))
    val_spec = pl.BlockSpec((1, 1, 1, n_cmp, LANES), lambda bi, gi, i: (1, bi, gi, 0, 0))
    seq_spec = pl.BlockSpec((1, 1, s, LANES), lambda bi, gi, i: (bi, gi, 0, 0))
    resident = 2 * s * LANES * 2 + 3 * n_cmp * LANES * 2 + n_blk * n_cmp * 2
    rows = NSA_GROUP * tq
    return pl.pallas_call(
        _cmpwin_kernel,
        grid=(b, g, s // tq),
        in_specs=[q_spec, key_spec, key_spec, val_spec,
                  pl.BlockSpec((n_cmp, n_blk), lambda bi, gi, i: (0, 0)),
                  seq_spec, seq_spec,
                  pl.BlockSpec((1, tq, LANES), lambda bi, gi, i: (bi, i, 0))],
        out_specs=(q_spec, pl.BlockSpec((1, 1, tq, n_blk), lambda bi, gi, i: (bi, gi, i, 0))),
        out_shape=(jax.ShapeDtypeStruct((b, s, w), F32),
                   jax.ShapeDtypeStruct((b, g, s, n_blk), BF16)),
        compiler_params=pltpu.CompilerParams(
            dimension_semantics=("arbitrary", "arbitrary", "arbitrary"),
            vmem_limit_bytes=_vmem_limit(resident + tq * gw * 8, 12 * rows * WIN_KEYS * 4)),
        name="cmpwin",
    )(nq, cmp_hi, cmp_lo, cmp_hi, jnp.asarray(_block_overlap(n_blk, n_cmp).T, BF16), kw, vw_ones, misc)


def _sel_kernel(q_ref, k_ref, v_ref, selb_ref, part_ref, misc_ref, z_ref, o_ref,
                qa_ref, gate_ref, *flash):
    tq = SEL_Q
    gi = pl.program_id(1)
    qi = pl.program_id(2)
    q0 = pl.multiple_of(qi * tq, tq)
    n_full = q0 // SEL_KV

    def prologue():
        q = q_ref[0]
        selb = selb_ref[0, 0]
        lane = lax.broadcasted_iota(jnp.int32, (tq, LANES), 1)
        for h in range(NSA_GROUP):
            r = slice(h * tq, (h + 1) * tq)
            slab = q[:, (h // 2) * LANES:(h // 2 + 1) * LANES]
            own = (lane < HEAD_DIM) if h % 2 == 0 else (lane >= HEAD_DIM)
            qa_ref[r, 0:LANES] = jnp.where(own, slab, jnp.zeros_like(slab))
            qa_ref[r, LANES:2 * LANES] = selb
        gates = _group_gates(misc_ref[0], gi)
        for h in range(NSA_GROUP):
            gate_ref[h] = _gate_lanes(gates, h, 1)

    def causal_mask(s, k0):
        t_row = q0 + (lax.broadcasted_iota(jnp.int32, s.shape, 0) & (tq - 1))
        pos = k0 + lax.broadcasted_iota(jnp.int32, s.shape, 1)
        return jnp.where(pos <= t_row, s, MASK_VALUE)

    def epilogue():
        acc_ref = flash[1]
        gated = []
        for h in range(NSA_GROUP):
            acc = acc_ref[h * tq:(h + 1) * tq, :]
            scaled = acc * (gate_ref[h] * (1.0 / acc[:, HEAD_DIM:HEAD_DIM + 1]))
            gated.append(scaled if h % 2 == 0 else pltpu.roll(scaled, HEAD_DIM, 1))
        for pp, slab in enumerate(_pair_slabs(gated)):
            c = slice(pp * LANES, (pp + 1) * LANES)
            o_ref[0, :, c] = ((part_ref[0, :, c] + slab) * z_ref[0, :, c].astype(F32)).astype(BF16)

    _causal_flash(n_full, SEL_KV, SEL_UNROLL, qa_ref,
                  lambda k0: k_ref[0, 0, pl.ds(k0, SEL_KV), :],
                  lambda p, k0: jnp.dot(p, v_ref[0, 0, pl.ds(k0, SEL_KV), :],
                                        preferred_element_type=F32),
                  causal_mask, prologue, epilogue, flash)


def _sel(nq, k_aug, v_ones, selb, part, misc, nz):
    b, s, w = nq.shape
    g = NSA_KV_HEADS
    tq = SEL_Q
    gw = NSA_GROUP * HEAD_DIM
    n_blk = s // SLC_BLOCK
    rows = NSA_GROUP * tq
    q_spec = pl.BlockSpec((1, tq, gw), lambda bi, gi, i: (bi, i, gi))
    resident = s * 3 * LANES * 2
    return pl.pallas_call(
        _sel_kernel,
        grid=(b, g, s // tq),
        in_specs=[q_spec,
                  pl.BlockSpec((1, 1, s, 2 * LANES), lambda bi, gi, i: (bi, gi, 0, 0)),
                  pl.BlockSpec((1, 1, s, LANES), lambda bi, gi, i: (bi, gi, 0, 0)),
                  pl.BlockSpec((1, 1, tq, n_blk), lambda bi, gi, i: (bi, gi, i, 0)),
                  q_spec,
                  pl.BlockSpec((1, tq, LANES), lambda bi, gi, i: (bi, i, 0)),
                  q_spec],
        out_specs=q_spec,
        out_shape=jax.ShapeDtypeStruct((b, s, w), BF16),
        scratch_shapes=[pltpu.VMEM((rows, 2 * LANES), BF16),
                        pltpu.VMEM((NSA_GROUP, tq, LANES), F32)] + _flash_scratch(rows, SEL_KV),
        compiler_params=pltpu.CompilerParams(
            dimension_semantics=("arbitrary", "arbitrary", "arbitrary"),
            vmem_limit_bytes=_vmem_limit(resident + tq * gw * 10,
                                         _flash_scratch_bytes(rows, SEL_KV) + 4 * rows * SEL_KV * 4)),
        name="sel",
    )(nq, k_aug, v_ones, selb, part, misc, nz)


def _out_kernel(yf_ref, yn_ref, w_ref, g_ref, mod_ref, x_ref, o_ref):
    y = (jnp.dot(yf_ref[0], w_ref[0:FOX_WIDTH, :], preferred_element_type=F32)
         + jnp.dot(yn_ref[0], w_ref[FOX_WIDTH:, :], preferred_element_type=F32))
    yn = y * lax.rsqrt(jnp.mean(y * y, axis=-1, keepdims=True) + RMS_EPS)
    o_ref[0] = x_ref[0] + mod_ref[0, 2:3, :] * (yn * g_ref[...])


def _out(y_fox, y_nsa, w_out, g_post, mod, x):
    b, s, d = x.shape
    tm = PROJ_ROWS
    row = lambda bi, i: (bi, i, 0)
    half = pl.BlockSpec((1, tm, FOX_WIDTH), row)
    pipelined = 2 * tm * FOX_WIDTH * 2 + 2 * tm * d * 4 + d * d * 2
    return pl.pallas_call(
        _out_kernel,
        grid=(b, s // tm),
        in_specs=[half, half,
                  pl.BlockSpec((d, d), lambda bi, i: (0, 0)),
                  pl.BlockSpec((1, d), lambda bi, i: (0, 0)),
                  pl.BlockSpec((1, 3, d), lambda bi, i: (bi, 0, 0)),
                  pl.BlockSpec((1, tm, d), row)],
        out_specs=pl.BlockSpec((1, tm, d), row),
        out_shape=jax.ShapeDtypeStruct((b, s, d), F32),
        compiler_params=pltpu.CompilerParams(
            dimension_semantics=("arbitrary", "arbitrary"),
            vmem_limit_bytes=_vmem_limit(pipelined, 6 * tm * d * 4)),
        name="out",
    )(y_fox, y_nsa, w_out, g_post, mod, x)


def _rope_slabs(seq_len):
    inv = 1.0 / (ROPE_THETA ** (jnp.arange(0, HEAD_DIM, 2, dtype=F32) / HEAD_DIM))
    ang = jnp.arange(seq_len, dtype=F32)[:, None] * inv[None, :]
    cos, sin = jnp.cos(ang), jnp.sin(ang)
    reps = LANES // (HEAD_DIM // 2)
    sign = jnp.tile(jnp.concatenate([-jnp.ones((HEAD_DIM // 2,), F32), jnp.ones((HEAD_DIM // 2,), F32)]),
                    LANES // HEAD_DIM)
    return jnp.tile(cos, (1, reps)), jnp.tile(sin, (1, reps)) * sign[None, :]


def _reorder_w_in(w_in):
    fw, kv = FOX_WIDTH, NSA_KV_WIDTH
    o = 0
    cols = {}
    for name, n in (("fq", fw), ("fk", fw), ("fv", fw), ("ff", FOX_HEADS), ("fz", fw), ("nq", NSA_WIDTH),
                    ("kc", kv), ("vc", kv), ("ks", kv), ("vs", kv), ("kw", kv), ("vw", kv),
                    ("ng", 3 * NSA_HEADS), ("nz", NSA_WIDTH)):
        cols[name] = w_in[:, o:o + n]
        o += n
    pad = jnp.zeros((w_in.shape[0], LANES - FOX_HEADS - 3 * NSA_HEADS), w_in.dtype)
    order = ("fq", "fk", "fv", "fz", "nq", "kc", "vc", "ks", "vs", "kw", "vw", "nz", "ff", "ng")
    return jnp.concatenate([cols[k] for k in order] + [pad], axis=1).astype(BF16)


def _layer(x, c8, g_pre, g_post, w_ada, b_ada, w_in, b_forget, w_cmp_k1, w_cmp_k2,
           w_cmp_v1, w_cmp_v2, pe_cmp_k, pe_cmp_v, w_out, cos128, sin128):
    b, s, d = x.shape
    mod = _ada(c8, w_ada, b_ada)[:b].reshape(b, 3, d)
    bf128 = jnp.pad(b_forget, (0, LANES - FOX_HEADS)).reshape(1, LANES)
    (fq, fk, fv, fz, nq, cmp_in, ks_aug, vs_ones, kw2, vw_ones, nz, misc, log_f) = _proj(
        x, mod, g_pre.reshape(1, d), _reorder_w_in(w_in), cos128, sin128, bf128)

    cum_split = tuple(p.reshape(b, FOX_HEADS, s)
                      for p in _cumsum_lanes_split(log_f.reshape(b * FOX_HEADS, s)))
    y_fox = _fox(fq, fk, fv, cum_split, fz)

    flat = CMP_STRIDE * HEAD_DIM
    t = cmp_in.reshape(2, b, NSA_KV_HEADS, s // CMP_STRIDE, flat)
    pe = jnp.stack([pe_cmp_k.reshape(1, 2 * flat), pe_cmp_v.reshape(1, 2 * flat)])
    w1 = jnp.stack([w_cmp_k1, w_cmp_v1])
    w2 = jnp.stack([w_cmp_k2, w_cmp_v2])
    cmp_hi, cmp_lo = _compress(t, jnp.concatenate([w1[:, :flat], w1[:, flat:]], axis=-1),
                               jnp.concatenate([w2, w2], axis=-1),
                               jnp.broadcast_to(pe, (2, SUBLANES, 2 * flat)))
    part, selb = _cmpwin(nq, cmp_hi, cmp_lo, kw2, vw_ones, misc)
    y_nsa = _sel(nq, ks_aug, vs_ones, selb, part, misc, nz)

    return _out(y_fox, y_nsa, w_out.astype(BF16), g_post.reshape(1, d), mod, x)


def kernel(x, c, g_pre, g_post, w_ada, b_ada, w_in, b_forget, w_cmp_k1, w_cmp_k2,
           w_cmp_v1, w_cmp_v2, pe_cmp_k, pe_cmp_v, w_out):
    cos128, sin128 = _rope_slabs(x.shape[1])
    c8 = jnp.pad(c, ((0, SUBLANES - c.shape[0]), (0, 0)))
    for layer in range(g_pre.shape[0]):
        x = _layer(x, c8, g_pre[layer], g_post[layer], w_ada[layer], b_ada[layer], w_in[layer],
                   b_forget[layer], w_cmp_k1[layer], w_cmp_k2[layer], w_cmp_v1[layer],
                   w_cmp_v2[layer], pe_cmp_k[layer], pe_cmp_v[layer], w_out[layer], cos128, sin128)
    return x
```
